```python
import jax, jax.numpy as jnp
from jax import lax
import numpy as np

D_MODEL = 1024
BATCH = 16
SEQ = 256
DEPTH = 2
DEC_BATCH = 2
DEC_SEQ = 4096
PAST_LEN = 512

GRID_W = 64
CHUNK = 128
N_DIR = 2
EPS = 1e-6
M_HEADS = 4
M_DK = D_MODEL // M_HEADS
M_DV = D_MODEL // M_HEADS
M_W = M_HEADS * M_DV
L_W = D_MODEL
L_BLOCKS = 8
L_BW = L_W // L_BLOCKS
L_CONV = 4
L_CONV_LEFT = 2
L_C = 8.0
R_HEADS = 8
R_DK = D_MODEL // R_HEADS
R_DV = D_MODEL // R_HEADS
R_W = R_HEADS * R_DV
ROPE_BASE = 10000.0
N_BRANCH = 3
BRANCH_W = D_MODEL
N_EXPERTS = 16
EC_FACTOR = 2
D_FF = D_MODEL
IN_SIZES = (M_HEADS * M_DK, M_HEADS * M_DK, M_W, M_W, 4 * M_HEADS,
            L_W, L_W,
            R_HEADS * R_DK, R_HEADS * R_DK, R_W, R_W)
D_IN = sum(IN_SIZES)

kernel_name = 'hybrid_mlstm_rglru_retention_ec_diffusion_step'


def rms_norm(x, g):
    xf = x.astype(jnp.float32)
    y = xf * lax.rsqrt(jnp.mean(xf * xf, axis=-1, keepdims=True) + EPS)
    return (y * g.astype(jnp.float32)).astype(x.dtype)


def head_norm(x, g):
    B, T, H, d = x.shape
    y = x * lax.rsqrt(jnp.mean(x * x, axis=-1, keepdims=True) + EPS)
    return y.reshape(B, T, H * d) * g.astype(jnp.float32)


def to_heads(a, n_heads):
    B, T, _ = a.shape
    return a.reshape(B, T, n_heads, -1).transpose(0, 2, 1, 3).astype(jnp.float32)


def to_chunks(a):
    B, H, T = a.shape[:3]
    a = a.reshape((B, H, T // CHUNK, CHUNK) + a.shape[3:])
    return jnp.moveaxis(a, 2, 0)


def from_chunks(a):
    nc, B, H, L, d = a.shape
    return jnp.moveaxis(a, 0, 2).reshape(B, H, nc * L, d)


def mlstm_dir(q, k, v, log_i, log_f, C0, n0, m0):
    idx = jnp.arange(CHUNK)
    causal = idx[:, None] >= idx[None, :]

    def step(carry, xs):
        C, n, m = carry
        qc, kc, vc, ic, fc = xs
        b = jnp.cumsum(fc, axis=-1)
        logw = jnp.where(causal, b[..., :, None] - b[..., None, :] + ic[..., None, :], -jnp.inf)
        log_prev = b + m[..., None]
        m_i = jnp.maximum(log_prev, jnp.max(logw, axis=-1))
        w_prev = jnp.exp(log_prev - m_i)
        s = jnp.einsum('bhid,bhjd->bhij', qc, kc) * jnp.exp(logw - m_i[..., None])
        num = jnp.einsum('bhij,bhje->bhie', s, vc) + w_prev[..., None] * jnp.einsum('bhid,bhde->bhie', qc, C)
        den = jnp.sum(s, axis=-1) + w_prev * jnp.einsum('bhid,bhd->bhi', qc, n)
        h = num / jnp.maximum(jnp.abs(den), jnp.exp(-m_i))[..., None]
        b_last = b[..., -1]
        log_k = b_last[..., None] - b + ic
        m_new = jnp.maximum(b_last + m, jnp.max(log_k, axis=-1))
        w_k = jnp.exp(log_k - m_new[..., None])
        w_C = jnp.exp(b_last + m - m_new)
        C_new = w_C[..., None, None] * C + jnp.einsum('bhjd,bhje->bhde', kc * w_k[..., None], vc)
        n_new = w_C[..., None] * n + jnp.einsum('bhj,bhjd->bhd', w_k, kc)
        return (C_new, n_new, m_new), h

    xs = (to_chunks(q), to_chunks(k), to_chunks(v), to_chunks(log_i), to_chunks(log_f))
    carry0 = (C0.astype(jnp.float32), n0.astype(jnp.float32), m0.astype(jnp.float32))
    (C, n, m), h = lax.scan(step, carry0, xs)
    return from_chunks(h), C, n, m


def retention_dir(q, k, v, log_g, S0):
    pos = jnp.arange(CHUNK, dtype=jnp.float32)
    diff = pos[:, None] - pos[None, :]
    causal = diff >= 0
    dmat = jnp.where(causal, jnp.exp(log_g[:, None, None] * jnp.where(causal, diff, 0.0)), 0.0)
    q_dec = jnp.exp(log_g[:, None] * (pos + 1.0))
    k_dec = jnp.exp(log_g[:, None] * (CHUNK - 1.0 - pos))
    g_chunk = jnp.exp(log_g * CHUNK)

    def step(S, xs):
        qc, kc, vc = xs
        s = jnp.einsum('bhid,bhjd->bhij', qc, kc) * dmat
        o = jnp.einsum('bhij,bhje->bhie', s, vc) + q_dec[..., None] * jnp.einsum('bhid,bhde->bhie', qc, S)
        S = g_chunk[:, None, None] * S + jnp.einsum('bhjd,bhje->bhde', kc * k_dec[..., None], vc)
        return S, o

    S, o = lax.scan(step, S0.astype(jnp.float32), (to_chunks(q), to_chunks(k), to_chunks(v)))
    return from_chunks(o), S


def dwconv_centred(x, w, b):
    T = x.shape[1]
    xp = jnp.pad(x, ((0, 0), (L_CONV_LEFT, L_CONV - 1 - L_CONV_LEFT), (0, 0)))
    y = b
    for j in range(L_CONV):
        y = y + xp[:, j:j + T] * w[j]
    return y


def rglru_dir(u, wr, br, wi, bi, lam, h0):
    B, T, W = u.shape
    ub = u.reshape(B, T, L_BLOCKS, L_BW)
    r = jax.nn.sigmoid(jnp.einsum('btnc,ncd->btnd', ub, wr.astype(jnp.float32)).reshape(B, T, W) + br.astype(jnp.float32))
    i = jax.nn.sigmoid(jnp.einsum('btnc,ncd->btnd', ub, wi.astype(jnp.float32)).reshape(B, T, W) + bi.astype(jnp.float32))
    log_a = -L_C * jax.nn.softplus(-lam.astype(jnp.float32)) * r
    a = jnp.exp(log_a)
    bterm = jnp.sqrt(-jnp.expm1(2.0 * log_a)) * (i * u)
    bterm = bterm.at[:, 0].add(a[:, 0] * h0.astype(jnp.float32))

    def combine(left, right):
        a1, b1 = left
        a2, b2 = right
        return a1 * a2, a2 * b1 + b2

    _, h = lax.associative_scan(combine, (a, bterm), axis=1)
    return h, h[:, -1]


def axial_rope(x, row, col):
    n_freq = R_DK // 4
    half = R_DK // 2
    freqs = jnp.power(ROPE_BASE, -jnp.arange(n_freq, dtype=jnp.float32) / n_freq)

    def rot(xp, p):
        ang = p[:, None] * freqs
        cos, sin = jnp.cos(ang), jnp.sin(ang)
        x1, x2 = xp[..., :n_freq], xp[..., n_freq:]
        return jnp.concatenate([x1 * cos - x2 * sin, x1 * sin + x2 * cos], axis=-1)

    return jnp.concatenate([rot(x[..., :half], row), rot(x[..., half:], col)], axis=-1)


def token_mix(h, rope_pos, st, p):
    B, T, _ = h.shape
    split_points = [int(s) for s in np.cumsum(IN_SIZES)[:-1]]
    proj = h @ p['w_in']
    mq, mk, mv, mo, mg, lx, lz, rq, rk, rv, rg = jnp.split(proj, split_points, axis=-1)
    fl = lambda a: jnp.flip(a, 2)

    q = to_heads(mq, M_HEADS)
    k = to_heads(mk, M_HEADS) * (M_DK ** -0.5)
    v = to_heads(mv, M_HEADS)
    gates = (mg.reshape(B, T, 4, M_HEADS).astype(jnp.float32)
             + p['mlstm_gate_bias'].astype(jnp.float32)).transpose(0, 2, 3, 1)
    ig_f, lf_f = gates[:, 0], jax.nn.log_sigmoid(gates[:, 1])
    ig_b, lf_b = gates[:, 2], jax.nn.log_sigmoid(gates[:, 3])
    hf, Cf, nf, mf = mlstm_dir(q, k, v, ig_f, lf_f, st['mC'][:, 0], st['mn'][:, 0], st['mm'][:, 0])
    hb, Cb, nb, mb = mlstm_dir(fl(q), fl(k), fl(v), fl(ig_b), fl(lf_b), st['mC'][:, 1], st['mn'][:, 1], st['mm'][:, 1])
    hm = (hf + fl(hb)).transpose(0, 2, 1, 3)
    y_m = (head_norm(hm, p['mlstm_norm_g']) * jax.nn.sigmoid(mo.astype(jnp.float32))).astype(h.dtype)

    u = dwconv_centred(lx.astype(jnp.float32), p['lru_conv_w'].astype(jnp.float32), p['lru_conv_b'].astype(jnp.float32))
    lf_h, lhf = rglru_dir(u, p['lru_wr'][0], p['lru_br'][0], p['lru_wi'][0], p['lru_bi'][0], p['lru_lambda'][0], st['lh'][:, 0])
    lb_h, lhb = rglru_dir(jnp.flip(u, 1), p['lru_wr'][1], p['lru_br'][1], p['lru_wi'][1], p['lru_bi'][1], p['lru_lambda'][1], st['lh'][:, 1])
    y_l = ((lf_h + jnp.flip(lb_h, 1)) * jax.nn.gelu(lz.astype(jnp.float32))).astype(h.dtype)

    q = to_heads(rq, R_HEADS)
    k = to_heads(rk, R_HEADS) * (R_DK ** -0.5)
    v = to_heads(rv, R_HEADS)
    if rope_pos is not None:
        q = axial_rope(q, rope_pos[0], rope_pos[1])
        k = axial_rope(k, rope_pos[0], rope_pos[1])
    log_g = -jnp.exp(p['ret_decay'].astype(jnp.float32))
    of, Sf = retention_dir(q, k, v, log_g[0], st['rS'][:, 0])
    ob, Sb = retention_dir(fl(q), fl(k), fl(v), log_g[1], st['rS'][:, 1])
    ro = (of + fl(ob)).transpose(0, 2, 1, 3)
    y_r = (head_norm(ro, p['ret_norm_g']) * jax.nn.silu(rg.astype(jnp.float32))).astype(h.dtype)

    gate = jax.nn.sigmoid(h @ p['w_merge'] + p['b_merge']).reshape(B, T, N_BRANCH, -1)
    branches = jnp.stack([y_m, y_l, y_r], axis=2)
    merged = jnp.einsum('btkd,btkd->btd', gate, jnp.einsum('btkc,kcd->btkd', branches, p['w_branch']))
    out = merged @ p['w_out']
    new_st = dict(mC=jnp.stack([Cf, Cb], axis=1), mn=jnp.stack([nf, nb], axis=1),
                  mm=jnp.stack([mf, mb], axis=1), lh=jnp.stack([lhf, lhb], axis=1),
                  rS=jnp.stack([Sf, Sb], axis=1))
    return out, new_st


def expert_choice(h, router_w, w1, w3, w2):
    B, N, D = h.shape
    cap = EC_FACTOR * N // N_EXPERTS
    aff = jax.nn.softmax((h @ router_w).astype(jnp.float32), axis=-1)
    g, idx = lax.top_k(jnp.swapaxes(aff, 1, 2), cap)
    xs = jax.vmap(lambda hb, ib: hb[ib])(h, idx)
    a = jnp.einsum('becd,edf->becf', xs, w1)
    b = jnp.einsum('becd,edf->becf', xs, w3)
    y = jnp.einsum('becf,efd->becd', jax.nn.silu(a) * b, w2) * g[..., None].astype(h.dtype)
    return jax.vmap(lambda ib, yb: jnp.zeros((N, D), yb.dtype).at[ib.reshape(-1)].add(yb.reshape(-1, D)))(idx, y)


def trunk_layer(x, cond, rope_pos, st, p):
    mod = (jax.nn.silu(cond) @ p['w_ada'] + p['b_ada'])[:, None, :]
    sh1, sc1, g1, sh2, sc2, g2 = jnp.split(mod, 6, axis=-1)
    h = rms_norm(x, p['norm1_g']) * (1 + sc1) + sh1
    mix, new_st = token_mix(h, rope_pos, st, p)
    x = x + g1 * mix
    h = rms_norm(x, p['norm2_g']) * (1 + sc2) + sh2
    x = x + g2 * expert_choice(h, p['router_w'], p['exp_w1'], p['exp_w3'], p['exp_w2'])
    return x, new_st


def setup_inputs(seed: int = 0) -> dict:
    key = jax.random.key(seed)
    ks = jax.random.split(key, 40)
    f32 = jnp.float32

    def nrm(k, shape, scale):
        return jax.random.normal(k, shape, f32) * scale

    D = D_MODEL
    f_bias = jnp.linspace(3.0, 6.0, M_HEADS, dtype=f32)
    zero_b = jnp.zeros((M_HEADS,), f32)
    gate_base = jnp.stack([zero_b, f_bias, zero_b, f_bias], axis=0)
    u = jax.random.uniform(ks[20], (DEPTH, N_DIR, L_W), f32, 0.9, 0.999)
    a0 = jnp.power(u, 1.0 / L_C)
    ret_base = jnp.log(-jnp.log1p(-jnp.power(2.0, -5.0 - jnp.arange(R_HEADS, dtype=f32))))
    return {
        'x_prompt': nrm(ks[0], (BATCH, SEQ, D), 1.0),
        'x_sample': nrm(ks[1], (DEC_BATCH, DEC_SEQ, D), 1.0),
        'c': nrm(ks[2], (DEC_BATCH, D), 1.0),
        'state_mlstm_C': nrm(ks[3], (DEC_BATCH, DEPTH, N_DIR, M_HEADS, M_DK, M_DV), 0.5),
        'state_mlstm_n': nrm(ks[4], (DEC_BATCH, DEPTH, N_DIR, M_HEADS, M_DK), 0.5),
        'state_mlstm_m': nrm(ks[5], (DEC_BATCH, DEPTH, N_DIR, M_HEADS), 0.5),
        'state_lru_h': nrm(ks[6], (DEC_BATCH, DEPTH, N_DIR, L_W), 0.5),
        'state_ret_S': nrm(ks[7], (DEC_BATCH, DEPTH, N_DIR, R_HEADS, R_DK, R_DV), 0.5),
        'c_ctx': nrm(ks[8], (D,), 1.0),
        'w_ada': nrm(ks[9], (DEPTH, D, 6 * D), 0.5 * D ** -0.5),
        'b_ada': nrm(ks[10], (DEPTH, 6 * D), 0.02),
        'norm1_g': 1.0 + nrm(ks[11], (DEPTH, D), 0.02),
        'norm2_g': 1.0 + nrm(ks[12], (DEPTH, D), 0.02),
        'w_in': nrm(ks[13], (DEPTH, D, D_IN), D ** -0.5),
        'mlstm_gate_bias': gate_base + nrm(ks[14], (DEPTH, 4, M_HEADS), 0.1),
        'mlstm_norm_g': 1.0 + nrm(ks[15], (DEPTH, M_W), 0.02),
        'lru_conv_w': nrm(ks[16], (DEPTH, L_CONV, L_W), L_CONV ** -0.5),
        'lru_conv_b': nrm(ks[17], (DEPTH, L_W), 0.02),
        'lru_wr': nrm(ks[18], (DEPTH, N_DIR, L_BLOCKS, L_BW, L_BW), L_BW ** -0.5),
        'lru_br': nrm(ks[19], (DEPTH, N_DIR, L_W), 0.02),
        'lru_wi': nrm(ks[21], (DEPTH, N_DIR, L_BLOCKS, L_BW, L_BW), L_BW ** -0.5),
        'lru_bi': nrm(ks[22], (DEPTH, N_DIR, L_W), 0.02),
        'lru_lambda': jnp.log(a0 / (1.0 - a0)),
        'ret_decay': ret_base + nrm(ks[23], (DEPTH, N_DIR, R_HEADS), 0.01),
        'ret_norm_g': 1.0 + nrm(ks[24], (DEPTH, R_W), 0.02),
        'w_branch': nrm(ks[25], (DEPTH, N_BRANCH, BRANCH_W, D), BRANCH_W ** -0.5),
        'w_merge': nrm(ks[26], (DEPTH, D, N_BRANCH * D), D ** -0.5),
        'b_merge': nrm(ks[27], (DEPTH, N_BRANCH * D), 0.02),
        'w_out': nrm(ks[28], (DEPTH, D, D), D ** -0.5),
        'router_w': nrm(ks[29], (DEPTH, D, N_EXPERTS), D ** -0.5),
        'exp_w1': nrm(ks[30], (DEPTH, N_EXPERTS, D, D_FF), D ** -0.5),
        'exp_w3': nrm(ks[31], (DEPTH, N_EXPERTS, D, D_FF), D ** -0.5),
        'exp_w2': nrm(ks[32], (DEPTH, N_EXPERTS, D_FF, D), D_FF ** -0.5),
        'final_g': 1.0 + nrm(ks[33], (D,), 0.02),
    }


def reference(x_prompt, x_sample, c, state_mlstm_C, state_mlstm_n, state_mlstm_m, state_lru_h, state_ret_S,
              c_ctx, w_ada, b_ada, norm1_g, norm2_g, w_in, mlstm_gate_bias, mlstm_norm_g,
              lru_conv_w, lru_conv_b, lru_wr, lru_br, lru_wi, lru_bi, lru_lambda,
              ret_decay, ret_norm_g, w_branch, w_merge, b_merge, w_out,
              router_w, exp_w1, exp_w3, exp_w2, final_g):
    layer_params = []
    for l in range(DEPTH):
        layer_params.append(dict(
            w_ada=w_ada[l], b_ada=b_ada[l], norm1_g=norm1_g[l], norm2_g=norm2_g[l], w_in=w_in[l],
            mlstm_gate_bias=mlstm_gate_bias[l], mlstm_norm_g=mlstm_norm_g[l],
            lru_conv_w=lru_conv_w[l], lru_conv_b=lru_conv_b[l], lru_wr=lru_wr[l], lru_br=lru_br[l],
            lru_wi=lru_wi[l], lru_bi=lru_bi[l], lru_lambda=lru_lambda[l],
            ret_decay=ret_decay[l], ret_norm_g=ret_norm_g[l], w_branch=w_branch[l],
            w_merge=w_merge[l], b_merge=b_merge[l], w_out=w_out[l], router_w=router_w[l],
            exp_w1=exp_w1[l], exp_w3=exp_w3[l], exp_w2=exp_w2[l]))

    Bp = x_prompt.shape[0]
    zero_st = dict(mC=jnp.zeros((Bp, N_DIR, M_HEADS, M_DK, M_DV), jnp.float32),
                   mn=jnp.zeros((Bp, N_DIR, M_HEADS, M_DK), jnp.float32),
                   mm=jnp.zeros((Bp, N_DIR, M_HEADS), jnp.float32),
                   lh=jnp.zeros((Bp, N_DIR, L_W), jnp.float32),
                   rS=jnp.zeros((Bp, N_DIR, R_HEADS, R_DK, R_DV), jnp.float32))
    ctx_cond = c_ctx[None, :]
    xp = x_prompt
    prompt_states = []
    for l in range(DEPTH):
        xp, st = trunk_layer(xp, ctx_cond, None, zero_st, layer_params[l])
        prompt_states.append(st)
    y_prompt = rms_norm(xp, final_g)

    T = x_sample.shape[1]
    rows = T // GRID_W
    row = jnp.repeat(jnp.arange(rows, dtype=jnp.float32), GRID_W)
    col = jnp.tile(jnp.arange(GRID_W, dtype=jnp.float32), rows)
    xs = x_sample
    for l in range(DEPTH):
        st = dict(mC=state_mlstm_C[:, l], mn=state_mlstm_n[:, l], mm=state_mlstm_m[:, l],
                  lh=state_lru_h[:, l], rS=state_ret_S[:, l])
        xs, _ = trunk_layer(xs, c, (row, col), st, layer_params[l])
    y_sample = rms_norm(xs, final_g)

    new_mlstm_C = jnp.stack([s['mC'] for s in prompt_states], axis=1)
    new_mlstm_n = jnp.stack([s['mn'] for s in prompt_states], axis=1)
    new_mlstm_m = jnp.stack([s['mm'] for s in prompt_states], axis=1)
    new_lru_h = jnp.stack([s['lh'] for s in prompt_states], axis=1)
    new_ret_S = jnp.stack([s['rS'] for s in prompt_states], axis=1)
    return (y_prompt, y_sample, new_mlstm_C, new_mlstm_n, new_mlstm_m, new_lru_h, new_ret_S)
```

```python
import functools

import jax
import jax.numpy as jnp
from jax import lax
from jax.experimental import pallas as pl
from jax.experimental.pallas import tpu as pltpu

F32 = jnp.float32
BF16 = jnp.bfloat16

D = 1024
DEPTH = 2
N_PROMPT_SEQ = 16
T_PROMPT = 256
N_SAMPLE_SEQ = 2
T_SAMPLE = 4096
ROWS_PROMPT = N_PROMPT_SEQ * T_PROMPT
ROWS = ROWS_PROMPT + N_SAMPLE_SEQ * T_SAMPLE
GROUP_ROWS = 4096
GRID_W = 64
CHUNK = 128
EPS = 1e-6
M_HEADS = 4
M_DH = 256
R_HEADS = 8
R_DH = 128
L_BLOCKS = 8
L_BW = 128
L_C = 8.0
ROPE_BASE = 10000.0
N_EXPERTS = 16
CAP_PROMPT = 2 * T_PROMPT // N_EXPERTS
CAP_SAMPLE = 2 * T_SAMPLE // N_EXPERTS
ROWS_PER_EXPERT = N_PROMPT_SEQ * CAP_PROMPT + N_SAMPLE_SEQ * CAP_SAMPLE
N_GATE_COLS = 16
D_MAIN = 10 * D
LANES = 128
VMEM_LIMIT = 56 * 2 ** 20

COL_MQ, COL_MK, COL_MV, COL_MO, COL_LX, COL_LZ, COL_RQ, COL_RK, COL_RV, COL_RG = (
    i * D for i in range(10))


def _cparams(sem, vmem=None):
    return pltpu.CompilerParams(dimension_semantics=sem, vmem_limit_bytes=vmem)


def _dot(a, b):
    return jnp.dot(a, b, preferred_element_type=F32)


def _dot_nt(a, b):
    return lax.dot_general(a, b, (((1,), (1,)), ((), ())), preferred_element_type=F32)


def _dot_tn(a, b):
    return lax.dot_general(a, b, (((0,), (0,)), ((), ())), preferred_element_type=F32)


def _split3(x):
    a = x.astype(BF16)
    r = x - a.astype(F32)
    b = r.astype(BF16)
    c = (r - b.astype(F32)).astype(BF16)
    return a, b, c


def _log_sigmoid(x):
    return jnp.minimum(x, 0.0) - jnp.log1p(jnp.exp(-jnp.abs(x)))


def _rms(x):
    return x * lax.rsqrt(jnp.mean(x * x, axis=-1, keepdims=True) + EPS)


def _ada_kernel(c_ref, w_ref, b_ref, o_ref):
    c = c_ref[...]
    s = (c * jax.nn.sigmoid(c)).astype(BF16)
    o_ref[0] = _dot(s, w_ref[0].astype(BF16)) + b_ref[0]


def _ada_call(cond8, w_ada, b_ada):
    tn = 1536
    return pl.pallas_call(
        _ada_kernel,
        grid=(DEPTH, 6 * D // tn),
        in_specs=[pl.BlockSpec((8, D), lambda l, j: (0, 0)),
                  pl.BlockSpec((1, D, tn), lambda l, j: (l, 0, j)),
                  pl.BlockSpec((1, 1, tn), lambda l, j: (l, 0, j))],
        out_specs=pl.BlockSpec((1, 8, tn), lambda l, j: (l, 0, j)),
        out_shape=jax.ShapeDtypeStruct((DEPTH, 8, 6 * D), F32),
        compiler_params=_cparams(("arbitrary", "arbitrary"), VMEM_LIMIT),
        name="ada",
    )(cond8, w_ada, b_ada.reshape(DEPTH, 1, 6 * D))


def _norm1_kernel(x_ref, g_ref, mod_ref, wg_ref, h_ref, gate_ref):
    y = _rms(x_ref[...]) * g_ref[...]
    h = (y * (1.0 + mod_ref[0, 1:2, :]) + mod_ref[0, 0:1, :]).astype(BF16)
    h_ref[...] = h
    gate_ref[...] = _dot(h, wg_ref[...])


def _norm1_call(x, g, mod, w_gate):
    tm = 512
    return pl.pallas_call(
        _norm1_kernel,
        grid=(ROWS // tm,),
        in_specs=[pl.BlockSpec((tm, D), lambda i: (i, 0)),
                  pl.BlockSpec((1, D), lambda i: (0, 0)),
                  pl.BlockSpec((1, 6, D), lambda i: (i * tm // GROUP_ROWS, 0, 0)),
                  pl.BlockSpec((D, LANES), lambda i: (0, 0))],
        out_specs=[pl.BlockSpec((tm, D), lambda i: (i, 0)),
                   pl.BlockSpec((tm, LANES), lambda i: (i, 0))],
        out_shape=[jax.ShapeDtypeStruct((ROWS, D), BF16),
                   jax.ShapeDtypeStruct((ROWS, LANES), F32)],
        compiler_params=_cparams(("arbitrary",), VMEM_LIMIT),
        name="norm1",
    )(x, g.reshape(1, D), mod, w_gate)


def _mm_kernel(x_ref, w_ref, o_ref):
    o_ref[...] = _dot(x_ref[...], w_ref[...]).astype(o_ref.dtype)


def _inproj_call(h, w_main):
    tm, tn = 1024, 1024
    return pl.pallas_call(
        _mm_kernel,
        grid=(D_MAIN // tn, ROWS // tm),
        in_specs=[pl.BlockSpec((tm, D), lambda j, i: (i, 0)),
                  pl.BlockSpec((D, tn), lambda j, i: (0, j))],
        out_specs=pl.BlockSpec((tm, tn), lambda j, i: (i, j)),
        out_shape=jax.ShapeDtypeStruct((ROWS, D_MAIN), BF16),
        compiler_params=_cparams(("arbitrary", "arbitrary"), VMEM_LIMIT),
        name="inproj",
    )(h, w_main)


def _mlstm_kernel(*refs, n_chunks, has_init, emit_state):
    q_ref, k_ref, v_ref, o_ref, g_ref, ng_ref = refs[:6]
    pos = 6
    if has_init:
        c0_ref, n0_ref, m0_ref = refs[pos:pos + 3]
        pos += 3
    y_ref = refs[pos]
    pos += 1
    if emit_state:
        co_ref, no_ref, mo_ref = refs[pos:pos + 3]
        pos += 3
    c_sc, n_sc, m_sc, hf_sc, hb_sc = refs[pos:]

    if has_init:
        c_sc[...] = c0_ref[0, :, 0]
        n_sc[...] = n0_ref[0, :, 0]
        m_sc[...] = m0_ref[0, :, 0]
    else:
        c_sc[...] = jnp.zeros_like(c_sc)
        n_sc[...] = jnp.zeros_like(n_sc)
        m_sc[...] = jnp.zeros_like(m_sc)

    ri = lax.broadcasted_iota(jnp.int32, (CHUNK, CHUNK), 0)
    ci = lax.broadcasted_iota(jnp.int32, (CHUNK, CHUNK), 1)
    r8 = lax.broadcasted_iota(jnp.int32, (8, CHUNK), 0)
    zpad = jnp.zeros((CHUNK - 8, CHUNK), F32)

    def step(d, c):
        r0 = pl.multiple_of(c * CHUNK, CHUNK)
        rows = pl.ds(r0, CHUNK)
        g = g_ref[0, :, rows]
        i_row_idx, f_row_idx = (0, 1) if d == 0 else (2, 3)
        ls = jnp.where(r8 == f_row_idx, _log_sigmoid(g), g)
        tri = (ri <= ci) if d == 0 else (ri >= ci)
        tri = jnp.where(tri, 1.0, 0.0).astype(BF16)
        cum = sum(_dot(p, tri) for p in _split3(ls))
        b_row = cum[f_row_idx:f_row_idx + 1]
        i_row = ls[i_row_idx:i_row_idx + 1]
        packed = jnp.where(r8 == 0, jnp.broadcast_to(i_row, (8, CHUNK)),
                           jnp.where(r8 == 1, jnp.broadcast_to(b_row, (8, CHUNK)), 0.0))
        cols = jnp.concatenate([packed, zpad], axis=0).T
        i_col = cols[:, 0:1]
        b_col = cols[:, 1:2]

        m = m_sc[d][:, 0:1]
        n = n_sc[d]
        mask = (ri >= ci) if d == 0 else (ci >= ri)
        logw = jnp.where(mask, b_col - b_row + i_row, -jnp.inf)
        log_prev = b_col + m
        m_i = jnp.maximum(log_prev, jnp.max(logw, axis=-1, keepdims=True))
        w_prev = jnp.exp(log_prev - m_i)

        q = q_ref[rows, :]
        v = v_ref[rows, :]
        kf = k_ref[rows, :].astype(F32) * (M_DH ** -0.5)
        s = _dot_nt(q, kf.astype(BF16)) * jnp.exp(logw - m_i)
        num = _dot(s.astype(BF16), v) + w_prev * _dot(q, c_sc[d].astype(BF16))
        qn = jnp.sum(q.astype(F32) * n, axis=-1, keepdims=True)
        den = jnp.sum(s, axis=-1, keepdims=True) + w_prev * qn
        h = num / jnp.maximum(jnp.abs(den), jnp.exp(-m_i))
        if d == 0:
            hf_sc[rows, :] = h
        else:
            hb_sc[rows, :] = h

        b_last = b_row[:, CHUNK - 1:CHUNK] if d == 0 else b_row[:, 0:1]
        log_k = b_last - b_col + i_col
        m_new = jnp.maximum(b_last + m, jnp.max(log_k, axis=0, keepdims=True))
        w_k = jnp.exp(log_k - m_new)
        w_c = jnp.exp(b_last + m - m_new)
        kw = kf * w_k
        c_sc[d] = w_c * c_sc[d] + _dot_tn(kw.astype(BF16), v)
        n_sc[d] = w_c * n + jnp.sum(kw, axis=0, keepdims=True)
        m_sc[d] = jnp.broadcast_to(m_new, (1, LANES))

    def scan_body(c, carry):
        step(0, c)
        step(1, n_chunks - 1 - c)
        return carry

    lax.fori_loop(0, n_chunks, scan_body, 0)

    def out_body(c, carry):
        rows = pl.ds(pl.multiple_of(c * CHUNK, CHUNK), CHUNK)
        y = _rms(hf_sc[rows, :] + hb_sc[rows, :]) * ng_ref[...]
        y_ref[rows, :] = (y * jax.nn.sigmoid(o_ref[rows, :].astype(F32))).astype(y_ref.dtype)
        return carry

    lax.fori_loop(0, n_chunks, out_body, 0)

    if emit_state:
        co_ref[0, :, 0] = c_sc[...]
        no_ref[0, :, 0] = n_sc[...]
        mo_ref[0, :, 0] = m_sc[...]


def _mlstm_call(proj, grow, norm_g, state, *, t, n_seq, row_block0, emit_state):
    has_init = state is not None
    wb = M_DH // M_DH

    def col(base):
        return lambda s, h: (row_block0 + s, base // M_DH + h * wb)

    st_c = pl.BlockSpec((1, 2, 1, M_DH, M_DH), lambda s, h: (s, 0, h, 0, 0))
    st_n = pl.BlockSpec((1, 2, 1, 1, M_DH), lambda s, h: (s, 0, h, 0, 0))
    st_m = pl.BlockSpec((1, 2, 1, 1, LANES), lambda s, h: (s, 0, h, 0, 0))
    in_specs = [pl.BlockSpec((t, M_DH), col(COL_MQ)),
                pl.BlockSpec((t, M_DH), col(COL_MK)),
                pl.BlockSpec((t, M_DH), col(COL_MV)),
                pl.BlockSpec((t, M_DH), col(COL_MO)),
                pl.BlockSpec((1, 8, t), lambda s, h: (h, 0, row_block0 + s)),
                pl.BlockSpec((1, M_DH), lambda s, h: (0, h))]
    args = [proj, proj, proj, proj, grow, norm_g.reshape(1, D)]
    if has_init:
        in_specs += [st_c, st_n, st_m]
        args += list(state)
    out_specs = [pl.BlockSpec((t, M_DH), lambda s, h: (s, h))]
    out_shape = [jax.ShapeDtypeStruct((n_seq * t, D), BF16)]
    if emit_state:
        out_specs += [st_c, st_n, st_m]
        out_shape += [jax.ShapeDtypeStruct((n_seq, 2, M_HEADS, M_DH, M_DH), F32),
                      jax.ShapeDtypeStruct((n_seq, 2, M_HEADS, 1, M_DH), F32),
                      jax.ShapeDtypeStruct((n_seq, 2, M_HEADS, 1, LANES), F32)]
    return pl.pallas_call(
        functools.partial(_mlstm_kernel, n_chunks=t // CHUNK, has_init=has_init,
                          emit_state=emit_state),
        grid=(n_seq, M_HEADS),
        in_specs=in_specs,
        out_specs=out_specs,
        out_shape=out_shape,
        scratch_shapes=[pltpu.VMEM((2, M_DH, M_DH), F32),
                        pltpu.VMEM((2, 1, M_DH), F32),
                        pltpu.VMEM((2, 1, LANES), F32),
                        pltpu.VMEM((t, M_DH), F32),
                        pltpu.VMEM((t, M_DH), F32)],
        compiler_params=_cparams(("arbitrary", "arbitrary"), VMEM_LIMIT),
        name="mlstm",
    )(*args)


def _ret_kernel(*refs, n_chunks, rope, has_init, emit_state):
    q_ref, k_ref, v_ref, g_ref, dec_ref, ng_ref = refs[:6]
    pos = 6
    if rope:
        cos_ref, sa_ref, sb_ref = refs[pos:pos + 3]
        pos += 3
    if has_init:
        s0_ref = refs[pos]
        pos += 1
    y_ref = refs[pos]
    pos += 1
    if emit_state:
        so_ref = refs[pos]
        pos += 1
    s_sc, of_sc, ob_sc, kr_sc = refs[pos:pos + 4]
    qr_sc = refs[pos + 4] if rope else None

    if has_init:
        s_sc[...] = s0_ref[0, :, 0]
    else:
        s_sc[...] = jnp.zeros_like(s_sc)

    def rotate(x, rows):
        return (x * cos_ref[rows, :] + pltpu.roll(x, LANES - 32, 1) * sa_ref[rows, :]
                + pltpu.roll(x, 32, 1) * sb_ref[rows, :])

    def prep_body(c, carry):
        rows = pl.ds(pl.multiple_of(c * CHUNK, CHUNK), CHUNK)
        kf = k_ref[rows, :].astype(F32) * (R_DH ** -0.5)
        if rope:
            kf = rotate(kf, rows)
            qr_sc[rows, :] = rotate(q_ref[rows, :].astype(F32), rows).astype(BF16)
        kr_sc[rows, :] = kf.astype(BF16)
        return carry

    lax.fori_loop(0, n_chunks, prep_body, 0)
    qsrc = qr_sc if rope else q_ref

    ri = lax.broadcasted_iota(jnp.int32, (CHUNK, CHUNK), 0)
    ci = lax.broadcasted_iota(jnp.int32, (CHUNK, CHUNK), 1)
    pos_col = ri[:, 0:1].astype(F32)
    consts = []
    for d in range(2):
        lg = -jnp.exp(dec_ref[0, d:d + 1, :])
        lg11 = lg[:, 0:1]
        diff = (ri - ci) if d == 0 else (ci - ri)
        causal = diff >= 0
        dmat = jnp.where(causal, jnp.exp(lg * jnp.where(causal, diff, 0).astype(F32)), 0.0)
        if d == 0:
            q_dec = jnp.exp(lg11 * (pos_col + 1.0))
            k_dec = jnp.exp(lg11 * (CHUNK - 1.0 - pos_col))
        else:
            q_dec = jnp.exp(lg11 * (CHUNK - pos_col))
            k_dec = jnp.exp(lg11 * pos_col)
        g_chunk = jnp.exp(lg11 * float(CHUNK))
        consts.append((dmat, q_dec, k_dec, g_chunk))

    def step(d, c):
        dmat, q_dec, k_dec, g_chunk = consts[d]
        rows = pl.ds(pl.multiple_of(c * CHUNK, CHUNK), CHUNK)
        q = qsrc[rows, :]
        k = kr_sc[rows, :]
        v = v_ref[rows, :]
        s = _dot_nt(q, k) * dmat
        o = _dot(s.astype(BF16), v) + q_dec * _dot(q, s_sc[d].astype(BF16))
        if d == 0:
            of_sc[rows, :] = o
        else:
            ob_sc[rows, :] = o
        kd = (k.astype(F32) * k_dec).astype(BF16)
        s_sc[d] = g_chunk * s_sc[d] + _dot_tn(kd, v)

    def scan_body(c, carry):
        step(0, c)
        step(1, n_chunks - 1 - c)
        return carry

    lax.fori_loop(0, n_chunks, scan_body, 0)

    def out_body(c, carry):
        rows = pl.ds(pl.multiple_of(c * CHUNK, CHUNK), CHUNK)
        y = _rms(of_sc[rows, :] + ob_sc[rows, :]) * ng_ref[...]
        gate = g_ref[rows, :].astype(F32)
        y_ref[rows, :] = (y * (gate * jax.nn.sigmoid(gate))).astype(y_ref.dtype)
        return carry

    lax.fori_loop(0, n_chunks, out_body, 0)

    if emit_state:
        so_ref[0, :, 0] = s_sc[...]


def _ret_call(proj, decay, norm_g, rope_tabs, state, *, t, n_seq, row_block0, emit_state):
    has_init = state is not None
    rope = rope_tabs is not None

    def col(base):
        return lambda s, h: (row_block0 + s, base // R_DH + h)

    st_s = pl.BlockSpec((1, 2, 1, R_DH, R_DH), lambda s, h: (s, 0, h, 0, 0))
    in_specs = [pl.BlockSpec((t, R_DH), col(COL_RQ)),
                pl.BlockSpec((t, R_DH), col(COL_RK)),
                pl.BlockSpec((t, R_DH), col(COL_RV)),
                pl.BlockSpec((t, R_DH), col(COL_RG)),
                pl.BlockSpec((1, 2, LANES), lambda s, h: (h, 0, 0)),
                pl.BlockSpec((1, R_DH), lambda s, h: (0, h))]
    args = [proj, proj, proj, proj, decay, norm_g.reshape(1, D)]
    if rope:
        in_specs += [pl.BlockSpec((t, R_DH), lambda s, h: (0, 0))] * 3
        args += list(rope_tabs)
    if has_init:
        in_specs.append(st_s)
        args.append(state)
    out_specs = [pl.BlockSpec((t, R_DH), lambda s, h: (s, h))]
    out_shape = [jax.ShapeDtypeStruct((n_seq * t, D), BF16)]
    if emit_state:
        out_specs.append(st_s)
        out_shape.append(jax.ShapeDtypeStruct((n_seq, 2, R_HEADS, R_DH, R_DH), F32))
    scratch = [pltpu.VMEM((2, R_DH, R_DH), F32),
               pltpu.VMEM((t, R_DH), F32),
               pltpu.VMEM((t, R_DH), F32),
               pltpu.VMEM((t, R_DH), BF16)]
    if rope:
        scratch.append(pltpu.VMEM((t, R_DH), BF16))
    return pl.pallas_call(
        functools.partial(_ret_kernel, n_chunks=t // CHUNK, rope=rope, has_init=has_init,
                          emit_state=emit_state),
        grid=(n_seq, R_HEADS),
        in_specs=in_specs,
        out_specs=out_specs,
        out_shape=out_shape,
        scratch_shapes=scratch,
        compiler_params=_cparams(("arbitrary", "arbitrary"), VMEM_LIMIT),
        name="retention",
    )(*args)


def _lru_kernel(*refs, t, n_seg, has_init, emit_state):
    lx_ref, lz_ref, cw_ref, cb_ref, wr_ref, wi_ref, br_ref, bi_ref, lam_ref = refs[:9]
    pos = 9
    if has_init:
        h0_ref = refs[pos]
        pos += 1
    y_ref = refs[pos]
    pos += 1
    if emit_state:
        hfin_ref = refs[pos]
        pos += 1
    xpad_sc, a_sc, b_sc = refs[pos:]
    seg = t // n_seg
    rc = min(t, 256)

    xpad_sc[0:8, :] = jnp.zeros((8, L_BW), F32)
    xpad_sc[t + 8:t + 16, :] = jnp.zeros((8, L_BW), F32)

    def pad_body(c, carry):
        r0 = pl.multiple_of(c * rc, rc)
        xpad_sc[pl.ds(r0 + 8, rc), :] = lx_ref[pl.ds(r0, rc), :].astype(F32)
        return carry

    lax.fori_loop(0, t // rc, pad_body, 0)

    wr = [wr_ref[d, 0].astype(BF16) for d in range(2)]
    wi = [wi_ref[d, 0].astype(BF16) for d in range(2)]
    neg_c = []
    for d in range(2):
        lam = lam_ref[d:d + 1, :]
        softplus_neg = jnp.maximum(-lam, 0.0) + jnp.log1p(jnp.exp(-jnp.abs(lam)))
        neg_c.append(-L_C * softplus_neg)

    def gate_body(c, carry):
        r0 = pl.multiple_of(c * rc, rc)
        xe = xpad_sc[pl.ds(r0, rc + 16), :]
        n = rc + 16
        u = (cb_ref[...] + cw_ref[0:1, :] * pltpu.roll(xe, 2, 0)[8:8 + rc]
             + cw_ref[1:2, :] * pltpu.roll(xe, 1, 0)[8:8 + rc]
             + cw_ref[2:3, :] * xe[8:8 + rc]
             + cw_ref[3:4, :] * pltpu.roll(xe, n - 1, 0)[8:8 + rc])
        ub = u.astype(BF16)
        for d in range(2):
            r = jax.nn.sigmoid(_dot(ub, wr[d]) + br_ref[d:d + 1, :])
            i = jax.nn.sigmoid(_dot(ub, wi[d]) + bi_ref[d:d + 1, :])
            log_a = neg_c[d] * r
            e2 = jnp.exp(2.0 * log_a)
            mult = jnp.sqrt(-jnp.tanh(log_a) * (e2 + 1.0))
            a_sc[d, pl.ds(r0, rc), :] = jnp.exp(log_a)
            b_sc[d, pl.ds(r0, rc), :] = mult * (i * u)
        return carry

    lax.fori_loop(0, t // rc, gate_body, 0)

    def scan_body(i, carry):
        hf, pf, hb, pb = carry
        j = seg - 1 - i
        fa = a_sc[0, pl.ds(i, n_seg, stride=seg), :]
        fb = b_sc[0, pl.ds(i, n_seg, stride=seg), :]
        ba = a_sc[1, pl.ds(j, n_seg, stride=seg), :]
        bb = b_sc[1, pl.ds(j, n_seg, stride=seg), :]
        hf = fa * hf + fb
        pf = pf * fa
        hb = ba * hb + bb
        pb = pb * ba
        b_sc[0, pl.ds(i, n_seg, stride=seg), :] = hf
        a_sc[0, pl.ds(i, n_seg, stride=seg), :] = pf
        b_sc[1, pl.ds(j, n_seg, stride=seg), :] = hb
        a_sc[1, pl.ds(j, n_seg, stride=seg), :] = pb
        return hf, pf, hb, pb

    zero = jnp.zeros((n_seg, L_BW), F32)
    one = jnp.ones((n_seg, L_BW), F32)
    lax.fori_loop(0, seg, scan_body, (zero, one, zero, one))

    if has_init:
        cf = h0_ref[0, 0:1, :]
        cbk = h0_ref[0, 1:2, :]
    else:
        cf = jnp.zeros((1, L_BW), F32)
        cbk = jnp.zeros((1, L_BW), F32)
    carry_f = []
    for s in range(n_seg):
        carry_f.append(cf)
        last = (s + 1) * seg - 1
        cf = b_sc[0, last:last + 1, :] + a_sc[0, last:last + 1, :] * cf
    carry_b = [None] * n_seg
    for s in range(n_seg - 1, -1, -1):
        carry_b[s] = cbk
        first = s * seg
        cbk = b_sc[1, first:first + 1, :] + a_sc[1, first:first + 1, :] * cbk
    if emit_state:
        hfin_ref[0, 0:1, :] = cf
        hfin_ref[0, 1:2, :] = cbk

    for s in range(n_seg):
        rows = slice(s * seg, (s + 1) * seg)
        hf = b_sc[0, rows, :] + a_sc[0, rows, :] * carry_f[s]
        hb = b_sc[1, rows, :] + a_sc[1, rows, :] * carry_b[s]
        z = lz_ref[rows, :].astype(F32)
        gelu = 0.5 * z * (1.0 + jnp.tanh(0.7978845608028654 * (z + 0.044715 * (z * z * z))))
        y_ref[rows, :] = ((hf + hb) * gelu).astype(y_ref.dtype)


def _lru_call(proj, p, state, *, t, n_seq, row_block0, n_seg, emit_state):
    has_init = state is not None

    def col(base):
        return lambda s, b: (row_block0 + s, base // L_BW + b)

    vec2 = pl.BlockSpec((2, L_BW), lambda s, b: (0, b))
    st = pl.BlockSpec((1, 2, L_BW), lambda s, b: (s, 0, b))
    in_specs = [pl.BlockSpec((t, L_BW), col(COL_LX)),
                pl.BlockSpec((t, L_BW), col(COL_LZ)),
                pl.BlockSpec((4, L_BW), lambda s, b: (0, b)),
                pl.BlockSpec((1, L_BW), lambda s, b: (0, b)),
                pl.BlockSpec((2, 1, L_BW, L_BW), lambda s, b: (0, b, 0, 0)),
                pl.BlockSpec((2, 1, L_BW, L_BW), lambda s, b: (0, b, 0, 0)),
                vec2, vec2, vec2]
    args = [proj, proj, p['conv_w'], p['conv_b'].reshape(1, D), p['wr'], p['wi'],
            p['br'], p['bi'], p['lam']]
    if has_init:
        in_specs.append(st)
        args.append(state)
    out_specs = [pl.BlockSpec((t, L_BW), lambda s, b: (s, b))]
    out_shape = [jax.ShapeDtypeStruct((n_seq * t, D), BF16)]
    if emit_state:
        out_specs.append(st)
        out_shape.append(jax.ShapeDtypeStruct((n_seq, 2, D), F32))
    return pl.pallas_call(
        functools.partial(_lru_kernel, t=t, n_seg=n_seg, has_init=has_init, emit_state=emit_state),
        grid=(n_seq, L_BLOCKS),
        in_specs=in_specs,
        out_specs=out_specs,
        out_shape=out_shape,
        scratch_shapes=[pltpu.VMEM((t + 16, L_BW), F32),
                        pltpu.VMEM((2, t, L_BW), F32),
                        pltpu.VMEM((2, t, L_BW), F32)],
        compiler_params=_cparams(("arbitrary", "arbitrary"), VMEM_LIMIT),
        name="rglru",
    )(*args)


def _merge_kernel(h_ref, ym_ref, yl_ref, yr_ref, x_ref, mod_ref, wm_ref, bm_ref, wb_ref, wo_ref,
                  n2_ref, rwh_ref, rwl_ref, xo_ref, h2_ref, lg_ref):
    h = h_ref[...]
    merged = None
    for k, y_ref in enumerate((ym_ref, yl_ref, yr_ref)):
        gate = jax.nn.sigmoid(_dot(h, wm_ref[:, k * D:(k + 1) * D]) + bm_ref[:, k * D:(k + 1) * D])
        term = gate * _dot(y_ref[...], wb_ref[k])
        merged = term if merged is None else merged + term
    mix = _dot(merged.astype(BF16), wo_ref[...])
    xn = x_ref[...] + mod_ref[0, 2:3, :] * mix
    xo_ref[...] = xn
    hn = _rms(xn) * n2_ref[...] * (1.0 + mod_ref[0, 4:5, :]) + mod_ref[0, 3:4, :]
    hi = hn.astype(BF16)
    h2_ref[...] = hi
    lo = (hn - hi.astype(F32)).astype(BF16)
    lg_ref[...] = _dot(hi, rwh_ref[...]) + _dot(lo, rwh_ref[...]) + _dot(hi, rwl_ref[...])


def _merge_call(h, ym, yl, yr, x, mod, wm, bm, wb, wo, n2g, rwh, rwl):
    tm = 256
    row = lambda i: (i, 0)
    const2 = lambda i: (0, 0)
    return pl.pallas_call(
        _merge_kernel,
        grid=(ROWS // tm,),
        in_specs=[pl.BlockSpec((tm, D), row),
                  pl.BlockSpec((tm, D), row),
                  pl.BlockSpec((tm, D), row),
                  pl.BlockSpec((tm, D), row),
                  pl.BlockSpec((tm, D), row),
                  pl.BlockSpec((1, 6, D), lambda i: (i * tm // GROUP_ROWS, 0, 0)),
                  pl.BlockSpec((D, 3 * D), const2),
                  pl.BlockSpec((1, 3 * D), const2),
                  pl.BlockSpec((3, D, D), lambda i: (0, 0, 0)),
                  pl.BlockSpec((D, D), const2),
                  pl.BlockSpec((1, D), const2),
                  pl.BlockSpec((D, LANES), const2),
                  pl.BlockSpec((D, LANES), const2)],
        out_specs=[pl.BlockSpec((tm, D), row),
                   pl.BlockSpec((tm, D), row),
                   pl.BlockSpec((tm, LANES), row)],
        out_shape=[jax.ShapeDtypeStruct((ROWS, D), F32),
                   jax.ShapeDtypeStruct((ROWS, D), BF16),
                   jax.ShapeDtypeStruct((ROWS, LANES), F32)],
        compiler_params=_cparams(("arbitrary",), VMEM_LIMIT),
        name="merge",
    )(h, ym, yl, yr, x, mod, wm, bm.reshape(1, 3 * D), wb, wo, n2g.reshape(1, D), rwh, rwl)


def _expert_kernel(xs_ref, g_ref, w1_ref, w3_ref, w2_ref, y_ref, w1_sc, w3_sc, w2_sc):
    @pl.when(pl.program_id(1) == 0)
    def _():
        w1_sc[...] = w1_ref[0].astype(BF16)
        w3_sc[...] = w3_ref[0].astype(BF16)
        w2_sc[...] = w2_ref[0].astype(BF16)

    xs = xs_ref[0]
    a = _dot(xs, w1_sc[...])
    b = _dot(xs, w3_sc[...])
    mid = (a * jax.nn.sigmoid(a) * b).astype(BF16)
    y_ref[0] = _dot(mid, w2_sc[...]) * g_ref[0]


def _expert_call(xs, gv, w1, w3, w2):
    tm = 512
    wspec = pl.BlockSpec((1, D, D), lambda e, m: (e, 0, 0))
    return pl.pallas_call(
        _expert_kernel,
        grid=(N_EXPERTS, ROWS_PER_EXPERT // tm),
        in_specs=[pl.BlockSpec((1, tm, D), lambda e, m: (e, m, 0)),
                  pl.BlockSpec((1, tm, 1), lambda e, m: (e, m, 0)),
                  wspec, wspec, wspec],
        out_specs=pl.BlockSpec((1, tm, D), lambda e, m: (e, m, 0)),
        out_shape=jax.ShapeDtypeStruct((N_EXPERTS, ROWS_PER_EXPERT, D), F32),
        scratch_shapes=[pltpu.VMEM((D, D), BF16)] * 3,
        compiler_params=_cparams(("arbitrary", "arbitrary"), VMEM_LIMIT),
        name="experts",
    )(xs, gv, w1, w3, w2)


def _final_norm_kernel(x_ref, g_ref, o_ref):
    o_ref[...] = _rms(x_ref[...]) * g_ref[...]


def _final_norm_call(x, g):
    tm = 512
    return pl.pallas_call(
        _final_norm_kernel,
        grid=(ROWS // tm,),
        in_specs=[pl.BlockSpec((tm, D), lambda i: (i, 0)),
                  pl.BlockSpec((1, D), lambda i: (0, 0))],
        out_specs=pl.BlockSpec((tm, D), lambda i: (i, 0)),
        out_shape=jax.ShapeDtypeStruct((ROWS, D), F32),
        compiler_params=_cparams(("arbitrary",), VMEM_LIMIT),
        name="final_norm",
    )(x, g.reshape(1, D))


def _rope_tables():
    tpos = jnp.arange(T_SAMPLE)
    lane = jnp.arange(R_DH)
    pos = jnp.where(lane[None, :] < R_DH // 2, (tpos // GRID_W)[:, None], (tpos % GRID_W)[:, None])
    n_freq = R_DH // 4
    freqs = jnp.power(ROPE_BASE, -jnp.arange(n_freq, dtype=F32) / n_freq)
    ang = pos.astype(F32) * freqs[lane % n_freq][None, :]
    first = ((lane % (R_DH // 2)) < n_freq)[None, :]
    cos, sin = jnp.cos(ang), jnp.sin(ang)
    return cos, jnp.where(first, -sin, 0.0), jnp.where(first, 0.0, sin)


def _route(logits):
    aff = jax.nn.softmax(logits[:, :N_EXPERTS], axis=-1)
    ap = aff[:ROWS_PROMPT].reshape(N_PROMPT_SEQ, T_PROMPT, N_EXPERTS).swapaxes(1, 2)
    as_ = aff[ROWS_PROMPT:].reshape(N_SAMPLE_SEQ, T_SAMPLE, N_EXPERTS).swapaxes(1, 2)
    gp, ip = lax.top_k(ap, CAP_PROMPT)
    gs, is_ = lax.top_k(as_, CAP_SAMPLE)
    ip = ip + (jnp.arange(N_PROMPT_SEQ) * T_PROMPT)[:, None, None]
    is_ = is_ + ROWS_PROMPT + (jnp.arange(N_SAMPLE_SEQ) * T_SAMPLE)[:, None, None]
    rows = jnp.concatenate([ip.swapaxes(0, 1).reshape(N_EXPERTS, -1),
                            is_.swapaxes(0, 1).reshape(N_EXPERTS, -1)], axis=1)
    gv = jnp.concatenate([gp.swapaxes(0, 1).reshape(N_EXPERTS, -1),
                          gs.swapaxes(0, 1).reshape(N_EXPERTS, -1)], axis=1)
    return rows, gv


def _layer(x, mod, p, states, rope_tabs):
    w_in = p['w_in']
    w_main = jnp.concatenate([w_in[:, :4 * D], w_in[:, 4 * D + N_GATE_COLS:]], axis=1).astype(BF16)
    w_gate = jnp.pad(w_in[:, 4 * D:4 * D + N_GATE_COLS], ((0, 0), (0, LANES - N_GATE_COLS))).astype(BF16)

    h, gates = _norm1_call(x, p['norm1_g'], mod, w_gate)
    proj = _inproj_call(h, w_main)

    gb = gates[:, :N_GATE_COLS] + p['mlstm_gate_bias'].reshape(1, N_GATE_COLS)
    grow = gb.reshape(ROWS, 4, M_HEADS).transpose(2, 1, 0)
    grow = jnp.pad(grow, ((0, 0), (0, 4), (0, 0)))

    sm_c, sm_n, sm_m, s_lh, s_rs = states
    prompt_kw = dict(t=T_PROMPT, n_seq=N_PROMPT_SEQ, row_block0=0, emit_state=True)
    sample_kw = dict(t=T_SAMPLE, n_seq=N_SAMPLE_SEQ, row_block0=ROWS_PROMPT // T_SAMPLE, emit_state=False)

    ym_p, new_c, new_n, new_m = _mlstm_call(proj, grow, p['mlstm_norm_g'], None, **prompt_kw)
    (ym_s,) = _mlstm_call(proj, grow, p['mlstm_norm_g'],
                          (sm_c, sm_n.reshape(N_SAMPLE_SEQ, 2, M_HEADS, 1, M_DH),
                           jnp.broadcast_to(sm_m[..., None, None], (N_SAMPLE_SEQ, 2, M_HEADS, 1, LANES))),
                          **sample_kw)

    lru_p = dict(conv_w=p['lru_conv_w'], conv_b=p['lru_conv_b'], wr=p['lru_wr'], wi=p['lru_wi'],
                 br=p['lru_br'], bi=p['lru_bi'], lam=p['lru_lambda'])
    yl_p, new_lh = _lru_call(proj, lru_p, None, n_seg=8, **prompt_kw)
    (yl_s,) = _lru_call(proj, lru_p, s_lh, n_seg=32, **sample_kw)

    decay = jnp.broadcast_to(p['ret_decay'].T[:, :, None], (R_HEADS, 2, LANES))
    yr_p, new_rs = _ret_call(proj, decay, p['ret_norm_g'], None, None, **prompt_kw)
    (yr_s,) = _ret_call(proj, decay, p['ret_norm_g'], rope_tabs, s_rs, **sample_kw)

    ym = jnp.concatenate([ym_p, ym_s], axis=0)
    yl = jnp.concatenate([yl_p, yl_s], axis=0)
    yr = jnp.concatenate([yr_p, yr_s], axis=0)

    rw = jnp.pad(p['router_w'], ((0, 0), (0, LANES - N_EXPERTS)))
    rwh = rw.astype(BF16)
    rwl = (rw - rwh.astype(F32)).astype(BF16)
    xn, h2, logits = _merge_call(h, ym, yl, yr, x, mod, p['w_merge'].astype(BF16), p['b_merge'],
                                 p['w_branch'].astype(BF16), p['w_out'].astype(BF16), p['norm2_g'],
                                 rwh, rwl)

    rows, gv = _route(logits)
    xs = h2[rows]
    y = _expert_call(xs, gv[..., None], p['exp_w1'], p['exp_w3'], p['exp_w2'])
    ec = jnp.zeros((ROWS, D), F32).at[rows.reshape(-1)].add(y.reshape(-1, D))
    g2 = jnp.repeat(mod[:3, 5, :], GROUP_ROWS, axis=0)
    x_out = xn + g2 * ec
    new_states = (new_c, new_n[:, :, :, 0, :], new_m[:, :, :, 0, 0], new_lh, new_rs)
    return x_out, new_states


def kernel(x_prompt, x_sample, c, state_mlstm_C, state_mlstm_n, state_mlstm_m, state_lru_h, state_ret_S, c_ctx, w_ada, b_ada, norm1_g, norm2_g, w_in, mlstm_gate_bias, mlstm_norm_g, lru_conv_w, lru_conv_b, lru_wr, lru_br, lru_wi, lru_bi, lru_lambda, ret_decay, ret_norm_g, w_branch, w_merge, b_merge, w_out, router_w, exp_w1, exp_w3, exp_w2, final_g):
    x = jnp.concatenate([x_prompt.reshape(ROWS_PROMPT, D), x_sample.reshape(-1, D)], axis=0)
    cond8 = jnp.concatenate([c_ctx[None, :], c, jnp.zeros((8 - 1 - N_SAMPLE_SEQ, D), F32)], axis=0)
    mod_all = _ada_call(cond8, w_ada, b_ada).reshape(DEPTH, 8, 6, D)
    rope_tabs = _rope_tables()

    per_layer = []
    for l in range(DEPTH):
        p = dict(norm1_g=norm1_g[l], norm2_g=norm2_g[l], w_in=w_in[l],
                 mlstm_gate_bias=mlstm_gate_bias[l], mlstm_norm_g=mlstm_norm_g[l],
                 lru_conv_w=lru_conv_w[l], lru_conv_b=lru_conv_b[l], lru_wr=lru_wr[l],
                 lru_br=lru_br[l], lru_wi=lru_wi[l], lru_bi=lru_bi[l], lru_lambda=lru_lambda[l],
                 ret_decay=ret_decay[l], ret_norm_g=ret_norm_g[l], w_branch=w_branch[l],
                 w_merge=w_merge[l], b_merge=b_merge[l], w_out=w_out[l], router_w=router_w[l],
                 exp_w1=exp_w1[l], exp_w3=exp_w3[l], exp_w2=exp_w2[l])
        states = (state_mlstm_C[:, l], state_mlstm_n[:, l], state_mlstm_m[:, l],
                  state_lru_h[:, l], state_ret_S[:, l])
        x, new_states = _layer(x, mod_all[l], p, states, rope_tabs)
        per_layer.append(new_states)

    y = _final_norm_call(x, final_g)
    y_prompt = y[:ROWS_PROMPT].reshape(N_PROMPT_SEQ, T_PROMPT, D)
    y_sample = y[ROWS_PROMPT:].reshape(N_SAMPLE_SEQ, T_SAMPLE, D)
    stacked = [jnp.stack([s[i] for s in per_layer], axis=1) for i in range(5)]
    return (y_prompt, y_sample, *stacked)
```

```python
import functools

import jax
import jax.numpy as jnp
from jax import lax
from jax.experimental import pallas as pl
from jax.experimental.pallas import tpu as pltpu

F32 = jnp.float32
BF16 = jnp.bfloat16

D = 1024
DEPTH = 2
N_PROMPT_SEQ = 16
T_PROMPT = 256
N_SAMPLE_SEQ = 2
T_SAMPLE = 4096
ROWS_PROMPT = N_PROMPT_SEQ * T_PROMPT
ROWS = ROWS_PROMPT + N_SAMPLE_SEQ * T_SAMPLE
GROUP_ROWS = 4096
GRID_W = 64
CHUNK = 256
EPS = 1e-6
F32_TINY = 1.1754944e-38
M_HEADS = 4
M_DH = 256
R_HEADS = 8
R_DH = 128
L_BLOCKS = 8
L_BW = 128
L_C = 8.0
ROPE_BASE = 10000.0
N_EXPERTS = 16
CAP_PROMPT = 2 * T_PROMPT // N_EXPERTS
CAP_SAMPLE = 2 * T_SAMPLE // N_EXPERTS
ROWS_PER_EXPERT = N_PROMPT_SEQ * CAP_PROMPT + N_SAMPLE_SEQ * CAP_SAMPLE
N_GATE_COLS = 16
LANES = 128
SUBLANES = 8
VMEM_LIMIT = 56 * 2 ** 20

COL_MQ, COL_MV, COL_MO, COL_LX, COL_LZ, COL_RQ, COL_RV, COL_RG = (i * D for i in range(8))
D_MAIN = 8 * D
ROW_MK, ROW_RK = 0, D
D_KT = 2 * D


def _cparams(sem, vmem=None):
    return pltpu.CompilerParams(dimension_semantics=sem, vmem_limit_bytes=vmem)


def _dot(a, b):
    return jnp.dot(a, b, preferred_element_type=F32)


def _dot_nt(a, b):
    return lax.dot_general(a, b, (((1,), (1,)), ((), ())), preferred_element_type=F32)


def _split3(x):
    a = x.astype(BF16)
    r = x - a.astype(F32)
    b = r.astype(BF16)
    c = (r - b.astype(F32)).astype(BF16)
    return a, b, c


def _log_sigmoid(x):
    return jnp.minimum(x, 0.0) - jnp.log1p(jnp.exp(-jnp.abs(x)))


def _sigmoid(x):
    return 0.5 * jnp.tanh(0.5 * x) + 0.5


def _rms(x):
    return x * lax.rsqrt(jnp.mean(x * x, axis=-1, keepdims=True) + EPS)


def _chunk(c):
    return pl.ds(pl.multiple_of(c * CHUNK, CHUNK), CHUNK)


def _ada_kernel(c_ref, w_ref, b_ref, o_ref):
    c = c_ref[...]
    s = (c * jax.nn.sigmoid(c)).astype(BF16)
    o_ref[0] = _dot(s, w_ref[0].astype(BF16)) + b_ref[0]


def _ada_call(cond8, w_ada, b_ada):
    tn = 1536
    return pl.pallas_call(
        _ada_kernel,
        grid=(DEPTH, 6 * D // tn),
        in_specs=[pl.BlockSpec((8, D), lambda l, j: (0, 0)),
                  pl.BlockSpec((1, D, tn), lambda l, j: (l, 0, j)),
                  pl.BlockSpec((1, 1, tn), lambda l, j: (l, 0, j))],
        out_specs=pl.BlockSpec((1, 8, tn), lambda l, j: (l, 0, j)),
        out_shape=jax.ShapeDtypeStruct((DEPTH, 8, 6 * D), F32),
        compiler_params=_cparams(("arbitrary", "arbitrary"), VMEM_LIMIT),
        name="ada",
    )(cond8, w_ada, b_ada.reshape(DEPTH, 1, 6 * D))


def _norm1_kernel(x_ref, g_ref, mod_ref, wg_ref, h_ref, gate_ref):
    y = _rms(x_ref[...]) * g_ref[...]
    h = (y * (1.0 + mod_ref[0, 1:2, :]) + mod_ref[0, 0:1, :]).astype(BF16)
    h_ref[...] = h
    gate_ref[...] = _dot(h, wg_ref[...])


def _norm1_call(x, g, mod, w_gate):
    tm = 512
    return pl.pallas_call(
        _norm1_kernel,
        grid=(ROWS // tm,),
        in_specs=[pl.BlockSpec((tm, D), lambda i: (i, 0)),
                  pl.BlockSpec((1, D), lambda i: (0, 0)),
                  pl.BlockSpec((1, 6, D), lambda i: (i * tm // GROUP_ROWS, 0, 0)),
                  pl.BlockSpec((D, LANES), lambda i: (0, 0))],
        out_specs=[pl.BlockSpec((tm, D), lambda i: (i, 0)),
                   pl.BlockSpec((tm, LANES), lambda i: (i, 0))],
        out_shape=[jax.ShapeDtypeStruct((ROWS, D), BF16),
                   jax.ShapeDtypeStruct((ROWS, LANES), F32)],
        compiler_params=_cparams(("arbitrary",), VMEM_LIMIT),
        name="norm1",
    )(x, g.reshape(1, D), mod, w_gate)


def _mm_kernel(x_ref, w_ref, o_ref):
    o_ref[...] = _dot(x_ref[...], w_ref[...]).astype(o_ref.dtype)


def _inproj_call(h, w_main):
    tm, tn = 1024, 1024
    return pl.pallas_call(
        _mm_kernel,
        grid=(D_MAIN // tn, ROWS // tm),
        in_specs=[pl.BlockSpec((tm, D), lambda j, i: (i, 0)),
                  pl.BlockSpec((D, tn), lambda j, i: (0, j))],
        out_specs=pl.BlockSpec((tm, tn), lambda j, i: (i, j)),
        out_shape=jax.ShapeDtypeStruct((ROWS, D_MAIN), BF16),
        compiler_params=_cparams(("arbitrary", "arbitrary"), VMEM_LIMIT),
        name="inproj",
    )(h, w_main)


def _mm_t_kernel(wt_ref, x_ref, o_ref):
    o_ref[...] = _dot_nt(wt_ref[...], x_ref[...]).astype(o_ref.dtype)


def _inproj_t_call(h, w_kt):
    tm, tn = 1024, 1024
    return pl.pallas_call(
        _mm_t_kernel,
        grid=(D_KT // tn, ROWS // tm),
        in_specs=[pl.BlockSpec((tn, D), lambda j, i: (j, 0)),
                  pl.BlockSpec((tm, D), lambda j, i: (i, 0))],
        out_specs=pl.BlockSpec((tn, tm), lambda j, i: (j, i)),
        out_shape=jax.ShapeDtypeStruct((D_KT, ROWS), BF16),
        compiler_params=_cparams(("arbitrary", "arbitrary"), VMEM_LIMIT),
        name="inproj_t",
    )(w_kt, h)


M_AUG = M_DH + LANES


def _mlstm_kernel(*refs, n_chunks, has_init, emit_state):
    q_ref, kt_ref, v_ref, o_ref, g_ref, ng_ref = refs[:6]
    pos = 6
    if has_init:
        c0_ref, m0_ref = refs[pos:pos + 2]
        pos += 2
    y_ref = refs[pos]
    pos += 1
    if emit_state:
        co_ref, no_ref, mo_ref = refs[pos:pos + 3]
        pos += 3
    caug_sc, m_sc, hf_sc, hb_sc = refs[pos:]

    if has_init:
        caug_sc[...] = c0_ref[0, :, 0]
        m_sc[...] = m0_ref[0, :, 0]
    else:
        caug_sc[...] = jnp.zeros_like(caug_sc)
        m_sc[...] = jnp.zeros_like(m_sc)

    ri = lax.broadcasted_iota(jnp.int32, (CHUNK, CHUNK), 0)
    ci = lax.broadcasted_iota(jnp.int32, (CHUNK, CHUNK), 1)
    r8 = lax.broadcasted_iota(jnp.int32, (SUBLANES, CHUNK), 0)
    ones_b = jnp.ones((CHUNK, LANES), BF16)
    k_scale = M_DH ** -0.5
    allowed = [ri >= ci, ci >= ri]
    allowed_b = [jnp.where(a, 1.0, 0.0).astype(BF16) for a in allowed]
    tri = [jnp.where(ri <= ci, 1.0, 0.0).astype(BF16), jnp.where(ri >= ci, 1.0, 0.0).astype(BF16)]

    def step(d, c, m_t):
        rows = _chunk(c)
        i_idx, f_idx = (0, 1) if d == 0 else (2, 3)
        g8 = g_ref[0, :, rows]
        ls = jnp.where(r8 == f_idx, _log_sigmoid(g8), g8)
        cum = sum(_dot(p, tri[d]) for p in _split3(ls))
        f_row = ls[f_idx:f_idx + 1]
        b_row = cum[f_idx:f_idx + 1]
        g_row = ls[i_idx:i_idx + 1] - b_row
        gmat = jnp.where(allowed[d], jnp.broadcast_to(g_row, (CHUNK, CHUNK)), -jnp.inf)
        mp_col = jnp.max(gmat, axis=-1, keepdims=True)
        mp = jnp.broadcast_to(mp_col, (CHUNK, LANES))
        fp = _split3(f_row)
        bc = _dot(allowed_b[d] * fp[0], ones_b) + _dot(allowed_b[d] * fp[1], ones_b)

        q = q_ref[rows, :]
        kt = kt_ref[:, rows]
        vaug = jnp.concatenate([v_ref[rows, :], ones_b], axis=1)
        s1 = (_dot(q, kt) * k_scale) * jnp.exp(gmat - mp_col)
        sv1 = _dot(s1.astype(BF16), vaug)
        b_last_11 = b_row[:, CHUNK - 1:CHUNK] if d == 0 else b_row[:, 0:1]
        log_k = b_last_11 + g_row
        a_11 = jnp.max(log_k, axis=-1, keepdims=True)
        kw1 = (kt.astype(F32) * (jnp.exp(log_k - a_11) * k_scale)).astype(BF16)
        u1 = _dot(kw1, vaug)
        b_last = jnp.broadcast_to(b_last_11, (SUBLANES, LANES))
        a_t = jnp.broadcast_to(a_11, (SUBLANES, LANES))

        m_b = jnp.broadcast_to(m_t[0:1, :], (CHUNK, LANES))
        mi = jnp.maximum(m_b, mp)
        r = jnp.exp(mp - mi)
        w_prev = jnp.exp(m_b - mi)
        floor = jnp.exp(-(bc + mi))
        qc = _dot(q, caug_sc[d].astype(BF16))
        den = r * sv1[:, M_DH:] + w_prev * qc[:, M_DH:]
        inv = 1.0 / jnp.maximum(jnp.abs(den), floor)
        h_sc = hf_sc if d == 0 else hb_sc
        for blk in range(M_DH // LANES):
            cols = slice(blk * LANES, (blk + 1) * LANES)
            h_sc[rows, cols] = (r * sv1[:, cols] + w_prev * qc[:, cols]) * inv

        m_new = jnp.maximum(b_last + m_t, a_t)
        w_c = jnp.exp(b_last + m_t - m_new)[0:1, :]
        w_u = jnp.exp(a_t - m_new)[0:1, :]
        for blk in range(M_AUG // LANES):
            cols = slice(blk * LANES, (blk + 1) * LANES)
            caug_sc[d, :, cols] = w_c * caug_sc[d, :, cols] + w_u * u1[:, cols]
        return m_new

    def scan_body(c, carry):
        mf, mb = carry
        mf = step(0, c, mf)
        mb = step(1, n_chunks - 1 - c, mb)
        return mf, mb

    m_init = tuple(jnp.broadcast_to(m_sc[d], (SUBLANES, LANES)) for d in range(2))
    mf, mb = lax.fori_loop(0, n_chunks, scan_body, m_init)
    m_sc[0] = mf[0:1, :]
    m_sc[1] = mb[0:1, :]

    def out_body(c, carry):
        rows = _chunk(c)
        y = _rms(hf_sc[rows, :] + hb_sc[rows, :]) * ng_ref[...]
        y_ref[rows, :] = (y * _sigmoid(o_ref[rows, :].astype(F32))).astype(y_ref.dtype)
        return carry

    lax.fori_loop(0, n_chunks, out_body, 0)

    if emit_state:
        co_ref[0, :, 0] = caug_sc[:, :, 0:M_DH]
        for d in range(2):
            no_ref[0, d, 0] = caug_sc[d, :, M_DH:].T[0:1, :]
        mo_ref[0, :, 0] = m_sc[...]


def _mlstm_call(proj, proj_t, grow, norm_g, state, *, t, n_seq, row_block0, emit_state):
    has_init = state is not None

    def col(base):
        return lambda s, h: (row_block0 + s, base // M_DH + h)

    st_caug = pl.BlockSpec((1, 2, 1, M_DH, M_AUG), lambda s, h: (s, 0, h, 0, 0))
    st_c = pl.BlockSpec((1, 2, 1, M_DH, M_DH), lambda s, h: (s, 0, h, 0, 0))
    st_n = pl.BlockSpec((1, 2, 1, 1, M_DH), lambda s, h: (s, 0, h, 0, 0))
    st_m = pl.BlockSpec((1, 2, 1, 1, LANES), lambda s, h: (s, 0, h, 0, 0))
    in_specs = [pl.BlockSpec((t, M_DH), col(COL_MQ)),
                pl.BlockSpec((M_DH, t), lambda s, h: (ROW_MK // M_DH + h, row_block0 + s)),
                pl.BlockSpec((t, M_DH), col(COL_MV)),
                pl.BlockSpec((t, M_DH), col(COL_MO)),
                pl.BlockSpec((1, SUBLANES, t), lambda s, h: (h, 0, row_block0 + s)),
                pl.BlockSpec((1, M_DH), lambda s, h: (0, h))]
    args = [proj, proj_t, proj, proj, grow, norm_g.reshape(1, D)]
    if has_init:
        in_specs += [st_caug, st_m]
        args += list(state)
    out_specs = [pl.BlockSpec((t, M_DH), lambda s, h: (s, h))]
    out_shape = [jax.ShapeDtypeStruct((n_seq * t, D), BF16)]
    if emit_state:
        out_specs += [st_c, st_n, st_m]
        out_shape += [jax.ShapeDtypeStruct((n_seq, 2, M_HEADS, M_DH, M_DH), F32),
                      jax.ShapeDtypeStruct((n_seq, 2, M_HEADS, 1, M_DH), F32),
                      jax.ShapeDtypeStruct((n_seq, 2, M_HEADS, 1, LANES), F32)]
    return pl.pallas_call(
        functools.partial(_mlstm_kernel, n_chunks=t // CHUNK, has_init=has_init,
                          emit_state=emit_state),
        grid=(n_seq, M_HEADS),
        in_specs=in_specs,
        out_specs=out_specs,
        out_shape=out_shape,
        scratch_shapes=[pltpu.VMEM((2, M_DH, M_AUG), F32),
                        pltpu.VMEM((2, 1, LANES), F32),
                        pltpu.VMEM((t, M_DH), F32),
                        pltpu.VMEM((t, M_DH), F32)],
        compiler_params=_cparams(("arbitrary", "arbitrary"), VMEM_LIMIT),
        name="mlstm",
    )(*args)


def _ret_kernel(*refs, n_chunks, rope, has_init, emit_state):
    q_ref, kt_ref, v_ref, g_ref, dec_ref, ng_ref = refs[:6]
    pos = 6
    if rope:
        cos_ref, sin_ref, cos_t_ref, sin_t_ref = refs[pos:pos + 4]
        pos += 4
    if has_init:
        s0_ref = refs[pos]
        pos += 1
    y_ref = refs[pos]
    pos += 1
    if emit_state:
        so_ref = refs[pos]
        pos += 1
    s_sc, of_sc, ob_sc, ktr_sc = refs[pos:pos + 4]
    qr_sc = refs[pos + 4] if rope else None

    if has_init:
        s_sc[...] = s0_ref[0, :, 0]
    else:
        s_sc[...] = jnp.zeros_like(s_sc)

    ri = lax.broadcasted_iota(jnp.int32, (CHUNK, CHUNK), 0)
    ci = lax.broadcasted_iota(jnp.int32, (CHUNK, CHUNK), 1)
    quarter = R_DH // 4
    if rope:
        fr = lax.broadcasted_iota(jnp.int32, (R_DH, R_DH), 0)
        fc = lax.broadcasted_iota(jnp.int32, (R_DH, R_DH), 1)
        partner = fc + jnp.where((fc % (2 * quarter)) < quarter, quarter, -quarter)
        perm_b = jnp.where(fr == partner, 1.0, 0.0).astype(BF16)

    def prep_body(c, carry):
        rows = _chunk(c)
        ktf = kt_ref[:, rows].astype(F32) * (R_DH ** -0.5)
        if rope:
            swapped = jnp.concatenate([ktf[quarter:2 * quarter], ktf[0:quarter],
                                       ktf[3 * quarter:], ktf[2 * quarter:3 * quarter]], axis=0)
            ktf = ktf * cos_t_ref[:, rows] + swapped * sin_t_ref[:, rows]
            q = q_ref[rows, :]
            qr = q.astype(F32) * cos_ref[rows, :] + _dot(q, perm_b) * sin_ref[rows, :]
            qr_sc[rows, :] = qr.astype(BF16)
        ktr_sc[:, rows] = ktf.astype(BF16)
        return carry

    lax.fori_loop(0, n_chunks, prep_body, 0)
    qsrc = qr_sc if rope else q_ref

    lane = lax.broadcasted_iota(jnp.int32, (1, CHUNK), 1).astype(F32)
    row_f = lax.broadcasted_iota(jnp.int32, (CHUNK, R_DH), 0).astype(F32)
    consts = []
    for d in range(2):
        lg = -jnp.exp(dec_ref[0, d:d + 1, :])
        lg11 = lg[:, 0:1]
        diff = (ri - ci) if d == 0 else (ci - ri)
        causal = diff >= 0
        dmat = jnp.where(causal, jnp.exp(lg11 * jnp.where(causal, diff, 0).astype(F32)), 0.0)
        if d == 0:
            q_dec = jnp.exp(lg * (row_f + 1.0))
            k_dec = jnp.exp(lg11 * (CHUNK - 1.0 - lane))
        else:
            q_dec = jnp.exp(lg * (CHUNK - row_f))
            k_dec = jnp.exp(lg11 * lane)
        g_chunk = jnp.exp(lg * float(CHUNK))
        consts.append((dmat, q_dec, k_dec, g_chunk))

    def step(d, c):
        dmat, q_dec, k_dec, g_chunk = consts[d]
        rows = _chunk(c)
        q = qsrc[rows, :]
        kt = ktr_sc[:, rows]
        v = v_ref[rows, :]
        s = _dot(q, kt) * dmat
        o = _dot(s.astype(BF16), v) + q_dec * _dot(q, s_sc[d].astype(BF16))
        if d == 0:
            of_sc[rows, :] = o
        else:
            ob_sc[rows, :] = o
        kd = (kt.astype(F32) * k_dec).astype(BF16)
        s_sc[d] = g_chunk * s_sc[d] + _dot(kd, v)

    def scan_body(c, carry):
        step(0, c)
        step(1, n_chunks - 1 - c)
        return carry

    lax.fori_loop(0, n_chunks, scan_body, 0)

    def out_body(c, carry):
        rows = _chunk(c)
        y = _rms(of_sc[rows, :] + ob_sc[rows, :]) * ng_ref[...]
        gate = g_ref[rows, :].astype(F32)
        y_ref[rows, :] = (y * (gate * _sigmoid(gate))).astype(y_ref.dtype)
        return carry

    lax.fori_loop(0, n_chunks, out_body, 0)

    if emit_state:
        so_ref[0, :, 0] = s_sc[...]


def _ret_call(proj, proj_t, decay, norm_g, rope_tabs, state, *, t, n_seq, row_block0, emit_state):
    has_init = state is not None
    rope = rope_tabs is not None

    def col(base):
        return lambda s, h: (row_block0 + s, base // R_DH + h)

    st_s = pl.BlockSpec((1, 2, 1, R_DH, R_DH), lambda s, h: (s, 0, h, 0, 0))
    in_specs = [pl.BlockSpec((t, R_DH), col(COL_RQ)),
                pl.BlockSpec((R_DH, t), lambda s, h: (ROW_RK // R_DH + h, row_block0 + s)),
                pl.BlockSpec((t, R_DH), col(COL_RV)),
                pl.BlockSpec((t, R_DH), col(COL_RG)),
                pl.BlockSpec((1, 2, LANES), lambda s, h: (h, 0, 0)),
                pl.BlockSpec((1, R_DH), lambda s, h: (0, h))]
    args = [proj, proj_t, proj, proj, decay, norm_g.reshape(1, D)]
    if rope:
        in_specs += [pl.BlockSpec((t, R_DH), lambda s, h: (0, 0))] * 2
        in_specs += [pl.BlockSpec((R_DH, t), lambda s, h: (0, 0))] * 2
        args += list(rope_tabs)
    if has_init:
        in_specs.append(st_s)
        args.append(state)
    out_specs = [pl.BlockSpec((t, R_DH), lambda s, h: (s, h))]
    out_shape = [jax.ShapeDtypeStruct((n_seq * t, D), BF16)]
    if emit_state:
        out_specs.append(st_s)
        out_shape.append(jax.ShapeDtypeStruct((n_seq, 2, R_HEADS, R_DH, R_DH), F32))
    scratch = [pltpu.VMEM((2, R_DH, R_DH), F32),
               pltpu.VMEM((t, R_DH), F32),
               pltpu.VMEM((t, R_DH), F32),
               pltpu.VMEM((R_DH, t), BF16)]
    if rope:
        scratch.append(pltpu.VMEM((t, R_DH), BF16))
    return pl.pallas_call(
        functools.partial(_ret_kernel, n_chunks=t // CHUNK, rope=rope, has_init=has_init,
                          emit_state=emit_state),
        grid=(n_seq, R_HEADS),
        in_specs=in_specs,
        out_specs=out_specs,
        out_shape=out_shape,
        scratch_shapes=scratch,
        compiler_params=_cparams(("arbitrary", "arbitrary"), VMEM_LIMIT),
        name="retention",
    )(*args)


LRU_SLAB = 64


def _tile_scan(a, b, reverse):
    row = lax.broadcasted_iota(jnp.int32, (SUBLANES, L_BW), 0)
    for k in (1, 2, 4):
        if reverse:
            keep = row < SUBLANES - k
            shift = SUBLANES - k
        else:
            keep = row >= k
            shift = k
        a_sh = jnp.where(keep, pltpu.roll(a, shift, 0), 1.0)
        b_sh = jnp.where(keep, pltpu.roll(b, shift, 0), 0.0)
        b = a * b_sh + b
        a = a * a_sh
    return a, b


def _lru_kernel(*refs, t, has_init, emit_state):
    lx_ref, lz_ref, cw_ref, cb_ref, wr_ref, wi_ref, br_ref, bi_ref, lam_ref = refs[:9]
    pos = 9
    if has_init:
        h0_ref = refs[pos]
        pos += 1
    y_ref = refs[pos]
    pos += 1
    if emit_state:
        hfin_ref = refs[pos]
        pos += 1
    xpad_sc, a_sc, b_sc = refs[pos:]
    rc = min(t, 256)

    xpad_sc[0:8, :] = jnp.zeros((8, L_BW), F32)
    xpad_sc[t + 8:t + 16, :] = jnp.zeros((8, L_BW), F32)

    def pad_body(c, carry):
        r0 = pl.multiple_of(c * rc, rc)
        xpad_sc[pl.ds(r0 + 8, rc), :] = lx_ref[pl.ds(r0, rc), :].astype(F32)
        return carry

    lax.fori_loop(0, t // rc, pad_body, 0)

    wr = [wr_ref[d, 0].astype(BF16) for d in range(2)]
    wi = [wi_ref[d, 0].astype(BF16) for d in range(2)]
    neg_c = []
    for d in range(2):
        lam = lam_ref[d:d + 1, :]
        softplus_neg = jnp.maximum(-lam, 0.0) + jnp.log1p(jnp.exp(-jnp.abs(lam)))
        neg_c.append(-L_C * softplus_neg)

    def gate_body(c, carry):
        r0 = pl.multiple_of(c * rc, rc)
        xe = xpad_sc[pl.ds(r0, rc + 16), :]
        n = rc + 16
        u = (cb_ref[...] + cw_ref[0:1, :] * pltpu.roll(xe, 2, 0)[8:8 + rc]
             + cw_ref[1:2, :] * pltpu.roll(xe, 1, 0)[8:8 + rc]
             + cw_ref[2:3, :] * xe[8:8 + rc]
             + cw_ref[3:4, :] * pltpu.roll(xe, n - 1, 0)[8:8 + rc])
        ub = u.astype(BF16)
        for d in range(2):
            r = _sigmoid(_dot(ub, wr[d]) + br_ref[d:d + 1, :])
            i = _sigmoid(_dot(ub, wi[d]) + bi_ref[d:d + 1, :])
            log_a = neg_c[d] * r
            a = jnp.exp(log_a)
            z = -jnp.tanh(log_a) * (a * a + 1.0)
            mult = z * lax.rsqrt(jnp.maximum(z, F32_TINY))
            a_sc[d, pl.ds(r0, rc), :] = a
            b_sc[d, pl.ds(r0, rc), :] = mult * (i * u)
        return carry

    lax.fori_loop(0, t // rc, gate_body, 0)

    n_slabs = t // LRU_SLAB
    tiles = LRU_SLAB // SUBLANES

    row8 = lax.broadcasted_iota(jnp.int32, (SUBLANES, L_BW), 0)

    def bcast_row(x, r):
        return jnp.broadcast_to(x[r:r + 1, :], (SUBLANES, L_BW))

    def slab_scan(d, r0, c_in):
        reverse = d == 1
        edge = 0 if reverse else SUBLANES - 1
        local = []
        spa = jnp.ones((SUBLANES, L_BW), F32)
        shl = jnp.zeros((SUBLANES, L_BW), F32)
        for k in range(tiles):
            rows = pl.ds(r0 + k * SUBLANES, SUBLANES)
            pa, hl = _tile_scan(a_sc[d, rows, :], b_sc[d, rows, :], reverse)
            local.append((rows, pa, hl))
            spa = jnp.where(row8 == k, bcast_row(pa, edge), spa)
            shl = jnp.where(row8 == k, bcast_row(hl, edge), shl)
        cpa, chl = _tile_scan(spa, shl, reverse)
        after = chl + cpa * c_in
        for k, (rows, pa, hl) in enumerate(local):
            prev = k + 1 if reverse else k - 1
            cin = c_in if (prev < 0 or prev >= tiles) else bcast_row(after, prev)
            b_sc[d, rows, :] = hl + pa * cin
        return bcast_row(after, 0 if reverse else tiles - 1)

    def scan_body(c, carry):
        cf, cbk = carry
        cf = slab_scan(0, pl.multiple_of(c * LRU_SLAB, LRU_SLAB), cf)
        cbk = slab_scan(1, pl.multiple_of((n_slabs - 1 - c) * LRU_SLAB, LRU_SLAB), cbk)
        return cf, cbk

    if has_init:
        cf0 = jnp.broadcast_to(h0_ref[0, 0:1, :], (SUBLANES, L_BW))
        cb0 = jnp.broadcast_to(h0_ref[0, 1:2, :], (SUBLANES, L_BW))
    else:
        cf0 = jnp.zeros((SUBLANES, L_BW), F32)
        cb0 = jnp.zeros((SUBLANES, L_BW), F32)
    cf, cbk = lax.fori_loop(0, n_slabs, scan_body, (cf0, cb0))
    if emit_state:
        hfin_ref[0, 0:1, :] = cf[0:1, :]
        hfin_ref[0, 1:2, :] = cbk[0:1, :]

    def out_body(c, carry):
        rows = pl.ds(pl.multiple_of(c * rc, rc), rc)
        z = lz_ref[rows, :].astype(F32)
        gelu = 0.5 * z * (1.0 + jnp.tanh(0.7978845608028654 * (z + 0.044715 * (z * z * z))))
        y_ref[rows, :] = ((b_sc[0, rows, :] + b_sc[1, rows, :]) * gelu).astype(y_ref.dtype)
        return carry

    lax.fori_loop(0, t // rc, out_body, 0)


def _lru_call(proj, p, state, *, t, n_seq, row_block0, emit_state):
    has_init = state is not None

    def col(base):
        return lambda s, b: (row_block0 + s, base // L_BW + b)

    vec2 = pl.BlockSpec((2, L_BW), lambda s, b: (0, b))
    st = pl.BlockSpec((1, 2, L_BW), lambda s, b: (s, 0, b))
    in_specs = [pl.BlockSpec((t, L_BW), col(COL_LX)),
                pl.BlockSpec((t, L_BW), col(COL_LZ)),
                pl.BlockSpec((4, L_BW), lambda s, b: (0, b)),
                pl.BlockSpec((1, L_BW), lambda s, b: (0, b)),
                pl.BlockSpec((2, 1, L_BW, L_BW), lambda s, b: (0, b, 0, 0)),
                pl.BlockSpec((2, 1, L_BW, L_BW), lambda s, b: (0, b, 0, 0)),
                vec2, vec2, vec2]
    args = [proj, proj, p['conv_w'], p['conv_b'].reshape(1, D), p['wr'], p['wi'],
            p['br'], p['bi'], p['lam']]
    if has_init:
        in_specs.append(st)
        args.append(state)
    out_specs = [pl.BlockSpec((t, L_BW), lambda s, b: (s, b))]
    out_shape = [jax.ShapeDtypeStruct((n_seq * t, D), BF16)]
    if emit_state:
        out_specs.append(st)
        out_shape.append(jax.ShapeDtypeStruct((n_seq, 2, D), F32))
    return pl.pallas_call(
        functools.partial(_lru_kernel, t=t, has_init=has_init, emit_state=emit_state),
        grid=(n_seq, L_BLOCKS),
        in_specs=in_specs,
        out_specs=out_specs,
        out_shape=out_shape,
        scratch_shapes=[pltpu.VMEM((t + 16, L_BW), F32),
                        pltpu.VMEM((2, t, L_BW), F32),
                        pltpu.VMEM((2, t, L_BW), F32)],
        compiler_params=_cparams(("arbitrary", "arbitrary"), VMEM_LIMIT),
        name="rglru",
    )(*args)


def _merge_kernel(h_ref, ym_ref, yl_ref, yr_ref, x_ref, mod_ref, wm_ref, bm_ref, wb_ref, wo_ref,
                  n2_ref, rwh_ref, rwl_ref, xo_ref, h2_ref, lg_ref):
    h = h_ref[...]
    merged = None
    for k, y_ref in enumerate((ym_ref, yl_ref, yr_ref)):
        gate = jax.nn.sigmoid(_dot(h, wm_ref[:, k * D:(k + 1) * D]) + bm_ref[:, k * D:(k + 1) * D])
        term = gate * _dot(y_ref[...], wb_ref[k])
        merged = term if merged is None else merged + term
    mix = _dot(merged.astype(BF16), wo_ref[...])
    xn = x_ref[...] + mod_ref[0, 2:3, :] * mix
    xo_ref[...] = xn
    hn = _rms(xn) * n2_ref[...] * (1.0 + mod_ref[0, 4:5, :]) + mod_ref[0, 3:4, :]
    hi = hn.astype(BF16)
    h2_ref[...] = hi
    lo = (hn - hi.astype(F32)).astype(BF16)
    lg_ref[...] = _dot(hi, rwh_ref[...]) + _dot(lo, rwh_ref[...]) + _dot(hi, rwl_ref[...])


def _merge_call(h, ym, yl, yr, x, mod, wm, bm, wb, wo, n2g, rwh, rwl):
    tm = 256
    row = lambda i: (i, 0)
    const2 = lambda i: (0, 0)
    return pl.pallas_call(
        _merge_kernel,
        grid=(ROWS // tm,),
        in_specs=[pl.BlockSpec((tm, D), row),
                  pl.BlockSpec((tm, D), row),
                  pl.BlockSpec((tm, D), row),
                  pl.BlockSpec((tm, D), row),
                  pl.BlockSpec((tm, D), row),
                  pl.BlockSpec((1, 6, D), lambda i: (i * tm // GROUP_ROWS, 0, 0)),
                  pl.BlockSpec((D, 3 * D), const2),
                  pl.BlockSpec((1, 3 * D), const2),
                  pl.BlockSpec((3, D, D), lambda i: (0, 0, 0)),
                  pl.BlockSpec((D, D), const2),
                  pl.BlockSpec((1, D), const2),
                  pl.BlockSpec((D, LANES), const2),
                  pl.BlockSpec((D, LANES), const2)],
        out_specs=[pl.BlockSpec((tm, D), row),
                   pl.BlockSpec((tm, D), row),
                   pl.BlockSpec((tm, LANES), row)],
        out_shape=[jax.ShapeDtypeStruct((ROWS, D), F32),
                   jax.ShapeDtypeStruct((ROWS, D), BF16),
                   jax.ShapeDtypeStruct((ROWS, LANES), F32)],
        compiler_params=_cparams(("arbitrary",), VMEM_LIMIT),
        name="merge",
    )(h, ym, yl, yr, x, mod, wm, bm.reshape(1, 3 * D), wb, wo, n2g.reshape(1, D), rwh, rwl)


def _expert_kernel(xs_ref, g_ref, w1_ref, w3_ref, w2_ref, y_ref, w1_sc, w3_sc, w2_sc):
    @pl.when(pl.program_id(1) == 0)
    def _():
        w1_sc[...] = w1_ref[0, 0].astype(BF16)
        w3_sc[...] = w3_ref[0, 0].astype(BF16)
        w2_sc[...] = w2_ref[0, 0].astype(BF16)

    xs = xs_ref[0]
    a = _dot(xs, w1_sc[...])
    b = _dot(xs, w3_sc[...])
    mid = (a * jax.nn.sigmoid(a) * b).astype(BF16)
    y_ref[0] = _dot(mid, w2_sc[...]) * g_ref[0]


def _expert_call(xs, gv, w1, w3, w2, layer):
    tm = 512
    wspec = pl.BlockSpec((1, 1, D, D), lambda e, m: (layer, e, 0, 0))
    return pl.pallas_call(
        _expert_kernel,
        grid=(N_EXPERTS, ROWS_PER_EXPERT // tm),
        in_specs=[pl.BlockSpec((1, tm, D), lambda e, m: (e, m, 0)),
                  pl.BlockSpec((1, tm, 1), lambda e, m: (e, m, 0)),
                  wspec, wspec, wspec],
        out_specs=pl.BlockSpec((1, tm, D), lambda e, m: (e, m, 0)),
        out_shape=jax.ShapeDtypeStruct((N_EXPERTS, ROWS_PER_EXPERT, D), F32),
        scratch_shapes=[pltpu.VMEM((D, D), BF16)] * 3,
        compiler_params=_cparams(("arbitrary", "arbitrary"), VMEM_LIMIT),
        name="experts",
    )(xs, gv, w1, w3, w2)


def _final_norm_kernel(x_ref, g_ref, o_ref):
    o_ref[...] = _rms(x_ref[...]) * g_ref[...]


def _final_norm_call(x, g):
    tm = 512
    return pl.pallas_call(
        _final_norm_kernel,
        grid=(ROWS // tm,),
        in_specs=[pl.BlockSpec((tm, D), lambda i: (i, 0)),
                  pl.BlockSpec((1, D), lambda i: (0, 0))],
        out_specs=pl.BlockSpec((tm, D), lambda i: (i, 0)),
        out_shape=jax.ShapeDtypeStruct((ROWS, D), F32),
        compiler_params=_cparams(("arbitrary",), VMEM_LIMIT),
        name="final_norm",
    )(x, g.reshape(1, D))


def _rope_tables():
    tpos = jnp.arange(T_SAMPLE)
    lane = jnp.arange(R_DH)
    pos = jnp.where(lane[None, :] < R_DH // 2, (tpos // GRID_W)[:, None], (tpos % GRID_W)[:, None])
    n_freq = R_DH // 4
    freqs = jnp.power(ROPE_BASE, -jnp.arange(n_freq, dtype=F32) / n_freq)
    ang = pos.astype(F32) * freqs[lane % n_freq][None, :]
    first = ((lane % (R_DH // 2)) < n_freq)[None, :]
    cos, sin = jnp.cos(ang), jnp.sin(ang)
    sin = jnp.where(first, -sin, sin)
    return cos, sin, cos.T, sin.T


def _route(logits):
    aff = jax.nn.softmax(logits[:, :N_EXPERTS], axis=-1)
    ap = aff[:ROWS_PROMPT].reshape(N_PROMPT_SEQ, T_PROMPT, N_EXPERTS).swapaxes(1, 2)
    as_ = aff[ROWS_PROMPT:].reshape(N_SAMPLE_SEQ, T_SAMPLE, N_EXPERTS).swapaxes(1, 2)
    gp, ip = lax.top_k(ap, CAP_PROMPT)
    gs, is_ = lax.top_k(as_, CAP_SAMPLE)
    ip = ip + (jnp.arange(N_PROMPT_SEQ) * T_PROMPT)[:, None, None]
    is_ = is_ + ROWS_PROMPT + (jnp.arange(N_SAMPLE_SEQ) * T_SAMPLE)[:, None, None]
    rows = jnp.concatenate([ip.swapaxes(0, 1).reshape(N_EXPERTS, -1),
                            is_.swapaxes(0, 1).reshape(N_EXPERTS, -1)], axis=1)
    gv = jnp.concatenate([gp.swapaxes(0, 1).reshape(N_EXPERTS, -1),
                          gs.swapaxes(0, 1).reshape(N_EXPERTS, -1)], axis=1)
    return rows, gv


def _layer(x, mod, p, states, rope_tabs, stacked, layer):
    w_in = p['w_in']
    mq, mk, mv, mo, mg, lx, lz, rq, rk, rv, rg = jnp.split(
        w_in, [1024, 2048, 3072, 4096, 4112, 5136, 6160, 7184, 8208, 9232], axis=1)
    w_main = jnp.concatenate([mq, mv, mo, lx, lz, rq, rv, rg], axis=1).astype(BF16)
    w_kt = jnp.concatenate([mk, rk], axis=1).T.astype(BF16)
    w_gate = jnp.pad(mg, ((0, 0), (0, LANES - N_GATE_COLS))).astype(BF16)

    h, gates = _norm1_call(x, p['norm1_g'], mod, w_gate)
    proj = _inproj_call(h, w_main)
    proj_t = _inproj_t_call(h, w_kt)

    gb = gates[:, :N_GATE_COLS] + p['mlstm_gate_bias'].reshape(1, N_GATE_COLS)
    grow = gb.reshape(ROWS, 4, M_HEADS).transpose(2, 1, 0)
    grow = jnp.pad(grow, ((0, 0), (0, SUBLANES - 4), (0, 0)))

    sm_c, sm_n, sm_m, s_lh, s_rs = states
    prompt_kw = dict(t=T_PROMPT, n_seq=N_PROMPT_SEQ, row_block0=0, emit_state=True)
    sample_kw = dict(t=T_SAMPLE, n_seq=N_SAMPLE_SEQ, row_block0=ROWS_PROMPT // T_SAMPLE, emit_state=False)

    caug0 = jnp.concatenate(
        [sm_c, jnp.broadcast_to(sm_n[..., None], sm_n.shape + (LANES,))], axis=-1)
    m0 = jnp.broadcast_to(sm_m[..., None, None], (N_SAMPLE_SEQ, 2, M_HEADS, 1, LANES))
    ym_p, new_c, new_n, new_m = _mlstm_call(proj, proj_t, grow, p['mlstm_norm_g'], None, **prompt_kw)
    (ym_s,) = _mlstm_call(proj, proj_t, grow, p['mlstm_norm_g'], (caug0, m0), **sample_kw)

    lru_p = dict(conv_w=p['lru_conv_w'], conv_b=p['lru_conv_b'], wr=p['lru_wr'], wi=p['lru_wi'],
                 br=p['lru_br'], bi=p['lru_bi'], lam=p['lru_lambda'])
    yl_p, new_lh = _lru_call(proj, lru_p, None, **prompt_kw)
    (yl_s,) = _lru_call(proj, lru_p, s_lh, **sample_kw)

    decay = jnp.broadcast_to(p['ret_decay'].T[:, :, None], (R_HEADS, 2, LANES))
    yr_p, new_rs = _ret_call(proj, proj_t, decay, p['ret_norm_g'], None, None, **prompt_kw)
    (yr_s,) = _ret_call(proj, proj_t, decay, p['ret_norm_g'], rope_tabs, s_rs, **sample_kw)

    ym = jnp.concatenate([ym_p, ym_s], axis=0)
    yl = jnp.concatenate([yl_p, yl_s], axis=0)
    yr = jnp.concatenate([yr_p, yr_s], axis=0)

    rw = jnp.pad(p['router_w'], ((0, 0), (0, LANES - N_EXPERTS)))
    rwh = rw.astype(BF16)
    rwl = (rw - rwh.astype(F32)).astype(BF16)
    xn, h2, logits = _merge_call(h, ym, yl, yr, x, mod, p['w_merge'].astype(BF16), p['b_merge'],
                                 p['w_branch'].astype(BF16), p['w_out'].astype(BF16), p['norm2_g'],
                                 rwh, rwl)

    rows, gv = _route(logits)
    xs = h2[rows]
    y = _expert_call(xs, gv[..., None], stacked['exp_w1'], stacked['exp_w3'], stacked['exp_w2'], layer)
    ec = jnp.zeros((ROWS, D), F32).at[rows.reshape(-1)].add(y.reshape(-1, D))
    g2 = jnp.repeat(mod[:3, 5, :], GROUP_ROWS, axis=0)
    x_out = xn + g2 * ec
    new_states = (new_c, new_n[:, :, :, 0, :], new_m[:, :, :, 0, 0], new_lh, new_rs)
    return x_out, new_states


def kernel(x_prompt, x_sample, c, state_mlstm_C, state_mlstm_n, state_mlstm_m, state_lru_h, state_ret_S, c_ctx, w_ada, b_ada, norm1_g, norm2_g, w_in, mlstm_gate_bias, mlstm_norm_g, lru_conv_w, lru_conv_b, lru_wr, lru_br, lru_wi, lru_bi, lru_lambda, ret_decay, ret_norm_g, w_branch, w_merge, b_merge, w_out, router_w, exp_w1, exp_w3, exp_w2, final_g):
    x = jnp.concatenate([x_prompt.reshape(ROWS_PROMPT, D), x_sample.reshape(-1, D)], axis=0)
    cond8 = jnp.concatenate([c_ctx[None, :], c, jnp.zeros((8 - 1 - N_SAMPLE_SEQ, D), F32)], axis=0)
    mod_all = _ada_call(cond8, w_ada, b_ada).reshape(DEPTH, 8, 6, D)
    rope_tabs = _rope_tables()
    stacked = dict(exp_w1=exp_w1, exp_w3=exp_w3, exp_w2=exp_w2)

    per_layer = []
    for l in range(DEPTH):
        p = dict(norm1_g=norm1_g[l], norm2_g=norm2_g[l], w_in=w_in[l],
                 mlstm_gate_bias=mlstm_gate_bias[l], mlstm_norm_g=mlstm_norm_g[l],
                 lru_conv_w=lru_conv_w[l], lru_conv_b=lru_conv_b[l], lru_wr=lru_wr[l],
                 lru_br=lru_br[l], lru_wi=lru_wi[l], lru_bi=lru_bi[l], lru_lambda=lru_lambda[l],
                 ret_decay=ret_decay[l], ret_norm_g=ret_norm_g[l], w_branch=w_branch[l],
                 w_merge=w_merge[l], b_merge=b_merge[l], w_out=w_out[l], router_w=router_w[l])
        states = (state_mlstm_C[:, l], state_mlstm_n[:, l], state_mlstm_m[:, l],
                  state_lru_h[:, l], state_ret_S[:, l])
        x, new_states = _layer(x, mod_all[l], p, states, rope_tabs, stacked, l)
        per_layer.append(new_states)

    y = _final_norm_call(x, final_g)
    y_prompt = y[:ROWS_PROMPT].reshape(N_PROMPT_SEQ, T_PROMPT, D)
    y_sample = y[ROWS_PROMPT:].reshape(N_SAMPLE_SEQ, T_SAMPLE, D)
    outs = [jnp.stack([s[i] for s in per_layer], axis=1) for i in range(5)]
    return (y_prompt, y_sample, *outs)
```

```python
import functools

import jax
import jax.numpy as jnp
from jax import lax
from jax.experimental import pallas as pl
from jax.experimental.pallas import tpu as pltpu

F32 = jnp.float32
BF16 = jnp.bfloat16

D = 1024
DEPTH = 2
N_PROMPT_SEQ = 16
T_PROMPT = 256
N_SAMPLE_SEQ = 2
T_SAMPLE = 4096
ROWS_PROMPT = N_PROMPT_SEQ * T_PROMPT
ROWS = ROWS_PROMPT + N_SAMPLE_SEQ * T_SAMPLE
GROUP_ROWS = 4096
GRID_W = 64
CHUNK = 256
EPS = 1e-6
F32_TINY = 1.1754944e-38
M_HEADS = 4
M_DH = 256
R_HEADS = 8
R_DH = 128
L_BLOCKS = 8
L_BW = 128
L_C = 8.0
ROPE_BASE = 10000.0
N_EXPERTS = 16
CAP_PROMPT = 2 * T_PROMPT // N_EXPERTS
CAP_SAMPLE = 2 * T_SAMPLE // N_EXPERTS
ROWS_PER_EXPERT = N_PROMPT_SEQ * CAP_PROMPT + N_SAMPLE_SEQ * CAP_SAMPLE
N_GATE_COLS = 16
LANES = 128
SUBLANES = 8
VMEM_LIMIT = 56 * 2 ** 20

COL_MQ, COL_MV, COL_MO, COL_LX, COL_LZ, COL_RQ, COL_RV, COL_RG = (i * D for i in range(8))
D_MAIN = 8 * D
ROW_MK, ROW_RK = 0, D
D_KT = 2 * D


def _cparams(sem, vmem=None):
    return pltpu.CompilerParams(dimension_semantics=sem, vmem_limit_bytes=vmem)


def _dot(a, b):
    return jnp.dot(a, b, preferred_element_type=F32)


def _dot_nt(a, b):
    return lax.dot_general(a, b, (((1,), (1,)), ((), ())), preferred_element_type=F32)


def _split3(x):
    a = x.astype(BF16)
    r = x - a.astype(F32)
    b = r.astype(BF16)
    c = (r - b.astype(F32)).astype(BF16)
    return a, b, c


def _log_sigmoid(x):
    return jnp.minimum(x, 0.0) - jnp.log1p(jnp.exp(-jnp.abs(x)))


def _sigmoid(x):
    return 0.5 * jnp.tanh(0.5 * x) + 0.5


def _rms(x):
    return x * lax.rsqrt(jnp.mean(x * x, axis=-1, keepdims=True) + EPS)


def _chunk(c):
    return pl.ds(pl.multiple_of(c * CHUNK, CHUNK), CHUNK)


def _alias_prev(args, in_specs, prev):
    aliases = {}
    for out_idx, arr in enumerate(prev):
        if arr is not None:
            aliases[len(args)] = out_idx
            args.append(arr)
            in_specs.append(pl.BlockSpec(memory_space=pl.ANY))
    return aliases, len(aliases)


def _ada_kernel(c_ref, w_ref, b_ref, o_ref):
    c = c_ref[...]
    s = (c * jax.nn.sigmoid(c)).astype(BF16)
    o_ref[0] = _dot(s, w_ref[0].astype(BF16)) + b_ref[0]


def _ada_call(cond8, w_ada, b_ada):
    tn = 1536
    return pl.pallas_call(
        _ada_kernel,
        grid=(DEPTH, 6 * D // tn),
        in_specs=[pl.BlockSpec((8, D), lambda l, j: (0, 0)),
                  pl.BlockSpec((1, D, tn), lambda l, j: (l, 0, j)),
                  pl.BlockSpec((1, 1, tn), lambda l, j: (l, 0, j))],
        out_specs=pl.BlockSpec((1, 8, tn), lambda l, j: (l, 0, j)),
        out_shape=jax.ShapeDtypeStruct((DEPTH, 8, 6 * D), F32),
        compiler_params=_cparams(("arbitrary", "arbitrary"), VMEM_LIMIT),
        name="ada",
    )(cond8, w_ada, b_ada.reshape(DEPTH, 1, 6 * D))


def _norm1_kernel(x_ref, g_ref, mod_ref, wg_ref, h_ref, gate_ref):
    y = _rms(x_ref[...]) * g_ref[...]
    h = (y * (1.0 + mod_ref[0, 1:2, :]) + mod_ref[0, 0:1, :]).astype(BF16)
    h_ref[...] = h
    gate_ref[...] = _dot(h, wg_ref[...])


def _norm1_call(x, g, mod, w_gate):
    tm = 512
    return pl.pallas_call(
        _norm1_kernel,
        grid=(ROWS // tm,),
        in_specs=[pl.BlockSpec((tm, D), lambda i: (i, 0)),
                  pl.BlockSpec((1, D), lambda i: (0, 0)),
                  pl.BlockSpec((1, 6, D), lambda i: (i * tm // GROUP_ROWS, 0, 0)),
                  pl.BlockSpec((D, LANES), lambda i: (0, 0))],
        out_specs=[pl.BlockSpec((tm, D), lambda i: (i, 0)),
                   pl.BlockSpec((tm, LANES), lambda i: (i, 0))],
        out_shape=[jax.ShapeDtypeStruct((ROWS, D), BF16),
                   jax.ShapeDtypeStruct((ROWS, LANES), F32)],
        compiler_params=_cparams(("arbitrary",), VMEM_LIMIT),
        name="norm1",
    )(x, g.reshape(1, D), mod, w_gate)


def _mm_kernel(x_ref, w_ref, o_ref):
    o_ref[...] = _dot(x_ref[...], w_ref[...]).astype(o_ref.dtype)


def _inproj_call(h, w_main):
    tm, tn = 1024, 1024
    return pl.pallas_call(
        _mm_kernel,
        grid=(D_MAIN // tn, ROWS // tm),
        in_specs=[pl.BlockSpec((tm, D), lambda j, i: (i, 0)),
                  pl.BlockSpec((D, tn), lambda j, i: (0, j))],
        out_specs=pl.BlockSpec((tm, tn), lambda j, i: (i, j)),
        out_shape=jax.ShapeDtypeStruct((ROWS, D_MAIN), BF16),
        compiler_params=_cparams(("arbitrary", "arbitrary"), VMEM_LIMIT),
        name="inproj",
    )(h, w_main)


def _mm_t_kernel(wt_ref, x_ref, o_ref):
    o_ref[...] = _dot_nt(wt_ref[...], x_ref[...]).astype(o_ref.dtype)


def _inproj_t_call(h, w_kt):
    tm, tn = 1024, 1024
    return pl.pallas_call(
        _mm_t_kernel,
        grid=(D_KT // tn, ROWS // tm),
        in_specs=[pl.BlockSpec((tn, D), lambda j, i: (j, 0)),
                  pl.BlockSpec((tm, D), lambda j, i: (i, 0))],
        out_specs=pl.BlockSpec((tn, tm), lambda j, i: (j, i)),
        out_shape=jax.ShapeDtypeStruct((D_KT, ROWS), BF16),
        compiler_params=_cparams(("arbitrary", "arbitrary"), VMEM_LIMIT),
        name="inproj_t",
    )(w_kt, h)


M_AUG = M_DH + LANES


def _mlstm_kernel(*refs, n_chunks, has_init, emit_state, n_prev):
    q_ref, kt_ref, v_ref, o_ref, g_ref, ng_ref = refs[:6]
    pos = 6
    if has_init:
        c0_ref, m0_ref = refs[pos:pos + 2]
        pos += 2
    pos += n_prev
    y_ref = refs[pos]
    pos += 1
    if emit_state:
        co_ref, no_ref, mo_ref = refs[pos:pos + 3]
        pos += 3
    caug_sc, m_sc, hf_sc, hb_sc = refs[pos:]

    if has_init:
        caug_sc[...] = c0_ref[0, :, 0]
        m_sc[...] = m0_ref[0, :, 0]
    else:
        caug_sc[...] = jnp.zeros_like(caug_sc)
        m_sc[...] = jnp.zeros_like(m_sc)

    ri = lax.broadcasted_iota(jnp.int32, (CHUNK, CHUNK), 0)
    ci = lax.broadcasted_iota(jnp.int32, (CHUNK, CHUNK), 1)
    r8 = lax.broadcasted_iota(jnp.int32, (SUBLANES, CHUNK), 0)
    ones_b = jnp.ones((CHUNK, LANES), BF16)
    k_scale = M_DH ** -0.5
    allowed = [ri >= ci, ci >= ri]
    allowed_b = [jnp.where(a, 1.0, 0.0).astype(BF16) for a in allowed]
    tri = [jnp.where(ri <= ci, 1.0, 0.0).astype(BF16), jnp.where(ri >= ci, 1.0, 0.0).astype(BF16)]

    def step(d, c, m_t):
        rows = _chunk(c)
        i_idx, f_idx = (0, 1) if d == 0 else (2, 3)
        g8 = g_ref[0, :, rows]
        ls = jnp.where(r8 == f_idx, _log_sigmoid(g8), g8)
        cum = sum(_dot(p, tri[d]) for p in _split3(ls))
        f_row = ls[f_idx:f_idx + 1]
        b_row = cum[f_idx:f_idx + 1]
        g_row = ls[i_idx:i_idx + 1] - b_row
        gmat = jnp.where(allowed[d], jnp.broadcast_to(g_row, (CHUNK, CHUNK)), -jnp.inf)
        mp_col = jnp.max(gmat, axis=-1, keepdims=True)
        mp = jnp.broadcast_to(mp_col, (CHUNK, LANES))
        fp = _split3(f_row)
        bc = _dot(allowed_b[d] * fp[0], ones_b) + _dot(allowed_b[d] * fp[1], ones_b)

        q = q_ref[rows, :]
        kt = kt_ref[:, rows]
        vaug = jnp.concatenate([v_ref[rows, :], ones_b], axis=1)
        s1 = (_dot(q, kt) * k_scale) * jnp.exp(gmat - mp_col)
        sv1 = _dot(s1.astype(BF16), vaug)
        b_last_11 = b_row[:, CHUNK - 1:CHUNK] if d == 0 else b_row[:, 0:1]
        log_k = b_last_11 + g_row
        a_11 = jnp.max(log_k, axis=-1, keepdims=True)
        kw1 = (kt.astype(F32) * (jnp.exp(log_k - a_11) * k_scale)).astype(BF16)
        u1 = _dot(kw1, vaug)
        b_last = jnp.broadcast_to(b_last_11, (SUBLANES, LANES))
        a_t = jnp.broadcast_to(a_11, (SUBLANES, LANES))

        m_b = jnp.broadcast_to(m_t[0:1, :], (CHUNK, LANES))
        mi = jnp.maximum(m_b, mp)
        r = jnp.exp(mp - mi)
        w_prev = jnp.exp(m_b - mi)
        floor = jnp.exp(-(bc + mi))
        qc = _dot(q, caug_sc[d].astype(BF16))
        den = r * sv1[:, M_DH:] + w_prev * qc[:, M_DH:]
        inv = 1.0 / jnp.maximum(jnp.abs(den), floor)
        h_sc = hf_sc if d == 0 else hb_sc
        for blk in range(M_DH // LANES):
            cols = slice(blk * LANES, (blk + 1) * LANES)
            h_sc[rows, cols] = (r * sv1[:, cols] + w_prev * qc[:, cols]) * inv

        m_new = jnp.maximum(b_last + m_t, a_t)
        w_c = jnp.exp(b_last + m_t - m_new)[0:1, :]
        w_u = jnp.exp(a_t - m_new)[0:1, :]
        for blk in range(M_AUG // LANES):
            cols = slice(blk * LANES, (blk + 1) * LANES)
            caug_sc[d, :, cols] = w_c * caug_sc[d, :, cols] + w_u * u1[:, cols]
        return m_new

    def scan_body(c, carry):
        mf, mb = carry
        mf = step(0, c, mf)
        mb = step(1, n_chunks - 1 - c, mb)
        return mf, mb

    m_init = tuple(jnp.broadcast_to(m_sc[d], (SUBLANES, LANES)) for d in range(2))
    mf, mb = lax.fori_loop(0, n_chunks, scan_body, m_init)
    m_sc[0] = mf[0:1, :]
    m_sc[1] = mb[0:1, :]

    def out_body(c, carry):
        rows = _chunk(c)
        y = _rms(hf_sc[rows, :] + hb_sc[rows, :]) * ng_ref[...]
        y_ref[rows, :] = (y * _sigmoid(o_ref[rows, :].astype(F32))).astype(y_ref.dtype)
        return carry

    lax.fori_loop(0, n_chunks, out_body, 0)

    if emit_state:
        co_ref[0, 0, :, 0] = caug_sc[:, :, 0:M_DH]
        for d in range(2):
            no_ref[0, 0, d, 0] = caug_sc[d, :, M_DH:].T[0:1, :]
        mo_ref[0, 0, :, 0] = m_sc[...]


def _mlstm_call(proj, proj_t, grow, norm_g, state, prev, *, t, n_seq, row_block0, layer, emit_state):
    has_init = state is not None

    def col(base):
        return lambda s, h: (row_block0 + s, base // M_DH + h)

    st_caug = pl.BlockSpec((1, 2, 1, M_DH, M_AUG), lambda s, h: (s, 0, h, 0, 0))
    st_m = pl.BlockSpec((1, 2, 1, 1, LANES), lambda s, h: (s, 0, h, 0, 0))
    so_c = pl.BlockSpec((1, 1, 2, 1, M_DH, M_DH), lambda s, h: (s, layer, 0, h, 0, 0))
    so_n = pl.BlockSpec((1, 1, 2, 1, 1, M_DH), lambda s, h: (s, layer, 0, h, 0, 0))
    so_m = pl.BlockSpec((1, 1, 2, 1, 1, LANES), lambda s, h: (s, layer, 0, h, 0, 0))
    in_specs = [pl.BlockSpec((t, M_DH), col(COL_MQ)),
                pl.BlockSpec((M_DH, t), lambda s, h: (ROW_MK // M_DH + h, row_block0 + s)),
                pl.BlockSpec((t, M_DH), col(COL_MV)),
                pl.BlockSpec((t, M_DH), col(COL_MO)),
                pl.BlockSpec((1, SUBLANES, t), lambda s, h: (h, 0, row_block0 + s)),
                pl.BlockSpec((1, M_DH), lambda s, h: (0, h))]
    args = [proj, proj_t, proj, proj, grow, norm_g.reshape(1, D)]
    if has_init:
        in_specs += [st_caug, st_m]
        args += list(state)
    out_specs = [pl.BlockSpec((t, M_DH), lambda s, h: (row_block0 + s, h))]
    out_shape = [jax.ShapeDtypeStruct((ROWS, D), BF16)]
    if emit_state:
        out_specs += [so_c, so_n, so_m]
        out_shape += [jax.ShapeDtypeStruct((n_seq, DEPTH, 2, M_HEADS, M_DH, M_DH), F32),
                      jax.ShapeDtypeStruct((n_seq, DEPTH, 2, M_HEADS, 1, M_DH), F32),
                      jax.ShapeDtypeStruct((n_seq, DEPTH, 2, M_HEADS, 1, LANES), F32)]
    aliases, n_prev = _alias_prev(args, in_specs, prev)
    return pl.pallas_call(
        functools.partial(_mlstm_kernel, n_chunks=t // CHUNK, has_init=has_init,
                          emit_state=emit_state, n_prev=n_prev),
        grid=(n_seq, M_HEADS),
        in_specs=in_specs,
        out_specs=out_specs,
        out_shape=out_shape,
        input_output_aliases=aliases,
        scratch_shapes=[pltpu.VMEM((2, M_DH, M_AUG), F32),
                        pltpu.VMEM((2, 1, LANES), F32),
                        pltpu.VMEM((t, M_DH), F32),
                        pltpu.VMEM((t, M_DH), F32)],
        compiler_params=_cparams(("arbitrary", "arbitrary"), VMEM_LIMIT),
        name="mlstm",
    )(*args)


def _ret_kernel(*refs, n_chunks, rope, has_init, emit_state, n_prev):
    q_ref, kt_ref, v_ref, g_ref, dec_ref, ng_ref = refs[:6]
    pos = 6
    if rope:
        cos_ref, sin_ref, cos_t_ref, sin_t_ref = refs[pos:pos + 4]
        pos += 4
    if has_init:
        s0_ref = refs[pos]
        pos += 1
    pos += n_prev
    y_ref = refs[pos]
    pos += 1
    if emit_state:
        so_ref = refs[pos]
        pos += 1
    s_sc, of_sc, ob_sc, ktr_sc = refs[pos:pos + 4]
    qr_sc = refs[pos + 4] if rope else None

    if has_init:
        s_sc[...] = s0_ref[0, :, 0]
    else:
        s_sc[...] = jnp.zeros_like(s_sc)

    ri = lax.broadcasted_iota(jnp.int32, (CHUNK, CHUNK), 0)
    ci = lax.broadcasted_iota(jnp.int32, (CHUNK, CHUNK), 1)
    quarter = R_DH // 4
    if rope:
        fr = lax.broadcasted_iota(jnp.int32, (R_DH, R_DH), 0)
        fc = lax.broadcasted_iota(jnp.int32, (R_DH, R_DH), 1)
        partner = fc + jnp.where((fc % (2 * quarter)) < quarter, quarter, -quarter)
        perm_b = jnp.where(fr == partner, 1.0, 0.0).astype(BF16)

    def prep_body(c, carry):
        rows = _chunk(c)
        ktf = kt_ref[:, rows].astype(F32) * (R_DH ** -0.5)
        if rope:
            swapped = jnp.concatenate([ktf[quarter:2 * quarter], ktf[0:quarter],
                                       ktf[3 * quarter:], ktf[2 * quarter:3 * quarter]], axis=0)
            ktf = ktf * cos_t_ref[:, rows] + swapped * sin_t_ref[:, rows]
            q = q_ref[rows, :]
            qr = q.astype(F32) * cos_ref[rows, :] + _dot(q, perm_b) * sin_ref[rows, :]
            qr_sc[rows, :] = qr.astype(BF16)
        ktr_sc[:, rows] = ktf.astype(BF16)
        return carry

    lax.fori_loop(0, n_chunks, prep_body, 0)
    qsrc = qr_sc if rope else q_ref

    lane = lax.broadcasted_iota(jnp.int32, (1, CHUNK), 1).astype(F32)
    row_f = lax.broadcasted_iota(jnp.int32, (CHUNK, R_DH), 0).astype(F32)
    consts = []
    for d in range(2):
        lg = -jnp.exp(dec_ref[0, d:d + 1, :])
        lg11 = lg[:, 0:1]
        diff = (ri - ci) if d == 0 else (ci - ri)
        causal = diff >= 0
        dmat = jnp.where(causal, jnp.exp(lg11 * jnp.where(causal, diff, 0).astype(F32)), 0.0)
        if d == 0:
            q_dec = jnp.exp(lg * (row_f + 1.0))
            k_dec = jnp.exp(lg11 * (CHUNK - 1.0 - lane))
        else:
            q_dec = jnp.exp(lg * (CHUNK - row_f))
            k_dec = jnp.exp(lg11 * lane)
        g_chunk = jnp.exp(lg * float(CHUNK))
        consts.append((dmat, q_dec, k_dec, g_chunk))

    def step(d, c):
        dmat, q_dec, k_dec, g_chunk = consts[d]
        rows = _chunk(c)
        q = qsrc[rows, :]
        kt = ktr_sc[:, rows]
        v = v_ref[rows, :]
        s = _dot(q, kt) * dmat
        o = _dot(s.astype(BF16), v) + q_dec * _dot(q, s_sc[d].astype(BF16))
        if d == 0:
            of_sc[rows, :] = o
        else:
            ob_sc[rows, :] = o
        kd = (kt.astype(F32) * k_dec).astype(BF16)
        s_sc[d] = g_chunk * s_sc[d] + _dot(kd, v)

    def scan_body(c, carry):
        step(0, c)
        step(1, n_chunks - 1 - c)
        return carry

    lax.fori_loop(0, n_chunks, scan_body, 0, unroll=min(n_chunks, 2))

    def out_body(c, carry):
        rows = _chunk(c)
        y = _rms(of_sc[rows, :] + ob_sc[rows, :]) * ng_ref[...]
        gate = g_ref[rows, :].astype(F32)
        y_ref[rows, :] = (y * (gate * _sigmoid(gate))).astype(y_ref.dtype)
        return carry

    lax.fori_loop(0, n_chunks, out_body, 0)

    if emit_state:
        so_ref[0, 0, :, 0] = s_sc[...]


def _ret_call(proj, proj_t, decay, norm_g, rope_tabs, state, prev, *, t, n_seq, row_block0, layer, emit_state):
    has_init = state is not None
    rope = rope_tabs is not None

    def col(base):
        return lambda s, h: (row_block0 + s, base // R_DH + h)

    st_s = pl.BlockSpec((1, 2, 1, R_DH, R_DH), lambda s, h: (s, 0, h, 0, 0))
    in_specs = [pl.BlockSpec((t, R_DH), col(COL_RQ)),
                pl.BlockSpec((R_DH, t), lambda s, h: (ROW_RK // R_DH + h, row_block0 + s)),
                pl.BlockSpec((t, R_DH), col(COL_RV)),
                pl.BlockSpec((t, R_DH), col(COL_RG)),
                pl.BlockSpec((1, 2, LANES), lambda s, h: (h, 0, 0)),
                pl.BlockSpec((1, R_DH), lambda s, h: (0, h))]
    args = [proj, proj_t, proj, proj, decay, norm_g.reshape(1, D)]
    if rope:
        in_specs += [pl.BlockSpec((t, R_DH), lambda s, h: (0, 0))] * 2
        in_specs += [pl.BlockSpec((R_DH, t), lambda s, h: (0, 0))] * 2
        args += list(rope_tabs)
    if has_init:
        in_specs.append(st_s)
        args.append(state)
    out_specs = [pl.BlockSpec((t, R_DH), lambda s, h: (row_block0 + s, h))]
    out_shape = [jax.ShapeDtypeStruct((ROWS, D), BF16)]
    if emit_state:
        out_specs.append(pl.BlockSpec((1, 1, 2, 1, R_DH, R_DH), lambda s, h: (s, layer, 0, h, 0, 0)))
        out_shape.append(jax.ShapeDtypeStruct((n_seq, DEPTH, 2, R_HEADS, R_DH, R_DH), F32))
    aliases, n_prev = _alias_prev(args, in_specs, prev)
    scratch = [pltpu.VMEM((2, R_DH, R_DH), F32),
               pltpu.VMEM((t, R_DH), F32),
               pltpu.VMEM((t, R_DH), F32),
               pltpu.VMEM((R_DH, t), BF16)]
    if rope:
        scratch.append(pltpu.VMEM((t, R_DH), BF16))
    return pl.pallas_call(
        functools.partial(_ret_kernel, n_chunks=t // CHUNK, rope=rope, has_init=has_init,
                          emit_state=emit_state, n_prev=n_prev),
        grid=(n_seq, R_HEADS),
        in_specs=in_specs,
        out_specs=out_specs,
        out_shape=out_shape,
        input_output_aliases=aliases,
        scratch_shapes=scratch,
        compiler_params=_cparams(("arbitrary", "arbitrary"), VMEM_LIMIT),
        name="retention",
    )(*args)


LRU_SLAB = 64


def _tile_scan(a, b, reverse):
    row = lax.broadcasted_iota(jnp.int32, (SUBLANES, L_BW), 0)
    for k in (1, 2, 4):
        if reverse:
            keep = row < SUBLANES - k
            shift = SUBLANES - k
        else:
            keep = row >= k
            shift = k
        a_sh = jnp.where(keep, pltpu.roll(a, shift, 0), 1.0)
        b_sh = jnp.where(keep, pltpu.roll(b, shift, 0), 0.0)
        b = a * b_sh + b
        a = a * a_sh
    return a, b


def _lru_kernel(*refs, t, has_init, emit_state, n_prev):
    lx_ref, lz_ref, cw_ref, cb_ref, wr_ref, wi_ref, br_ref, bi_ref, lam_ref = refs[:9]
    pos = 9
    if has_init:
        h0_ref = refs[pos]
        pos += 1
    pos += n_prev
    y_ref = refs[pos]
    pos += 1
    if emit_state:
        hfin_ref = refs[pos]
        pos += 1
    xpad_sc, a_sc, b_sc = refs[pos:]
    rc = min(t, 256)

    xpad_sc[0:8, :] = jnp.zeros((8, L_BW), F32)
    xpad_sc[t + 8:t + 16, :] = jnp.zeros((8, L_BW), F32)

    def pad_body(c, carry):
        r0 = pl.multiple_of(c * rc, rc)
        xpad_sc[pl.ds(r0 + 8, rc), :] = lx_ref[pl.ds(r0, rc), :].astype(F32)
        return carry

    lax.fori_loop(0, t // rc, pad_body, 0)

    wr = [wr_ref[d, 0].astype(BF16) for d in range(2)]
    wi = [wi_ref[d, 0].astype(BF16) for d in range(2)]
    neg_c = []
    for d in range(2):
        lam = lam_ref[d:d + 1, :]
        softplus_neg = jnp.maximum(-lam, 0.0) + jnp.log1p(jnp.exp(-jnp.abs(lam)))
        neg_c.append(-L_C * softplus_neg)

    def gate_body(c, carry):
        r0 = pl.multiple_of(c * rc, rc)
        xe = xpad_sc[pl.ds(r0, rc + 16), :]
        n = rc + 16
        u = (cb_ref[...] + cw_ref[0:1, :] * pltpu.roll(xe, 2, 0)[8:8 + rc]
             + cw_ref[1:2, :] * pltpu.roll(xe, 1, 0)[8:8 + rc]
             + cw_ref[2:3, :] * xe[8:8 + rc]
             + cw_ref[3:4, :] * pltpu.roll(xe, n - 1, 0)[8:8 + rc])
        ub = u.astype(BF16)
        for d in range(2):
            r = _sigmoid(_dot(ub, wr[d]) + br_ref[d:d + 1, :])
            i = _sigmoid(_dot(ub, wi[d]) + bi_ref[d:d + 1, :])
            log_a = neg_c[d] * r
            a = jnp.exp(log_a)
            z = -jnp.tanh(log_a) * (a * a + 1.0)
            mult = z * lax.rsqrt(jnp.maximum(z, F32_TINY))
            a_sc[d, pl.ds(r0, rc), :] = a
            b_sc[d, pl.ds(r0, rc), :] = mult * (i * u)
        return carry

    lax.fori_loop(0, t // rc, gate_body, 0)

    n_slabs = t // LRU_SLAB
    tiles = LRU_SLAB // SUBLANES

    row8 = lax.broadcasted_iota(jnp.int32, (SUBLANES, L_BW), 0)

    def bcast_row(x, r):
        return jnp.broadcast_to(x[r:r + 1, :], (SUBLANES, L_BW))

    def slab_scan(d, r0, c_in):
        reverse = d == 1
        edge = 0 if reverse else SUBLANES - 1
        local = []
        spa = jnp.ones((SUBLANES, L_BW), F32)
        shl = jnp.zeros((SUBLANES, L_BW), F32)
        for k in range(tiles):
            rows = pl.ds(r0 + k * SUBLANES, SUBLANES)
            pa, hl = _tile_scan(a_sc[d, rows, :], b_sc[d, rows, :], reverse)
            local.append((rows, pa, hl))
            spa = jnp.where(row8 == k, bcast_row(pa, edge), spa)
            shl = jnp.where(row8 == k, bcast_row(hl, edge), shl)
        cpa, chl = _tile_scan(spa, shl, reverse)
        after = chl + cpa * c_in
        for k, (rows, pa, hl) in enumerate(local):
            prev = k + 1 if reverse else k - 1
            cin = c_in if (prev < 0 or prev >= tiles) else bcast_row(after, prev)
            b_sc[d, rows, :] = hl + pa * cin
        return bcast_row(after, 0 if reverse else tiles - 1)

    def scan_body(c, carry):
        cf, cbk = carry
        cf = slab_scan(0, pl.multiple_of(c * LRU_SLAB, LRU_SLAB), cf)
        cbk = slab_scan(1, pl.multiple_of((n_slabs - 1 - c) * LRU_SLAB, LRU_SLAB), cbk)
        return cf, cbk

    if has_init:
        cf0 = jnp.broadcast_to(h0_ref[0, 0:1, :], (SUBLANES, L_BW))
        cb0 = jnp.broadcast_to(h0_ref[0, 1:2, :], (SUBLANES, L_BW))
    else:
        cf0 = jnp.zeros((SUBLANES, L_BW), F32)
        cb0 = jnp.zeros((SUBLANES, L_BW), F32)
    cf, cbk = lax.fori_loop(0, n_slabs, scan_body, (cf0, cb0))
    if emit_state:
        hfin_ref[0, 0, 0:1, :] = cf[0:1, :]
        hfin_ref[0, 0, 1:2, :] = cbk[0:1, :]

    def out_body(c, carry):
        rows = pl.ds(pl.multiple_of(c * rc, rc), rc)
        z = lz_ref[rows, :].astype(F32)
        gelu = 0.5 * z * (1.0 + jnp.tanh(0.7978845608028654 * (z + 0.044715 * (z * z * z))))
        y_ref[rows, :] = ((b_sc[0, rows, :] + b_sc[1, rows, :]) * gelu).astype(y_ref.dtype)
        return carry

    lax.fori_loop(0, t // rc, out_body, 0)


def _lru_call(proj, p, state, prev, *, t, n_seq, row_block0, layer, emit_state):
    has_init = state is not None

    def col(base):
        return lambda s, b: (row_block0 + s, base // L_BW + b)

    vec2 = pl.BlockSpec((2, L_BW), lambda s, b: (0, b))
    st = pl.BlockSpec((1, 2, L_BW), lambda s, b: (s, 0, b))
    in_specs = [pl.BlockSpec((t, L_BW), col(COL_LX)),
                pl.BlockSpec((t, L_BW), col(COL_LZ)),
                pl.BlockSpec((4, L_BW), lambda s, b: (0, b)),
                pl.BlockSpec((1, L_BW), lambda s, b: (0, b)),
                pl.BlockSpec((2, 1, L_BW, L_BW), lambda s, b: (0, b, 0, 0)),
                pl.BlockSpec((2, 1, L_BW, L_BW), lambda s, b: (0, b, 0, 0)),
                vec2, vec2, vec2]
    args = [proj, proj, p['conv_w'], p['conv_b'].reshape(1, D), p['wr'], p['wi'],
            p['br'], p['bi'], p['lam']]
    if has_init:
        in_specs.append(st)
        args.append(state)
    out_specs = [pl.BlockSpec((t, L_BW), lambda s, b: (row_block0 + s, b))]
    out_shape = [jax.ShapeDtypeStruct((ROWS, D), BF16)]
    if emit_state:
        out_specs.append(pl.BlockSpec((1, 1, 2, L_BW), lambda s, b: (s, layer, 0, b)))
        out_shape.append(jax.ShapeDtypeStruct((n_seq, DEPTH, 2, D), F32))
    aliases, n_prev = _alias_prev(args, in_specs, prev)
    return pl.pallas_call(
        functools.partial(_lru_kernel, t=t, has_init=has_init, emit_state=emit_state, n_prev=n_prev),
        grid=(n_seq, L_BLOCKS),
        in_specs=in_specs,
        out_specs=out_specs,
        out_shape=out_shape,
        input_output_aliases=aliases,
        scratch_shapes=[pltpu.VMEM((t + 16, L_BW), F32),
                        pltpu.VMEM((2, t, L_BW), F32),
                        pltpu.VMEM((2, t, L_BW), F32)],
        compiler_params=_cparams(("arbitrary", "arbitrary"), VMEM_LIMIT),
        name="rglru",
    )(*args)


def _merge_kernel(h_ref, ym_ref, yl_ref, yr_ref, x_ref, mod_ref, wm_ref, bm_ref, wb_ref, wo_ref,
                  n2_ref, rwh_ref, rwl_ref, xo_ref, h2_ref, lg_ref):
    h = h_ref[...]
    merged = None
    for k, y_ref in enumerate((ym_ref, yl_ref, yr_ref)):
        gate = jax.nn.sigmoid(_dot(h, wm_ref[:, k * D:(k + 1) * D]) + bm_ref[:, k * D:(k + 1) * D])
        term = gate * _dot(y_ref[...], wb_ref[k])
        merged = term if merged is None else merged + term
    mix = _dot(merged.astype(BF16), wo_ref[...])
    xn = x_ref[...] + mod_ref[0, 2:3, :] * mix
    xo_ref[...] = xn
    hn = _rms(xn) * n2_ref[...] * (1.0 + mod_ref[0, 4:5, :]) + mod_ref[0, 3:4, :]
    hi = hn.astype(BF16)
    h2_ref[...] = hi
    lo = (hn - hi.astype(F32)).astype(BF16)
    lg_ref[...] = _dot(hi, rwh_ref[...]) + _dot(lo, rwh_ref[...]) + _dot(hi, rwl_ref[...])


def _merge_call(h, ym, yl, yr, x, mod, wm, bm, wb, wo, n2g, rwh, rwl):
    tm = 256
    row = lambda i: (i, 0)
    const2 = lambda i: (0, 0)
    return pl.pallas_call(
        _merge_kernel,
        grid=(ROWS // tm,),
        in_specs=[pl.BlockSpec((tm, D), row),
                  pl.BlockSpec((tm, D), row),
                  pl.BlockSpec((tm, D), row),
                  pl.BlockSpec((tm, D), row),
                  pl.BlockSpec((tm, D), row),
                  pl.BlockSpec((1, 6, D), lambda i: (i * tm // GROUP_ROWS, 0, 0)),
                  pl.BlockSpec((D, 3 * D), const2),
                  pl.BlockSpec((1, 3 * D), const2),
                  pl.BlockSpec((3, D, D), lambda i: (0, 0, 0)),
                  pl.BlockSpec((D, D), const2),
                  pl.BlockSpec((1, D), const2),
                  pl.BlockSpec((D, LANES), const2),
                  pl.BlockSpec((D, LANES), const2)],
        out_specs=[pl.BlockSpec((tm, D), row),
                   pl.BlockSpec((tm, D), row),
                   pl.BlockSpec((tm, LANES), row)],
        out_shape=[jax.ShapeDtypeStruct((ROWS, D), F32),
                   jax.ShapeDtypeStruct((ROWS, D), BF16),
                   jax.ShapeDtypeStruct((ROWS, LANES), F32)],
        compiler_params=_cparams(("arbitrary",), VMEM_LIMIT),
        name="merge",
    )(h, ym, yl, yr, x, mod, wm, bm.reshape(1, 3 * D), wb, wo, n2g.reshape(1, D), rwh, rwl)


EXPERT_TM = 512
assert N_PROMPT_SEQ * CAP_PROMPT == EXPERT_TM and CAP_SAMPLE == EXPERT_TM


def _expert_kernel(xs_ref, g_ref, mod_ref, w1_ref, w3_ref, w2_ref, y_ref, w1_sc, w3_sc, w2_sc):
    @pl.when(pl.program_id(1) == 0)
    def _():
        w1_sc[...] = w1_ref[0, 0].astype(BF16)
        w3_sc[...] = w3_ref[0, 0].astype(BF16)
        w2_sc[...] = w2_ref[0, 0].astype(BF16)

    xs = xs_ref[0]
    a = _dot(xs, w1_sc[...])
    b = _dot(xs, w3_sc[...])
    mid = (a * jax.nn.sigmoid(a) * b).astype(BF16)
    y_ref[0] = (_dot(mid, w2_sc[...]) * g_ref[0]) * mod_ref[0, 5:6, :]


def _expert_call(xs, gv, mod, w1, w3, w2, layer):
    tm = EXPERT_TM
    wspec = pl.BlockSpec((1, 1, D, D), lambda e, m: (layer, e, 0, 0))
    return pl.pallas_call(
        _expert_kernel,
        grid=(N_EXPERTS, ROWS_PER_EXPERT // tm),
        in_specs=[pl.BlockSpec((1, tm, D), lambda e, m: (e, m, 0)),
                  pl.BlockSpec((1, tm, 1), lambda e, m: (e, m, 0)),
                  pl.BlockSpec((1, 6, D), lambda e, m: (m, 0, 0)),
                  wspec, wspec, wspec],
        out_specs=pl.BlockSpec((1, tm, D), lambda e, m: (e, m, 0)),
        out_shape=jax.ShapeDtypeStruct((N_EXPERTS, ROWS_PER_EXPERT, D), F32),
        scratch_shapes=[pltpu.VMEM((D, D), BF16)] * 3,
        compiler_params=_cparams(("arbitrary", "arbitrary"), VMEM_LIMIT),
        name="experts",
    )(xs, gv, mod, w1, w3, w2)


def _final_norm_kernel(x_ref, g_ref, o_ref):
    o_ref[...] = _rms(x_ref[...]) * g_ref[...]


def _final_norm_call(x, g, row0, n_rows):
    tm = 512
    return pl.pallas_call(
        _final_norm_kernel,
        grid=(n_rows // tm,),
        in_specs=[pl.BlockSpec((tm, D), lambda i: (row0 // tm + i, 0)),
                  pl.BlockSpec((1, D), lambda i: (0, 0))],
        out_specs=pl.BlockSpec((tm, D), lambda i: (i, 0)),
        out_shape=jax.ShapeDtypeStruct((n_rows, D), F32),
        compiler_params=_cparams(("arbitrary",), VMEM_LIMIT),
        name="final_norm",
    )(x, g.reshape(1, D))


def _rope_tables():
    tpos = jnp.arange(T_SAMPLE)
    lane = jnp.arange(R_DH)
    pos = jnp.where(lane[None, :] < R_DH // 2, (tpos // GRID_W)[:, None], (tpos % GRID_W)[:, None])
    n_freq = R_DH // 4
    freqs = jnp.power(ROPE_BASE, -jnp.arange(n_freq, dtype=F32) / n_freq)
    ang = pos.astype(F32) * freqs[lane % n_freq][None, :]
    first = ((lane % (R_DH // 2)) < n_freq)[None, :]
    cos, sin = jnp.cos(ang), jnp.sin(ang)
    sin = jnp.where(first, -sin, sin)
    return cos, sin, cos.T, sin.T


def _route(logits):
    aff = jax.nn.softmax(logits[:, :N_EXPERTS], axis=-1)
    ap = aff[:ROWS_PROMPT].reshape(N_PROMPT_SEQ, T_PROMPT, N_EXPERTS).swapaxes(1, 2)
    as_ = aff[ROWS_PROMPT:].reshape(N_SAMPLE_SEQ, T_SAMPLE, N_EXPERTS).swapaxes(1, 2)
    gp, ip = lax.top_k(ap, CAP_PROMPT)
    gs, is_ = lax.top_k(as_, CAP_SAMPLE)
    ip = ip + (jnp.arange(N_PROMPT_SEQ) * T_PROMPT)[:, None, None]
    is_ = is_ + ROWS_PROMPT + (jnp.arange(N_SAMPLE_SEQ) * T_SAMPLE)[:, None, None]
    rows = jnp.concatenate([ip.swapaxes(0, 1).reshape(N_EXPERTS, -1),
                            is_.swapaxes(0, 1).reshape(N_EXPERTS, -1)], axis=1)
    gv = jnp.concatenate([gp.swapaxes(0, 1).reshape(N_EXPERTS, -1),
                          gs.swapaxes(0, 1).reshape(N_EXPERTS, -1)], axis=1)
    return rows, gv


def _layer(x, mod, p, states, rope_tabs, stacked, layer, prev_states):
    w_in = p['w_in']
    mq, mk, mv, mo, mg, lx, lz, rq, rk, rv, rg = jnp.split(
        w_in, [1024, 2048, 3072, 4096, 4112, 5136, 6160, 7184, 8208, 9232], axis=1)
    w_main = jnp.concatenate([mq, mv, mo, lx, lz, rq, rv, rg], axis=1).astype(BF16)
    w_kt = jnp.concatenate([mk, rk], axis=1).T.astype(BF16)
    w_gate = jnp.pad(mg, ((0, 0), (0, LANES - N_GATE_COLS))).astype(BF16)

    h, gates = _norm1_call(x, p['norm1_g'], mod, w_gate)
    proj = _inproj_call(h, w_main)
    proj_t = _inproj_t_call(h, w_kt)

    gb = gates[:, :N_GATE_COLS] + p['mlstm_gate_bias'].reshape(1, N_GATE_COLS)
    grow = gb.reshape(ROWS, 4, M_HEADS).transpose(2, 1, 0)
    grow = jnp.pad(grow, ((0, 0), (0, SUBLANES - 4), (0, 0)))

    sm_c, sm_n, sm_m, s_lh, s_rs = states
    prompt_kw = dict(t=T_PROMPT, n_seq=N_PROMPT_SEQ, row_block0=0, layer=layer, emit_state=True)
    sample_kw = dict(t=T_SAMPLE, n_seq=N_SAMPLE_SEQ, row_block0=ROWS_PROMPT // T_SAMPLE, layer=layer,
                     emit_state=False)
    pc, pn, pm, plh, prs = prev_states

    caug0 = jnp.concatenate(
        [sm_c, jnp.broadcast_to(sm_n[..., None], sm_n.shape + (LANES,))], axis=-1)
    m0 = jnp.broadcast_to(sm_m[..., None, None], (N_SAMPLE_SEQ, 2, M_HEADS, 1, LANES))
    (ym,) = _mlstm_call(proj, proj_t, grow, p['mlstm_norm_g'], (caug0, m0), [None], **sample_kw)
    ym, new_c, new_n, new_m = _mlstm_call(proj, proj_t, grow, p['mlstm_norm_g'], None,
                                          [ym, pc, pn, pm], **prompt_kw)

    lru_p = dict(conv_w=p['lru_conv_w'], conv_b=p['lru_conv_b'], wr=p['lru_wr'], wi=p['lru_wi'],
                 br=p['lru_br'], bi=p['lru_bi'], lam=p['lru_lambda'])
    (yl,) = _lru_call(proj, lru_p, s_lh, [None], **sample_kw)
    yl, new_lh = _lru_call(proj, lru_p, None, [yl, plh], **prompt_kw)

    decay = jnp.broadcast_to(p['ret_decay'].T[:, :, None], (R_HEADS, 2, LANES))
    (yr,) = _ret_call(proj, proj_t, decay, p['ret_norm_g'], rope_tabs, s_rs, [None], **sample_kw)
    yr, new_rs = _ret_call(proj, proj_t, decay, p['ret_norm_g'], None, None, [yr, prs], **prompt_kw)

    rw = jnp.pad(p['router_w'], ((0, 0), (0, LANES - N_EXPERTS)))
    rwh = rw.astype(BF16)
    rwl = (rw - rwh.astype(F32)).astype(BF16)
    xn, h2, logits = _merge_call(h, ym, yl, yr, x, mod, p['w_merge'].astype(BF16), p['b_merge'],
                                 p['w_branch'].astype(BF16), p['w_out'].astype(BF16), p['norm2_g'],
                                 rwh, rwl)

    rows, gv = _route(logits)
    xs = h2[rows]
    y = _expert_call(xs, gv[..., None], mod, stacked['exp_w1'], stacked['exp_w3'], stacked['exp_w2'], layer)
    x_out = xn.at[rows.reshape(-1)].add(y.reshape(-1, D))
    return x_out, (new_c, new_n, new_m, new_lh, new_rs)


def kernel(x_prompt, x_sample, c, state_mlstm_C, state_mlstm_n, state_mlstm_m, state_lru_h, state_ret_S, c_ctx, w_ada, b_ada, norm1_g, norm2_g, w_in, mlstm_gate_bias, mlstm_norm_g, lru_conv_w, lru_conv_b, lru_wr, lru_br, lru_wi, lru_bi, lru_lambda, ret_decay, ret_norm_g, w_branch, w_merge, b_merge, w_out, router_w, exp_w1, exp_w3, exp_w2, final_g):
    x = jnp.concatenate([x_prompt.reshape(ROWS_PROMPT, D), x_sample.reshape(-1, D)], axis=0)
    cond8 = jnp.concatenate([c_ctx[None, :], c, jnp.zeros((8 - 1 - N_SAMPLE_SEQ, D), F32)], axis=0)
    mod_all = _ada_call(cond8, w_ada, b_ada).reshape(DEPTH, 8, 6, D)
    rope_tabs = _rope_tables()
    stacked = dict(exp_w1=exp_w1, exp_w3=exp_w3, exp_w2=exp_w2)

    new_states = (None,) * 5
    for l in range(DEPTH):
        p = dict(norm1_g=norm1_g[l], norm2_g=norm2_g[l], w_in=w_in[l],
                 mlstm_gate_bias=mlstm_gate_bias[l], mlstm_norm_g=mlstm_norm_g[l],
                 lru_conv_w=lru_conv_w[l], lru_conv_b=lru_conv_b[l], lru_wr=lru_wr[l],
                 lru_br=lru_br[l], lru_wi=lru_wi[l], lru_bi=lru_bi[l], lru_lambda=lru_lambda[l],
                 ret_decay=ret_decay[l], ret_norm_g=ret_norm_g[l], w_branch=w_branch[l],
                 w_merge=w_merge[l], b_merge=b_merge[l], w_out=w_out[l], router_w=router_w[l])
        states = (state_mlstm_C[:, l], state_mlstm_n[:, l], state_mlstm_m[:, l],
                  state_lru_h[:, l], state_ret_S[:, l])
        x, new_states = _layer(x, mod_all[l], p, states, rope_tabs, stacked, l, new_states)

    y_prompt = _final_norm_call(x, final_g, 0, ROWS_PROMPT).reshape(N_PROMPT_SEQ, T_PROMPT, D)
    y_sample = _final_norm_call(x, final_g, ROWS_PROMPT, ROWS - ROWS_PROMPT).reshape(
        N_SAMPLE_SEQ, T_SAMPLE, D)
    new_c, new_n, new_m, new_lh, new_rs = new_states
    return (y_prompt, y_sample, new_c, new_n[:, :, :, :, 0, :], new_m[:, :, :, :, 0, 0], new_lh, new_rs)
```

```python
import functools

import jax
import jax.numpy as jnp
from jax import lax
from jax.experimental import pallas as pl
from jax.experimental.pallas import tpu as pltpu

F32 = jnp.float32
BF16 = jnp.bfloat16

D = 1024
DEPTH = 2
N_PROMPT_SEQ = 16
T_PROMPT = 256
N_SAMPLE_SEQ = 2
T_SAMPLE = 4096
ROWS_PROMPT = N_PROMPT_SEQ * T_PROMPT
ROWS = ROWS_PROMPT + N_SAMPLE_SEQ * T_SAMPLE
GROUP_ROWS = 4096
GRID_W = 64
CHUNK = 256
EPS = 1e-6
F32_TINY = 1.1754944e-38
M_HEADS = 4
M_DH = 256
R_HEADS = 8
R_DH = 128
L_BLOCKS = 8
L_BW = 128
L_C = 8.0
ROPE_BASE = 10000.0
N_EXPERTS = 16
CAP_PROMPT = 2 * T_PROMPT // N_EXPERTS
CAP_SAMPLE = 2 * T_SAMPLE // N_EXPERTS
ROWS_PER_EXPERT = N_PROMPT_SEQ * CAP_PROMPT + N_SAMPLE_SEQ * CAP_SAMPLE
N_GATE_COLS = 16
LANES = 128
SUBLANES = 8
VMEM_LIMIT = 56 * 2 ** 20

COL_MQ, COL_MV, COL_MO, COL_LX, COL_LZ, COL_RQ, COL_RV, COL_RG = (i * D for i in range(8))
D_MAIN = 8 * D
ROW_MK, ROW_RK = 0, D
D_KT = 2 * D


def _cparams(sem, vmem=None):
    return pltpu.CompilerParams(dimension_semantics=sem, vmem_limit_bytes=vmem)


def _dot(a, b):
    return jnp.dot(a, b, preferred_element_type=F32)


def _dot_nt(a, b):
    return lax.dot_general(a, b, (((1,), (1,)), ((), ())), preferred_element_type=F32)


def _split3(x):
    a = x.astype(BF16)
    r = x - a.astype(F32)
    b = r.astype(BF16)
    c = (r - b.astype(F32)).astype(BF16)
    return a, b, c


def _log_sigmoid(x):
    return jnp.minimum(x, 0.0) - jnp.log1p(jnp.exp(-jnp.abs(x)))


def _sigmoid(x):
    return 0.5 * jnp.tanh(0.5 * x) + 0.5


def _rms(x):
    return x * lax.rsqrt(jnp.mean(x * x, axis=-1, keepdims=True) + EPS)


def _chunk(c):
    return pl.ds(pl.multiple_of(c * CHUNK, CHUNK), CHUNK)


def _alias_prev(args, in_specs, prev):
    aliases = {}
    for out_idx, arr in enumerate(prev):
        if arr is not None:
            aliases[len(args)] = out_idx
            args.append(arr)
            in_specs.append(pl.BlockSpec(memory_space=pl.ANY))
    return aliases, len(aliases)


def _ada_kernel(c_ref, w_ref, b_ref, o_ref):
    c = c_ref[...]
    s = (c * jax.nn.sigmoid(c)).astype(BF16)
    o_ref[0] = _dot(s, w_ref[0].astype(BF16)) + b_ref[0]


def _ada_call(cond8, w_ada, b_ada):
    tn = 1536
    return pl.pallas_call(
        _ada_kernel,
        grid=(DEPTH, 6 * D // tn),
        in_specs=[pl.BlockSpec((8, D), lambda l, j: (0, 0)),
                  pl.BlockSpec((1, D, tn), lambda l, j: (l, 0, j)),
                  pl.BlockSpec((1, 1, tn), lambda l, j: (l, 0, j))],
        out_specs=pl.BlockSpec((1, 8, tn), lambda l, j: (l, 0, j)),
        out_shape=jax.ShapeDtypeStruct((DEPTH, 8, 6 * D), F32),
        compiler_params=_cparams(("arbitrary", "arbitrary"), VMEM_LIMIT),
        name="ada",
    )(cond8, w_ada, b_ada.reshape(DEPTH, 1, 6 * D))


def _norm1_kernel(x_ref, g_ref, mod_ref, wg_ref, h_ref, gate_ref):
    y = _rms(x_ref[...]) * g_ref[...]
    h = (y * (1.0 + mod_ref[0, 1:2, :]) + mod_ref[0, 0:1, :]).astype(BF16)
    h_ref[...] = h
    gate_ref[...] = _dot(h, wg_ref[...])


def _norm1_call(x, g, mod, w_gate):
    tm = 512
    return pl.pallas_call(
        _norm1_kernel,
        grid=(ROWS // tm,),
        in_specs=[pl.BlockSpec((tm, D), lambda i: (i, 0)),
                  pl.BlockSpec((1, D), lambda i: (0, 0)),
                  pl.BlockSpec((1, 6, D), lambda i: (i * tm // GROUP_ROWS, 0, 0)),
                  pl.BlockSpec((D, LANES), lambda i: (0, 0))],
        out_specs=[pl.BlockSpec((tm, D), lambda i: (i, 0)),
                   pl.BlockSpec((tm, LANES), lambda i: (i, 0))],
        out_shape=[jax.ShapeDtypeStruct((ROWS, D), BF16),
                   jax.ShapeDtypeStruct((ROWS, LANES), F32)],
        compiler_params=_cparams(("arbitrary",), VMEM_LIMIT),
        name="norm1",
    )(x, g.reshape(1, D), mod, w_gate)


def _mm_kernel(x_ref, w_ref, o_ref):
    o_ref[...] = _dot(x_ref[...], w_ref[...]).astype(o_ref.dtype)


def _inproj_call(h, w_main):
    tm, tn = 1024, 1024
    return pl.pallas_call(
        _mm_kernel,
        grid=(D_MAIN // tn, ROWS // tm),
        in_specs=[pl.BlockSpec((tm, D), lambda j, i: (i, 0)),
                  pl.BlockSpec((D, tn), lambda j, i: (0, j))],
        out_specs=pl.BlockSpec((tm, tn), lambda j, i: (i, j)),
        out_shape=jax.ShapeDtypeStruct((ROWS, D_MAIN), BF16),
        compiler_params=_cparams(("arbitrary", "arbitrary"), VMEM_LIMIT),
        name="inproj",
    )(h, w_main)


def _mm_t_kernel(wt_ref, x_ref, o_ref):
    o_ref[...] = _dot_nt(wt_ref[...], x_ref[...]).astype(o_ref.dtype)


def _inproj_t_call(h, w_kt):
    tm, tn = 1024, 1024
    return pl.pallas_call(
        _mm_t_kernel,
        grid=(D_KT // tn, ROWS // tm),
        in_specs=[pl.BlockSpec((tn, D), lambda j, i: (j, 0)),
                  pl.BlockSpec((tm, D), lambda j, i: (i, 0))],
        out_specs=pl.BlockSpec((tn, tm), lambda j, i: (j, i)),
        out_shape=jax.ShapeDtypeStruct((D_KT, ROWS), BF16),
        compiler_params=_cparams(("arbitrary", "arbitrary"), VMEM_LIMIT),
        name="inproj_t",
    )(w_kt, h)


M_AUG = M_DH + LANES


def _mlstm_kernel(*refs, n_chunks, has_init, emit_state, n_prev):
    q_ref, kt_ref, v_ref, o_ref, g_ref, ng_ref = refs[:6]
    pos = 6
    if has_init:
        c0_ref, m0_ref = refs[pos:pos + 2]
        pos += 2
    pos += n_prev
    y_ref = refs[pos]
    pos += 1
    if emit_state:
        co_ref, no_ref, mo_ref = refs[pos:pos + 3]
        pos += 3
    caug_sc, m_sc, hf_sc, hb_sc = refs[pos:]

    zero_state = (not has_init) and n_chunks == 1
    if has_init:
        caug_sc[...] = c0_ref[0, :, 0]
        m_sc[...] = m0_ref[0, :, 0]
    else:
        if not zero_state:
            caug_sc[...] = jnp.zeros_like(caug_sc)
        m_sc[...] = jnp.zeros_like(m_sc)

    ri = lax.broadcasted_iota(jnp.int32, (CHUNK, CHUNK), 0)
    ci = lax.broadcasted_iota(jnp.int32, (CHUNK, CHUNK), 1)
    r8 = lax.broadcasted_iota(jnp.int32, (SUBLANES, CHUNK), 0)
    ones_b = jnp.ones((CHUNK, LANES), BF16)
    allowed = [ri >= ci, ci >= ri]
    allowed_b = [jnp.where(a, 1.0, 0.0).astype(BF16) for a in allowed]
    tri = [jnp.where(ri <= ci, 1.0, 0.0).astype(BF16), jnp.where(ri >= ci, 1.0, 0.0).astype(BF16)]

    def step(d, c, m_t):
        rows = _chunk(c)
        i_idx, f_idx = (0, 1) if d == 0 else (2, 3)
        g8 = g_ref[0, :, rows]
        ls = jnp.where(r8 == f_idx, _log_sigmoid(g8), g8)
        cum = sum(_dot(p, tri[d]) for p in _split3(ls))
        f_row = ls[f_idx:f_idx + 1]
        b_row = cum[f_idx:f_idx + 1]
        g_row = ls[i_idx:i_idx + 1] - b_row
        gmat = jnp.where(allowed[d], jnp.broadcast_to(g_row, (CHUNK, CHUNK)), -jnp.inf)
        mp_col = jnp.max(gmat, axis=-1, keepdims=True)
        mp = jnp.broadcast_to(mp_col, (CHUNK, LANES))
        fp = _split3(f_row)
        bc = _dot(allowed_b[d] * fp[0], ones_b) + _dot(allowed_b[d] * fp[1], ones_b)

        q = q_ref[rows, :]
        kt = kt_ref[:, rows]
        vaug = jnp.concatenate([v_ref[rows, :], ones_b], axis=1)
        s1 = _dot(q, kt) * jnp.exp(gmat - mp_col)
        sv1 = _dot(s1.astype(BF16), vaug)
        b_last_11 = b_row[:, CHUNK - 1:CHUNK] if d == 0 else b_row[:, 0:1]
        log_k = b_last_11 + g_row
        a_11 = jnp.max(log_k, axis=-1, keepdims=True)
        kw1 = (kt.astype(F32) * jnp.exp(log_k - a_11)).astype(BF16)
        u1 = _dot(kw1, vaug)
        b_last = jnp.broadcast_to(b_last_11, (SUBLANES, LANES))
        a_t = jnp.broadcast_to(a_11, (SUBLANES, LANES))

        m_b = jnp.broadcast_to(m_t[0:1, :], (CHUNK, LANES))
        mi = jnp.maximum(m_b, mp)
        r = jnp.exp(mp - mi)
        w_prev = jnp.exp(m_b - mi)
        floor = jnp.exp(-(bc + mi))
        if zero_state:
            qc = None
            den = r * sv1[:, M_DH:]
        else:
            qc = _dot(q, caug_sc[d].astype(BF16))
            den = r * sv1[:, M_DH:] + w_prev * qc[:, M_DH:]
        inv = 1.0 / jnp.maximum(jnp.abs(den), floor)
        h_sc = hf_sc if d == 0 else hb_sc
        for blk in range(M_DH // LANES):
            cols = slice(blk * LANES, (blk + 1) * LANES)
            num = r * sv1[:, cols]
            if not zero_state:
                num = num + w_prev * qc[:, cols]
            h_sc[rows, cols] = num * inv

        m_new = jnp.maximum(b_last + m_t, a_t)
        w_c = jnp.exp(b_last + m_t - m_new)[0:1, :]
        w_u = jnp.exp(a_t - m_new)[0:1, :]
        for blk in range(M_AUG // LANES):
            cols = slice(blk * LANES, (blk + 1) * LANES)
            if zero_state:
                caug_sc[d, :, cols] = w_u * u1[:, cols]
            else:
                caug_sc[d, :, cols] = w_c * caug_sc[d, :, cols] + w_u * u1[:, cols]
        return m_new

    def scan_body(c, carry):
        mf, mb = carry
        mf = step(0, c, mf)
        mb = step(1, n_chunks - 1 - c, mb)
        return mf, mb

    m_init = tuple(jnp.broadcast_to(m_sc[d], (SUBLANES, LANES)) for d in range(2))
    mf, mb = lax.fori_loop(0, n_chunks, scan_body, m_init)
    m_sc[0] = mf[0:1, :]
    m_sc[1] = mb[0:1, :]

    def out_body(c, carry):
        rows = _chunk(c)
        y = _rms(hf_sc[rows, :] + hb_sc[rows, :]) * ng_ref[...]
        y_ref[rows, :] = (y * _sigmoid(o_ref[rows, :].astype(F32))).astype(y_ref.dtype)
        return carry

    lax.fori_loop(0, n_chunks, out_body, 0)

    if emit_state:
        co_ref[0, 0, :, 0] = caug_sc[:, :, 0:M_DH]
        for d in range(2):
            no_ref[0, 0, d, 0] = caug_sc[d, :, M_DH:].T[0:1, :]
        mo_ref[0, 0, :, 0] = m_sc[...]


def _mlstm_call(proj, proj_t, grow, norm_g, state, prev, *, t, n_seq, row_block0, layer, emit_state):
    has_init = state is not None

    def col(base):
        return lambda s, h: (row_block0 + s, base // M_DH + h)

    st_caug = pl.BlockSpec((1, 2, 1, M_DH, M_AUG), lambda s, h: (s, 0, h, 0, 0))
    st_m = pl.BlockSpec((1, 2, 1, 1, LANES), lambda s, h: (s, 0, h, 0, 0))
    so_c = pl.BlockSpec((1, 1, 2, 1, M_DH, M_DH), lambda s, h: (s, layer, 0, h, 0, 0))
    so_n = pl.BlockSpec((1, 1, 2, 1, 1, M_DH), lambda s, h: (s, layer, 0, h, 0, 0))
    so_m = pl.BlockSpec((1, 1, 2, 1, 1, LANES), lambda s, h: (s, layer, 0, h, 0, 0))
    in_specs = [pl.BlockSpec((t, M_DH), col(COL_MQ)),
                pl.BlockSpec((M_DH, t), lambda s, h: (ROW_MK // M_DH + h, row_block0 + s)),
                pl.BlockSpec((t, M_DH), col(COL_MV)),
                pl.BlockSpec((t, M_DH), col(COL_MO)),
                pl.BlockSpec((1, SUBLANES, t), lambda s, h: (h, 0, row_block0 + s)),
                pl.BlockSpec((1, M_DH), lambda s, h: (0, h))]
    args = [proj, proj_t, proj, proj, grow, norm_g.reshape(1, D)]
    if has_init:
        in_specs += [st_caug, st_m]
        args += list(state)
    out_specs = [pl.BlockSpec((t, M_DH), lambda s, h: (row_block0 + s, h))]
    out_shape = [jax.ShapeDtypeStruct((ROWS, D), BF16)]
    if emit_state:
        out_specs += [so_c, so_n, so_m]
        out_shape += [jax.ShapeDtypeStruct((n_seq, DEPTH, 2, M_HEADS, M_DH, M_DH), F32),
                      jax.ShapeDtypeStruct((n_seq, DEPTH, 2, M_HEADS, 1, M_DH), F32),
                      jax.ShapeDtypeStruct((n_seq, DEPTH, 2, M_HEADS, 1, LANES), F32)]
    aliases, n_prev = _alias_prev(args, in_specs, prev)
    return pl.pallas_call(
        functools.partial(_mlstm_kernel, n_chunks=t // CHUNK, has_init=has_init,
                          emit_state=emit_state, n_prev=n_prev),
        grid=(n_seq, M_HEADS),
        in_specs=in_specs,
        out_specs=out_specs,
        out_shape=out_shape,
        input_output_aliases=aliases,
        scratch_shapes=[pltpu.VMEM((2, M_DH, M_AUG), F32),
                        pltpu.VMEM((2, 1, LANES), F32),
                        pltpu.VMEM((t, M_DH), F32),
                        pltpu.VMEM((t, M_DH), F32)],
        compiler_params=_cparams(("arbitrary", "arbitrary"), VMEM_LIMIT),
        name="mlstm",
    )(*args)


def _ret_kernel(*refs, n_chunks, rope, has_init, emit_state, n_prev):
    q_ref, kt_ref, v_ref, g_ref, dec_ref, ng_ref = refs[:6]
    pos = 6
    if rope:
        cos_ref, sin_ref, cos_t_ref, sin_t_ref = refs[pos:pos + 4]
        pos += 4
    if has_init:
        s0_ref = refs[pos]
        pos += 1
    pos += n_prev
    y_ref = refs[pos]
    pos += 1
    if emit_state:
        so_ref = refs[pos]
        pos += 1
    s_sc, of_sc, ob_sc, ktr_sc = refs[pos:pos + 4]
    qr_sc = refs[pos + 4] if rope else None

    zero_state = (not has_init) and n_chunks == 1
    if has_init:
        s_sc[...] = s0_ref[0, :, 0]
    elif not zero_state:
        s_sc[...] = jnp.zeros_like(s_sc)

    ri = lax.broadcasted_iota(jnp.int32, (CHUNK, CHUNK), 0)
    ci = lax.broadcasted_iota(jnp.int32, (CHUNK, CHUNK), 1)
    quarter = R_DH // 4
    if rope:
        fr = lax.broadcasted_iota(jnp.int32, (R_DH, R_DH), 0)
        fc = lax.broadcasted_iota(jnp.int32, (R_DH, R_DH), 1)
        partner = fc + jnp.where((fc % (2 * quarter)) < quarter, quarter, -quarter)
        perm_b = jnp.where(fr == partner, 1.0, 0.0).astype(BF16)

    def prep_body(c, carry):
        rows = _chunk(c)
        ktf = kt_ref[:, rows].astype(F32) * (R_DH ** -0.5)
        if rope:
            swapped = jnp.concatenate([ktf[quarter:2 * quarter], ktf[0:quarter],
                                       ktf[3 * quarter:], ktf[2 * quarter:3 * quarter]], axis=0)
            ktf = ktf * cos_t_ref[:, rows] + swapped * sin_t_ref[:, rows]
            q = q_ref[rows, :]
            qr = q.astype(F32) * cos_ref[rows, :] + _dot(q, perm_b) * sin_ref[rows, :]
            qr_sc[rows, :] = qr.astype(BF16)
        ktr_sc[:, rows] = ktf.astype(BF16)
        return carry

    lax.fori_loop(0, n_chunks, prep_body, 0, unroll=min(n_chunks, 2))
    qsrc = qr_sc if rope else q_ref

    lane = lax.broadcasted_iota(jnp.int32, (1, CHUNK), 1).astype(F32)
    row_f = lax.broadcasted_iota(jnp.int32, (CHUNK, R_DH), 0).astype(F32)
    consts = []
    for d in range(2):
        lg = -jnp.exp(dec_ref[0, d:d + 1, :])
        lg11 = lg[:, 0:1]
        diff = (ri - ci) if d == 0 else (ci - ri)
        causal = diff >= 0
        dmat = jnp.where(causal, jnp.exp(lg11 * jnp.where(causal, diff, 0).astype(F32)), 0.0)
        if d == 0:
            q_dec = jnp.exp(lg * (row_f + 1.0))
            k_dec = jnp.exp(lg11 * (CHUNK - 1.0 - lane))
        else:
            q_dec = jnp.exp(lg * (CHUNK - row_f))
            k_dec = jnp.exp(lg11 * lane)
        g_chunk = jnp.exp(lg * float(CHUNK))
        consts.append((dmat, q_dec, k_dec, g_chunk))

    def step(d, c):
        dmat, q_dec, k_dec, g_chunk = consts[d]
        rows = _chunk(c)
        q = qsrc[rows, :]
        kt = ktr_sc[:, rows]
        v = v_ref[rows, :]
        s = _dot(q, kt) * dmat
        o = _dot(s.astype(BF16), v)
        if not zero_state:
            o = o + q_dec * _dot(q, s_sc[d].astype(BF16))
        if d == 0:
            of_sc[rows, :] = o
        else:
            ob_sc[rows, :] = o
        kd = (kt.astype(F32) * k_dec).astype(BF16)
        if zero_state:
            s_sc[d] = _dot(kd, v)
        else:
            s_sc[d] = g_chunk * s_sc[d] + _dot(kd, v)

    def scan_body(c, carry):
        step(0, c)
        step(1, n_chunks - 1 - c)
        return carry

    lax.fori_loop(0, n_chunks, scan_body, 0, unroll=min(n_chunks, 2))

    def out_body(c, carry):
        rows = _chunk(c)
        y = _rms(of_sc[rows, :] + ob_sc[rows, :]) * ng_ref[...]
        gate = g_ref[rows, :].astype(F32)
        y_ref[rows, :] = (y * (gate * _sigmoid(gate))).astype(y_ref.dtype)
        return carry

    lax.fori_loop(0, n_chunks, out_body, 0, unroll=min(n_chunks, 2))

    if emit_state:
        so_ref[0, 0, :, 0] = s_sc[...]


def _ret_call(proj, proj_t, decay, norm_g, rope_tabs, state, prev, *, t, n_seq, row_block0, layer, emit_state):
    has_init = state is not None
    rope = rope_tabs is not None

    def col(base):
        return lambda s, h: (row_block0 + s, base // R_DH + h)

    st_s = pl.BlockSpec((1, 2, 1, R_DH, R_DH), lambda s, h: (s, 0, h, 0, 0))
    in_specs = [pl.BlockSpec((t, R_DH), col(COL_RQ)),
                pl.BlockSpec((R_DH, t), lambda s, h: (ROW_RK // R_DH + h, row_block0 + s)),
                pl.BlockSpec((t, R_DH), col(COL_RV)),
                pl.BlockSpec((t, R_DH), col(COL_RG)),
                pl.BlockSpec((1, 2, LANES), lambda s, h: (h, 0, 0)),
                pl.BlockSpec((1, R_DH), lambda s, h: (0, h))]
    args = [proj, proj_t, proj, proj, decay, norm_g.reshape(1, D)]
    if rope:
        in_specs += [pl.BlockSpec((t, R_DH), lambda s, h: (0, 0))] * 2
        in_specs += [pl.BlockSpec((R_DH, t), lambda s, h: (0, 0))] * 2
        args += list(rope_tabs)
    if has_init:
        in_specs.append(st_s)
        args.append(state)
    out_specs = [pl.BlockSpec((t, R_DH), lambda s, h: (row_block0 + s, h))]
    out_shape = [jax.ShapeDtypeStruct((ROWS, D), BF16)]
    if emit_state:
        out_specs.append(pl.BlockSpec((1, 1, 2, 1, R_DH, R_DH), lambda s, h: (s, layer, 0, h, 0, 0)))
        out_shape.append(jax.ShapeDtypeStruct((n_seq, DEPTH, 2, R_HEADS, R_DH, R_DH), F32))
    aliases, n_prev = _alias_prev(args, in_specs, prev)
    scratch = [pltpu.VMEM((2, R_DH, R_DH), F32),
               pltpu.VMEM((t, R_DH), F32),
               pltpu.VMEM((t, R_DH), F32),
               pltpu.VMEM((R_DH, t), BF16)]
    if rope:
        scratch.append(pltpu.VMEM((t, R_DH), BF16))
    return pl.pallas_call(
        functools.partial(_ret_kernel, n_chunks=t // CHUNK, rope=rope, has_init=has_init,
                          emit_state=emit_state, n_prev=n_prev),
        grid=(n_seq, R_HEADS),
        in_specs=in_specs,
        out_specs=out_specs,
        out_shape=out_shape,
        input_output_aliases=aliases,
        scratch_shapes=scratch,
        compiler_params=_cparams(("arbitrary", "arbitrary"), VMEM_LIMIT),
        name="retention",
    )(*args)


LRU_SLAB = 64


def _tile_scan(a, b, reverse):
    row = lax.broadcasted_iota(jnp.int32, (SUBLANES, L_BW), 0)
    for k in (1, 2, 4):
        if reverse:
            keep = row < SUBLANES - k
            shift = SUBLANES - k
        else:
            keep = row >= k
            shift = k
        a_sh = jnp.where(keep, pltpu.roll(a, shift, 0), 1.0)
        b_sh = jnp.where(keep, pltpu.roll(b, shift, 0), 0.0)
        b = a * b_sh + b
        a = a * a_sh
    return a, b


def _lru_kernel(*refs, t, has_init, emit_state, n_prev):
    lx_ref, lz_ref, cw_ref, cb_ref, wr_ref, wi_ref, br_ref, bi_ref, lam_ref = refs[:9]
    pos = 9
    if has_init:
        h0_ref = refs[pos]
        pos += 1
    pos += n_prev
    y_ref = refs[pos]
    pos += 1
    if emit_state:
        hfin_ref = refs[pos]
        pos += 1
    xpad_sc, a_sc, b_sc = refs[pos:]
    rc = min(t, 256)

    xpad_sc[0:8, :] = jnp.zeros((8, L_BW), F32)
    xpad_sc[t + 8:t + 16, :] = jnp.zeros((8, L_BW), F32)

    def pad_body(c, carry):
        r0 = pl.multiple_of(c * rc, rc)
        xpad_sc[pl.ds(r0 + 8, rc), :] = lx_ref[pl.ds(r0, rc), :].astype(F32)
        return carry

    lax.fori_loop(0, t // rc, pad_body, 0)

    wr = [(0.5 * wr_ref[d, 0]).astype(BF16) for d in range(2)]
    wi = [(0.5 * wi_ref[d, 0]).astype(BF16) for d in range(2)]
    half_br = [0.5 * br_ref[d:d + 1, :] for d in range(2)]
    half_bi = [0.5 * bi_ref[d:d + 1, :] for d in range(2)]
    half_c = []
    for d in range(2):
        lam = lam_ref[d:d + 1, :]
        softplus_neg = jnp.maximum(-lam, 0.0) + jnp.log1p(jnp.exp(-jnp.abs(lam)))
        half_c.append(-0.5 * L_C * softplus_neg)

    def gate_body(c, carry):
        r0 = pl.multiple_of(c * rc, rc)
        xe = xpad_sc[pl.ds(r0, rc + 16), :]
        n = rc + 16
        u = (cb_ref[...] + cw_ref[0:1, :] * pltpu.roll(xe, 2, 0)[8:8 + rc]
             + cw_ref[1:2, :] * pltpu.roll(xe, 1, 0)[8:8 + rc]
             + cw_ref[2:3, :] * xe[8:8 + rc]
             + cw_ref[3:4, :] * pltpu.roll(xe, n - 1, 0)[8:8 + rc])
        ub = u.astype(BF16)
        half_u = 0.5 * u
        for d in range(2):
            tr = jnp.tanh(_dot(ub, wr[d]) + half_br[d])
            ti = jnp.tanh(_dot(ub, wi[d]) + half_bi[d])
            log_a = half_c[d] * tr + half_c[d]
            a = jnp.exp(log_a)
            z = jnp.tanh(log_a) * (-1.0 - a * a)
            mult = z * lax.rsqrt(jnp.maximum(z, F32_TINY))
            a_sc[d, pl.ds(r0, rc), :] = a
            b_sc[d, pl.ds(r0, rc), :] = mult * (ti * half_u + half_u)
        return carry

    lax.fori_loop(0, t // rc, gate_body, 0)

    n_slabs = t // LRU_SLAB
    tiles = LRU_SLAB // SUBLANES

    row8 = lax.broadcasted_iota(jnp.int32, (SUBLANES, L_BW), 0)

    def bcast_row(x, r):
        return jnp.broadcast_to(x[r:r + 1, :], (SUBLANES, L_BW))

    def slab_scan(d, r0, c_in):
        reverse = d == 1
        edge = 0 if reverse else SUBLANES - 1
        local = []
        spa = jnp.ones((SUBLANES, L_BW), F32)
        shl = jnp.zeros((SUBLANES, L_BW), F32)
        for k in range(tiles):
            rows = pl.ds(r0 + k * SUBLANES, SUBLANES)
            pa, hl = _tile_scan(a_sc[d, rows, :], b_sc[d, rows, :], reverse)
            local.append((rows, pa, hl))
            spa = jnp.where(row8 == k, bcast_row(pa, edge), spa)
            shl = jnp.where(row8 == k, bcast_row(hl, edge), shl)
        cpa, chl = _tile_scan(spa, shl, reverse)
        after = chl + cpa * c_in
        for k, (rows, pa, hl) in enumerate(local):
            prev = k + 1 if reverse else k - 1
            cin = c_in if (prev < 0 or prev >= tiles) else bcast_row(after, prev)
            b_sc[d, rows, :] = hl + pa * cin
        return bcast_row(after, 0 if reverse else tiles - 1)

    def scan_body(c, carry):
        cf, cbk = carry
        cf = slab_scan(0, pl.multiple_of(c * LRU_SLAB, LRU_SLAB), cf)
        cbk = slab_scan(1, pl.multiple_of((n_slabs - 1 - c) * LRU_SLAB, LRU_SLAB), cbk)
        return cf, cbk

    if has_init:
        cf0 = jnp.broadcast_to(h0_ref[0, 0:1, :], (SUBLANES, L_BW))
        cb0 = jnp.broadcast_to(h0_ref[0, 1:2, :], (SUBLANES, L_BW))
    else:
        cf0 = jnp.zeros((SUBLANES, L_BW), F32)
        cb0 = jnp.zeros((SUBLANES, L_BW), F32)
    cf, cbk = lax.fori_loop(0, n_slabs, scan_body, (cf0, cb0))
    if emit_state:
        hfin_ref[0, 0, 0:1, :] = cf[0:1, :]
        hfin_ref[0, 0, 1:2, :] = cbk[0:1, :]

    def out_body(c, carry):
        rows = pl.ds(pl.multiple_of(c * rc, rc), rc)
        z = lz_ref[rows, :].astype(F32)
        gelu = 0.5 * z * (1.0 + jnp.tanh(0.7978845608028654 * (z + 0.044715 * (z * z * z))))
        y_ref[rows, :] = ((b_sc[0, rows, :] + b_sc[1, rows, :]) * gelu).astype(y_ref.dtype)
        return carry

    lax.fori_loop(0, t // rc, out_body, 0)


def _lru_call(proj, p, state, prev, *, t, n_seq, row_block0, layer, emit_state):
    has_init = state is not None

    def col(base):
        return lambda s, b: (row_block0 + s, base // L_BW + b)

    vec2 = pl.BlockSpec((2, L_BW), lambda s, b: (0, b))
    st = pl.BlockSpec((1, 2, L_BW), lambda s, b: (s, 0, b))
    in_specs = [pl.BlockSpec((t, L_BW), col(COL_LX)),
                pl.BlockSpec((t, L_BW), col(COL_LZ)),
                pl.BlockSpec((4, L_BW), lambda s, b: (0, b)),
                pl.BlockSpec((1, L_BW), lambda s, b: (0, b)),
                pl.BlockSpec((2, 1, L_BW, L_BW), lambda s, b: (0, b, 0, 0)),
                pl.BlockSpec((2, 1, L_BW, L_BW), lambda s, b: (0, b, 0, 0)),
                vec2, vec2, vec2]
    args = [proj, proj, p['conv_w'], p['conv_b'].reshape(1, D), p['wr'], p['wi'],
            p['br'], p['bi'], p['lam']]
    if has_init:
        in_specs.append(st)
        args.append(state)
    out_specs = [pl.BlockSpec((t, L_BW), lambda s, b: (row_block0 + s, b))]
    out_shape = [jax.ShapeDtypeStruct((ROWS, D), BF16)]
    if emit_state:
        out_specs.append(pl.BlockSpec((1, 1, 2, L_BW), lambda s, b: (s, layer, 0, b)))
        out_shape.append(jax.ShapeDtypeStruct((n_seq, DEPTH, 2, D), F32))
    aliases, n_prev = _alias_prev(args, in_specs, prev)
    return pl.pallas_call(
        functools.partial(_lru_kernel, t=t, has_init=has_init, emit_state=emit_state, n_prev=n_prev),
        grid=(n_seq, L_BLOCKS),
        in_specs=in_specs,
        out_specs=out_specs,
        out_shape=out_shape,
        input_output_aliases=aliases,
        scratch_shapes=[pltpu.VMEM((t + 16, L_BW), F32),
                        pltpu.VMEM((2, t, L_BW), F32),
                        pltpu.VMEM((2, t, L_BW), F32)],
        compiler_params=_cparams(("arbitrary", "arbitrary"), VMEM_LIMIT),
        name="rglru",
    )(*args)


def _merge_kernel(h_ref, ym_ref, yl_ref, yr_ref, x_ref, mod_ref, wm_ref, bm_ref, wb_ref, wo_ref,
                  n2_ref, rwh_ref, rwl_ref, xo_ref, h2_ref, lg_ref):
    h = h_ref[...]
    merged = None
    for k, y_ref in enumerate((ym_ref, yl_ref, yr_ref)):
        gate = jax.nn.sigmoid(_dot(h, wm_ref[:, k * D:(k + 1) * D]) + bm_ref[:, k * D:(k + 1) * D])
        term = gate * _dot(y_ref[...], wb_ref[k])
        merged = term if merged is None else merged + term
    mix = _dot(merged.astype(BF16), wo_ref[...])
    xn = x_ref[...] + mod_ref[0, 2:3, :] * mix
    xo_ref[...] = xn
    hn = _rms(xn) * n2_ref[...] * (1.0 + mod_ref[0, 4:5, :]) + mod_ref[0, 3:4, :]
    hi = hn.astype(BF16)
    h2_ref[...] = hi
    lo = (hn - hi.astype(F32)).astype(BF16)
    lg_ref[...] = _dot(hi, rwh_ref[...]) + _dot(lo, rwh_ref[...]) + _dot(hi, rwl_ref[...])


def _merge_call(h, ym, yl, yr, x, mod, wm, bm, wb, wo, n2g, rwh, rwl):
    tm = 256
    row = lambda i: (i, 0)
    const2 = lambda i: (0, 0)
    return pl.pallas_call(
        _merge_kernel,
        grid=(ROWS // tm,),
        in_specs=[pl.BlockSpec((tm, D), row),
                  pl.BlockSpec((tm, D), row),
                  pl.BlockSpec((tm, D), row),
                  pl.BlockSpec((tm, D), row),
                  pl.BlockSpec((tm, D), row),
                  pl.BlockSpec((1, 6, D), lambda i: (i * tm // GROUP_ROWS, 0, 0)),
                  pl.BlockSpec((D, 3 * D), const2),
                  pl.BlockSpec((1, 3 * D), const2),
                  pl.BlockSpec((3, D, D), lambda i: (0, 0, 0)),
                  pl.BlockSpec((D, D), const2),
                  pl.BlockSpec((1, D), const2),
                  pl.BlockSpec((D, LANES), const2),
                  pl.BlockSpec((D, LANES), const2)],
        out_specs=[pl.BlockSpec((tm, D), row),
                   pl.BlockSpec((tm, D), row),
                   pl.BlockSpec((tm, LANES), row)],
        out_shape=[jax.ShapeDtypeStruct((ROWS, D), F32),
                   jax.ShapeDtypeStruct((ROWS, D), BF16),
                   jax.ShapeDtypeStruct((ROWS, LANES), F32)],
        compiler_params=_cparams(("arbitrary",), VMEM_LIMIT),
        name="merge",
    )(h, ym, yl, yr, x, mod, wm, bm.reshape(1, 3 * D), wb, wo, n2g.reshape(1, D), rwh, rwl)


EXPERT_TM = 512
assert N_PROMPT_SEQ * CAP_PROMPT == EXPERT_TM and CAP_SAMPLE == EXPERT_TM


def _expert_kernel(xp_ref, xs_ref, g_ref, mod_ref, w1_ref, w3_ref, w2_ref, yp_ref, ys_ref,
                   w1_sc, w3_sc, w2_sc):
    m = pl.program_id(1)

    @pl.when(m == 0)
    def _():
        w1_sc[...] = w1_ref[0, 0].astype(BF16)
        w3_sc[...] = w3_ref[0, 0].astype(BF16)
        w2_sc[...] = w2_ref[0, 0].astype(BF16)

    xs = jnp.where(m == 0, xp_ref[0], xs_ref[0])
    a = _dot(xs, w1_sc[...])
    b = _dot(xs, w3_sc[...])
    mid = (a * jax.nn.sigmoid(a) * b).astype(BF16)
    y = (_dot(mid, w2_sc[...]) * g_ref[0]) * mod_ref[0, 5:6, :]

    @pl.when(m == 0)
    def _():
        yp_ref[0] = y

    @pl.when(m > 0)
    def _():
        ys_ref[0] = y


def _expert_call(xs_p, xs_s, gv, mod, w1, w3, w2, layer):
    tm = EXPERT_TM
    wspec = pl.BlockSpec((1, 1, D, D), lambda e, m: (layer, e, 0, 0))
    p_spec = pl.BlockSpec((1, tm, D), lambda e, m: (e, 0, 0))
    s_spec = pl.BlockSpec((1, tm, D), lambda e, m: (e, jnp.maximum(m - 1, 0), 0))
    return pl.pallas_call(
        _expert_kernel,
        grid=(N_EXPERTS, ROWS_PER_EXPERT // tm),
        in_specs=[p_spec, s_spec,
                  pl.BlockSpec((1, tm, 1), lambda e, m: (e, m, 0)),
                  pl.BlockSpec((1, 6, D), lambda e, m: (m, 0, 0)),
                  wspec, wspec, wspec],
        out_specs=[p_spec, s_spec],
        out_shape=[jax.ShapeDtypeStruct((N_EXPERTS, tm, D), F32),
                   jax.ShapeDtypeStruct((N_EXPERTS, N_SAMPLE_SEQ * tm, D), F32)],
        scratch_shapes=[pltpu.VMEM((D, D), BF16)] * 3,
        compiler_params=_cparams(("arbitrary", "arbitrary"), VMEM_LIMIT),
        name="experts",
    )(xs_p, xs_s, gv, mod, w1, w3, w2)


PROMPT_SLOTS = N_EXPERTS * CAP_PROMPT


def _prompt_gather_kernel(h_ref, idx_ref, o_ref):
    tok = lax.broadcasted_iota(jnp.int32, (PROMPT_SLOTS, T_PROMPT), 1)
    onehot = jnp.where(tok == idx_ref[0], 1.0, 0.0).astype(BF16)
    o_ref[...] = _dot(onehot, h_ref[...]).astype(o_ref.dtype).reshape(N_EXPERTS, CAP_PROMPT, D)


def _prompt_gather_call(h2, idx_col):
    return pl.pallas_call(
        _prompt_gather_kernel,
        grid=(N_PROMPT_SEQ,),
        in_specs=[pl.BlockSpec((T_PROMPT, D), lambda s: (s, 0)),
                  pl.BlockSpec((1, PROMPT_SLOTS, 1), lambda s: (s, 0, 0))],
        out_specs=pl.BlockSpec((N_EXPERTS, CAP_PROMPT, D), lambda s: (0, s, 0)),
        out_shape=jax.ShapeDtypeStruct((N_EXPERTS, N_PROMPT_SEQ * CAP_PROMPT, D), BF16),
        compiler_params=_cparams(("arbitrary",), VMEM_LIMIT),
        name="prompt_gather",
    )(h2, idx_col)


def _prompt_combine_kernel(x_ref, y_ref, idx_ref, o_ref):
    tok = lax.broadcasted_iota(jnp.int32, (T_PROMPT, PROMPT_SLOTS), 0)
    onehot = jnp.where(tok == idx_ref[0], 1.0, 0.0).astype(BF16)
    y = y_ref[...].reshape(PROMPT_SLOTS, D)
    hi = y.astype(BF16)
    lo = (y - hi.astype(F32)).astype(BF16)
    o_ref[...] = x_ref[...] + (_dot(onehot, hi) + _dot(onehot, lo))


def _prompt_combine_call(x, y_p, idx_row):
    return pl.pallas_call(
        _prompt_combine_kernel,
        grid=(N_PROMPT_SEQ,),
        in_specs=[pl.BlockSpec((T_PROMPT, D), lambda s: (s, 0)),
                  pl.BlockSpec((N_EXPERTS, CAP_PROMPT, D), lambda s: (0, s, 0)),
                  pl.BlockSpec((1, 1, PROMPT_SLOTS), lambda s: (s, 0, 0))],
        out_specs=pl.BlockSpec((T_PROMPT, D), lambda s: (s, 0)),
        out_shape=jax.ShapeDtypeStruct((ROWS, D), F32),
        input_output_aliases={0: 0},
        compiler_params=_cparams(("arbitrary",), VMEM_LIMIT),
        name="prompt_combine",
    )(x, y_p, idx_row)


def _final_norm_kernel(x_ref, g_ref, o_ref):
    o_ref[...] = _rms(x_ref[...]) * g_ref[...]


def _final_norm_call(x, g, row0, n_rows):
    tm = 512
    return pl.pallas_call(
        _final_norm_kernel,
        grid=(n_rows // tm,),
        in_specs=[pl.BlockSpec((tm, D), lambda i: (row0 // tm + i, 0)),
                  pl.BlockSpec((1, D), lambda i: (0, 0))],
        out_specs=pl.BlockSpec((tm, D), lambda i: (i, 0)),
        out_shape=jax.ShapeDtypeStruct((n_rows, D), F32),
        compiler_params=_cparams(("arbitrary",), VMEM_LIMIT),
        name="final_norm",
    )(x, g.reshape(1, D))


def _rope_tables():
    tpos = jnp.arange(T_SAMPLE)
    lane = jnp.arange(R_DH)
    pos = jnp.where(lane[None, :] < R_DH // 2, (tpos // GRID_W)[:, None], (tpos % GRID_W)[:, None])
    n_freq = R_DH // 4
    freqs = jnp.power(ROPE_BASE, -jnp.arange(n_freq, dtype=F32) / n_freq)
    ang = pos.astype(F32) * freqs[lane % n_freq][None, :]
    first = ((lane % (R_DH // 2)) < n_freq)[None, :]
    cos, sin = jnp.cos(ang), jnp.sin(ang)
    sin = jnp.where(first, -sin, sin)
    return cos, sin, cos.T, sin.T


def _route(logits):
    aff = jax.nn.softmax(logits[:, :N_EXPERTS], axis=-1)
    ap = aff[:ROWS_PROMPT].reshape(N_PROMPT_SEQ, T_PROMPT, N_EXPERTS).swapaxes(1, 2)
    as_ = aff[ROWS_PROMPT:].reshape(N_SAMPLE_SEQ, T_SAMPLE, N_EXPERTS).swapaxes(1, 2)
    gp, ip = lax.top_k(ap, CAP_PROMPT)
    gs, is_ = lax.top_k(as_, CAP_SAMPLE)
    is_ = is_ + ROWS_PROMPT + (jnp.arange(N_SAMPLE_SEQ) * T_SAMPLE)[:, None, None]
    rows_s = is_.swapaxes(0, 1).reshape(N_EXPERTS, -1)
    gv = jnp.concatenate([gp.swapaxes(0, 1).reshape(N_EXPERTS, -1),
                          gs.swapaxes(0, 1).reshape(N_EXPERTS, -1)], axis=1)
    return ip.reshape(N_PROMPT_SEQ, PROMPT_SLOTS), rows_s, gv


def _layer(x, mod, p, states, rope_tabs, stacked, layer, prev_states):
    w_in = p['w_in']
    mq, mk, mv, mo, mg, lx, lz, rq, rk, rv, rg = jnp.split(
        w_in, [1024, 2048, 3072, 4096, 4112, 5136, 6160, 7184, 8208, 9232], axis=1)
    w_main = jnp.concatenate([mq, mv, mo, lx, lz, rq, rv, rg], axis=1).astype(BF16)
    w_kt = jnp.concatenate([mk * M_DH ** -0.5, rk], axis=1).T.astype(BF16)
    w_gate = jnp.pad(mg, ((0, 0), (0, LANES - N_GATE_COLS))).astype(BF16)

    h, gates = _norm1_call(x, p['norm1_g'], mod, w_gate)
    proj = _inproj_call(h, w_main)
    proj_t = _inproj_t_call(h, w_kt)

    gb = gates[:, :N_GATE_COLS] + p['mlstm_gate_bias'].reshape(1, N_GATE_COLS)
    grow = gb.reshape(ROWS, 4, M_HEADS).transpose(2, 1, 0)
    grow = jnp.pad(grow, ((0, 0), (0, SUBLANES - 4), (0, 0)))

    sm_c, sm_n, sm_m, s_lh, s_rs = states
    prompt_kw = dict(t=T_PROMPT, n_seq=N_PROMPT_SEQ, row_block0=0, layer=layer, emit_state=True)
    sample_kw = dict(t=T_SAMPLE, n_seq=N_SAMPLE_SEQ, row_block0=ROWS_PROMPT // T_SAMPLE, layer=layer,
                     emit_state=False)
    pc, pn, pm, plh, prs = prev_states

    caug0 = jnp.concatenate(
        [sm_c, jnp.broadcast_to(sm_n[..., None], sm_n.shape + (LANES,))], axis=-1)
    m0 = jnp.broadcast_to(sm_m[..., None, None], (N_SAMPLE_SEQ, 2, M_HEADS, 1, LANES))
    (ym,) = _mlstm_call(proj, proj_t, grow, p['mlstm_norm_g'], (caug0, m0), [None], **sample_kw)
    ym, new_c, new_n, new_m = _mlstm_call(proj, proj_t, grow, p['mlstm_norm_g'], None,
                                          [ym, pc, pn, pm], **prompt_kw)

    lru_p = dict(conv_w=p['lru_conv_w'], conv_b=p['lru_conv_b'], wr=p['lru_wr'], wi=p['lru_wi'],
                 br=p['lru_br'], bi=p['lru_bi'], lam=p['lru_lambda'])
    (yl,) = _lru_call(proj, lru_p, s_lh, [None], **sample_kw)
    yl, new_lh = _lru_call(proj, lru_p, None, [yl, plh], **prompt_kw)

    decay = jnp.broadcast_to(p['ret_decay'].T[:, :, None], (R_HEADS, 2, LANES))
    (yr,) = _ret_call(proj, proj_t, decay, p['ret_norm_g'], rope_tabs, s_rs, [None], **sample_kw)
    yr, new_rs = _ret_call(proj, proj_t, decay, p['ret_norm_g'], None, None, [yr, prs], **prompt_kw)

    rw = jnp.pad(p['router_w'], ((0, 0), (0, LANES - N_EXPERTS)))
    rwh = rw.astype(BF16)
    rwl = (rw - rwh.astype(F32)).astype(BF16)
    xn, h2, logits = _merge_call(h, ym, yl, yr, x, mod, p['w_merge'].astype(BF16), p['b_merge'],
                                 p['w_branch'].astype(BF16), p['w_out'].astype(BF16), p['norm2_g'],
                                 rwh, rwl)

    ip, rows_s, gv = _route(logits)
    xs_p = _prompt_gather_call(h2, ip[:, :, None])
    xs_s = h2[rows_s]
    y_p, y_s = _expert_call(xs_p, xs_s, gv[..., None], mod, stacked['exp_w1'], stacked['exp_w3'],
                            stacked['exp_w2'], layer)
    x_out = xn.at[rows_s.reshape(-1)].add(y_s.reshape(-1, D))
    x_out = _prompt_combine_call(x_out, y_p, ip[:, None, :])
    return x_out, (new_c, new_n, new_m, new_lh, new_rs)


def kernel(x_prompt, x_sample, c, state_mlstm_C, state_mlstm_n, state_mlstm_m, state_lru_h, state_ret_S, c_ctx, w_ada, b_ada, norm1_g, norm2_g, w_in, mlstm_gate_bias, mlstm_norm_g, lru_conv_w, lru_conv_b, lru_wr, lru_br, lru_wi, lru_bi, lru_lambda, ret_decay, ret_norm_g, w_branch, w_merge, b_merge, w_out, router_w, exp_w1, exp_w3, exp_w2, final_g):
    x = jnp.concatenate([x_prompt.reshape(ROWS_PROMPT, D), x_sample.reshape(-1, D)], axis=0)
    cond8 = jnp.concatenate([c_ctx[None, :], c, jnp.zeros((8 - 1 - N_SAMPLE_SEQ, D), F32)], axis=0)
    mod_all = _ada_call(cond8, w_ada, b_ada).reshape(DEPTH, 8, 6, D)
    rope_tabs = _rope_tables()
    stacked = dict(exp_w1=exp_w1, exp_w3=exp_w3, exp_w2=exp_w2)

    new_states = (None,) * 5
    for l in range(DEPTH):
        p = dict(norm1_g=norm1_g[l], norm2_g=norm2_g[l], w_in=w_in[l],
                 mlstm_gate_bias=mlstm_gate_bias[l], mlstm_norm_g=mlstm_norm_g[l],
                 lru_conv_w=lru_conv_w[l], lru_conv_b=lru_conv_b[l], lru_wr=lru_wr[l],
                 lru_br=lru_br[l], lru_wi=lru_wi[l], lru_bi=lru_bi[l], lru_lambda=lru_lambda[l],
                 ret_decay=ret_decay[l], ret_norm_g=ret_norm_g[l], w_branch=w_branch[l],
                 w_merge=w_merge[l], b_merge=b_merge[l], w_out=w_out[l], router_w=router_w[l])
        states = (state_mlstm_C[:, l], state_mlstm_n[:, l], state_mlstm_m[:, l],
                  state_lru_h[:, l], state_ret_S[:, l])
        x, new_states = _layer(x, mod_all[l], p, states, rope_tabs, stacked, l, new_states)

    y_prompt = _final_norm_call(x, final_g, 0, ROWS_PROMPT).reshape(N_PROMPT_SEQ, T_PROMPT, D)
    y_sample = _final_norm_call(x, final_g, ROWS_PROMPT, ROWS - ROWS_PROMPT).reshape(
        N_SAMPLE_SEQ, T_SAMPLE, D)
    new_c, new_n, new_m, new_lh, new_rs = new_states
    return (y_prompt, y_sample, new_c, new_n[:, :, :, :, 0, :], new_m[:, :, :, :, 0, 0], new_lh, new_rs)
```

```python
import functools

import jax
import jax.numpy as jnp
from jax import lax
from jax.experimental import pallas as pl
from jax.experimental.pallas import tpu as pltpu

F32 = jnp.float32
BF16 = jnp.bfloat16

D = 1024
DEPTH = 2
N_PROMPT_SEQ = 16
T_PROMPT = 256
N_SAMPLE_SEQ = 2
T_SAMPLE = 4096
ROWS_PROMPT = N_PROMPT_SEQ * T_PROMPT
ROWS = ROWS_PROMPT + N_SAMPLE_SEQ * T_SAMPLE
GROUP_ROWS = 4096
GRID_W = 64
CHUNK = 256
EPS = 1e-6
F32_TINY = 1.1754944e-38
M_HEADS = 4
M_DH = 256
R_HEADS = 8
R_DH = 128
L_BLOCKS = 8
L_BW = 128
L_C = 8.0
ROPE_BASE = 10000.0
N_EXPERTS = 16
CAP_PROMPT = 2 * T_PROMPT // N_EXPERTS
CAP_SAMPLE = 2 * T_SAMPLE // N_EXPERTS
ROWS_PER_EXPERT = N_PROMPT_SEQ * CAP_PROMPT + N_SAMPLE_SEQ * CAP_SAMPLE
N_GATE_COLS = 16
LANES = 128
SUBLANES = 8
VMEM_LIMIT = 56 * 2 ** 20

COL_MQ, COL_MV, COL_MO, COL_LX, COL_LZ, COL_RQ, COL_RV, COL_RG = (i * D for i in range(8))
D_MAIN = 8 * D
ROW_MK, ROW_RK = 0, D
D_KT = 2 * D


def _cparams(sem, vmem=None):
    return pltpu.CompilerParams(dimension_semantics=sem, vmem_limit_bytes=vmem)


def _dot(a, b):
    return jnp.dot(a, b, preferred_element_type=F32)


def _dot_nt(a, b):
    return lax.dot_general(a, b, (((1,), (1,)), ((), ())), preferred_element_type=F32)


def _split3(x):
    a = x.astype(BF16)
    r = x - a.astype(F32)
    b = r.astype(BF16)
    c = (r - b.astype(F32)).astype(BF16)
    return a, b, c


def _log_sigmoid(x):
    return jnp.minimum(x, 0.0) - jnp.log1p(jnp.exp(-jnp.abs(x)))


def _sigmoid(x):
    return 0.5 * jnp.tanh(0.5 * x) + 0.5


def _rms(x):
    return x * lax.rsqrt(jnp.mean(x * x, axis=-1, keepdims=True) + EPS)


def _chunk(c):
    return pl.ds(pl.multiple_of(c * CHUNK, CHUNK), CHUNK)


def _alias_prev(args, in_specs, prev):
    aliases = {}
    for out_idx, arr in enumerate(prev):
        if arr is not None:
            aliases[len(args)] = out_idx
            args.append(arr)
            in_specs.append(pl.BlockSpec(memory_space=pl.ANY))
    return aliases, len(aliases)


def _ada_kernel(c_ref, w_ref, b_ref, o_ref):
    c = c_ref[...]
    s = (c * jax.nn.sigmoid(c)).astype(BF16)
    o_ref[0] = _dot(s, w_ref[0].astype(BF16)) + b_ref[0]


def _ada_call(cond8, w_ada, b_ada):
    tn = 1536
    return pl.pallas_call(
        _ada_kernel,
        grid=(DEPTH, 6 * D // tn),
        in_specs=[pl.BlockSpec((8, D), lambda l, j: (0, 0)),
                  pl.BlockSpec((1, D, tn), lambda l, j: (l, 0, j)),
                  pl.BlockSpec((1, 1, tn), lambda l, j: (l, 0, j))],
        out_specs=pl.BlockSpec((1, 8, tn), lambda l, j: (l, 0, j)),
        out_shape=jax.ShapeDtypeStruct((DEPTH, 8, 6 * D), F32),
        compiler_params=_cparams(("arbitrary", "arbitrary"), VMEM_LIMIT),
        name="ada",
    )(cond8, w_ada, b_ada.reshape(DEPTH, 1, 6 * D))


def _norm1_kernel(x_ref, g_ref, mod_ref, wg_ref, h_ref, gate_ref):
    y = _rms(x_ref[...]) * g_ref[...]
    h = (y * (1.0 + mod_ref[0, 1:2, :]) + mod_ref[0, 0:1, :]).astype(BF16)
    h_ref[...] = h
    gate_ref[...] = _dot(h, wg_ref[...])


def _norm1_call(x, g, mod, w_gate):
    tm = 512
    return pl.pallas_call(
        _norm1_kernel,
        grid=(ROWS // tm,),
        in_specs=[pl.BlockSpec((tm, D), lambda i: (i, 0)),
                  pl.BlockSpec((1, D), lambda i: (0, 0)),
                  pl.BlockSpec((1, 6, D), lambda i: (i * tm // GROUP_ROWS, 0, 0)),
                  pl.BlockSpec((D, LANES), lambda i: (0, 0))],
        out_specs=[pl.BlockSpec((tm, D), lambda i: (i, 0)),
                   pl.BlockSpec((tm, LANES), lambda i: (i, 0))],
        out_shape=[jax.ShapeDtypeStruct((ROWS, D), BF16),
                   jax.ShapeDtypeStruct((ROWS, LANES), F32)],
        compiler_params=_cparams(("arbitrary",), VMEM_LIMIT),
        name="norm1",
    )(x, g.reshape(1, D), mod, w_gate)


def _mm_kernel(x_ref, w_ref, o_ref):
    o_ref[...] = _dot(x_ref[...], w_ref[...]).astype(o_ref.dtype)


def _inproj_call(h, w_main):
    tm, tn = 1024, 1024
    return pl.pallas_call(
        _mm_kernel,
        grid=(D_MAIN // tn, ROWS // tm),
        in_specs=[pl.BlockSpec((tm, D), lambda j, i: (i, 0)),
                  pl.BlockSpec((D, tn), lambda j, i: (0, j))],
        out_specs=pl.BlockSpec((tm, tn), lambda j, i: (i, j)),
        out_shape=jax.ShapeDtypeStruct((ROWS, D_MAIN), BF16),
        compiler_params=_cparams(("arbitrary", "arbitrary"), VMEM_LIMIT),
        name="inproj",
    )(h, w_main)


def _mm_t_kernel(wt_ref, x_ref, o_ref):
    o_ref[...] = _dot_nt(wt_ref[...], x_ref[...]).astype(o_ref.dtype)


def _inproj_t_call(h, w_kt):
    tm, tn = 1024, 1024
    return pl.pallas_call(
        _mm_t_kernel,
        grid=(D_KT // tn, ROWS // tm),
        in_specs=[pl.BlockSpec((tn, D), lambda j, i: (j, 0)),
                  pl.BlockSpec((tm, D), lambda j, i: (i, 0))],
        out_specs=pl.BlockSpec((tn, tm), lambda j, i: (j, i)),
        out_shape=jax.ShapeDtypeStruct((D_KT, ROWS), BF16),
        compiler_params=_cparams(("arbitrary", "arbitrary"), VMEM_LIMIT),
        name="inproj_t",
    )(w_kt, h)


M_AUG = M_DH + LANES
M_GROUP = 1


def _mlstm_kernel(*refs, n_chunks, has_init, emit_state, n_prev):
    q_ref, kt_ref, v_ref, o_ref, g_ref, ng_ref = refs[:6]
    pos = 6
    if has_init:
        c0_ref, m0_ref = refs[pos:pos + 2]
        pos += 2
    pos += n_prev
    y_ref = refs[pos]
    pos += 1
    if emit_state:
        co_ref, no_ref, mo_ref = refs[pos:pos + 3]
        pos += 3
    caug_sc, m_sc, hf_sc, hb_sc = refs[pos:]

    zero_state = (not has_init) and n_chunks == 1
    if has_init:
        for g in range(M_GROUP):
            caug_sc[g] = c0_ref[0, :, g]
            m_sc[g] = m0_ref[0, :, g]
    else:
        if not zero_state:
            caug_sc[...] = jnp.zeros_like(caug_sc)
        m_sc[...] = jnp.zeros_like(m_sc)

    ri = lax.broadcasted_iota(jnp.int32, (CHUNK, CHUNK), 0)
    ci = lax.broadcasted_iota(jnp.int32, (CHUNK, CHUNK), 1)
    r8 = lax.broadcasted_iota(jnp.int32, (SUBLANES, CHUNK), 0)
    ones_b = jnp.ones((CHUNK, LANES), BF16)
    allowed = [ri >= ci, ci >= ri]
    allowed_b = [jnp.where(a, 1.0, 0.0).astype(BF16) for a in allowed]
    tri = [jnp.where(ri <= ci, 1.0, 0.0).astype(BF16), jnp.where(ri >= ci, 1.0, 0.0).astype(BF16)]

    def step(g, d, c, m_t):
        rows = _chunk(c)
        hcols = slice(g * M_DH, (g + 1) * M_DH)
        i_idx, f_idx = (0, 1) if d == 0 else (2, 3)
        g8 = g_ref[g, :, rows]
        ls = jnp.where(r8 == f_idx, _log_sigmoid(g8), g8)
        cum = sum(_dot(p, tri[d]) for p in _split3(ls))
        f_row = ls[f_idx:f_idx + 1]
        b_row = cum[f_idx:f_idx + 1]
        g_row = ls[i_idx:i_idx + 1] - b_row
        gmat = jnp.where(allowed[d], jnp.broadcast_to(g_row, (CHUNK, CHUNK)), -jnp.inf)
        mp_col = jnp.max(gmat, axis=-1, keepdims=True)
        mp = jnp.broadcast_to(mp_col, (CHUNK, LANES))
        fp = _split3(f_row)
        bc = _dot(allowed_b[d] * fp[0], ones_b) + _dot(allowed_b[d] * fp[1], ones_b)

        q = q_ref[rows, hcols]
        kt = kt_ref[hcols, rows]
        vaug = jnp.concatenate([v_ref[rows, hcols], ones_b], axis=1)
        s1 = _dot(q, kt) * jnp.exp(gmat - mp_col)
        sv1 = _dot(s1.astype(BF16), vaug)
        b_last_11 = b_row[:, CHUNK - 1:CHUNK] if d == 0 else b_row[:, 0:1]
        log_k = b_last_11 + g_row
        a_11 = jnp.max(log_k, axis=-1, keepdims=True)
        kw1 = (kt.astype(F32) * jnp.exp(log_k - a_11)).astype(BF16)
        u1 = _dot(kw1, vaug)
        b_last = jnp.broadcast_to(b_last_11, (SUBLANES, LANES))
        a_t = jnp.broadcast_to(a_11, (SUBLANES, LANES))

        m_b = jnp.broadcast_to(m_t[0:1, :], (CHUNK, LANES))
        mi = jnp.maximum(m_b, mp)
        r = jnp.exp(mp - mi)
        w_prev = jnp.exp(m_b - mi)
        floor = jnp.exp(-(bc + mi))
        if zero_state:
            qc = None
            den = r * sv1[:, M_DH:]
        else:
            qc = _dot(q, caug_sc[g, d].astype(BF16))
            den = r * sv1[:, M_DH:] + w_prev * qc[:, M_DH:]
        inv = 1.0 / jnp.maximum(jnp.abs(den), floor)
        h_sc = hf_sc if d == 0 else hb_sc
        for blk in range(M_DH // LANES):
            cols = slice(blk * LANES, (blk + 1) * LANES)
            num = r * sv1[:, cols]
            if not zero_state:
                num = num + w_prev * qc[:, cols]
            h_sc[rows, slice(g * M_DH + blk * LANES, g * M_DH + (blk + 1) * LANES)] = num * inv

        m_new = jnp.maximum(b_last + m_t, a_t)
        w_c = jnp.exp(b_last + m_t - m_new)[0:1, :]
        w_u = jnp.exp(a_t - m_new)[0:1, :]
        for blk in range(M_AUG // LANES):
            cols = slice(blk * LANES, (blk + 1) * LANES)
            if zero_state:
                caug_sc[g, d, :, cols] = w_u * u1[:, cols]
            else:
                caug_sc[g, d, :, cols] = w_c * caug_sc[g, d, :, cols] + w_u * u1[:, cols]
        return m_new

    def scan_body(c, carry):
        new = []
        for g in range(M_GROUP):
            new.append(step(g, 0, c, carry[2 * g]))
            new.append(step(g, 1, n_chunks - 1 - c, carry[2 * g + 1]))
        return tuple(new)

    m_init = tuple(jnp.broadcast_to(m_sc[g, d], (SUBLANES, LANES))
                   for g in range(M_GROUP) for d in range(2))
    m_fin = lax.fori_loop(0, n_chunks, scan_body, m_init)
    for g in range(M_GROUP):
        for d in range(2):
            m_sc[g, d] = m_fin[2 * g + d][0:1, :]

    def out_body(c, carry):
        rows = _chunk(c)
        for g in range(M_GROUP):
            hcols = slice(g * M_DH, (g + 1) * M_DH)
            y = _rms(hf_sc[rows, hcols] + hb_sc[rows, hcols]) * ng_ref[:, hcols]
            y_ref[rows, hcols] = (y * _sigmoid(o_ref[rows, hcols].astype(F32))).astype(y_ref.dtype)
        return carry

    lax.fori_loop(0, n_chunks, out_body, 0)

    if emit_state:
        for g in range(M_GROUP):
            co_ref[0, 0, :, g] = caug_sc[g, :, :, 0:M_DH]
            for d in range(2):
                no_ref[0, 0, d, g] = caug_sc[g, d, :, M_DH:].T[0:1, :]
            mo_ref[0, 0, :, g] = m_sc[g]


def _mlstm_call(proj, proj_t, grow, norm_g, state, prev, *, t, n_seq, row_block0, layer, emit_state):
    has_init = state is not None

    gw = M_GROUP * M_DH
    big = dict(pipeline_mode=pl.Buffered(1)) if t * gw * 2 > 2 ** 21 else {}

    def col(base):
        return lambda s, h: (row_block0 + s, base // gw + h)

    st_caug = pl.BlockSpec((1, 2, M_GROUP, M_DH, M_AUG), lambda s, h: (s, 0, h, 0, 0))
    st_m = pl.BlockSpec((1, 2, M_GROUP, 1, LANES), lambda s, h: (s, 0, h, 0, 0))
    so_c = pl.BlockSpec((1, 1, 2, M_GROUP, M_DH, M_DH), lambda s, h: (s, layer, 0, h, 0, 0))
    so_n = pl.BlockSpec((1, 1, 2, M_GROUP, 1, M_DH), lambda s, h: (s, layer, 0, h, 0, 0))
    so_m = pl.BlockSpec((1, 1, 2, M_GROUP, 1, LANES), lambda s, h: (s, layer, 0, h, 0, 0))
    in_specs = [pl.BlockSpec((t, gw), col(COL_MQ), **big),
                pl.BlockSpec((gw, t), lambda s, h: (ROW_MK // gw + h, row_block0 + s), **big),
                pl.BlockSpec((t, gw), col(COL_MV), **big),
                pl.BlockSpec((t, gw), col(COL_MO), **big),
                pl.BlockSpec((M_GROUP, SUBLANES, t), lambda s, h: (h, 0, row_block0 + s)),
                pl.BlockSpec((1, gw), lambda s, h: (0, h))]
    args = [proj, proj_t, proj, proj, grow, norm_g.reshape(1, D)]
    if has_init:
        in_specs += [st_caug, st_m]
        args += list(state)
    out_specs = [pl.BlockSpec((t, gw), lambda s, h: (row_block0 + s, h))]
    out_shape = [jax.ShapeDtypeStruct((ROWS, D), BF16)]
    if emit_state:
        out_specs += [so_c, so_n, so_m]
        out_shape += [jax.ShapeDtypeStruct((n_seq, DEPTH, 2, M_HEADS, M_DH, M_DH), F32),
                      jax.ShapeDtypeStruct((n_seq, DEPTH, 2, M_HEADS, 1, M_DH), F32),
                      jax.ShapeDtypeStruct((n_seq, DEPTH, 2, M_HEADS, 1, LANES), F32)]
    aliases, n_prev = _alias_prev(args, in_specs, prev)
    return pl.pallas_call(
        functools.partial(_mlstm_kernel, n_chunks=t // CHUNK, has_init=has_init,
                          emit_state=emit_state, n_prev=n_prev),
        grid=(n_seq, M_HEADS // M_GROUP),
        in_specs=in_specs,
        out_specs=out_specs,
        out_shape=out_shape,
        input_output_aliases=aliases,
        scratch_shapes=[pltpu.VMEM((M_GROUP, 2, M_DH, M_AUG), F32),
                        pltpu.VMEM((M_GROUP, 2, 1, LANES), F32),
                        pltpu.VMEM((t, gw), F32),
                        pltpu.VMEM((t, gw), F32)],
        compiler_params=_cparams(("arbitrary", "arbitrary"), VMEM_LIMIT),
        name="mlstm",
    )(*args)


R_GROUP = 2

def _ret_kernel(*refs, n_chunks, rope, has_init, emit_state, n_prev):
    q_ref, kt_ref, v_ref, g_ref, dec_ref, ng_ref = refs[:6]
    pos = 6
    if rope:
        cos_ref, sin_ref, cos_t_ref, sin_t_ref = refs[pos:pos + 4]
        pos += 4
    if has_init:
        s0_ref = refs[pos]
        pos += 1
    pos += n_prev
    y_ref = refs[pos]
    pos += 1
    if emit_state:
        so_ref = refs[pos]
        pos += 1
    s_sc, of_sc, ob_sc, ktr_sc = refs[pos:pos + 4]
    qr_sc = refs[pos + 4] if rope else None

    zero_state = (not has_init) and n_chunks == 1
    if has_init:
        for g in range(R_GROUP):
            s_sc[g] = s0_ref[0, :, g]
    elif not zero_state:
        s_sc[...] = jnp.zeros_like(s_sc)

    ri = lax.broadcasted_iota(jnp.int32, (CHUNK, CHUNK), 0)
    ci = lax.broadcasted_iota(jnp.int32, (CHUNK, CHUNK), 1)
    quarter = R_DH // 4
    if rope:
        fr = lax.broadcasted_iota(jnp.int32, (R_DH, R_DH), 0)
        fc = lax.broadcasted_iota(jnp.int32, (R_DH, R_DH), 1)
        partner = fc + jnp.where((fc % (2 * quarter)) < quarter, quarter, -quarter)
        perm_b = jnp.where(fr == partner, 1.0, 0.0).astype(BF16)

    def prep_body(c, carry):
        rows = _chunk(c)
        for g in range(R_GROUP):
            hcols = slice(g * R_DH, (g + 1) * R_DH)
            ktf = kt_ref[hcols, rows].astype(F32) * (R_DH ** -0.5)
            if rope:
                swapped = jnp.concatenate([ktf[quarter:2 * quarter], ktf[0:quarter],
                                           ktf[3 * quarter:], ktf[2 * quarter:3 * quarter]], axis=0)
                ktf = ktf * cos_t_ref[:, rows] + swapped * sin_t_ref[:, rows]
                q = q_ref[rows, hcols]
                qr = q.astype(F32) * cos_ref[rows, :] + _dot(q, perm_b) * sin_ref[rows, :]
                qr_sc[rows, hcols] = qr.astype(BF16)
            ktr_sc[hcols, rows] = ktf.astype(BF16)
        return carry

    lax.fori_loop(0, n_chunks, prep_body, 0, unroll=min(n_chunks, 2))
    qsrc = qr_sc if rope else q_ref

    lane = lax.broadcasted_iota(jnp.int32, (1, CHUNK), 1).astype(F32)
    row_f = lax.broadcasted_iota(jnp.int32, (CHUNK, R_DH), 0).astype(F32)
    consts = []
    for gd in range(2 * R_GROUP):
        g, d = divmod(gd, 2)
        lg = -jnp.exp(dec_ref[g, d:d + 1, :])
        lg11 = lg[:, 0:1]
        diff = (ri - ci) if d == 0 else (ci - ri)
        causal = diff >= 0
        dmat = jnp.where(causal, jnp.exp(lg11 * jnp.where(causal, diff, 0).astype(F32)), 0.0)
        if d == 0:
            q_dec = jnp.exp(lg * (row_f + 1.0))
            k_dec = jnp.exp(lg11 * (CHUNK - 1.0 - lane))
        else:
            q_dec = jnp.exp(lg * (CHUNK - row_f))
            k_dec = jnp.exp(lg11 * lane)
        g_chunk = jnp.exp(lg * float(CHUNK))
        consts.append((dmat, q_dec, k_dec, g_chunk))

    def step(g, d, c):
        dmat, q_dec, k_dec, g_chunk = consts[2 * g + d]
        rows = _chunk(c)
        hcols = slice(g * R_DH, (g + 1) * R_DH)
        q = qsrc[rows, hcols]
        kt = ktr_sc[hcols, rows]
        v = v_ref[rows, hcols]
        s = _dot(q, kt) * dmat
        o = _dot(s.astype(BF16), v)
        if not zero_state:
            o = o + q_dec * _dot(q, s_sc[g, d].astype(BF16))
        if d == 0:
            of_sc[rows, hcols] = o
        else:
            ob_sc[rows, hcols] = o
        kd = (kt.astype(F32) * k_dec).astype(BF16)
        if zero_state:
            s_sc[g, d] = _dot(kd, v)
        else:
            s_sc[g, d] = g_chunk * s_sc[g, d] + _dot(kd, v)

    def scan_body(c, carry):
        for g in range(R_GROUP):
            step(g, 0, c)
            step(g, 1, n_chunks - 1 - c)
        return carry

    lax.fori_loop(0, n_chunks, scan_body, 0, unroll=min(n_chunks, 2))

    def out_body(c, carry):
        rows = _chunk(c)
        for g in range(R_GROUP):
            hcols = slice(g * R_DH, (g + 1) * R_DH)
            y = _rms(of_sc[rows, hcols] + ob_sc[rows, hcols]) * ng_ref[:, hcols]
            gate = g_ref[rows, hcols].astype(F32)
            y_ref[rows, hcols] = (y * (gate * _sigmoid(gate))).astype(y_ref.dtype)
        return carry

    lax.fori_loop(0, n_chunks, out_body, 0, unroll=min(n_chunks, 2))

    if emit_state:
        for g in range(R_GROUP):
            so_ref[0, 0, :, g] = s_sc[g]


def _ret_call(proj, proj_t, decay, norm_g, rope_tabs, state, prev, *, t, n_seq, row_block0, layer, emit_state):
    has_init = state is not None
    rope = rope_tabs is not None

    gw = R_GROUP * R_DH

    def col(base):
        return lambda s, h: (row_block0 + s, base // gw + h)

    st_s = pl.BlockSpec((1, 2, R_GROUP, R_DH, R_DH), lambda s, h: (s, 0, h, 0, 0))
    in_specs = [pl.BlockSpec((t, gw), col(COL_RQ)),
                pl.BlockSpec((gw, t), lambda s, h: (ROW_RK // gw + h, row_block0 + s)),
                pl.BlockSpec((t, gw), col(COL_RV)),
                pl.BlockSpec((t, gw), col(COL_RG)),
                pl.BlockSpec((R_GROUP, 2, LANES), lambda s, h: (h, 0, 0)),
                pl.BlockSpec((1, gw), lambda s, h: (0, h))]
    args = [proj, proj_t, proj, proj, decay, norm_g.reshape(1, D)]
    if rope:
        in_specs += [pl.BlockSpec((t, R_DH), lambda s, h: (0, 0))] * 2
        in_specs += [pl.BlockSpec((R_DH, t), lambda s, h: (0, 0))] * 2
        args += list(rope_tabs)
    if has_init:
        in_specs.append(st_s)
        args.append(state)
    out_specs = [pl.BlockSpec((t, gw), lambda s, h: (row_block0 + s, h))]
    out_shape = [jax.ShapeDtypeStruct((ROWS, D), BF16)]
    if emit_state:
        out_specs.append(pl.BlockSpec((1, 1, 2, R_GROUP, R_DH, R_DH), lambda s, h: (s, layer, 0, h, 0, 0)))
        out_shape.append(jax.ShapeDtypeStruct((n_seq, DEPTH, 2, R_HEADS, R_DH, R_DH), F32))
    aliases, n_prev = _alias_prev(args, in_specs, prev)
    scratch = [pltpu.VMEM((R_GROUP, 2, R_DH, R_DH), F32),
               pltpu.VMEM((t, gw), F32),
               pltpu.VMEM((t, gw), F32),
               pltpu.VMEM((gw, t), BF16)]
    if rope:
        scratch.append(pltpu.VMEM((t, gw), BF16))
    return pl.pallas_call(
        functools.partial(_ret_kernel, n_chunks=t // CHUNK, rope=rope, has_init=has_init,
                          emit_state=emit_state, n_prev=n_prev),
        grid=(n_seq, R_HEADS // R_GROUP),
        in_specs=in_specs,
        out_specs=out_specs,
        out_shape=out_shape,
        input_output_aliases=aliases,
        scratch_shapes=scratch,
        compiler_params=_cparams(("arbitrary", "arbitrary"), VMEM_LIMIT),
        name="retention",
    )(*args)


LRU_SLAB = 64


def _tile_scan(a, b, reverse):
    row = lax.broadcasted_iota(jnp.int32, (SUBLANES, L_BW), 0)
    for k in (1, 2, 4):
        if reverse:
            keep = row < SUBLANES - k
            shift = SUBLANES - k
        else:
            keep = row >= k
            shift = k
        a_sh = jnp.where(keep, pltpu.roll(a, shift, 0), 1.0)
        b_sh = jnp.where(keep, pltpu.roll(b, shift, 0), 0.0)
        b = a * b_sh + b
        a = a * a_sh
    return a, b


def _lru_kernel(*refs, t, has_init, emit_state, n_prev):
    lx_ref, lz_ref, cw_ref, cb_ref, wr_ref, wi_ref, br_ref, bi_ref, lam_ref = refs[:9]
    pos = 9
    if has_init:
        h0_ref = refs[pos]
        pos += 1
    pos += n_prev
    y_ref = refs[pos]
    pos += 1
    if emit_state:
        hfin_ref = refs[pos]
        pos += 1
    xpad_sc, a_sc, b_sc = refs[pos:]
    rc = min(t, 256)

    xpad_sc[0:8, :] = jnp.zeros((8, L_BW), F32)
    xpad_sc[t + 8:t + 16, :] = jnp.zeros((8, L_BW), F32)

    def pad_body(c, carry):
        r0 = pl.multiple_of(c * rc, rc)
        xpad_sc[pl.ds(r0 + 8, rc), :] = lx_ref[pl.ds(r0, rc), :].astype(F32)
        return carry

    lax.fori_loop(0, t // rc, pad_body, 0)

    wr = [(0.5 * wr_ref[d, 0]).astype(BF16) for d in range(2)]
    wi = [(0.5 * wi_ref[d, 0]).astype(BF16) for d in range(2)]
    half_br = [0.5 * br_ref[d:d + 1, :] for d in range(2)]
    half_bi = [0.5 * bi_ref[d:d + 1, :] for d in range(2)]
    half_c = []
    for d in range(2):
        lam = lam_ref[d:d + 1, :]
        softplus_neg = jnp.maximum(-lam, 0.0) + jnp.log1p(jnp.exp(-jnp.abs(lam)))
        half_c.append(-0.5 * L_C * softplus_neg)

    def gate_body(c, carry):
        r0 = pl.multiple_of(c * rc, rc)
        xe = xpad_sc[pl.ds(r0, rc + 16), :]
        n = rc + 16
        u = (cb_ref[...] + cw_ref[0:1, :] * pltpu.roll(xe, 2, 0)[8:8 + rc]
             + cw_ref[1:2, :] * pltpu.roll(xe, 1, 0)[8:8 + rc]
             + cw_ref[2:3, :] * xe[8:8 + rc]
             + cw_ref[3:4, :] * pltpu.roll(xe, n - 1, 0)[8:8 + rc])
        ub = u.astype(BF16)
        half_u = 0.5 * u
        for d in range(2):
            tr = jnp.tanh(_dot(ub, wr[d]) + half_br[d])
            ti = jnp.tanh(_dot(ub, wi[d]) + half_bi[d])
            log_a = half_c[d] * tr + half_c[d]
            a = jnp.exp(log_a)
            z = jnp.tanh(log_a) * (-1.0 - a * a)
            mult = z * lax.rsqrt(jnp.maximum(z, F32_TINY))
            a_sc[d, pl.ds(r0, rc), :] = a
            b_sc[d, pl.ds(r0, rc), :] = mult * (ti * half_u + half_u)
        return carry

    lax.fori_loop(0, t // rc, gate_body, 0)

    n_slabs = t // LRU_SLAB
    tiles = LRU_SLAB // SUBLANES

    row8 = lax.broadcasted_iota(jnp.int32, (SUBLANES, L_BW), 0)

    def bcast_row(x, r):
        return jnp.broadcast_to(x[r:r + 1, :], (SUBLANES, L_BW))

    def slab_scan(d, r0, c_in):
        reverse = d == 1
        edge = 0 if reverse else SUBLANES - 1
        local = []
        spa = jnp.ones((SUBLANES, L_BW), F32)
        shl = jnp.zeros((SUBLANES, L_BW), F32)
        for k in range(tiles):
            rows = pl.ds(r0 + k * SUBLANES, SUBLANES)
            pa, hl = _tile_scan(a_sc[d, rows, :], b_sc[d, rows, :], reverse)
            local.append((rows, pa, hl))
            spa = jnp.where(row8 == k, bcast_row(pa, edge), spa)
            shl = jnp.where(row8 == k, bcast_row(hl, edge), shl)
        cpa, chl = _tile_scan(spa, shl, reverse)
        after = chl + cpa * c_in
        for k, (rows, pa, hl) in enumerate(local):
            prev = k + 1 if reverse else k - 1
            cin = c_in if (prev < 0 or prev >= tiles) else bcast_row(after, prev)
            b_sc[d, rows, :] = hl + pa * cin
        return bcast_row(after, 0 if reverse else tiles - 1)

    def scan_body(c, carry):
        cf, cbk = carry
        cf = slab_scan(0, pl.multiple_of(c * LRU_SLAB, LRU_SLAB), cf)
        cbk = slab_scan(1, pl.multiple_of((n_slabs - 1 - c) * LRU_SLAB, LRU_SLAB), cbk)
        return cf, cbk

    if has_init:
        cf0 = jnp.broadcast_to(h0_ref[0, 0:1, :], (SUBLANES, L_BW))
        cb0 = jnp.broadcast_to(h0_ref[0, 1:2, :], (SUBLANES, L_BW))
    else:
        cf0 = jnp.zeros((SUBLANES, L_BW), F32)
        cb0 = jnp.zeros((SUBLANES, L_BW), F32)
    cf, cbk = lax.fori_loop(0, n_slabs, scan_body, (cf0, cb0))
    if emit_state:
        hfin_ref[0, 0, 0:1, :] = cf[0:1, :]
        hfin_ref[0, 0, 1:2, :] = cbk[0:1, :]

    def out_body(c, carry):
        rows = pl.ds(pl.multiple_of(c * rc, rc), rc)
        z = lz_ref[rows, :].astype(F32)
        gelu = 0.5 * z * (1.0 + jnp.tanh(0.7978845608028654 * (z + 0.044715 * (z * z * z))))
        y_ref[rows, :] = ((b_sc[0, rows, :] + b_sc[1, rows, :]) * gelu).astype(y_ref.dtype)
        return carry

    lax.fori_loop(0, t // rc, out_body, 0)


def _lru_call(proj, p, state, prev, *, t, n_seq, row_block0, layer, emit_state):
    has_init = state is not None

    def col(base):
        return lambda s, b: (row_block0 + s, base // L_BW + b)

    vec2 = pl.BlockSpec((2, L_BW), lambda s, b: (0, b))
    st = pl.BlockSpec((1, 2, L_BW), lambda s, b: (s, 0, b))
    in_specs = [pl.BlockSpec((t, L_BW), col(COL_LX)),
                pl.BlockSpec((t, L_BW), col(COL_LZ)),
                pl.BlockSpec((4, L_BW), lambda s, b: (0, b)),
                pl.BlockSpec((1, L_BW), lambda s, b: (0, b)),
                pl.BlockSpec((2, 1, L_BW, L_BW), lambda s, b: (0, b, 0, 0)),
                pl.BlockSpec((2, 1, L_BW, L_BW), lambda s, b: (0, b, 0, 0)),
                vec2, vec2, vec2]
    args = [proj, proj, p['conv_w'], p['conv_b'].reshape(1, D), p['wr'], p['wi'],
            p['br'], p['bi'], p['lam']]
    if has_init:
        in_specs.append(st)
        args.append(state)
    out_specs = [pl.BlockSpec((t, L_BW), lambda s, b: (row_block0 + s, b))]
    out_shape = [jax.ShapeDtypeStruct((ROWS, D), BF16)]
    if emit_state:
        out_specs.append(pl.BlockSpec((1, 1, 2, L_BW), lambda s, b: (s, layer, 0, b)))
        out_shape.append(jax.ShapeDtypeStruct((n_seq, DEPTH, 2, D), F32))
    aliases, n_prev = _alias_prev(args, in_specs, prev)
    return pl.pallas_call(
        functools.partial(_lru_kernel, t=t, has_init=has_init, emit_state=emit_state, n_prev=n_prev),
        grid=(n_seq, L_BLOCKS),
        in_specs=in_specs,
        out_specs=out_specs,
        out_shape=out_shape,
        input_output_aliases=aliases,
        scratch_shapes=[pltpu.VMEM((t + 16, L_BW), F32),
                        pltpu.VMEM((2, t, L_BW), F32),
                        pltpu.VMEM((2, t, L_BW), F32)],
        compiler_params=_cparams(("arbitrary", "arbitrary"), VMEM_LIMIT),
        name="rglru",
    )(*args)


def _merge_kernel(h_ref, ym_ref, yl_ref, yr_ref, x_ref, mod_ref, wm_ref, bm_ref, wb_ref, wo_ref,
                  n2_ref, rwh_ref, rwl_ref, xo_ref, h2_ref, lg_ref):
    h = h_ref[...]
    merged = None
    for k, y_ref in enumerate((ym_ref, yl_ref, yr_ref)):
        gate = jax.nn.sigmoid(_dot(h, wm_ref[:, k * D:(k + 1) * D]) + bm_ref[:, k * D:(k + 1) * D])
        term = gate * _dot(y_ref[...], wb_ref[k])
        merged = term if merged is None else merged + term
    mix = _dot(merged.astype(BF16), wo_ref[...])
    xn = x_ref[...] + mod_ref[0, 2:3, :] * mix
    xo_ref[...] = xn
    hn = _rms(xn) * n2_ref[...] * (1.0 + mod_ref[0, 4:5, :]) + mod_ref[0, 3:4, :]
    hi = hn.astype(BF16)
    h2_ref[...] = hi
    lo = (hn - hi.astype(F32)).astype(BF16)
    lg_ref[...] = _dot(hi, rwh_ref[...]) + _dot(lo, rwh_ref[...]) + _dot(hi, rwl_ref[...])


def _merge_call(h, ym, yl, yr, x, mod, wm, bm, wb, wo, n2g, rwh, rwl):
    tm = 256
    row = lambda i: (i, 0)
    const2 = lambda i: (0, 0)
    return pl.pallas_call(
        _merge_kernel,
        grid=(ROWS // tm,),
        in_specs=[pl.BlockSpec((tm, D), row),
                  pl.BlockSpec((tm, D), row),
                  pl.BlockSpec((tm, D), row),
                  pl.BlockSpec((tm, D), row),
                  pl.BlockSpec((tm, D), row),
                  pl.BlockSpec((1, 6, D), lambda i: (i * tm // GROUP_ROWS, 0, 0)),
                  pl.BlockSpec((D, 3 * D), const2),
                  pl.BlockSpec((1, 3 * D), const2),
                  pl.BlockSpec((3, D, D), lambda i: (0, 0, 0)),
                  pl.BlockSpec((D, D), const2),
                  pl.BlockSpec((1, D), const2),
                  pl.BlockSpec((D, LANES), const2),
                  pl.BlockSpec((D, LANES), const2)],
        out_specs=[pl.BlockSpec((tm, D), row),
                   pl.BlockSpec((tm, D), row),
                   pl.BlockSpec((tm, LANES), row)],
        out_shape=[jax.ShapeDtypeStruct((ROWS, D), F32),
                   jax.ShapeDtypeStruct((ROWS, D), BF16),
                   jax.ShapeDtypeStruct((ROWS, LANES), F32)],
        compiler_params=_cparams(("arbitrary",), VMEM_LIMIT),
        name="merge",
    )(h, ym, yl, yr, x, mod, wm, bm.reshape(1, 3 * D), wb, wo, n2g.reshape(1, D), rwh, rwl)


EXPERT_TM = 512
assert N_PROMPT_SEQ * CAP_PROMPT == EXPERT_TM and CAP_SAMPLE == EXPERT_TM


def _expert_kernel(xp_ref, xs_ref, g_ref, mod_ref, w1_ref, w3_ref, w2_ref, yp_ref, ys_ref,
                   w1_sc, w3_sc, w2_sc):
    m = pl.program_id(1)

    @pl.when(m == 0)
    def _():
        w1_sc[...] = w1_ref[0, 0].astype(BF16)
        w3_sc[...] = w3_ref[0, 0].astype(BF16)
        w2_sc[...] = w2_ref[0, 0].astype(BF16)

    xs = jnp.where(m == 0, xp_ref[0], xs_ref[0])
    a = _dot(xs, w1_sc[...])
    b = _dot(xs, w3_sc[...])
    mid = (a * jax.nn.sigmoid(a) * b).astype(BF16)

    def down(y_ref):
        y_ref[0] = (_dot(mid, w2_sc[...]) * g_ref[0]) * mod_ref[0, 5:6, :]

    pl.when(m == 0)(functools.partial(down, yp_ref))
    pl.when(m > 0)(functools.partial(down, ys_ref))


def _expert_call(xs_p, xs_s, gv, mod, w1, w3, w2, layer):
    tm = EXPERT_TM
    wspec = pl.BlockSpec((1, 1, D, D), lambda e, m: (layer, e, 0, 0))
    p_spec = pl.BlockSpec((1, tm, D), lambda e, m: (e, 0, 0))
    s_spec = pl.BlockSpec((1, tm, D), lambda e, m: (e, jnp.maximum(m - 1, 0), 0))
    return pl.pallas_call(
        _expert_kernel,
        grid=(N_EXPERTS, ROWS_PER_EXPERT // tm),
        in_specs=[p_spec, s_spec,
                  pl.BlockSpec((1, tm, 1), lambda e, m: (e, m, 0)),
                  pl.BlockSpec((1, 6, D), lambda e, m: (m, 0, 0)),
                  wspec, wspec, wspec],
        out_specs=[p_spec, s_spec],
        out_shape=[jax.ShapeDtypeStruct((N_EXPERTS, tm, D), F32),
                   jax.ShapeDtypeStruct((N_EXPERTS, N_SAMPLE_SEQ * tm, D), F32)],
        scratch_shapes=[pltpu.VMEM((D, D), BF16)] * 3,
        compiler_params=_cparams(("arbitrary", "arbitrary"), VMEM_LIMIT),
        name="experts",
    )(xs_p, xs_s, gv, mod, w1, w3, w2)


PROMPT_SLOTS = N_EXPERTS * CAP_PROMPT


def _prompt_gather_kernel(h_ref, idx_ref, o_ref):
    tok = lax.broadcasted_iota(jnp.int32, (PROMPT_SLOTS, T_PROMPT), 1)
    onehot = jnp.where(tok == idx_ref[0], 1.0, 0.0).astype(BF16)
    o_ref[...] = _dot(onehot, h_ref[...]).astype(o_ref.dtype).reshape(N_EXPERTS, CAP_PROMPT, D)


def _prompt_gather_call(h2, idx_col):
    return pl.pallas_call(
        _prompt_gather_kernel,
        grid=(N_PROMPT_SEQ,),
        in_specs=[pl.BlockSpec((T_PROMPT, D), lambda s: (s, 0)),
                  pl.BlockSpec((1, PROMPT_SLOTS, 1), lambda s: (s, 0, 0))],
        out_specs=pl.BlockSpec((N_EXPERTS, CAP_PROMPT, D), lambda s: (0, s, 0)),
        out_shape=jax.ShapeDtypeStruct((N_EXPERTS, N_PROMPT_SEQ * CAP_PROMPT, D), BF16),
        compiler_params=_cparams(("arbitrary",), VMEM_LIMIT),
        name="prompt_gather",
    )(h2, idx_col)


def _prompt_combine_kernel(x_ref, y_ref, idx_ref, o_ref):
    tok = lax.broadcasted_iota(jnp.int32, (T_PROMPT, PROMPT_SLOTS), 0)
    onehot = jnp.where(tok == idx_ref[0], 1.0, 0.0).astype(BF16)
    y = y_ref[...].reshape(PROMPT_SLOTS, D)
    hi = y.astype(BF16)
    lo = (y - hi.astype(F32)).astype(BF16)
    o_ref[...] = x_ref[...] + (_dot(onehot, hi) + _dot(onehot, lo))


def _prompt_combine_call(x, y_p, idx_row):
    return pl.pallas_call(
        _prompt_combine_kernel,
        grid=(N_PROMPT_SEQ,),
        in_specs=[pl.BlockSpec((T_PROMPT, D), lambda s: (s, 0)),
                  pl.BlockSpec((N_EXPERTS, CAP_PROMPT, D), lambda s: (0, s, 0)),
                  pl.BlockSpec((1, 1, PROMPT_SLOTS), lambda s: (s, 0, 0))],
        out_specs=pl.BlockSpec((T_PROMPT, D), lambda s: (s, 0)),
        out_shape=jax.ShapeDtypeStruct((ROWS, D), F32),
        input_output_aliases={0: 0},
        compiler_params=_cparams(("arbitrary",), VMEM_LIMIT),
        name="prompt_combine",
    )(x, y_p, idx_row)


def _final_norm_kernel(x_ref, g_ref, o_ref):
    o_ref[...] = _rms(x_ref[...]) * g_ref[...]


def _final_norm_call(x, g, row0, n_rows):
    tm = 512
    return pl.pallas_call(
        _final_norm_kernel,
        grid=(n_rows // tm,),
        in_specs=[pl.BlockSpec((tm, D), lambda i: (row0 // tm + i, 0)),
                  pl.BlockSpec((1, D), lambda i: (0, 0))],
        out_specs=pl.BlockSpec((tm, D), lambda i: (i, 0)),
        out_shape=jax.ShapeDtypeStruct((n_rows, D), F32),
        compiler_params=_cparams(("arbitrary",), VMEM_LIMIT),
        name="final_norm",
    )(x, g.reshape(1, D))


def _rope_tables():
    tpos = jnp.arange(T_SAMPLE)
    lane = jnp.arange(R_DH)
    pos = jnp.where(lane[None, :] < R_DH // 2, (tpos // GRID_W)[:, None], (tpos % GRID_W)[:, None])
    n_freq = R_DH // 4
    freqs = jnp.power(ROPE_BASE, -jnp.arange(n_freq, dtype=F32) / n_freq)
    ang = pos.astype(F32) * freqs[lane % n_freq][None, :]
    first = ((lane % (R_DH // 2)) < n_freq)[None, :]
    cos, sin = jnp.cos(ang), jnp.sin(ang)
    sin = jnp.where(first, -sin, sin)
    return cos, sin, cos.T, sin.T


def _route(logits):
    aff = jax.nn.softmax(logits[:, :N_EXPERTS], axis=-1)
    ap = aff[:ROWS_PROMPT].reshape(N_PROMPT_SEQ, T_PROMPT, N_EXPERTS).swapaxes(1, 2)
    as_ = aff[ROWS_PROMPT:].reshape(N_SAMPLE_SEQ, T_SAMPLE, N_EXPERTS).swapaxes(1, 2)
    gp, ip = lax.top_k(ap, CAP_PROMPT)
    gs, is_ = lax.top_k(as_, CAP_SAMPLE)
    is_ = is_ + ROWS_PROMPT + (jnp.arange(N_SAMPLE_SEQ) * T_SAMPLE)[:, None, None]
    rows_s = is_.swapaxes(0, 1).reshape(N_EXPERTS, -1)
    gv = jnp.concatenate([gp.swapaxes(0, 1).reshape(N_EXPERTS, -1),
                          gs.swapaxes(0, 1).reshape(N_EXPERTS, -1)], axis=1)
    return ip.reshape(N_PROMPT_SEQ, PROMPT_SLOTS), rows_s, gv


def _layer(x, mod, p, states, rope_tabs, stacked, layer, prev_states):
    w_in = p['w_in']
    mq, mk, mv, mo, mg, lx, lz, rq, rk, rv, rg = jnp.split(
        w_in, [1024, 2048, 3072, 4096, 4112, 5136, 6160, 7184, 8208, 9232], axis=1)
    w_main = jnp.concatenate([mq, mv, mo, lx, lz, rq, rv, rg], axis=1).astype(BF16)
    w_kt = jnp.concatenate([mk.astype(BF16) * (M_DH ** -0.5), rk.astype(BF16)], axis=1).T
    w_gate = jnp.pad(mg, ((0, 0), (0, LANES - N_GATE_COLS))).astype(BF16)

    h, gates = _norm1_call(x, p['norm1_g'], mod, w_gate)
    proj = _inproj_call(h, w_main)
    proj_t = _inproj_t_call(h, w_kt)

    gb = gates[:, :N_GATE_COLS] + p['mlstm_gate_bias'].reshape(1, N_GATE_COLS)
    grow = gb.reshape(ROWS, 4, M_HEADS).transpose(2, 1, 0)
    grow = jnp.pad(grow, ((0, 0), (0, SUBLANES - 4), (0, 0)))

    sm_c, sm_n, sm_m, s_lh, s_rs = states
    prompt_kw = dict(t=T_PROMPT, n_seq=N_PROMPT_SEQ, row_block0=0, layer=layer, emit_state=True)
    sample_kw = dict(t=T_SAMPLE, n_seq=N_SAMPLE_SEQ, row_block0=ROWS_PROMPT // T_SAMPLE, layer=layer,
                     emit_state=False)
    pc, pn, pm, plh, prs = prev_states

    caug0 = jnp.concatenate(
        [sm_c, jnp.broadcast_to(sm_n[..., None], sm_n.shape + (LANES,))], axis=-1)
    m0 = jnp.broadcast_to(sm_m[..., None, None], (N_SAMPLE_SEQ, 2, M_HEADS, 1, LANES))
    (ym,) = _mlstm_call(proj, proj_t, grow, p['mlstm_norm_g'], (caug0, m0), [None], **sample_kw)
    ym, new_c, new_n, new_m = _mlstm_call(proj, proj_t, grow, p['mlstm_norm_g'], None,
                                          [ym, pc, pn, pm], **prompt_kw)

    lru_p = dict(conv_w=p['lru_conv_w'], conv_b=p['lru_conv_b'], wr=p['lru_wr'], wi=p['lru_wi'],
                 br=p['lru_br'], bi=p['lru_bi'], lam=p['lru_lambda'])
    (yl,) = _lru_call(proj, lru_p, s_lh, [None], **sample_kw)
    yl, new_lh = _lru_call(proj, lru_p, None, [yl, plh], **prompt_kw)

    decay = jnp.broadcast_to(p['ret_decay'].T[:, :, None], (R_HEADS, 2, LANES))
    (yr,) = _ret_call(proj, proj_t, decay, p['ret_norm_g'], rope_tabs, s_rs, [None], **sample_kw)
    yr, new_rs = _ret_call(proj, proj_t, decay, p['ret_norm_g'], None, None, [yr, prs], **prompt_kw)

    rw = jnp.pad(p['router_w'], ((0, 0), (0, LANES - N_EXPERTS)))
    rwh = rw.astype(BF16)
    rwl = (rw - rwh.astype(F32)).astype(BF16)
    xn, h2, logits = _merge_call(h, ym, yl, yr, x, mod, p['w_merge'].astype(BF16), p['b_merge'],
                                 p['w_branch'].astype(BF16), p['w_out'].astype(BF16), p['norm2_g'],
                                 rwh, rwl)

    ip, rows_s, gv = _route(logits)
    xs_p = _prompt_gather_call(h2, ip[:, :, None])
    xs_s = h2[rows_s]
    y_p, y_s = _expert_call(xs_p, xs_s, gv[..., None], mod, stacked['exp_w1'], stacked['exp_w3'],
                            stacked['exp_w2'], layer)
    x_out = xn.at[rows_s.reshape(-1)].add(y_s.reshape(-1, D))
    x_out = _prompt_combine_call(x_out, y_p, ip[:, None, :])
    return x_out, (new_c, new_n, new_m, new_lh, new_rs)


def kernel(x_prompt, x_sample, c, state_mlstm_C, state_mlstm_n, state_mlstm_m, state_lru_h, state_ret_S, c_ctx, w_ada, b_ada, norm1_g, norm2_g, w_in, mlstm_gate_bias, mlstm_norm_g, lru_conv_w, lru_conv_b, lru_wr, lru_br, lru_wi, lru_bi, lru_lambda, ret_decay, ret_norm_g, w_branch, w_merge, b_merge, w_out, router_w, exp_w1, exp_w3, exp_w2, final_g):
    x = jnp.concatenate([x_prompt.reshape(ROWS_PROMPT, D), x_sample.reshape(-1, D)], axis=0)
    cond8 = jnp.concatenate([c_ctx[None, :], c, jnp.zeros((8 - 1 - N_SAMPLE_SEQ, D), F32)], axis=0)
    mod_all = _ada_call(cond8, w_ada, b_ada).reshape(DEPTH, 8, 6, D)
    rope_tabs = _rope_tables()
    stacked = dict(exp_w1=exp_w1, exp_w3=exp_w3, exp_w2=exp_w2)

    new_states = (None,) * 5
    for l in range(DEPTH):
        p = dict(norm1_g=norm1_g[l], norm2_g=norm2_g[l], w_in=w_in[l],
                 mlstm_gate_bias=mlstm_gate_bias[l], mlstm_norm_g=mlstm_norm_g[l],
                 lru_conv_w=lru_conv_w[l], lru_conv_b=lru_conv_b[l], lru_wr=lru_wr[l],
                 lru_br=lru_br[l], lru_wi=lru_wi[l], lru_bi=lru_bi[l], lru_lambda=lru_lambda[l],
                 ret_decay=ret_decay[l], ret_norm_g=ret_norm_g[l], w_branch=w_branch[l],
                 w_merge=w_merge[l], b_merge=b_merge[l], w_out=w_out[l], router_w=router_w[l])
        states = (state_mlstm_C[:, l], state_mlstm_n[:, l], state_mlstm_m[:, l],
                  state_lru_h[:, l], state_ret_S[:, l])
        x, new_states = _layer(x, mod_all[l], p, states, rope_tabs, stacked, l, new_states)

    y_prompt = _final_norm_call(x, final_g, 0, ROWS_PROMPT).reshape(N_PROMPT_SEQ, T_PROMPT, D)
    y_sample = _final_norm_call(x, final_g, ROWS_PROMPT, ROWS - ROWS_PROMPT).reshape(
        N_SAMPLE_SEQ, T_SAMPLE, D)
    new_c, new_n, new_m, new_lh, new_rs = new_states
    return (y_prompt, y_sample, new_c, new_n[:, :, :, :, 0, :], new_m[:, :, :, :, 0, 0], new_lh, new_rs)
```

```python
import functools

import jax
import jax.numpy as jnp
import numpy as np
from jax import lax
from jax.experimental import pallas as pl
from jax.experimental.pallas import tpu as pltpu

F32 = jnp.float32
BF16 = jnp.bfloat16

D = 1024
DEPTH = 2
N_PROMPT_SEQ = 16
T_PROMPT = 256
N_SAMPLE_SEQ = 2
T_SAMPLE = 4096
ROWS_PROMPT = N_PROMPT_SEQ * T_PROMPT
ROWS = ROWS_PROMPT + N_SAMPLE_SEQ * T_SAMPLE
GROUP_ROWS = 4096
GRID_W = 64
CHUNK = 256
EPS = 1e-6
F32_TINY = 1.1754944e-38
M_HEADS = 4
M_DH = 256
R_HEADS = 8
R_DH = 128
L_BLOCKS = 8
L_BW = 128
L_C = 8.0
ROPE_BASE = 10000.0
N_EXPERTS = 16
CAP_PROMPT = 2 * T_PROMPT // N_EXPERTS
CAP_SAMPLE = 2 * T_SAMPLE // N_EXPERTS
ROWS_PER_EXPERT = N_PROMPT_SEQ * CAP_PROMPT + N_SAMPLE_SEQ * CAP_SAMPLE
N_GATE_COLS = 16
LANES = 128
SUBLANES = 8
VMEM_LIMIT = 56 * 2 ** 20

COL_MQ, COL_MV, COL_MO, COL_LX, COL_LZ, COL_RQ, COL_RV, COL_RG = (i * D for i in range(8))
D_MAIN = 8 * D
ROW_MK, ROW_RK = 0, D
D_KT = 2 * D


def _cparams(sem, vmem=None):
    return pltpu.CompilerParams(dimension_semantics=sem, vmem_limit_bytes=vmem)


def _dot(a, b):
    return jnp.dot(a, b, preferred_element_type=F32)


def _dot_nt(a, b):
    return lax.dot_general(a, b, (((1,), (1,)), ((), ())), preferred_element_type=F32)


def _split3(x):
    a = x.astype(BF16)
    r = x - a.astype(F32)
    b = r.astype(BF16)
    c = (r - b.astype(F32)).astype(BF16)
    return a, b, c


def _log_sigmoid(x):
    return jnp.minimum(x, 0.0) - jnp.log1p(jnp.exp(-jnp.abs(x)))


def _sigmoid(x):
    return 0.5 * jnp.tanh(0.5 * x) + 0.5


def _rms(x):
    return x * lax.rsqrt(jnp.mean(x * x, axis=-1, keepdims=True) + EPS)


def _chunk(c):
    return pl.ds(pl.multiple_of(c * CHUNK, CHUNK), CHUNK)


def _alias_prev(args, in_specs, prev):
    aliases = {}
    for out_idx, arr in enumerate(prev):
        if arr is not None:
            aliases[len(args)] = out_idx
            args.append(arr)
            in_specs.append(pl.BlockSpec(memory_space=pl.ANY))
    return aliases, len(aliases)


def _ada_kernel(c_ref, w_ref, b_ref, o_ref):
    c = c_ref[...]
    s = (c * jax.nn.sigmoid(c)).astype(BF16)
    o_ref[0] = _dot(s, w_ref[0].astype(BF16)) + b_ref[0]


def _ada_call(cond8, w_ada, b_ada):
    tn = 1536
    return pl.pallas_call(
        _ada_kernel,
        grid=(DEPTH, 6 * D // tn),
        in_specs=[pl.BlockSpec((8, D), lambda l, j: (0, 0)),
                  pl.BlockSpec((1, D, tn), lambda l, j: (l, 0, j)),
                  pl.BlockSpec((1, 1, tn), lambda l, j: (l, 0, j))],
        out_specs=pl.BlockSpec((1, 8, tn), lambda l, j: (l, 0, j)),
        out_shape=jax.ShapeDtypeStruct((DEPTH, 8, 6 * D), F32),
        compiler_params=_cparams(("arbitrary", "arbitrary"), VMEM_LIMIT),
        name="ada",
    )(cond8, w_ada, b_ada.reshape(DEPTH, 1, 6 * D))


def _norm1_kernel(x_ref, g_ref, mod_ref, wg_ref, h_ref, gate_ref):
    y = _rms(x_ref[...]) * g_ref[...]
    h = (y * (1.0 + mod_ref[0, 1:2, :]) + mod_ref[0, 0:1, :]).astype(BF16)
    h_ref[...] = h
    gate_ref[...] = _dot(h, wg_ref[...])


def _norm1_call(x, g, mod, w_gate):
    tm = 512
    return pl.pallas_call(
        _norm1_kernel,
        grid=(ROWS // tm,),
        in_specs=[pl.BlockSpec((tm, D), lambda i: (i, 0)),
                  pl.BlockSpec((1, D), lambda i: (0, 0)),
                  pl.BlockSpec((1, 6, D), lambda i: (i * tm // GROUP_ROWS, 0, 0)),
                  pl.BlockSpec((D, LANES), lambda i: (0, 0))],
        out_specs=[pl.BlockSpec((tm, D), lambda i: (i, 0)),
                   pl.BlockSpec((tm, LANES), lambda i: (i, 0))],
        out_shape=[jax.ShapeDtypeStruct((ROWS, D), BF16),
                   jax.ShapeDtypeStruct((ROWS, LANES), F32)],
        compiler_params=_cparams(("arbitrary",), VMEM_LIMIT),
        name="norm1",
    )(x, g.reshape(1, D), mod, w_gate)


def _mm_kernel(x_ref, wt_ref, o_ref):
    o_ref[...] = _dot_nt(x_ref[...], wt_ref[...]).astype(o_ref.dtype)


def _inproj_call(h, w_main_t):
    tm, tn = 1024, 1024
    return pl.pallas_call(
        _mm_kernel,
        grid=(D_MAIN // tn, ROWS // tm),
        in_specs=[pl.BlockSpec((tm, D), lambda j, i: (i, 0)),
                  pl.BlockSpec((tn, D), lambda j, i: (j, 0))],
        out_specs=pl.BlockSpec((tm, tn), lambda j, i: (i, j)),
        out_shape=jax.ShapeDtypeStruct((ROWS, D_MAIN), BF16),
        compiler_params=_cparams(("arbitrary", "arbitrary"), VMEM_LIMIT),
        name="inproj",
    )(h, w_main_t)


def _mm_t_kernel(wt_ref, x_ref, o_ref):
    o_ref[...] = _dot_nt(wt_ref[...], x_ref[...]).astype(o_ref.dtype)


def _inproj_t_call(h, w_kt):
    tm, tn = 1024, 1024
    return pl.pallas_call(
        _mm_t_kernel,
        grid=(D_KT // tn, ROWS // tm),
        in_specs=[pl.BlockSpec((tn, D), lambda j, i: (j, 0)),
                  pl.BlockSpec((tm, D), lambda j, i: (i, 0))],
        out_specs=pl.BlockSpec((tn, tm), lambda j, i: (j, i)),
        out_shape=jax.ShapeDtypeStruct((D_KT, ROWS), BF16),
        compiler_params=_cparams(("arbitrary", "arbitrary"), VMEM_LIMIT),
        name="inproj_t",
    )(w_kt, h)


M_AUG = M_DH + LANES
M_GROUP = 1


def _mlstm_kernel(*refs, n_chunks, has_init, emit_state, n_prev):
    q_ref, kt_ref, v_ref, o_ref, g_ref, ng_ref = refs[:6]
    pos = 6
    if has_init:
        c0_ref, m0_ref = refs[pos:pos + 2]
        pos += 2
    pos += n_prev
    y_ref = refs[pos]
    pos += 1
    if emit_state:
        co_ref, no_ref, mo_ref = refs[pos:pos + 3]
        pos += 3
    caug_sc, m_sc, hf_sc, hb_sc = refs[pos:]

    zero_state = (not has_init) and n_chunks == 1
    if has_init:
        for g in range(M_GROUP):
            caug_sc[g] = c0_ref[0, :, g]
            m_sc[g] = m0_ref[0, :, g]
    else:
        if not zero_state:
            caug_sc[...] = jnp.zeros_like(caug_sc)
        m_sc[...] = jnp.zeros_like(m_sc)

    ri = lax.broadcasted_iota(jnp.int32, (CHUNK, CHUNK), 0)
    ci = lax.broadcasted_iota(jnp.int32, (CHUNK, CHUNK), 1)
    r8 = lax.broadcasted_iota(jnp.int32, (SUBLANES, CHUNK), 0)
    ones_b = jnp.ones((CHUNK, LANES), BF16)
    allowed = [ri >= ci, ci >= ri]
    allowed_b = [jnp.where(a, 1.0, 0.0).astype(BF16) for a in allowed]
    tri = [jnp.where(ri <= ci, 1.0, 0.0).astype(BF16), jnp.where(ri >= ci, 1.0, 0.0).astype(BF16)]

    def step(g, d, c, m_t):
        rows = _chunk(c)
        hcols = slice(g * M_DH, (g + 1) * M_DH)
        i_idx, f_idx = (0, 1) if d == 0 else (2, 3)
        g8 = g_ref[g, :, rows]
        ls = jnp.where(r8 == f_idx, _log_sigmoid(g8), g8)
        cum = sum(_dot(p, tri[d]) for p in _split3(ls))
        f_row = ls[f_idx:f_idx + 1]
        b_row = cum[f_idx:f_idx + 1]
        g_row = ls[i_idx:i_idx + 1] - b_row
        gmat = jnp.where(allowed[d], jnp.broadcast_to(g_row, (CHUNK, CHUNK)), -jnp.inf)
        mp_col = jnp.max(gmat, axis=-1, keepdims=True)
        mp = jnp.broadcast_to(mp_col, (CHUNK, LANES))
        fp = _split3(f_row)
        bc = _dot(allowed_b[d] * fp[0], ones_b) + _dot(allowed_b[d] * fp[1], ones_b)

        q = q_ref[rows, hcols]
        kt = kt_ref[hcols, rows]
        vaug = jnp.concatenate([v_ref[rows, hcols], ones_b], axis=1)
        s1 = _dot(q, kt) * jnp.exp(gmat - mp_col)
        sv1 = _dot(s1.astype(BF16), vaug)
        b_last_11 = b_row[:, CHUNK - 1:CHUNK] if d == 0 else b_row[:, 0:1]
        log_k = b_last_11 + g_row
        a_11 = jnp.max(log_k, axis=-1, keepdims=True)
        kw1 = (kt.astype(F32) * jnp.exp(log_k - a_11)).astype(BF16)
        u1 = _dot(kw1, vaug)
        b_last = jnp.broadcast_to(b_last_11, (SUBLANES, LANES))
        a_t = jnp.broadcast_to(a_11, (SUBLANES, LANES))

        m_b = jnp.broadcast_to(m_t[0:1, :], (CHUNK, LANES))
        mi = jnp.maximum(m_b, mp)
        r = jnp.exp(mp - mi)
        w_prev = jnp.exp(m_b - mi)
        floor = jnp.exp(-(bc + mi))
        if zero_state:
            qc = None
            den = r * sv1[:, M_DH:]
        else:
            qc = _dot(q, caug_sc[g, d].astype(BF16))
            den = r * sv1[:, M_DH:] + w_prev * qc[:, M_DH:]
        inv = 1.0 / jnp.maximum(jnp.abs(den), floor)
        h_sc = hf_sc if d == 0 else hb_sc
        for blk in range(M_DH // LANES):
            cols = slice(blk * LANES, (blk + 1) * LANES)
            num = r * sv1[:, cols]
            if not zero_state:
                num = num + w_prev * qc[:, cols]
            h_sc[rows, slice(g * M_DH + blk * LANES, g * M_DH + (blk + 1) * LANES)] = num * inv

        m_new = jnp.maximum(b_last + m_t, a_t)
        w_c = jnp.exp(b_last + m_t - m_new)[0:1, :]
        w_u = jnp.exp(a_t - m_new)[0:1, :]
        for blk in range(M_AUG // LANES):
            cols = slice(blk * LANES, (blk + 1) * LANES)
            if zero_state:
                caug_sc[g, d, :, cols] = w_u * u1[:, cols]
            else:
                caug_sc[g, d, :, cols] = w_c * caug_sc[g, d, :, cols] + w_u * u1[:, cols]
        return m_new

    def scan_body(c, carry):
        new = []
        for g in range(M_GROUP):
            new.append(step(g, 0, c, carry[2 * g]))
            new.append(step(g, 1, n_chunks - 1 - c, carry[2 * g + 1]))
        return tuple(new)

    m_init = tuple(jnp.broadcast_to(m_sc[g, d], (SUBLANES, LANES))
                   for g in range(M_GROUP) for d in range(2))
    m_fin = lax.fori_loop(0, n_chunks, scan_body, m_init)
    for g in range(M_GROUP):
        for d in range(2):
            m_sc[g, d] = m_fin[2 * g + d][0:1, :]

    def out_body(c, carry):
        rows = _chunk(c)
        for g in range(M_GROUP):
            hcols = slice(g * M_DH, (g + 1) * M_DH)
            y = _rms(hf_sc[rows, hcols] + hb_sc[rows, hcols]) * ng_ref[:, hcols]
            y_ref[rows, hcols] = (y * _sigmoid(o_ref[rows, hcols].astype(F32))).astype(y_ref.dtype)
        return carry

    lax.fori_loop(0, n_chunks, out_body, 0)

    if emit_state:
        for g in range(M_GROUP):
            co_ref[0, 0, :, g] = caug_sc[g, :, :, 0:M_DH]
            for d in range(2):
                no_ref[0, 0, d, g] = caug_sc[g, d, :, M_DH:].T[0:1, :]
            mo_ref[0, 0, :, g] = m_sc[g]


def _mlstm_call(proj, proj_t, grow, norm_g, state, prev, *, t, n_seq, row_block0, layer, emit_state):
    has_init = state is not None

    gw = M_GROUP * M_DH
    big = dict(pipeline_mode=pl.Buffered(1)) if t * gw * 2 > 2 ** 21 else {}

    def col(base):
        return lambda s, h: (row_block0 + s, base // gw + h)

    st_caug = pl.BlockSpec((1, 2, M_GROUP, M_DH, M_AUG), lambda s, h: (s, 0, h, 0, 0))
    st_m = pl.BlockSpec((1, 2, M_GROUP, 1, LANES), lambda s, h: (s, 0, h, 0, 0))
    so_c = pl.BlockSpec((1, 1, 2, M_GROUP, M_DH, M_DH), lambda s, h: (s, layer, 0, h, 0, 0))
    so_n = pl.BlockSpec((1, 1, 2, M_GROUP, 1, M_DH), lambda s, h: (s, layer, 0, h, 0, 0))
    so_m = pl.BlockSpec((1, 1, 2, M_GROUP, 1, LANES), lambda s, h: (s, layer, 0, h, 0, 0))
    in_specs = [pl.BlockSpec((t, gw), col(COL_MQ), **big),
                pl.BlockSpec((gw, t), lambda s, h: (ROW_MK // gw + h, row_block0 + s), **big),
                pl.BlockSpec((t, gw), col(COL_MV), **big),
                pl.BlockSpec((t, gw), col(COL_MO), **big),
                pl.BlockSpec((M_GROUP, SUBLANES, t), lambda s, h: (h, 0, row_block0 + s)),
                pl.BlockSpec((1, gw), lambda s, h: (0, h))]
    args = [proj, proj_t, proj, proj, grow, norm_g.reshape(1, D)]
    if has_init:
        in_specs += [st_caug, st_m]
        args += list(state)
    out_specs = [pl.BlockSpec((t, gw), lambda s, h: (row_block0 + s, h))]
    out_shape = [jax.ShapeDtypeStruct((ROWS, D), BF16)]
    if emit_state:
        out_specs += [so_c, so_n, so_m]
        out_shape += [jax.ShapeDtypeStruct((n_seq, DEPTH, 2, M_HEADS, M_DH, M_DH), F32),
                      jax.ShapeDtypeStruct((n_seq, DEPTH, 2, M_HEADS, 1, M_DH), F32),
                      jax.ShapeDtypeStruct((n_seq, DEPTH, 2, M_HEADS, 1, LANES), F32)]
    aliases, n_prev = _alias_prev(args, in_specs, prev)
    return pl.pallas_call(
        functools.partial(_mlstm_kernel, n_chunks=t // CHUNK, has_init=has_init,
                          emit_state=emit_state, n_prev=n_prev),
        grid=(n_seq, M_HEADS // M_GROUP),
        in_specs=in_specs,
        out_specs=out_specs,
        out_shape=out_shape,
        input_output_aliases=aliases,
        scratch_shapes=[pltpu.VMEM((M_GROUP, 2, M_DH, M_AUG), F32),
                        pltpu.VMEM((M_GROUP, 2, 1, LANES), F32),
                        pltpu.VMEM((t, gw), F32),
                        pltpu.VMEM((t, gw), F32)],
        compiler_params=_cparams(("arbitrary", "arbitrary"), VMEM_LIMIT),
        name="mlstm",
    )(*args)


R_GROUP = 2

def _ret_kernel(*refs, n_chunks, rope, has_init, emit_state, n_prev):
    q_ref, kt_ref, v_ref, g_ref, dec_ref, ng_ref = refs[:6]
    pos = 6
    if rope:
        cos_ref, sin_ref, cos_t_ref, sin_t_ref = refs[pos:pos + 4]
        pos += 4
    if has_init:
        s0_ref = refs[pos]
        pos += 1
    pos += n_prev
    y_ref = refs[pos]
    pos += 1
    if emit_state:
        so_ref = refs[pos]
        pos += 1
    s_sc, of_sc, ob_sc, ktr_sc = refs[pos:pos + 4]
    qr_sc = refs[pos + 4] if rope else None

    zero_state = (not has_init) and n_chunks == 1
    if has_init:
        for g in range(R_GROUP):
            s_sc[g] = s0_ref[0, :, g]
    elif not zero_state:
        s_sc[...] = jnp.zeros_like(s_sc)

    ri = lax.broadcasted_iota(jnp.int32, (CHUNK, CHUNK), 0)
    ci = lax.broadcasted_iota(jnp.int32, (CHUNK, CHUNK), 1)
    quarter = R_DH // 4
    if rope:
        fr = lax.broadcasted_iota(jnp.int32, (R_DH, R_DH), 0)
        fc = lax.broadcasted_iota(jnp.int32, (R_DH, R_DH), 1)
        partner = fc + jnp.where((fc % (2 * quarter)) < quarter, quarter, -quarter)
        perm_b = jnp.where(fr == partner, 1.0, 0.0).astype(BF16)

    def prep_body(c, carry):
        rows = _chunk(c)
        for g in range(R_GROUP):
            hcols = slice(g * R_DH, (g + 1) * R_DH)
            ktf = kt_ref[hcols, rows].astype(F32) * (R_DH ** -0.5)
            if rope:
                swapped = jnp.concatenate([ktf[quarter:2 * quarter], ktf[0:quarter],
                                           ktf[3 * quarter:], ktf[2 * quarter:3 * quarter]], axis=0)
                ktf = ktf * cos_t_ref[:, rows] + swapped * sin_t_ref[:, rows]
                q = q_ref[rows, hcols]
                qr = q.astype(F32) * cos_ref[rows, :] + _dot(q, perm_b) * sin_ref[rows, :]
                qr_sc[rows, hcols] = qr.astype(BF16)
            ktr_sc[hcols, rows] = ktf.astype(BF16)
        return carry

    lax.fori_loop(0, n_chunks, prep_body, 0, unroll=min(n_chunks, 2))
    qsrc = qr_sc if rope else q_ref

    lane = lax.broadcasted_iota(jnp.int32, (1, CHUNK), 1).astype(F32)
    row_f = lax.broadcasted_iota(jnp.int32, (CHUNK, R_DH), 0).astype(F32)
    consts = []
    for gd in range(2 * R_GROUP):
        g, d = divmod(gd, 2)
        lg = -jnp.exp(dec_ref[g, d:d + 1, :])
        lg11 = lg[:, 0:1]
        diff = (ri - ci) if d == 0 else (ci - ri)
        causal = diff >= 0
        dmat = jnp.where(causal, jnp.exp(lg11 * jnp.where(causal, diff, 0).astype(F32)), 0.0)
        if d == 0:
            q_dec = jnp.exp(lg * (row_f + 1.0))
            k_dec = jnp.exp(lg11 * (CHUNK - 1.0 - lane))
        else:
            q_dec = jnp.exp(lg * (CHUNK - row_f))
            k_dec = jnp.exp(lg11 * lane)
        g_chunk = jnp.exp(lg * float(CHUNK))
        consts.append((dmat, q_dec, k_dec, g_chunk))

    def step(g, d, c):
        dmat, q_dec, k_dec, g_chunk = consts[2 * g + d]
        rows = _chunk(c)
        hcols = slice(g * R_DH, (g + 1) * R_DH)
        q = qsrc[rows, hcols]
        kt = ktr_sc[hcols, rows]
        v = v_ref[rows, hcols]
        s = _dot(q, kt) * dmat
        o = _dot(s.astype(BF16), v)
        if not zero_state:
            o = o + q_dec * _dot(q, s_sc[g, d].astype(BF16))
        if d == 0:
            of_sc[rows, hcols] = o
        else:
            ob_sc[rows, hcols] = o
        kd = (kt.astype(F32) * k_dec).astype(BF16)
        if zero_state:
            s_sc[g, d] = _dot(kd, v)
        else:
            s_sc[g, d] = g_chunk * s_sc[g, d] + _dot(kd, v)

    def scan_body(c, carry):
        for g in range(R_GROUP):
            step(g, 0, c)
            step(g, 1, n_chunks - 1 - c)
        return carry

    lax.fori_loop(0, n_chunks, scan_body, 0, unroll=min(n_chunks, 2))

    def out_body(c, carry):
        rows = _chunk(c)
        for g in range(R_GROUP):
            hcols = slice(g * R_DH, (g + 1) * R_DH)
            y = _rms(of_sc[rows, hcols] + ob_sc[rows, hcols]) * ng_ref[:, hcols]
            gate = g_ref[rows, hcols].astype(F32)
            y_ref[rows, hcols] = (y * (gate * _sigmoid(gate))).astype(y_ref.dtype)
        return carry

    lax.fori_loop(0, n_chunks, out_body, 0, unroll=min(n_chunks, 2))

    if emit_state:
        for g in range(R_GROUP):
            so_ref[0, 0, :, g] = s_sc[g]


def _ret_call(proj, proj_t, decay, norm_g, rope_tabs, state, prev, *, t, n_seq, row_block0, layer, emit_state):
    has_init = state is not None
    rope = rope_tabs is not None

    gw = R_GROUP * R_DH

    def col(base):
        return lambda s, h: (row_block0 + s, base // gw + h)

    st_s = pl.BlockSpec((1, 2, R_GROUP, R_DH, R_DH), lambda s, h: (s, 0, h, 0, 0))
    in_specs = [pl.BlockSpec((t, gw), col(COL_RQ)),
                pl.BlockSpec((gw, t), lambda s, h: (ROW_RK // gw + h, row_block0 + s)),
                pl.BlockSpec((t, gw), col(COL_RV)),
                pl.BlockSpec((t, gw), col(COL_RG)),
                pl.BlockSpec((R_GROUP, 2, LANES), lambda s, h: (h, 0, 0)),
                pl.BlockSpec((1, gw), lambda s, h: (0, h))]
    args = [proj, proj_t, proj, proj, decay, norm_g.reshape(1, D)]
    if rope:
        in_specs += [pl.BlockSpec((t, R_DH), lambda s, h: (0, 0))] * 2
        in_specs += [pl.BlockSpec((R_DH, t), lambda s, h: (0, 0))] * 2
        args += list(rope_tabs)
    if has_init:
        in_specs.append(st_s)
        args.append(state)
    out_specs = [pl.BlockSpec((t, gw), lambda s, h: (row_block0 + s, h))]
    out_shape = [jax.ShapeDtypeStruct((ROWS, D), BF16)]
    if emit_state:
        out_specs.append(pl.BlockSpec((1, 1, 2, R_GROUP, R_DH, R_DH), lambda s, h: (s, layer, 0, h, 0, 0)))
        out_shape.append(jax.ShapeDtypeStruct((n_seq, DEPTH, 2, R_HEADS, R_DH, R_DH), F32))
    aliases, n_prev = _alias_prev(args, in_specs, prev)
    scratch = [pltpu.VMEM((R_GROUP, 2, R_DH, R_DH), F32),
               pltpu.VMEM((t, gw), F32),
               pltpu.VMEM((t, gw), F32),
               pltpu.VMEM((gw, t), BF16)]
    if rope:
        scratch.append(pltpu.VMEM((t, gw), BF16))
    return pl.pallas_call(
        functools.partial(_ret_kernel, n_chunks=t // CHUNK, rope=rope, has_init=has_init,
                          emit_state=emit_state, n_prev=n_prev),
        grid=(n_seq, R_HEADS // R_GROUP),
        in_specs=in_specs,
        out_specs=out_specs,
        out_shape=out_shape,
        input_output_aliases=aliases,
        scratch_shapes=scratch,
        compiler_params=_cparams(("arbitrary", "arbitrary"), VMEM_LIMIT),
        name="retention",
    )(*args)


LRU_SLAB = 64


def _tile_scan(a, b, reverse):
    row = lax.broadcasted_iota(jnp.int32, (SUBLANES, L_BW), 0)
    for k in (1, 2, 4):
        if reverse:
            keep = row < SUBLANES - k
            shift = SUBLANES - k
        else:
            keep = row >= k
            shift = k
        a_sh = jnp.where(keep, pltpu.roll(a, shift, 0), 1.0)
        b_sh = jnp.where(keep, pltpu.roll(b, shift, 0), 0.0)
        b = a * b_sh + b
        a = a * a_sh
    return a, b


def _lru_kernel(*refs, t, has_init, emit_state, n_prev):
    lx_ref, lz_ref, cw_ref, cb_ref, wr_ref, wi_ref, br_ref, bi_ref, lam_ref = refs[:9]
    pos = 9
    if has_init:
        h0_ref = refs[pos]
        pos += 1
    pos += n_prev
    y_ref = refs[pos]
    pos += 1
    if emit_state:
        hfin_ref = refs[pos]
        pos += 1
    xpad_sc, a_sc, b_sc = refs[pos:]
    rc = min(t, 256)

    xpad_sc[0:8, :] = jnp.zeros((8, L_BW), F32)
    xpad_sc[t + 8:t + 16, :] = jnp.zeros((8, L_BW), F32)

    def pad_body(c, carry):
        r0 = pl.multiple_of(c * rc, rc)
        xpad_sc[pl.ds(r0 + 8, rc), :] = lx_ref[pl.ds(r0, rc), :].astype(F32)
        return carry

    lax.fori_loop(0, t // rc, pad_body, 0)

    wr = [(0.5 * wr_ref[d, 0]).astype(BF16) for d in range(2)]
    wi = [(0.5 * wi_ref[d, 0]).astype(BF16) for d in range(2)]
    half_br = [0.5 * br_ref[d:d + 1, :] for d in range(2)]
    half_bi = [0.5 * bi_ref[d:d + 1, :] for d in range(2)]
    half_c = []
    for d in range(2):
        lam = lam_ref[d:d + 1, :]
        softplus_neg = jnp.maximum(-lam, 0.0) + jnp.log1p(jnp.exp(-jnp.abs(lam)))
        half_c.append(-0.5 * L_C * softplus_neg)

    def gate_body(c, carry):
        r0 = pl.multiple_of(c * rc, rc)
        xe = xpad_sc[pl.ds(r0, rc + 16), :]
        n = rc + 16
        u = (cb_ref[...] + cw_ref[0:1, :] * pltpu.roll(xe, 2, 0)[8:8 + rc]
             + cw_ref[1:2, :] * pltpu.roll(xe, 1, 0)[8:8 + rc]
             + cw_ref[2:3, :] * xe[8:8 + rc]
             + cw_ref[3:4, :] * pltpu.roll(xe, n - 1, 0)[8:8 + rc])
        ub = u.astype(BF16)
        half_u = 0.5 * u
        for d in range(2):
            tr = jnp.tanh(_dot(ub, wr[d]) + half_br[d])
            ti = jnp.tanh(_dot(ub, wi[d]) + half_bi[d])
            log_a = half_c[d] * tr + half_c[d]
            a = jnp.exp(log_a)
            z = jnp.tanh(log_a) * (-1.0 - a * a)
            mult = z * lax.rsqrt(jnp.maximum(z, F32_TINY))
            a_sc[d, pl.ds(r0, rc), :] = a
            b_sc[d, pl.ds(r0, rc), :] = mult * (ti * half_u + half_u)
        return carry

    lax.fori_loop(0, t // rc, gate_body, 0)

    n_slabs = t // LRU_SLAB
    tiles = LRU_SLAB // SUBLANES

    row8 = lax.broadcasted_iota(jnp.int32, (SUBLANES, L_BW), 0)

    def bcast_row(x, r):
        return jnp.broadcast_to(x[r:r + 1, :], (SUBLANES, L_BW))

    def slab_scan(d, r0, c_in):
        reverse = d == 1
        edge = 0 if reverse else SUBLANES - 1
        local = []
        spa = jnp.ones((SUBLANES, L_BW), F32)
        shl = jnp.zeros((SUBLANES, L_BW), F32)
        for k in range(tiles):
            rows = pl.ds(r0 + k * SUBLANES, SUBLANES)
            pa, hl = _tile_scan(a_sc[d, rows, :], b_sc[d, rows, :], reverse)
            local.append((rows, pa, hl))
            spa = jnp.where(row8 == k, bcast_row(pa, edge), spa)
            shl = jnp.where(row8 == k, bcast_row(hl, edge), shl)
        cpa, chl = _tile_scan(spa, shl, reverse)
        after = chl + cpa * c_in
        for k, (rows, pa, hl) in enumerate(local):
            prev = k + 1 if reverse else k - 1
            cin = c_in if (prev < 0 or prev >= tiles) else bcast_row(after, prev)
            b_sc[d, rows, :] = hl + pa * cin
        return bcast_row(after, 0 if reverse else tiles - 1)

    def scan_body(c, carry):
        cf, cbk = carry
        cf = slab_scan(0, pl.multiple_of(c * LRU_SLAB, LRU_SLAB), cf)
        cbk = slab_scan(1, pl.multiple_of((n_slabs - 1 - c) * LRU_SLAB, LRU_SLAB), cbk)
        return cf, cbk

    if has_init:
        cf0 = jnp.broadcast_to(h0_ref[0, 0:1, :], (SUBLANES, L_BW))
        cb0 = jnp.broadcast_to(h0_ref[0, 1:2, :], (SUBLANES, L_BW))
    else:
        cf0 = jnp.zeros((SUBLANES, L_BW), F32)
        cb0 = jnp.zeros((SUBLANES, L_BW), F32)
    cf, cbk = lax.fori_loop(0, n_slabs, scan_body, (cf0, cb0))
    if emit_state:
        hfin_ref[0, 0, 0:1, :] = cf[0:1, :]
        hfin_ref[0, 0, 1:2, :] = cbk[0:1, :]

    def out_body(c, carry):
        rows = pl.ds(pl.multiple_of(c * rc, rc), rc)
        z = lz_ref[rows, :].astype(F32)
        gelu = 0.5 * z * (1.0 + jnp.tanh(0.7978845608028654 * (z + 0.044715 * (z * z * z))))
        y_ref[rows, :] = ((b_sc[0, rows, :] + b_sc[1, rows, :]) * gelu).astype(y_ref.dtype)
        return carry

    lax.fori_loop(0, t // rc, out_body, 0)


def _lru_call(proj, p, state, prev, *, t, n_seq, row_block0, layer, emit_state):
    has_init = state is not None

    def col(base):
        return lambda s, b: (row_block0 + s, base // L_BW + b)

    vec2 = pl.BlockSpec((2, L_BW), lambda s, b: (0, b))
    st = pl.BlockSpec((1, 2, L_BW), lambda s, b: (s, 0, b))
    in_specs = [pl.BlockSpec((t, L_BW), col(COL_LX)),
                pl.BlockSpec((t, L_BW), col(COL_LZ)),
                pl.BlockSpec((4, L_BW), lambda s, b: (0, b)),
                pl.BlockSpec((1, L_BW), lambda s, b: (0, b)),
                pl.BlockSpec((2, 1, L_BW, L_BW), lambda s, b: (0, b, 0, 0)),
                pl.BlockSpec((2, 1, L_BW, L_BW), lambda s, b: (0, b, 0, 0)),
                vec2, vec2, vec2]
    args = [proj, proj, p['conv_w'], p['conv_b'].reshape(1, D), p['wr'], p['wi'],
            p['br'], p['bi'], p['lam']]
    if has_init:
        in_specs.append(st)
        args.append(state)
    out_specs = [pl.BlockSpec((t, L_BW), lambda s, b: (row_block0 + s, b))]
    out_shape = [jax.ShapeDtypeStruct((ROWS, D), BF16)]
    if emit_state:
        out_specs.append(pl.BlockSpec((1, 1, 2, L_BW), lambda s, b: (s, layer, 0, b)))
        out_shape.append(jax.ShapeDtypeStruct((n_seq, DEPTH, 2, D), F32))
    aliases, n_prev = _alias_prev(args, in_specs, prev)
    return pl.pallas_call(
        functools.partial(_lru_kernel, t=t, has_init=has_init, emit_state=emit_state, n_prev=n_prev),
        grid=(n_seq, L_BLOCKS),
        in_specs=in_specs,
        out_specs=out_specs,
        out_shape=out_shape,
        input_output_aliases=aliases,
        scratch_shapes=[pltpu.VMEM((t + 16, L_BW), F32),
                        pltpu.VMEM((2, t, L_BW), F32),
                        pltpu.VMEM((2, t, L_BW), F32)],
        compiler_params=_cparams(("arbitrary", "arbitrary"), VMEM_LIMIT),
        name="rglru",
    )(*args)


def _merge_kernel(h_ref, ym_ref, yl_ref, yr_ref, x_ref, mod_ref, wm_ref, bm_ref, wb_ref, wo_ref,
                  n2_ref, rwh_ref, rwl_ref, xo_ref, h2_ref, lg_ref):
    h = h_ref[...]
    merged = None
    for k, y_ref in enumerate((ym_ref, yl_ref, yr_ref)):
        gate = jax.nn.sigmoid(_dot(h, wm_ref[:, k * D:(k + 1) * D]) + bm_ref[:, k * D:(k + 1) * D])
        term = gate * _dot(y_ref[...], wb_ref[k])
        merged = term if merged is None else merged + term
    mix = _dot(merged.astype(BF16), wo_ref[...])
    xn = x_ref[...] + mod_ref[0, 2:3, :] * mix
    xo_ref[...] = xn
    hn = _rms(xn) * n2_ref[...] * (1.0 + mod_ref[0, 4:5, :]) + mod_ref[0, 3:4, :]
    hi = hn.astype(BF16)
    h2_ref[...] = hi
    lo = (hn - hi.astype(F32)).astype(BF16)
    lg_ref[...] = _dot(hi, rwh_ref[...]) + _dot(lo, rwh_ref[...]) + _dot(hi, rwl_ref[...])


def _merge_call(h, ym, yl, yr, x, mod, wm, bm, wb, wo, n2g, rwh, rwl):
    tm = 256
    row = lambda i: (i, 0)
    const2 = lambda i: (0, 0)
    return pl.pallas_call(
        _merge_kernel,
        grid=(ROWS // tm,),
        in_specs=[pl.BlockSpec((tm, D), row),
                  pl.BlockSpec((tm, D), row),
                  pl.BlockSpec((tm, D), row),
                  pl.BlockSpec((tm, D), row),
                  pl.BlockSpec((tm, D), row),
                  pl.BlockSpec((1, 6, D), lambda i: (i * tm // GROUP_ROWS, 0, 0)),
                  pl.BlockSpec((D, 3 * D), const2),
                  pl.BlockSpec((1, 3 * D), const2),
                  pl.BlockSpec((3, D, D), lambda i: (0, 0, 0)),
                  pl.BlockSpec((D, D), const2),
                  pl.BlockSpec((1, D), const2),
                  pl.BlockSpec((D, LANES), const2),
                  pl.BlockSpec((D, LANES), const2)],
        out_specs=[pl.BlockSpec((tm, D), row),
                   pl.BlockSpec((tm, D), row),
                   pl.BlockSpec((tm, LANES), row)],
        out_shape=[jax.ShapeDtypeStruct((ROWS, D), F32),
                   jax.ShapeDtypeStruct((ROWS, D), BF16),
                   jax.ShapeDtypeStruct((ROWS, LANES), F32)],
        compiler_params=_cparams(("arbitrary",), VMEM_LIMIT),
        name="merge",
    )(h, ym, yl, yr, x, mod, wm, bm.reshape(1, 3 * D), wb, wo, n2g.reshape(1, D), rwh, rwl)


EXPERT_TM = 512
assert N_PROMPT_SEQ * CAP_PROMPT == EXPERT_TM and CAP_SAMPLE == EXPERT_TM


def _expert_kernel(xp_ref, xs_ref, g_ref, mod_ref, w1_ref, w3_ref, w2_ref, yp_ref, ys_ref,
                   w1_sc, w3_sc, w2_sc):
    m = pl.program_id(1)

    @pl.when(m == 0)
    def _():
        w1_sc[...] = w1_ref[0, 0].astype(BF16)
        w3_sc[...] = w3_ref[0, 0].astype(BF16)
        w2_sc[...] = w2_ref[0, 0].astype(BF16)

    xs = jnp.where(m == 0, xp_ref[0], xs_ref[0])
    a = _dot(xs, w1_sc[...])
    b = _dot(xs, w3_sc[...])
    mid = (a * jax.nn.sigmoid(a) * b).astype(BF16)

    def down(y_ref):
        y_ref[0] = (_dot(mid, w2_sc[...]) * g_ref[0]) * mod_ref[0, 5:6, :]

    pl.when(m == 0)(functools.partial(down, yp_ref))
    pl.when(m > 0)(functools.partial(down, ys_ref))


def _expert_call(xs_p, xs_s, gv, mod, w1, w3, w2, layer):
    tm = EXPERT_TM
    wspec = pl.BlockSpec((1, 1, D, D), lambda e, m: (layer, e, 0, 0))
    p_spec = pl.BlockSpec((1, tm, D), lambda e, m: (e, 0, 0))
    s_spec = pl.BlockSpec((1, tm, D), lambda e, m: (e, jnp.maximum(m - 1, 0), 0))
    return pl.pallas_call(
        _expert_kernel,
        grid=(N_EXPERTS, ROWS_PER_EXPERT // tm),
        in_specs=[p_spec, s_spec,
                  pl.BlockSpec((1, tm, 1), lambda e, m: (e, m, 0)),
                  pl.BlockSpec((1, 6, D), lambda e, m: (m, 0, 0)),
                  wspec, wspec, wspec],
        out_specs=[p_spec, s_spec],
        out_shape=[jax.ShapeDtypeStruct((N_EXPERTS, tm, D), F32),
                   jax.ShapeDtypeStruct((N_EXPERTS, N_SAMPLE_SEQ * tm, D), F32)],
        scratch_shapes=[pltpu.VMEM((D, D), BF16)] * 3,
        compiler_params=_cparams(("arbitrary", "arbitrary"), VMEM_LIMIT),
        name="experts",
    )(xs_p, xs_s, gv, mod, w1, w3, w2)


PROMPT_SLOTS = N_EXPERTS * CAP_PROMPT


def _prompt_gather_kernel(h_ref, idx_ref, o_ref):
    tok = lax.broadcasted_iota(jnp.int32, (PROMPT_SLOTS, T_PROMPT), 1)
    onehot = jnp.where(tok == idx_ref[0], 1.0, 0.0).astype(BF16)
    o_ref[...] = _dot(onehot, h_ref[...]).astype(o_ref.dtype).reshape(N_EXPERTS, CAP_PROMPT, D)


def _prompt_gather_call(h2, idx_col):
    return pl.pallas_call(
        _prompt_gather_kernel,
        grid=(N_PROMPT_SEQ,),
        in_specs=[pl.BlockSpec((T_PROMPT, D), lambda s: (s, 0)),
                  pl.BlockSpec((1, PROMPT_SLOTS, 1), lambda s: (s, 0, 0))],
        out_specs=pl.BlockSpec((N_EXPERTS, CAP_PROMPT, D), lambda s: (0, s, 0)),
        out_shape=jax.ShapeDtypeStruct((N_EXPERTS, N_PROMPT_SEQ * CAP_PROMPT, D), BF16),
        compiler_params=_cparams(("arbitrary",), VMEM_LIMIT),
        name="prompt_gather",
    )(h2, idx_col)


def _prompt_combine_kernel(x_ref, y_ref, idx_ref, o_ref):
    tok = lax.broadcasted_iota(jnp.int32, (T_PROMPT, PROMPT_SLOTS), 0)
    onehot = jnp.where(tok == idx_ref[0], 1.0, 0.0).astype(BF16)
    y = y_ref[...].reshape(PROMPT_SLOTS, D)
    hi = y.astype(BF16)
    lo = (y - hi.astype(F32)).astype(BF16)
    o_ref[...] = x_ref[...] + (_dot(onehot, hi) + _dot(onehot, lo))


def _prompt_combine_call(x, y_p, idx_row):
    return pl.pallas_call(
        _prompt_combine_kernel,
        grid=(N_PROMPT_SEQ,),
        in_specs=[pl.BlockSpec((T_PROMPT, D), lambda s: (s, 0)),
                  pl.BlockSpec((N_EXPERTS, CAP_PROMPT, D), lambda s: (0, s, 0)),
                  pl.BlockSpec((1, 1, PROMPT_SLOTS), lambda s: (s, 0, 0))],
        out_specs=pl.BlockSpec((T_PROMPT, D), lambda s: (s, 0)),
        out_shape=jax.ShapeDtypeStruct((ROWS, D), F32),
        input_output_aliases={0: 0},
        compiler_params=_cparams(("arbitrary",), VMEM_LIMIT),
        name="prompt_combine",
    )(x, y_p, idx_row)


def _final_norm_kernel(x_ref, g_ref, o_ref):
    o_ref[...] = _rms(x_ref[...]) * g_ref[...]


def _final_norm_call(x, g, row0, n_rows):
    tm = 512
    return pl.pallas_call(
        _final_norm_kernel,
        grid=(n_rows // tm,),
        in_specs=[pl.BlockSpec((tm, D), lambda i: (row0 // tm + i, 0)),
                  pl.BlockSpec((1, D), lambda i: (0, 0))],
        out_specs=pl.BlockSpec((tm, D), lambda i: (i, 0)),
        out_shape=jax.ShapeDtypeStruct((n_rows, D), F32),
        compiler_params=_cparams(("arbitrary",), VMEM_LIMIT),
        name="final_norm",
    )(x, g.reshape(1, D))


def _rope_tables():
    tpos = np.arange(T_SAMPLE)
    lane = np.arange(R_DH)
    pos = np.where(lane[None, :] < R_DH // 2, (tpos // GRID_W)[:, None], (tpos % GRID_W)[:, None])
    n_freq = R_DH // 4
    freqs = np.power(np.float32(ROPE_BASE), -np.arange(n_freq, dtype=np.float32) / np.float32(n_freq))
    ang = pos.astype(np.float32) * freqs[lane % n_freq][None, :]
    first = ((lane % (R_DH // 2)) < n_freq)[None, :]
    cos, sin = np.cos(ang), np.sin(ang)
    sin = np.where(first, -sin, sin)
    return tuple(jnp.asarray(a, F32) for a in (cos, sin, np.ascontiguousarray(cos.T), np.ascontiguousarray(sin.T)))


def _route(logits):
    aff = jax.nn.softmax(logits[:, :N_EXPERTS], axis=-1)
    ap = aff[:ROWS_PROMPT].reshape(N_PROMPT_SEQ, T_PROMPT, N_EXPERTS).swapaxes(1, 2)
    as_ = aff[ROWS_PROMPT:].reshape(N_SAMPLE_SEQ, T_SAMPLE, N_EXPERTS).swapaxes(1, 2)
    gp, ip = lax.top_k(ap, CAP_PROMPT)
    gs, is_ = lax.top_k(as_, CAP_SAMPLE)
    is_ = is_ + ROWS_PROMPT + (jnp.arange(N_SAMPLE_SEQ) * T_SAMPLE)[:, None, None]
    rows_s = is_.swapaxes(0, 1).reshape(N_EXPERTS, -1)
    gv = jnp.concatenate([gp.swapaxes(0, 1).reshape(N_EXPERTS, -1),
                          gs.swapaxes(0, 1).reshape(N_EXPERTS, -1)], axis=1)
    return ip.reshape(N_PROMPT_SEQ, PROMPT_SLOTS), rows_s, gv


def _layer(x, mod, p, states, rope_tabs, stacked, layer, prev_states):
    mq, mk, mv, mo, mg, lx, lz, rq, rk, rv, rg = jnp.split(
        p['w_in'].T, [1024, 2048, 3072, 4096, 4112, 5136, 6160, 7184, 8208, 9232], axis=0)
    w_main_t = jnp.concatenate([mq, mv, mo, lx, lz, rq, rv, rg], axis=0).astype(BF16)
    w_kt = jnp.concatenate([mk.astype(BF16) * (M_DH ** -0.5), rk.astype(BF16)], axis=0)
    w_gate = jnp.pad(mg.T, ((0, 0), (0, LANES - N_GATE_COLS))).astype(BF16)

    h, gates = _norm1_call(x, p['norm1_g'], mod, w_gate)
    proj = _inproj_call(h, w_main_t)
    proj_t = _inproj_t_call(h, w_kt)

    gb = gates[:, :N_GATE_COLS] + p['mlstm_gate_bias'].reshape(1, N_GATE_COLS)
    grow = gb.reshape(ROWS, 4, M_HEADS).transpose(2, 1, 0)
    grow = jnp.pad(grow, ((0, 0), (0, SUBLANES - 4), (0, 0)))

    sm_c, sm_n, sm_m, s_lh, s_rs = states
    prompt_kw = dict(t=T_PROMPT, n_seq=N_PROMPT_SEQ, row_block0=0, layer=layer, emit_state=True)
    sample_kw = dict(t=T_SAMPLE, n_seq=N_SAMPLE_SEQ, row_block0=ROWS_PROMPT // T_SAMPLE, layer=layer,
                     emit_state=False)
    pc, pn, pm, plh, prs = prev_states

    caug0 = jnp.concatenate(
        [sm_c, jnp.broadcast_to(sm_n[..., None], sm_n.shape + (LANES,))], axis=-1)
    m0 = jnp.broadcast_to(sm_m[..., None, None], (N_SAMPLE_SEQ, 2, M_HEADS, 1, LANES))
    (ym,) = _mlstm_call(proj, proj_t, grow, p['mlstm_norm_g'], (caug0, m0), [None], **sample_kw)
    ym, new_c, new_n, new_m = _mlstm_call(proj, proj_t, grow, p['mlstm_norm_g'], None,
                                          [ym, pc, pn, pm], **prompt_kw)

    lru_p = dict(conv_w=p['lru_conv_w'], conv_b=p['lru_conv_b'], wr=p['lru_wr'], wi=p['lru_wi'],
                 br=p['lru_br'], bi=p['lru_bi'], lam=p['lru_lambda'])
    (yl,) = _lru_call(proj, lru_p, s_lh, [None], **sample_kw)
    yl, new_lh = _lru_call(proj, lru_p, None, [yl, plh], **prompt_kw)

    decay = jnp.broadcast_to(p['ret_decay'].T[:, :, None], (R_HEADS, 2, LANES))
    (yr,) = _ret_call(proj, proj_t, decay, p['ret_norm_g'], rope_tabs, s_rs, [None], **sample_kw)
    yr, new_rs = _ret_call(proj, proj_t, decay, p['ret_norm_g'], None, None, [yr, prs], **prompt_kw)

    rw = jnp.pad(p['router_w'], ((0, 0), (0, LANES - N_EXPERTS)))
    rwh = rw.astype(BF16)
    rwl = (rw - rwh.astype(F32)).astype(BF16)
    xn, h2, logits = _merge_call(h, ym, yl, yr, x, mod, p['w_merge'].astype(BF16), p['b_merge'],
                                 p['w_branch'].astype(BF16), p['w_out'].astype(BF16), p['norm2_g'],
                                 rwh, rwl)

    ip, rows_s, gv = _route(logits)
    xs_p = _prompt_gather_call(h2, ip[:, :, None])
    xs_s = h2.at[rows_s].get(mode='promise_in_bounds')
    y_p, y_s = _expert_call(xs_p, xs_s, gv[..., None], mod, stacked['exp_w1'], stacked['exp_w3'],
                            stacked['exp_w2'], layer)
    x_out = xn
    for e in range(N_EXPERTS):
        x_out = x_out.at[rows_s[e]].add(y_s[e], unique_indices=True, mode='promise_in_bounds')
    x_out = _prompt_combine_call(x_out, y_p, ip[:, None, :])
    return x_out, (new_c, new_n, new_m, new_lh, new_rs)


def kernel(x_prompt, x_sample, c, state_mlstm_C, state_mlstm_n, state_mlstm_m, state_lru_h, state_ret_S, c_ctx, w_ada, b_ada, norm1_g, norm2_g, w_in, mlstm_gate_bias, mlstm_norm_g, lru_conv_w, lru_conv_b, lru_wr, lru_br, lru_wi, lru_bi, lru_lambda, ret_decay, ret_norm_g, w_branch, w_merge, b_merge, w_out, router_w, exp_w1, exp_w3, exp_w2, final_g):
    x = jnp.concatenate([x_prompt.reshape(ROWS_PROMPT, D), x_sample.reshape(-1, D)], axis=0)
    cond8 = jnp.concatenate([c_ctx[None, :], c, jnp.zeros((8 - 1 - N_SAMPLE_SEQ, D), F32)], axis=0)
    mod_all = _ada_call(cond8, w_ada, b_ada).reshape(DEPTH, 8, 6, D)
    rope_tabs = _rope_tables()
    stacked = dict(exp_w1=exp_w1, exp_w3=exp_w3, exp_w2=exp_w2)

    new_states = (None,) * 5
    for l in range(DEPTH):
        p = dict(norm1_g=norm1_g[l], norm2_g=norm2_g[l], w_in=w_in[l],
                 mlstm_gate_bias=mlstm_gate_bias[l], mlstm_norm_g=mlstm_norm_g[l],
                 lru_conv_w=lru_conv_w[l], lru_conv_b=lru_conv_b[l], lru_wr=lru_wr[l],
                 lru_br=lru_br[l], lru_wi=lru_wi[l], lru_bi=lru_bi[l], lru_lambda=lru_lambda[l],
                 ret_decay=ret_decay[l], ret_norm_g=ret_norm_g[l], w_branch=w_branch[l],
                 w_merge=w_merge[l], b_merge=b_merge[l], w_out=w_out[l], router_w=router_w[l])
        states = (state_mlstm_C[:, l], state_mlstm_n[:, l], state_mlstm_m[:, l],
                  state_lru_h[:, l], state_ret_S[:, l])
        x, new_states = _layer(x, mod_all[l], p, states, rope_tabs, stacked, l, new_states)

    y_prompt = _final_norm_call(x, final_g, 0, ROWS_PROMPT).reshape(N_PROMPT_SEQ, T_PROMPT, D)
    y_sample = _final_norm_call(x, final_g, ROWS_PROMPT, ROWS - ROWS_PROMPT).reshape(
        N_SAMPLE_SEQ, T_SAMPLE, D)
    new_c, new_n, new_m, new_lh, new_rs = new_states
    return (y_prompt, y_sample, new_c, new_n[:, :, :, :, 0, :], new_m[:, :, :, :, 0, 0], new_lh, new_rs)
```

```python
import functools

import jax
import jax.numpy as jnp
import numpy as np
from jax import lax
from jax.experimental import pallas as pl
from jax.experimental.pallas import tpu as pltpu

F32 = jnp.float32
BF16 = jnp.bfloat16

D = 1024
DEPTH = 2
N_PROMPT_SEQ = 16
T_PROMPT = 256
N_SAMPLE_SEQ = 2
T_SAMPLE = 4096
ROWS_PROMPT = N_PROMPT_SEQ * T_PROMPT
ROWS = ROWS_PROMPT + N_SAMPLE_SEQ * T_SAMPLE
GROUP_ROWS = 4096
GRID_W = 64
CHUNK = 256
EPS = 1e-6
F32_TINY = 1.1754944e-38
M_HEADS = 4
M_DH = 256
R_HEADS = 8
R_DH = 128
L_BLOCKS = 8
L_BW = 128
L_C = 8.0
ROPE_BASE = 10000.0
N_EXPERTS = 16
CAP_PROMPT = 2 * T_PROMPT // N_EXPERTS
CAP_SAMPLE = 2 * T_SAMPLE // N_EXPERTS
ROWS_PER_EXPERT = N_PROMPT_SEQ * CAP_PROMPT + N_SAMPLE_SEQ * CAP_SAMPLE
N_GATE_COLS = 16
LANES = 128
SUBLANES = 8
VMEM_LIMIT = 56 * 2 ** 20

COL_MQ, COL_MV, COL_MO, COL_LX, COL_LZ, COL_RQ, COL_RV, COL_RG = (i * D for i in range(8))
D_MAIN = 8 * D
ROW_MK, ROW_RK = 0, D
D_KT = 2 * D


def _cparams(sem, vmem=None):
    return pltpu.CompilerParams(dimension_semantics=sem, vmem_limit_bytes=vmem)


def _dot(a, b):
    return jnp.dot(a, b, preferred_element_type=F32)


def _dot_nt(a, b):
    return lax.dot_general(a, b, (((1,), (1,)), ((), ())), preferred_element_type=F32)


def _split3(x):
    a = x.astype(BF16)
    r = x - a.astype(F32)
    b = r.astype(BF16)
    c = (r - b.astype(F32)).astype(BF16)
    return a, b, c


def _log_sigmoid(x):
    return jnp.minimum(x, 0.0) - jnp.log1p(jnp.exp(-jnp.abs(x)))


def _sigmoid(x):
    return 0.5 * jnp.tanh(0.5 * x) + 0.5


def _rms(x):
    return x * lax.rsqrt(jnp.mean(x * x, axis=-1, keepdims=True) + EPS)


def _chunk(c):
    return pl.ds(pl.multiple_of(c * CHUNK, CHUNK), CHUNK)


def _alias_prev(args, in_specs, prev):
    aliases = {}
    for out_idx, arr in enumerate(prev):
        if arr is not None:
            aliases[len(args)] = out_idx
            args.append(arr)
            in_specs.append(pl.BlockSpec(memory_space=pl.ANY))
    return aliases, len(aliases)


def _ada_kernel(c_ref, w_ref, b_ref, o_ref):
    c = c_ref[...]
    s = (c * jax.nn.sigmoid(c)).astype(BF16)
    o_ref[0] = _dot(s, w_ref[0].astype(BF16)) + b_ref[0]


def _ada_call(cond8, w_ada, b_ada):
    tn = 1536
    return pl.pallas_call(
        _ada_kernel,
        grid=(DEPTH, 6 * D // tn),
        in_specs=[pl.BlockSpec((8, D), lambda l, j: (0, 0)),
                  pl.BlockSpec((1, D, tn), lambda l, j: (l, 0, j)),
                  pl.BlockSpec((1, 1, tn), lambda l, j: (l, 0, j))],
        out_specs=pl.BlockSpec((1, 8, tn), lambda l, j: (l, 0, j)),
        out_shape=jax.ShapeDtypeStruct((DEPTH, 8, 6 * D), F32),
        compiler_params=_cparams(("arbitrary", "arbitrary"), VMEM_LIMIT),
        name="ada",
    )(cond8, w_ada, b_ada.reshape(DEPTH, 1, 6 * D))


def _norm1_kernel(x_ref, g_ref, mod_ref, wg_ref, h_ref, gate_ref):
    y = _rms(x_ref[...]) * g_ref[...]
    h = (y * (1.0 + mod_ref[0, 1:2, :]) + mod_ref[0, 0:1, :]).astype(BF16)
    h_ref[...] = h
    gate_ref[...] = _dot(h, wg_ref[...])


def _norm1_call(x, g, mod, w_gate):
    tm = 512
    return pl.pallas_call(
        _norm1_kernel,
        grid=(ROWS // tm,),
        in_specs=[pl.BlockSpec((tm, D), lambda i: (i, 0)),
                  pl.BlockSpec((1, D), lambda i: (0, 0)),
                  pl.BlockSpec((1, 6, D), lambda i: (i * tm // GROUP_ROWS, 0, 0)),
                  pl.BlockSpec((D, LANES), lambda i: (0, 0))],
        out_specs=[pl.BlockSpec((tm, D), lambda i: (i, 0)),
                   pl.BlockSpec((tm, LANES), lambda i: (i, 0))],
        out_shape=[jax.ShapeDtypeStruct((ROWS, D), BF16),
                   jax.ShapeDtypeStruct((ROWS, LANES), F32)],
        compiler_params=_cparams(("arbitrary",), VMEM_LIMIT),
        name="norm1",
    )(x, g.reshape(1, D), mod, w_gate)


def _mm_kernel(x_ref, wt_ref, o_ref):
    o_ref[...] = _dot_nt(x_ref[...], wt_ref[...]).astype(o_ref.dtype)


def _inproj_call(h, w_main_t):
    tm, tn = 1024, 1024
    return pl.pallas_call(
        _mm_kernel,
        grid=(D_MAIN // tn, ROWS // tm),
        in_specs=[pl.BlockSpec((tm, D), lambda j, i: (i, 0)),
                  pl.BlockSpec((tn, D), lambda j, i: (j, 0))],
        out_specs=pl.BlockSpec((tm, tn), lambda j, i: (i, j)),
        out_shape=jax.ShapeDtypeStruct((ROWS, D_MAIN), BF16),
        compiler_params=_cparams(("arbitrary", "arbitrary"), VMEM_LIMIT),
        name="inproj",
    )(h, w_main_t)


def _mm_t_kernel(wt_ref, x_ref, o_ref):
    o_ref[...] = _dot_nt(wt_ref[...], x_ref[...]).astype(o_ref.dtype)


def _inproj_t_call(h, w_kt):
    tm, tn = 1024, 1024
    return pl.pallas_call(
        _mm_t_kernel,
        grid=(D_KT // tn, ROWS // tm),
        in_specs=[pl.BlockSpec((tn, D), lambda j, i: (j, 0)),
                  pl.BlockSpec((tm, D), lambda j, i: (i, 0))],
        out_specs=pl.BlockSpec((tn, tm), lambda j, i: (j, i)),
        out_shape=jax.ShapeDtypeStruct((D_KT, ROWS), BF16),
        compiler_params=_cparams(("arbitrary", "arbitrary"), VMEM_LIMIT),
        name="inproj_t",
    )(w_kt, h)


M_AUG = M_DH + LANES
M_GROUP = 1


def _mlstm_kernel(*refs, n_chunks, has_init, emit_state, n_prev):
    q_ref, kt_ref, v_ref, o_ref, g_ref, ng_ref = refs[:6]
    pos = 6
    if has_init:
        c0_ref, m0_ref = refs[pos:pos + 2]
        pos += 2
    pos += n_prev
    y_ref = refs[pos]
    pos += 1
    if emit_state:
        co_ref, no_ref, mo_ref = refs[pos:pos + 3]
        pos += 3
    caug_sc, m_sc, hf_sc, hb_sc = refs[pos:]

    zero_state = (not has_init) and n_chunks == 1
    if has_init:
        for g in range(M_GROUP):
            caug_sc[g] = c0_ref[0, :, g]
            m_sc[g] = m0_ref[0, :, g]
    else:
        if not zero_state:
            caug_sc[...] = jnp.zeros_like(caug_sc)
        m_sc[...] = jnp.zeros_like(m_sc)

    ri = lax.broadcasted_iota(jnp.int32, (CHUNK, CHUNK), 0)
    ci = lax.broadcasted_iota(jnp.int32, (CHUNK, CHUNK), 1)
    r8 = lax.broadcasted_iota(jnp.int32, (SUBLANES, CHUNK), 0)
    ones_b = jnp.ones((CHUNK, LANES), BF16)
    allowed = [ri >= ci, ci >= ri]
    allowed_b = [jnp.where(a, 1.0, 0.0).astype(BF16) for a in allowed]
    tri = [jnp.where(ri <= ci, 1.0, 0.0).astype(BF16), jnp.where(ri >= ci, 1.0, 0.0).astype(BF16)]

    def step(g, d, c, m_t):
        rows = _chunk(c)
        hcols = slice(g * M_DH, (g + 1) * M_DH)
        i_idx, f_idx = (0, 1) if d == 0 else (2, 3)
        g8 = g_ref[g, :, rows]
        ls = jnp.where(r8 == f_idx, _log_sigmoid(g8), g8)
        cum = sum(_dot(p, tri[d]) for p in _split3(ls))
        f_row = ls[f_idx:f_idx + 1]
        b_row = cum[f_idx:f_idx + 1]
        g_row = ls[i_idx:i_idx + 1] - b_row
        gmat = jnp.where(allowed[d], jnp.broadcast_to(g_row, (CHUNK, CHUNK)), -jnp.inf)
        mp_col = jnp.max(gmat, axis=-1, keepdims=True)
        mp = jnp.broadcast_to(mp_col, (CHUNK, LANES))
        fp = _split3(f_row)
        bc = _dot(allowed_b[d] * fp[0], ones_b) + _dot(allowed_b[d] * fp[1], ones_b)

        q = q_ref[rows, hcols]
        kt = kt_ref[hcols, rows]
        vaug = jnp.concatenate([v_ref[rows, hcols], ones_b], axis=1)
        s1 = _dot(q, kt) * jnp.exp(gmat - mp_col)
        sv1 = _dot(s1.astype(BF16), vaug)
        b_last_11 = b_row[:, CHUNK - 1:CHUNK] if d == 0 else b_row[:, 0:1]
        log_k = b_last_11 + g_row
        a_11 = jnp.max(log_k, axis=-1, keepdims=True)
        kw1 = (kt.astype(F32) * jnp.exp(log_k - a_11)).astype(BF16)
        u1 = _dot(kw1, vaug)
        b_last = jnp.broadcast_to(b_last_11, (SUBLANES, LANES))
        a_t = jnp.broadcast_to(a_11, (SUBLANES, LANES))

        m_b = jnp.broadcast_to(m_t[0:1, :], (CHUNK, LANES))
        mi = jnp.maximum(m_b, mp)
        r = jnp.exp(mp - mi)
        w_prev = jnp.exp(m_b - mi)
        floor = jnp.exp(-(bc + mi))
        if zero_state:
            qc = None
            den = r * sv1[:, M_DH:]
        else:
            qc = _dot(q, caug_sc[g, d].astype(BF16))
            den = r * sv1[:, M_DH:] + w_prev * qc[:, M_DH:]
        inv = 1.0 / jnp.maximum(jnp.abs(den), floor)
        h_sc = hf_sc if d == 0 else hb_sc
        for blk in range(M_DH // LANES):
            cols = slice(blk * LANES, (blk + 1) * LANES)
            num = r * sv1[:, cols]
            if not zero_state:
                num = num + w_prev * qc[:, cols]
            h_sc[rows, slice(g * M_DH + blk * LANES, g * M_DH + (blk + 1) * LANES)] = num * inv

        m_new = jnp.maximum(b_last + m_t, a_t)
        w_c = jnp.exp(b_last + m_t - m_new)[0:1, :]
        w_u = jnp.exp(a_t - m_new)[0:1, :]
        for blk in range(M_AUG // LANES):
            cols = slice(blk * LANES, (blk + 1) * LANES)
            if zero_state:
                caug_sc[g, d, :, cols] = w_u * u1[:, cols]
            else:
                caug_sc[g, d, :, cols] = w_c * caug_sc[g, d, :, cols] + w_u * u1[:, cols]
        return m_new

    def scan_body(c, carry):
        new = []
        for g in range(M_GROUP):
            new.append(step(g, 0, c, carry[2 * g]))
            new.append(step(g, 1, n_chunks - 1 - c, carry[2 * g + 1]))
        return tuple(new)

    m_init = tuple(jnp.broadcast_to(m_sc[g, d], (SUBLANES, LANES))
                   for g in range(M_GROUP) for d in range(2))
    m_fin = lax.fori_loop(0, n_chunks, scan_body, m_init)
    for g in range(M_GROUP):
        for d in range(2):
            m_sc[g, d] = m_fin[2 * g + d][0:1, :]

    def out_body(c, carry):
        rows = _chunk(c)
        for g in range(M_GROUP):
            hcols = slice(g * M_DH, (g + 1) * M_DH)
            y = _rms(hf_sc[rows, hcols] + hb_sc[rows, hcols]) * ng_ref[:, hcols]
            y_ref[rows, hcols] = (y * _sigmoid(o_ref[rows, hcols].astype(F32))).astype(y_ref.dtype)
        return carry

    lax.fori_loop(0, n_chunks, out_body, 0)

    if emit_state:
        for g in range(M_GROUP):
            co_ref[0, 0, :, g] = caug_sc[g, :, :, 0:M_DH]
            for d in range(2):
                no_ref[0, 0, d, g] = caug_sc[g, d, :, M_DH:].T[0:1, :]
            mo_ref[0, 0, :, g] = m_sc[g]


def _mlstm_call(proj, proj_t, grow, norm_g, state, prev, *, t, n_seq, row_block0, layer, emit_state):
    has_init = state is not None

    gw = M_GROUP * M_DH
    big = dict(pipeline_mode=pl.Buffered(1)) if t * gw * 2 > 2 ** 21 else {}

    def col(base):
        return lambda s, h: (row_block0 + s, base // gw + h)

    st_caug = pl.BlockSpec((1, 2, M_GROUP, M_DH, M_AUG), lambda s, h: (s, 0, h, 0, 0))
    st_m = pl.BlockSpec((1, 2, M_GROUP, 1, LANES), lambda s, h: (s, 0, h, 0, 0))
    so_c = pl.BlockSpec((1, 1, 2, M_GROUP, M_DH, M_DH), lambda s, h: (s, layer, 0, h, 0, 0))
    so_n = pl.BlockSpec((1, 1, 2, M_GROUP, 1, M_DH), lambda s, h: (s, layer, 0, h, 0, 0))
    so_m = pl.BlockSpec((1, 1, 2, M_GROUP, 1, LANES), lambda s, h: (s, layer, 0, h, 0, 0))
    in_specs = [pl.BlockSpec((t, gw), col(COL_MQ), **big),
                pl.BlockSpec((gw, t), lambda s, h: (ROW_MK // gw + h, row_block0 + s), **big),
                pl.BlockSpec((t, gw), col(COL_MV), **big),
                pl.BlockSpec((t, gw), col(COL_MO), **big),
                pl.BlockSpec((M_GROUP, SUBLANES, t), lambda s, h: (h, 0, row_block0 + s)),
                pl.BlockSpec((1, gw), lambda s, h: (0, h))]
    args = [proj, proj_t, proj, proj, grow, norm_g.reshape(1, D)]
    if has_init:
        in_specs += [st_caug, st_m]
        args += list(state)
    out_specs = [pl.BlockSpec((t, gw), lambda s, h: (row_block0 + s, h))]
    out_shape = [jax.ShapeDtypeStruct((ROWS, D), BF16)]
    if emit_state:
        out_specs += [so_c, so_n, so_m]
        out_shape += [jax.ShapeDtypeStruct((n_seq, DEPTH, 2, M_HEADS, M_DH, M_DH), F32),
                      jax.ShapeDtypeStruct((n_seq, DEPTH, 2, M_HEADS, 1, M_DH), F32),
                      jax.ShapeDtypeStruct((n_seq, DEPTH, 2, M_HEADS, 1, LANES), F32)]
    aliases, n_prev = _alias_prev(args, in_specs, prev)
    return pl.pallas_call(
        functools.partial(_mlstm_kernel, n_chunks=t // CHUNK, has_init=has_init,
                          emit_state=emit_state, n_prev=n_prev),
        grid=(n_seq, M_HEADS // M_GROUP),
        in_specs=in_specs,
        out_specs=out_specs,
        out_shape=out_shape,
        input_output_aliases=aliases,
        scratch_shapes=[pltpu.VMEM((M_GROUP, 2, M_DH, M_AUG), F32),
                        pltpu.VMEM((M_GROUP, 2, 1, LANES), F32),
                        pltpu.VMEM((t, gw), F32),
                        pltpu.VMEM((t, gw), F32)],
        compiler_params=_cparams(("arbitrary", "arbitrary"), VMEM_LIMIT),
        name="mlstm",
    )(*args)


R_GROUP = 2

def _ret_kernel(*refs, n_chunks, rope, has_init, emit_state, n_prev):
    q_ref, kt_ref, v_ref, g_ref, dec_ref, ng_ref = refs[:6]
    pos = 6
    if rope:
        cos_ref, sin_ref, cos_t_ref, sin_t_ref = refs[pos:pos + 4]
        pos += 4
    if has_init:
        s0_ref = refs[pos]
        pos += 1
    pos += n_prev
    y_ref = refs[pos]
    pos += 1
    if emit_state:
        so_ref = refs[pos]
        pos += 1
    s_sc, of_sc, ob_sc, ktr_sc = refs[pos:pos + 4]
    qr_sc = refs[pos + 4] if rope else None

    zero_state = (not has_init) and n_chunks == 1
    if has_init:
        for g in range(R_GROUP):
            s_sc[g] = s0_ref[0, :, g]
    elif not zero_state:
        s_sc[...] = jnp.zeros_like(s_sc)

    ri = lax.broadcasted_iota(jnp.int32, (CHUNK, CHUNK), 0)
    ci = lax.broadcasted_iota(jnp.int32, (CHUNK, CHUNK), 1)
    quarter = R_DH // 4
    if rope:
        fr = lax.broadcasted_iota(jnp.int32, (R_DH, R_DH), 0)
        fc = lax.broadcasted_iota(jnp.int32, (R_DH, R_DH), 1)
        partner = fc + jnp.where((fc % (2 * quarter)) < quarter, quarter, -quarter)
        perm_b = jnp.where(fr == partner, 1.0, 0.0).astype(BF16)

    def prep_body(c, carry):
        rows = _chunk(c)
        for g in range(R_GROUP):
            hcols = slice(g * R_DH, (g + 1) * R_DH)
            ktf = kt_ref[hcols, rows].astype(F32) * (R_DH ** -0.5)
            if rope:
                swapped = jnp.concatenate([ktf[quarter:2 * quarter], ktf[0:quarter],
                                           ktf[3 * quarter:], ktf[2 * quarter:3 * quarter]], axis=0)
                ktf = ktf * cos_t_ref[:, rows] + swapped * sin_t_ref[:, rows]
                q = q_ref[rows, hcols]
                qr = q.astype(F32) * cos_ref[rows, :] + _dot(q, perm_b) * sin_ref[rows, :]
                qr_sc[rows, hcols] = qr.astype(BF16)
            ktr_sc[hcols, rows] = ktf.astype(BF16)
        return carry

    lax.fori_loop(0, n_chunks, prep_body, 0, unroll=min(n_chunks, 2))
    qsrc = qr_sc if rope else q_ref

    lane = lax.broadcasted_iota(jnp.int32, (1, CHUNK), 1).astype(F32)
    row_f = lax.broadcasted_iota(jnp.int32, (CHUNK, R_DH), 0).astype(F32)
    consts = []
    for gd in range(2 * R_GROUP):
        g, d = divmod(gd, 2)
        lg = -jnp.exp(dec_ref[g, d:d + 1, :])
        lg11 = lg[:, 0:1]
        diff = (ri - ci) if d == 0 else (ci - ri)
        causal = diff >= 0
        dmat = jnp.where(causal, jnp.exp(lg11 * jnp.where(causal, diff, 0).astype(F32)), 0.0)
        if d == 0:
            q_dec = jnp.exp(lg * (row_f + 1.0))
            k_dec = jnp.exp(lg11 * (CHUNK - 1.0 - lane))
        else:
            q_dec = jnp.exp(lg * (CHUNK - row_f))
            k_dec = jnp.exp(lg11 * lane)
        g_chunk = jnp.exp(lg * float(CHUNK))
        consts.append((dmat, q_dec, k_dec, g_chunk))

    def step(g, d, c):
        dmat, q_dec, k_dec, g_chunk = consts[2 * g + d]
        rows = _chunk(c)
        hcols = slice(g * R_DH, (g + 1) * R_DH)
        q = qsrc[rows, hcols]
        kt = ktr_sc[hcols, rows]
        v = v_ref[rows, hcols]
        s = _dot(q, kt) * dmat
        o = _dot(s.astype(BF16), v)
        if not zero_state:
            o = o + q_dec * _dot(q, s_sc[g, d].astype(BF16))
        if d == 0:
            of_sc[rows, hcols] = o
        else:
            ob_sc[rows, hcols] = o
        kd = (kt.astype(F32) * k_dec).astype(BF16)
        if zero_state:
            s_sc[g, d] = _dot(kd, v)
        else:
            s_sc[g, d] = g_chunk * s_sc[g, d] + _dot(kd, v)

    def scan_body(c, carry):
        for g in range(R_GROUP):
            step(g, 0, c)
            step(g, 1, n_chunks - 1 - c)
        return carry

    lax.fori_loop(0, n_chunks, scan_body, 0, unroll=min(n_chunks, 2))

    def out_body(c, carry):
        rows = _chunk(c)
        for g in range(R_GROUP):
            hcols = slice(g * R_DH, (g + 1) * R_DH)
            y = _rms(of_sc[rows, hcols] + ob_sc[rows, hcols]) * ng_ref[:, hcols]
            gate = g_ref[rows, hcols].astype(F32)
            y_ref[rows, hcols] = (y * (gate * _sigmoid(gate))).astype(y_ref.dtype)
        return carry

    lax.fori_loop(0, n_chunks, out_body, 0, unroll=min(n_chunks, 2))

    if emit_state:
        for g in range(R_GROUP):
            so_ref[0, 0, :, g] = s_sc[g]


def _ret_call(proj, proj_t, decay, norm_g, rope_tabs, state, prev, *, t, n_seq, row_block0, layer, emit_state):
    has_init = state is not None
    rope = rope_tabs is not None

    gw = R_GROUP * R_DH

    def col(base):
        return lambda s, h: (row_block0 + s, base // gw + h)

    st_s = pl.BlockSpec((1, 2, R_GROUP, R_DH, R_DH), lambda s, h: (s, 0, h, 0, 0))
    in_specs = [pl.BlockSpec((t, gw), col(COL_RQ)),
                pl.BlockSpec((gw, t), lambda s, h: (ROW_RK // gw + h, row_block0 + s)),
                pl.BlockSpec((t, gw), col(COL_RV)),
                pl.BlockSpec((t, gw), col(COL_RG)),
                pl.BlockSpec((R_GROUP, 2, LANES), lambda s, h: (h, 0, 0)),
                pl.BlockSpec((1, gw), lambda s, h: (0, h))]
    args = [proj, proj_t, proj, proj, decay, norm_g.reshape(1, D)]
    if rope:
        in_specs += [pl.BlockSpec((t, R_DH), lambda s, h: (0, 0))] * 2
        in_specs += [pl.BlockSpec((R_DH, t), lambda s, h: (0, 0))] * 2
        args += list(rope_tabs)
    if has_init:
        in_specs.append(st_s)
        args.append(state)
    out_specs = [pl.BlockSpec((t, gw), lambda s, h: (row_block0 + s, h))]
    out_shape = [jax.ShapeDtypeStruct((ROWS, D), BF16)]
    if emit_state:
        out_specs.append(pl.BlockSpec((1, 1, 2, R_GROUP, R_DH, R_DH), lambda s, h: (s, layer, 0, h, 0, 0)))
        out_shape.append(jax.ShapeDtypeStruct((n_seq, DEPTH, 2, R_HEADS, R_DH, R_DH), F32))
    aliases, n_prev = _alias_prev(args, in_specs, prev)
    scratch = [pltpu.VMEM((R_GROUP, 2, R_DH, R_DH), F32),
               pltpu.VMEM((t, gw), F32),
               pltpu.VMEM((t, gw), F32),
               pltpu.VMEM((gw, t), BF16)]
    if rope:
        scratch.append(pltpu.VMEM((t, gw), BF16))
    return pl.pallas_call(
        functools.partial(_ret_kernel, n_chunks=t // CHUNK, rope=rope, has_init=has_init,
                          emit_state=emit_state, n_prev=n_prev),
        grid=(n_seq, R_HEADS // R_GROUP),
        in_specs=in_specs,
        out_specs=out_specs,
        out_shape=out_shape,
        input_output_aliases=aliases,
        scratch_shapes=scratch,
        compiler_params=_cparams(("arbitrary", "arbitrary"), VMEM_LIMIT),
        name="retention",
    )(*args)


LRU_SLAB = 64


def _tile_scan(a, b, reverse):
    row = lax.broadcasted_iota(jnp.int32, (SUBLANES, L_BW), 0)
    for k in (1, 2, 4):
        if reverse:
            keep = row < SUBLANES - k
            shift = SUBLANES - k
        else:
            keep = row >= k
            shift = k
        a_sh = jnp.where(keep, pltpu.roll(a, shift, 0), 1.0)
        b_sh = jnp.where(keep, pltpu.roll(b, shift, 0), 0.0)
        b = a * b_sh + b
        a = a * a_sh
    return a, b


def _lru_kernel(*refs, t, has_init, emit_state, n_prev):
    lx_ref, lz_ref, cw_ref, cb_ref, wr_ref, wi_ref, br_ref, bi_ref, lam_ref = refs[:9]
    pos = 9
    if has_init:
        h0_ref = refs[pos]
        pos += 1
    pos += n_prev
    y_ref = refs[pos]
    pos += 1
    if emit_state:
        hfin_ref = refs[pos]
        pos += 1
    xpad_sc, a_sc, b_sc = refs[pos:]
    rc = min(t, 256)

    xpad_sc[0:8, :] = jnp.zeros((8, L_BW), F32)
    xpad_sc[t + 8:t + 16, :] = jnp.zeros((8, L_BW), F32)

    def pad_body(c, carry):
        r0 = pl.multiple_of(c * rc, rc)
        xpad_sc[pl.ds(r0 + 8, rc), :] = lx_ref[pl.ds(r0, rc), :].astype(F32)
        return carry

    lax.fori_loop(0, t // rc, pad_body, 0)

    wr = [(0.5 * wr_ref[d, 0]).astype(BF16) for d in range(2)]
    wi = [(0.5 * wi_ref[d, 0]).astype(BF16) for d in range(2)]
    half_br = [0.5 * br_ref[d:d + 1, :] for d in range(2)]
    half_bi = [0.5 * bi_ref[d:d + 1, :] for d in range(2)]
    half_c = []
    for d in range(2):
        lam = lam_ref[d:d + 1, :]
        softplus_neg = jnp.maximum(-lam, 0.0) + jnp.log1p(jnp.exp(-jnp.abs(lam)))
        half_c.append(-0.5 * L_C * softplus_neg)

    def gate_body(c, carry):
        r0 = pl.multiple_of(c * rc, rc)
        xe = xpad_sc[pl.ds(r0, rc + 16), :]
        n = rc + 16
        u = (cb_ref[...] + cw_ref[0:1, :] * pltpu.roll(xe, 2, 0)[8:8 + rc]
             + cw_ref[1:2, :] * pltpu.roll(xe, 1, 0)[8:8 + rc]
             + cw_ref[2:3, :] * xe[8:8 + rc]
             + cw_ref[3:4, :] * pltpu.roll(xe, n - 1, 0)[8:8 + rc])
        ub = u.astype(BF16)
        half_u = 0.5 * u
        for d in range(2):
            tr = jnp.tanh(_dot(ub, wr[d]) + half_br[d])
            ti = jnp.tanh(_dot(ub, wi[d]) + half_bi[d])
            log_a = half_c[d] * tr + half_c[d]
            a = jnp.exp(log_a)
            z = jnp.tanh(log_a) * (-1.0 - a * a)
            mult = z * lax.rsqrt(jnp.maximum(z, F32_TINY))
            a_sc[d, pl.ds(r0, rc), :] = a
            b_sc[d, pl.ds(r0, rc), :] = mult * (ti * half_u + half_u)
        return carry

    lax.fori_loop(0, t // rc, gate_body, 0)

    n_slabs = t // LRU_SLAB
    tiles = LRU_SLAB // SUBLANES

    row8 = lax.broadcasted_iota(jnp.int32, (SUBLANES, L_BW), 0)

    def bcast_row(x, r):
        return jnp.broadcast_to(x[r:r + 1, :], (SUBLANES, L_BW))

    def slab_scan(d, r0, c_in):
        reverse = d == 1
        edge = 0 if reverse else SUBLANES - 1
        local = []
        spa = jnp.ones((SUBLANES, L_BW), F32)
        shl = jnp.zeros((SUBLANES, L_BW), F32)
        for k in range(tiles):
            rows = pl.ds(r0 + k * SUBLANES, SUBLANES)
            pa, hl = _tile_scan(a_sc[d, rows, :], b_sc[d, rows, :], reverse)
            local.append((rows, pa, hl))
            spa = jnp.where(row8 == k, bcast_row(pa, edge), spa)
            shl = jnp.where(row8 == k, bcast_row(hl, edge), shl)
        cpa, chl = _tile_scan(spa, shl, reverse)
        after = chl + cpa * c_in
        for k, (rows, pa, hl) in enumerate(local):
            prev = k + 1 if reverse else k - 1
            cin = c_in if (prev < 0 or prev >= tiles) else bcast_row(after, prev)
            b_sc[d, rows, :] = hl + pa * cin
        return bcast_row(after, 0 if reverse else tiles - 1)

    def scan_body(c, carry):
        cf, cbk = carry
        cf = slab_scan(0, pl.multiple_of(c * LRU_SLAB, LRU_SLAB), cf)
        cbk = slab_scan(1, pl.multiple_of((n_slabs - 1 - c) * LRU_SLAB, LRU_SLAB), cbk)
        return cf, cbk

    if has_init:
        cf0 = jnp.broadcast_to(h0_ref[0, 0:1, :], (SUBLANES, L_BW))
        cb0 = jnp.broadcast_to(h0_ref[0, 1:2, :], (SUBLANES, L_BW))
    else:
        cf0 = jnp.zeros((SUBLANES, L_BW), F32)
        cb0 = jnp.zeros((SUBLANES, L_BW), F32)
    cf, cbk = lax.fori_loop(0, n_slabs, scan_body, (cf0, cb0))
    if emit_state:
        hfin_ref[0, 0, 0:1, :] = cf[0:1, :]
        hfin_ref[0, 0, 1:2, :] = cbk[0:1, :]

    def out_body(c, carry):
        rows = pl.ds(pl.multiple_of(c * rc, rc), rc)
        z = lz_ref[rows, :].astype(F32)
        gelu = 0.5 * z * (1.0 + jnp.tanh(0.7978845608028654 * (z + 0.044715 * (z * z * z))))
        y_ref[rows, :] = ((b_sc[0, rows, :] + b_sc[1, rows, :]) * gelu).astype(y_ref.dtype)
        return carry

    lax.fori_loop(0, t // rc, out_body, 0)


def _lru_call(proj, p, state, prev, *, t, n_seq, row_block0, layer, emit_state):
    has_init = state is not None

    def col(base):
        return lambda s, b: (row_block0 + s, base // L_BW + b)

    vec2 = pl.BlockSpec((2, L_BW), lambda s, b: (0, b))
    st = pl.BlockSpec((1, 2, L_BW), lambda s, b: (s, 0, b))
    in_specs = [pl.BlockSpec((t, L_BW), col(COL_LX)),
                pl.BlockSpec((t, L_BW), col(COL_LZ)),
                pl.BlockSpec((4, L_BW), lambda s, b: (0, b)),
                pl.BlockSpec((1, L_BW), lambda s, b: (0, b)),
                pl.BlockSpec((2, 1, L_BW, L_BW), lambda s, b: (0, b, 0, 0)),
                pl.BlockSpec((2, 1, L_BW, L_BW), lambda s, b: (0, b, 0, 0)),
                vec2, vec2, vec2]
    args = [proj, proj, p['conv_w'], p['conv_b'].reshape(1, D), p['wr'], p['wi'],
            p['br'], p['bi'], p['lam']]
    if has_init:
        in_specs.append(st)
        args.append(state)
    out_specs = [pl.BlockSpec((t, L_BW), lambda s, b: (row_block0 + s, b))]
    out_shape = [jax.ShapeDtypeStruct((ROWS, D), BF16)]
    if emit_state:
        out_specs.append(pl.BlockSpec((1, 1, 2, L_BW), lambda s, b: (s, layer, 0, b)))
        out_shape.append(jax.ShapeDtypeStruct((n_seq, DEPTH, 2, D), F32))
    aliases, n_prev = _alias_prev(args, in_specs, prev)
    return pl.pallas_call(
        functools.partial(_lru_kernel, t=t, has_init=has_init, emit_state=emit_state, n_prev=n_prev),
        grid=(n_seq, L_BLOCKS),
        in_specs=in_specs,
        out_specs=out_specs,
        out_shape=out_shape,
        input_output_aliases=aliases,
        scratch_shapes=[pltpu.VMEM((t + 16, L_BW), F32),
                        pltpu.VMEM((2, t, L_BW), F32),
                        pltpu.VMEM((2, t, L_BW), F32)],
        compiler_params=_cparams(("arbitrary", "arbitrary"), VMEM_LIMIT),
        name="rglru",
    )(*args)


def _merge_kernel(h_ref, ym_ref, yl_ref, yr_ref, x_ref, mod_ref, wm_ref, bm_ref, wb_ref, wo_ref,
                  n2_ref, rwh_ref, rwl_ref, xo_ref, h2_ref, lg_ref):
    h = h_ref[...]
    merged = None
    for k, y_ref in enumerate((ym_ref, yl_ref, yr_ref)):
        gate = jax.nn.sigmoid(_dot(h, wm_ref[:, k * D:(k + 1) * D]) + bm_ref[:, k * D:(k + 1) * D])
        term = gate * _dot(y_ref[...], wb_ref[k])
        merged = term if merged is None else merged + term
    mix = _dot(merged.astype(BF16), wo_ref[...])
    xn = x_ref[...] + mod_ref[0, 2:3, :] * mix
    xo_ref[...] = xn
    hn = _rms(xn) * n2_ref[...] * (1.0 + mod_ref[0, 4:5, :]) + mod_ref[0, 3:4, :]
    hi = hn.astype(BF16)
    h2_ref[...] = hi
    lo = (hn - hi.astype(F32)).astype(BF16)
    lg_ref[...] = _dot(hi, rwh_ref[...]) + _dot(lo, rwh_ref[...]) + _dot(hi, rwl_ref[...])


def _merge_call(h, ym, yl, yr, x, mod, wm, bm, wb, wo, n2g, rwh, rwl):
    tm = 256
    row = lambda i: (i, 0)
    const2 = lambda i: (0, 0)
    return pl.pallas_call(
        _merge_kernel,
        grid=(ROWS // tm,),
        in_specs=[pl.BlockSpec((tm, D), row),
                  pl.BlockSpec((tm, D), row),
                  pl.BlockSpec((tm, D), row),
                  pl.BlockSpec((tm, D), row),
                  pl.BlockSpec((tm, D), row),
                  pl.BlockSpec((1, 6, D), lambda i: (i * tm // GROUP_ROWS, 0, 0)),
                  pl.BlockSpec((D, 3 * D), const2),
                  pl.BlockSpec((1, 3 * D), const2),
                  pl.BlockSpec((3, D, D), lambda i: (0, 0, 0)),
                  pl.BlockSpec((D, D), const2),
                  pl.BlockSpec((1, D), const2),
                  pl.BlockSpec((D, LANES), const2),
                  pl.BlockSpec((D, LANES), const2)],
        out_specs=[pl.BlockSpec((tm, D), row),
                   pl.BlockSpec((tm, D), row),
                   pl.BlockSpec((tm, LANES), row)],
        out_shape=[jax.ShapeDtypeStruct((ROWS, D), F32),
                   jax.ShapeDtypeStruct((ROWS, D), BF16),
                   jax.ShapeDtypeStruct((ROWS, LANES), F32)],
        compiler_params=_cparams(("arbitrary",), VMEM_LIMIT),
        name="merge",
    )(h, ym, yl, yr, x, mod, wm, bm.reshape(1, 3 * D), wb, wo, n2g.reshape(1, D), rwh, rwl)


EXPERT_TM = 512
assert N_PROMPT_SEQ * CAP_PROMPT == EXPERT_TM and CAP_SAMPLE == EXPERT_TM


def _expert_kernel(xp_ref, xs_ref, g_ref, mod_ref, w1_ref, w3_ref, w2_ref, yp_ref, ys_ref,
                   w1_sc, w3_sc, w2_sc):
    m = pl.program_id(1)

    @pl.when(m == 0)
    def _():
        w1_sc[...] = w1_ref[0, 0].astype(BF16)
        w3_sc[...] = w3_ref[0, 0].astype(BF16)
        w2_sc[...] = w2_ref[0, 0].astype(BF16)

    xs = jnp.where(m == 0, xp_ref[0], xs_ref[0])
    a = _dot(xs, w1_sc[...])
    b = _dot(xs, w3_sc[...])
    mid = (a * jax.nn.sigmoid(a) * b).astype(BF16)

    def down(y_ref):
        y_ref[0] = (_dot(mid, w2_sc[...]) * g_ref[0]) * mod_ref[0, 5:6, :]

    pl.when(m == 0)(functools.partial(down, yp_ref))
    pl.when(m > 0)(functools.partial(down, ys_ref))


def _expert_call(xs_p, xs_s, gv, mod, w1, w3, w2, layer):
    tm = EXPERT_TM
    wspec = pl.BlockSpec((1, 1, D, D), lambda e, m: (layer, e, 0, 0))
    p_spec = pl.BlockSpec((1, tm, D), lambda e, m: (e, 0, 0))
    s_spec = pl.BlockSpec((1, tm, D), lambda e, m: (e, jnp.maximum(m - 1, 0), 0))
    return pl.pallas_call(
        _expert_kernel,
        grid=(N_EXPERTS, ROWS_PER_EXPERT // tm),
        in_specs=[p_spec, s_spec,
                  pl.BlockSpec((1, tm, 1), lambda e, m: (e, m, 0)),
                  pl.BlockSpec((1, 6, D), lambda e, m: (m, 0, 0)),
                  wspec, wspec, wspec],
        out_specs=[p_spec, s_spec],
        out_shape=[jax.ShapeDtypeStruct((N_EXPERTS, tm, D), F32),
                   jax.ShapeDtypeStruct((N_EXPERTS, N_SAMPLE_SEQ * tm, D), F32)],
        scratch_shapes=[pltpu.VMEM((D, D), BF16)] * 3,
        compiler_params=_cparams(("arbitrary", "arbitrary"), VMEM_LIMIT),
        name="experts",
    )(xs_p, xs_s, gv, mod, w1, w3, w2)


PROMPT_SLOTS = N_EXPERTS * CAP_PROMPT


def _slot_onehot(rank_rows):
    slot = lax.broadcasted_iota(jnp.int32, (CAP_PROMPT, T_PROMPT), 0).astype(F32)
    return jnp.concatenate(
        [jnp.where(slot == rank_rows[e:e + 1, :], 1.0, 0.0) for e in range(N_EXPERTS)], axis=0)


def _prompt_route_kernel(lg_ref, h_ref, xs_ref, gv_ref, rank_ref, rank_sc):
    lane = lax.broadcasted_iota(jnp.int32, (T_PROMPT, LANES), 1)
    x = jnp.where(lane < N_EXPERTS, lg_ref[...], -jnp.inf)
    e = jnp.exp(x - jnp.max(x, axis=-1, keepdims=True))
    aff = e / jnp.sum(e, axis=-1, keepdims=True)
    aff_t = aff.T
    ti = lax.broadcasted_iota(jnp.int32, (T_PROMPT, T_PROMPT), 0)
    tj = lax.broadcasted_iota(jnp.int32, (T_PROMPT, T_PROMPT), 1)
    earlier = jnp.where(ti < tj, 1.0, 0.0)
    for ex in range(N_EXPERTS):
        a_row = aff_t[ex:ex + 1, :]
        a_col = aff[:, ex:ex + 1]
        ahead = jnp.where(a_col > a_row, 1.0, jnp.where(a_col == a_row, earlier, 0.0))
        rank_sc[ex:ex + 1, :] = jnp.sum(ahead, axis=0, keepdims=True)
    rank = rank_sc[...]
    onehot = _slot_onehot(rank)
    xs_ref[...] = _dot(onehot.astype(BF16), h_ref[...]).astype(xs_ref.dtype).reshape(
        N_EXPERTS, CAP_PROMPT, D)
    for ex in range(N_EXPERTS):
        blk = onehot[ex * CAP_PROMPT:(ex + 1) * CAP_PROMPT, :]
        gv_ref[ex] = jnp.sum(blk * aff_t[ex:ex + 1, :], axis=-1, keepdims=True)
    rank_ref[0] = rank


def _prompt_route_call(logits, h2):
    return pl.pallas_call(
        _prompt_route_kernel,
        grid=(N_PROMPT_SEQ,),
        in_specs=[pl.BlockSpec((T_PROMPT, LANES), lambda s: (s, 0)),
                  pl.BlockSpec((T_PROMPT, D), lambda s: (s, 0))],
        out_specs=[pl.BlockSpec((N_EXPERTS, CAP_PROMPT, D), lambda s: (0, s, 0)),
                   pl.BlockSpec((N_EXPERTS, CAP_PROMPT, 1), lambda s: (0, s, 0)),
                   pl.BlockSpec((1, N_EXPERTS, T_PROMPT), lambda s: (s, 0, 0))],
        out_shape=[jax.ShapeDtypeStruct((N_EXPERTS, N_PROMPT_SEQ * CAP_PROMPT, D), BF16),
                   jax.ShapeDtypeStruct((N_EXPERTS, N_PROMPT_SEQ * CAP_PROMPT, 1), F32),
                   jax.ShapeDtypeStruct((N_PROMPT_SEQ, N_EXPERTS, T_PROMPT), F32)],
        scratch_shapes=[pltpu.VMEM((N_EXPERTS, T_PROMPT), F32)],
        compiler_params=_cparams(("arbitrary",), VMEM_LIMIT),
        name="prompt_route",
    )(logits, h2)


def _prompt_combine_kernel(x_ref, y_ref, rank_ref, o_ref):
    onehot = _slot_onehot(rank_ref[0]).astype(BF16)
    y = y_ref[...].reshape(PROMPT_SLOTS, D)
    hi = y.astype(BF16)
    lo = (y - hi.astype(F32)).astype(BF16)
    tn = (((0,), (0,)), ((), ()))
    o_ref[...] = x_ref[...] + (lax.dot_general(onehot, hi, tn, preferred_element_type=F32)
                               + lax.dot_general(onehot, lo, tn, preferred_element_type=F32))


def _prompt_combine_call(x, y_p, rank):
    return pl.pallas_call(
        _prompt_combine_kernel,
        grid=(N_PROMPT_SEQ,),
        in_specs=[pl.BlockSpec((T_PROMPT, D), lambda s: (s, 0)),
                  pl.BlockSpec((N_EXPERTS, CAP_PROMPT, D), lambda s: (0, s, 0)),
                  pl.BlockSpec((1, N_EXPERTS, T_PROMPT), lambda s: (s, 0, 0))],
        out_specs=pl.BlockSpec((T_PROMPT, D), lambda s: (s, 0)),
        out_shape=jax.ShapeDtypeStruct((ROWS, D), F32),
        input_output_aliases={0: 0},
        compiler_params=_cparams(("arbitrary",), VMEM_LIMIT),
        name="prompt_combine",
    )(x, y_p, rank)


def _final_norm_kernel(x_ref, g_ref, o_ref):
    o_ref[...] = _rms(x_ref[...]) * g_ref[...]


def _final_norm_call(x, g, row0, n_rows):
    tm = 512
    return pl.pallas_call(
        _final_norm_kernel,
        grid=(n_rows // tm,),
        in_specs=[pl.BlockSpec((tm, D), lambda i: (row0 // tm + i, 0)),
                  pl.BlockSpec((1, D), lambda i: (0, 0))],
        out_specs=pl.BlockSpec((tm, D), lambda i: (i, 0)),
        out_shape=jax.ShapeDtypeStruct((n_rows, D), F32),
        compiler_params=_cparams(("arbitrary",), VMEM_LIMIT),
        name="final_norm",
    )(x, g.reshape(1, D))


def _rope_tables():
    tpos = np.arange(T_SAMPLE)
    lane = np.arange(R_DH)
    pos = np.where(lane[None, :] < R_DH // 2, (tpos // GRID_W)[:, None], (tpos % GRID_W)[:, None])
    n_freq = R_DH // 4
    freqs = np.power(np.float32(ROPE_BASE), -np.arange(n_freq, dtype=np.float32) / np.float32(n_freq))
    ang = pos.astype(np.float32) * freqs[lane % n_freq][None, :]
    first = ((lane % (R_DH // 2)) < n_freq)[None, :]
    cos, sin = np.cos(ang), np.sin(ang)
    sin = np.where(first, -sin, sin)
    return tuple(jnp.asarray(a, F32) for a in (cos, sin, np.ascontiguousarray(cos.T), np.ascontiguousarray(sin.T)))


def _route_sample(logits):
    aff = jax.nn.softmax(logits[ROWS_PROMPT:, :N_EXPERTS], axis=-1)
    gs, is_ = lax.top_k(aff.reshape(N_SAMPLE_SEQ, T_SAMPLE, N_EXPERTS).swapaxes(1, 2), CAP_SAMPLE)
    is_ = is_ + ROWS_PROMPT + (jnp.arange(N_SAMPLE_SEQ) * T_SAMPLE)[:, None, None]
    rows_s = is_.swapaxes(0, 1).reshape(N_EXPERTS, -1)
    return rows_s, gs.swapaxes(0, 1).reshape(N_EXPERTS, -1)


def _layer(x, mod, p, states, rope_tabs, stacked, layer, prev_states):
    mq, mk, mv, mo, mg, lx, lz, rq, rk, rv, rg = jnp.split(
        p['w_in'].T, [1024, 2048, 3072, 4096, 4112, 5136, 6160, 7184, 8208, 9232], axis=0)
    w_main_t = jnp.concatenate([mq, mv, mo, lx, lz, rq, rv, rg], axis=0).astype(BF16)
    w_kt = jnp.concatenate([mk.astype(BF16) * (M_DH ** -0.5), rk.astype(BF16)], axis=0)
    w_gate = jnp.pad(mg.T, ((0, 0), (0, LANES - N_GATE_COLS))).astype(BF16)

    h, gates = _norm1_call(x, p['norm1_g'], mod, w_gate)
    proj = _inproj_call(h, w_main_t)
    proj_t = _inproj_t_call(h, w_kt)

    gb = gates[:, :N_GATE_COLS] + p['mlstm_gate_bias'].reshape(1, N_GATE_COLS)
    grow = gb.reshape(ROWS, 4, M_HEADS).transpose(2, 1, 0)
    grow = jnp.pad(grow, ((0, 0), (0, SUBLANES - 4), (0, 0)))

    sm_c, sm_n, sm_m, s_lh, s_rs = states
    prompt_kw = dict(t=T_PROMPT, n_seq=N_PROMPT_SEQ, row_block0=0, layer=layer, emit_state=True)
    sample_kw = dict(t=T_SAMPLE, n_seq=N_SAMPLE_SEQ, row_block0=ROWS_PROMPT // T_SAMPLE, layer=layer,
                     emit_state=False)
    pc, pn, pm, plh, prs = prev_states

    caug0 = jnp.concatenate(
        [sm_c, jnp.broadcast_to(sm_n[..., None], sm_n.shape + (LANES,))], axis=-1)
    m0 = jnp.broadcast_to(sm_m[..., None, None], (N_SAMPLE_SEQ, 2, M_HEADS, 1, LANES))
    (ym,) = _mlstm_call(proj, proj_t, grow, p['mlstm_norm_g'], (caug0, m0), [None], **sample_kw)
    ym, new_c, new_n, new_m = _mlstm_call(proj, proj_t, grow, p['mlstm_norm_g'], None,
                                          [ym, pc, pn, pm], **prompt_kw)

    lru_p = dict(conv_w=p['lru_conv_w'], conv_b=p['lru_conv_b'], wr=p['lru_wr'], wi=p['lru_wi'],
                 br=p['lru_br'], bi=p['lru_bi'], lam=p['lru_lambda'])
    (yl,) = _lru_call(proj, lru_p, s_lh, [None], **sample_kw)
    yl, new_lh = _lru_call(proj, lru_p, None, [yl, plh], **prompt_kw)

    decay = jnp.broadcast_to(p['ret_decay'].T[:, :, None], (R_HEADS, 2, LANES))
    (yr,) = _ret_call(proj, proj_t, decay, p['ret_norm_g'], rope_tabs, s_rs, [None], **sample_kw)
    yr, new_rs = _ret_call(proj, proj_t, decay, p['ret_norm_g'], None, None, [yr, prs], **prompt_kw)

    rw = jnp.pad(p['router_w'], ((0, 0), (0, LANES - N_EXPERTS)))
    rwh = rw.astype(BF16)
    rwl = (rw - rwh.astype(F32)).astype(BF16)
    xn, h2, logits = _merge_call(h, ym, yl, yr, x, mod, p['w_merge'].astype(BF16), p['b_merge'],
                                 p['w_branch'].astype(BF16), p['w_out'].astype(BF16), p['norm2_g'],
                                 rwh, rwl)

    xs_p, gv_p, rank_p = _prompt_route_call(logits, h2)
    rows_s, gv_s = _route_sample(logits)
    xs_s = h2.at[rows_s].get(mode='promise_in_bounds')
    gv = jnp.concatenate([gv_p, gv_s[..., None]], axis=1)
    y_p, y_s = _expert_call(xs_p, xs_s, gv, mod, stacked['exp_w1'], stacked['exp_w3'],
                            stacked['exp_w2'], layer)
    x_out = xn.at[rows_s.reshape(-1)].add(y_s.reshape(-1, D), mode='promise_in_bounds')
    x_out = _prompt_combine_call(x_out, y_p, rank_p)
    return x_out, (new_c, new_n, new_m, new_lh, new_rs)


def kernel(x_prompt, x_sample, c, state_mlstm_C, state_mlstm_n, state_mlstm_m, state_lru_h, state_ret_S, c_ctx, w_ada, b_ada, norm1_g, norm2_g, w_in, mlstm_gate_bias, mlstm_norm_g, lru_conv_w, lru_conv_b, lru_wr, lru_br, lru_wi, lru_bi, lru_lambda, ret_decay, ret_norm_g, w_branch, w_merge, b_merge, w_out, router_w, exp_w1, exp_w3, exp_w2, final_g):
    x = jnp.concatenate([x_prompt.reshape(ROWS_PROMPT, D), x_sample.reshape(-1, D)], axis=0)
    cond8 = jnp.concatenate([c_ctx[None, :], c, jnp.zeros((8 - 1 - N_SAMPLE_SEQ, D), F32)], axis=0)
    mod_all = _ada_call(cond8, w_ada, b_ada).reshape(DEPTH, 8, 6, D)
    rope_tabs = _rope_tables()
    stacked = dict(exp_w1=exp_w1, exp_w3=exp_w3, exp_w2=exp_w2)

    new_states = (None,) * 5
    for l in range(DEPTH):
        p = dict(norm1_g=norm1_g[l], norm2_g=norm2_g[l], w_in=w_in[l],
                 mlstm_gate_bias=mlstm_gate_bias[l], mlstm_norm_g=mlstm_norm_g[l],
                 lru_conv_w=lru_conv_w[l], lru_conv_b=lru_conv_b[l], lru_wr=lru_wr[l],
                 lru_br=lru_br[l], lru_wi=lru_wi[l], lru_bi=lru_bi[l], lru_lambda=lru_lambda[l],
                 ret_decay=ret_decay[l], ret_norm_g=ret_norm_g[l], w_branch=w_branch[l],
                 w_merge=w_merge[l], b_merge=b_merge[l], w_out=w_out[l], router_w=router_w[l])
        states = (state_mlstm_C[:, l], state_mlstm_n[:, l], state_mlstm_m[:, l],
                  state_lru_h[:, l], state_ret_S[:, l])
        x, new_states = _layer(x, mod_all[l], p, states, rope_tabs, stacked, l, new_states)

    y_prompt = _final_norm_call(x, final_g, 0, ROWS_PROMPT).reshape(N_PROMPT_SEQ, T_PROMPT, D)
    y_sample = _final_norm_call(x, final_g, ROWS_PROMPT, ROWS - ROWS_PROMPT).reshape(
        N_SAMPLE_SEQ, T_SAMPLE, D)
    new_c, new_n, new_m, new_lh, new_rs = new_states
    return (y_prompt, y_sample, new_c, new_n[:, :, :, :, 0, :], new_m[:, :, :, :, 0, 0], new_lh, new_rs)
```

```python
import functools

import jax
import jax.numpy as jnp
import numpy as np
from jax import lax
from jax.experimental import pallas as pl
from jax.experimental.pallas import tpu as pltpu

F32 = jnp.float32
BF16 = jnp.bfloat16

D = 1024
DEPTH = 2
N_PROMPT_SEQ = 16
T_PROMPT = 256
N_SAMPLE_SEQ = 2
T_SAMPLE = 4096
ROWS_PROMPT = N_PROMPT_SEQ * T_PROMPT
ROWS = ROWS_PROMPT + N_SAMPLE_SEQ * T_SAMPLE
GROUP_ROWS = 4096
GRID_W = 64
CHUNK = 256
EPS = 1e-6
F32_TINY = 1.1754944e-38
M_HEADS = 4
M_DH = 256
R_HEADS = 8
R_DH = 128
L_BLOCKS = 8
L_BW = 128
L_C = 8.0
ROPE_BASE = 10000.0
N_EXPERTS = 16
CAP_PROMPT = 2 * T_PROMPT // N_EXPERTS
CAP_SAMPLE = 2 * T_SAMPLE // N_EXPERTS
ROWS_PER_EXPERT = N_PROMPT_SEQ * CAP_PROMPT + N_SAMPLE_SEQ * CAP_SAMPLE
N_GATE_COLS = 16
LANES = 128
SUBLANES = 8
VMEM_LIMIT = 56 * 2 ** 20

COL_MQ, COL_MV, COL_MO, COL_LX, COL_LZ, COL_RQ, COL_RV, COL_RG = (i * D for i in range(8))
D_MAIN = 8 * D
ROW_MK, ROW_RK = 0, D
D_KT = 2 * D


def _cparams(sem, vmem=None):
    return pltpu.CompilerParams(dimension_semantics=sem, vmem_limit_bytes=vmem)


def _dot(a, b):
    return jnp.dot(a, b, preferred_element_type=F32)


def _dot_nt(a, b):
    return lax.dot_general(a, b, (((1,), (1,)), ((), ())), preferred_element_type=F32)


def _split3(x):
    a = x.astype(BF16)
    r = x - a.astype(F32)
    b = r.astype(BF16)
    c = (r - b.astype(F32)).astype(BF16)
    return a, b, c


def _log_sigmoid(x):
    return jnp.minimum(x, 0.0) - jnp.log1p(jnp.exp(-jnp.abs(x)))


def _sigmoid(x):
    return 0.5 * jnp.tanh(0.5 * x) + 0.5


def _rms(x):
    return x * lax.rsqrt(jnp.mean(x * x, axis=-1, keepdims=True) + EPS)


def _chunk(c):
    return pl.ds(pl.multiple_of(c * CHUNK, CHUNK), CHUNK)


def _alias_prev(args, in_specs, prev):
    aliases = {}
    for out_idx, arr in enumerate(prev):
        if arr is not None:
            aliases[len(args)] = out_idx
            args.append(arr)
            in_specs.append(pl.BlockSpec(memory_space=pl.ANY))
    return aliases, len(aliases)


def _ada_kernel(c_ref, w_ref, b_ref, o_ref):
    c = c_ref[...]
    s = (c * jax.nn.sigmoid(c)).astype(BF16)
    o_ref[0] = _dot(s, w_ref[0].astype(BF16)) + b_ref[0]


def _ada_call(cond8, w_ada, b_ada):
    tn = 1536
    return pl.pallas_call(
        _ada_kernel,
        grid=(DEPTH, 6 * D // tn),
        in_specs=[pl.BlockSpec((8, D), lambda l, j: (0, 0)),
                  pl.BlockSpec((1, D, tn), lambda l, j: (l, 0, j)),
                  pl.BlockSpec((1, 1, tn), lambda l, j: (l, 0, j))],
        out_specs=pl.BlockSpec((1, 8, tn), lambda l, j: (l, 0, j)),
        out_shape=jax.ShapeDtypeStruct((DEPTH, 8, 6 * D), F32),
        compiler_params=_cparams(("arbitrary", "arbitrary"), VMEM_LIMIT),
        name="ada",
    )(cond8, w_ada, b_ada.reshape(DEPTH, 1, 6 * D))


def _norm1_kernel(x_ref, g_ref, mod_ref, wg_ref, h_ref, gate_ref):
    y = _rms(x_ref[...]) * g_ref[...]
    h = (y * (1.0 + mod_ref[0, 1:2, :]) + mod_ref[0, 0:1, :]).astype(BF16)
    h_ref[...] = h
    gate_ref[...] = _dot(h, wg_ref[...])


def _norm1_call(x, g, mod, w_gate):
    tm = 512
    return pl.pallas_call(
        _norm1_kernel,
        grid=(ROWS // tm,),
        in_specs=[pl.BlockSpec((tm, D), lambda i: (i, 0)),
                  pl.BlockSpec((1, D), lambda i: (0, 0)),
                  pl.BlockSpec((1, 6, D), lambda i: (i * tm // GROUP_ROWS, 0, 0)),
                  pl.BlockSpec((D, LANES), lambda i: (0, 0))],
        out_specs=[pl.BlockSpec((tm, D), lambda i: (i, 0)),
                   pl.BlockSpec((tm, LANES), lambda i: (i, 0))],
        out_shape=[jax.ShapeDtypeStruct((ROWS, D), BF16),
                   jax.ShapeDtypeStruct((ROWS, LANES), F32)],
        compiler_params=_cparams(("arbitrary",), VMEM_LIMIT),
        name="norm1",
    )(x, g.reshape(1, D), mod, w_gate)


def _mm_kernel(x_ref, wt_ref, o_ref):
    o_ref[...] = _dot_nt(x_ref[...], wt_ref[...]).astype(o_ref.dtype)


def _inproj_call(h, w_main_t):
    tm, tn = 1024, 1024
    return pl.pallas_call(
        _mm_kernel,
        grid=(D_MAIN // tn, ROWS // tm),
        in_specs=[pl.BlockSpec((tm, D), lambda j, i: (i, 0)),
                  pl.BlockSpec((tn, D), lambda j, i: (j, 0))],
        out_specs=pl.BlockSpec((tm, tn), lambda j, i: (i, j)),
        out_shape=jax.ShapeDtypeStruct((ROWS, D_MAIN), BF16),
        compiler_params=_cparams(("arbitrary", "arbitrary"), VMEM_LIMIT),
        name="inproj",
    )(h, w_main_t)


def _mm_t_kernel(wt_ref, x_ref, o_ref):
    o_ref[...] = _dot_nt(wt_ref[...], x_ref[...]).astype(o_ref.dtype)


def _inproj_t_call(h, w_kt):
    tm, tn = 1024, 1024
    return pl.pallas_call(
        _mm_t_kernel,
        grid=(D_KT // tn, ROWS // tm),
        in_specs=[pl.BlockSpec((tn, D), lambda j, i: (j, 0)),
                  pl.BlockSpec((tm, D), lambda j, i: (i, 0))],
        out_specs=pl.BlockSpec((tn, tm), lambda j, i: (j, i)),
        out_shape=jax.ShapeDtypeStruct((D_KT, ROWS), BF16),
        compiler_params=_cparams(("arbitrary", "arbitrary"), VMEM_LIMIT),
        name="inproj_t",
    )(w_kt, h)


M_AUG = M_DH + LANES
M_GROUP = 1


def _mlstm_kernel(*refs, n_chunks, has_init, emit_state, n_prev):
    q_ref, kt_ref, v_ref, o_ref, g_ref, ng_ref = refs[:6]
    pos = 6
    if has_init:
        c0_ref, m0_ref = refs[pos:pos + 2]
        pos += 2
    pos += n_prev
    y_ref = refs[pos]
    pos += 1
    if emit_state:
        co_ref, no_ref, mo_ref = refs[pos:pos + 3]
        pos += 3
    caug_sc, m_sc, hf_sc, hb_sc, gp_mp, gp_bc, gp_row = refs[pos:]

    zero_state = (not has_init) and n_chunks == 1
    if has_init:
        for g in range(M_GROUP):
            caug_sc[g] = c0_ref[0, :, g]
            m_sc[g] = m0_ref[0, :, g]
    else:
        if not zero_state:
            caug_sc[...] = jnp.zeros_like(caug_sc)
        m_sc[...] = jnp.zeros_like(m_sc)

    ri = lax.broadcasted_iota(jnp.int32, (CHUNK, CHUNK), 0)
    ci = lax.broadcasted_iota(jnp.int32, (CHUNK, CHUNK), 1)
    r8 = lax.broadcasted_iota(jnp.int32, (SUBLANES, CHUNK), 0)
    ones_b = jnp.ones((CHUNK, LANES), BF16)
    allowed = [ri >= ci, ci >= ri]
    tri = [jnp.where(ri <= ci, 1.0, 0.0).astype(BF16), jnp.where(ri >= ci, 1.0, 0.0).astype(BF16)]

    def gate_pass(c, carry):
        rows = _chunk(c)
        for g in range(M_GROUP):
            g8 = g_ref[g, :, rows]
            for d in range(2):
                i_idx, f_idx = (0, 1) if d == 0 else (2, 3)
                ls = jnp.where(r8 == f_idx, _log_sigmoid(g8), g8)
                cum = sum(_dot(p, tri[d]) for p in _split3(ls))
                b_row = cum[f_idx:f_idx + 1]
                g_row = ls[i_idx:i_idx + 1] - b_row
                gmat = jnp.where(allowed[d], jnp.broadcast_to(g_row, (CHUNK, CHUNK)), -jnp.inf)
                mp_col = jnp.max(gmat, axis=-1, keepdims=True)
                gp_bc[g, d, c] = jnp.broadcast_to(b_row, (LANES, CHUNK)).T
                gp_mp[g, d, c] = jnp.broadcast_to(mp_col, (CHUNK, LANES))
                b_last_11 = b_row[:, CHUNK - 1:CHUNK] if d == 0 else b_row[:, 0:1]
                log_k = b_last_11 + g_row
                a_11 = jnp.max(log_k, axis=-1, keepdims=True)
                gp_row[g, d, c] = jnp.where(
                    r8 == 0, g_row, jnp.where(
                        r8 == 1, jnp.exp(log_k - a_11), jnp.where(r8 == 2, b_last_11, a_11)))
        return carry

    lax.fori_loop(0, n_chunks, gate_pass, 0, unroll=min(n_chunks, 8))

    def step(g, d, c, m_t):
        rows = _chunk(c)
        hcols = slice(g * M_DH, (g + 1) * M_DH)
        mp = gp_mp[g, d, c]
        bc = gp_bc[g, d, c]
        row_t = gp_row[g, d, c]
        gmat = jnp.where(allowed[d], jnp.broadcast_to(row_t[0:1], (CHUNK, CHUNK)), -jnp.inf)
        e1 = jnp.exp(gmat - jnp.concatenate([mp] * (CHUNK // LANES), axis=1))

        q = q_ref[rows, hcols]
        kt = kt_ref[hcols, rows]
        vaug = jnp.concatenate([v_ref[rows, hcols], ones_b], axis=1)
        s1 = _dot(q, kt) * e1
        sv1 = _dot(s1.astype(BF16), vaug)
        kw1 = (kt.astype(F32) * row_t[1:2]).astype(BF16)
        u1 = _dot(kw1, vaug)
        b_last = jnp.broadcast_to(row_t[2:3, 0:LANES], (SUBLANES, LANES))
        a_t = jnp.broadcast_to(row_t[3:4, 0:LANES], (SUBLANES, LANES))

        m_b = jnp.broadcast_to(m_t[0:1, :], (CHUNK, LANES))
        mi = jnp.maximum(m_b, mp)
        r = jnp.exp(mp - mi)
        w_prev = jnp.exp(m_b - mi)
        floor = jnp.exp(-(bc + mi))
        if zero_state:
            qc = None
            den = r * sv1[:, M_DH:]
        else:
            qc = _dot(q, caug_sc[g, d].astype(BF16))
            den = r * sv1[:, M_DH:] + w_prev * qc[:, M_DH:]
        inv = 1.0 / jnp.maximum(jnp.abs(den), floor)
        h_sc = hf_sc if d == 0 else hb_sc
        for blk in range(M_DH // LANES):
            cols = slice(blk * LANES, (blk + 1) * LANES)
            num = r * sv1[:, cols]
            if not zero_state:
                num = num + w_prev * qc[:, cols]
            h_sc[rows, slice(g * M_DH + blk * LANES, g * M_DH + (blk + 1) * LANES)] = num * inv

        m_new = jnp.maximum(b_last + m_t, a_t)
        w_c = jnp.exp(b_last + m_t - m_new)[0:1, :]
        w_u = jnp.exp(a_t - m_new)[0:1, :]
        for blk in range(M_AUG // LANES):
            cols = slice(blk * LANES, (blk + 1) * LANES)
            if zero_state:
                caug_sc[g, d, :, cols] = w_u * u1[:, cols]
            else:
                caug_sc[g, d, :, cols] = w_c * caug_sc[g, d, :, cols] + w_u * u1[:, cols]
        return m_new

    def scan_body(c, carry):
        new = []
        for g in range(M_GROUP):
            new.append(step(g, 0, c, carry[2 * g]))
            new.append(step(g, 1, n_chunks - 1 - c, carry[2 * g + 1]))
        return tuple(new)

    m_init = tuple(jnp.broadcast_to(m_sc[g, d], (SUBLANES, LANES))
                   for g in range(M_GROUP) for d in range(2))
    m_fin = lax.fori_loop(0, n_chunks, scan_body, m_init)
    for g in range(M_GROUP):
        for d in range(2):
            m_sc[g, d] = m_fin[2 * g + d][0:1, :]

    def out_body(c, carry):
        rows = _chunk(c)
        for g in range(M_GROUP):
            hcols = slice(g * M_DH, (g + 1) * M_DH)
            y = _rms(hf_sc[rows, hcols] + hb_sc[rows, hcols]) * ng_ref[:, hcols]
            y_ref[rows, hcols] = (y * _sigmoid(o_ref[rows, hcols].astype(F32))).astype(y_ref.dtype)
        return carry

    lax.fori_loop(0, n_chunks, out_body, 0)

    if emit_state:
        for g in range(M_GROUP):
            co_ref[0, 0, :, g] = caug_sc[g, :, :, 0:M_DH]
            for d in range(2):
                no_ref[0, 0, d, g] = caug_sc[g, d, :, M_DH:].T[0:1, :]
            mo_ref[0, 0, :, g] = m_sc[g]


def _mlstm_call(proj, proj_t, grow, norm_g, state, prev, *, t, n_seq, row_block0, layer, emit_state):
    has_init = state is not None

    gw = M_GROUP * M_DH
    big = dict(pipeline_mode=pl.Buffered(1)) if t * gw * 2 > 2 ** 21 else {}

    def col(base):
        return lambda s, h: (row_block0 + s, base // gw + h)

    st_caug = pl.BlockSpec((1, 2, M_GROUP, M_DH, M_AUG), lambda s, h: (s, 0, h, 0, 0))
    st_m = pl.BlockSpec((1, 2, M_GROUP, 1, LANES), lambda s, h: (s, 0, h, 0, 0))
    so_c = pl.BlockSpec((1, 1, 2, M_GROUP, M_DH, M_DH), lambda s, h: (s, layer, 0, h, 0, 0))
    so_n = pl.BlockSpec((1, 1, 2, M_GROUP, 1, M_DH), lambda s, h: (s, layer, 0, h, 0, 0))
    so_m = pl.BlockSpec((1, 1, 2, M_GROUP, 1, LANES), lambda s, h: (s, layer, 0, h, 0, 0))
    in_specs = [pl.BlockSpec((t, gw), col(COL_MQ), **big),
                pl.BlockSpec((gw, t), lambda s, h: (ROW_MK // gw + h, row_block0 + s), **big),
                pl.BlockSpec((t, gw), col(COL_MV), **big),
                pl.BlockSpec((t, gw), col(COL_MO), **big),
                pl.BlockSpec((M_GROUP, SUBLANES, t), lambda s, h: (h, 0, row_block0 + s)),
                pl.BlockSpec((1, gw), lambda s, h: (0, h))]
    args = [proj, proj_t, proj, proj, grow, norm_g.reshape(1, D)]
    if has_init:
        in_specs += [st_caug, st_m]
        args += list(state)
    out_specs = [pl.BlockSpec((t, gw), lambda s, h: (row_block0 + s, h))]
    out_shape = [jax.ShapeDtypeStruct((ROWS, D), BF16)]
    if emit_state:
        out_specs += [so_c, so_n, so_m]
        out_shape += [jax.ShapeDtypeStruct((n_seq, DEPTH, 2, M_HEADS, M_DH, M_DH), F32),
                      jax.ShapeDtypeStruct((n_seq, DEPTH, 2, M_HEADS, 1, M_DH), F32),
                      jax.ShapeDtypeStruct((n_seq, DEPTH, 2, M_HEADS, 1, LANES), F32)]
    aliases, n_prev = _alias_prev(args, in_specs, prev)
    return pl.pallas_call(
        functools.partial(_mlstm_kernel, n_chunks=t // CHUNK, has_init=has_init,
                          emit_state=emit_state, n_prev=n_prev),
        grid=(n_seq, M_HEADS // M_GROUP),
        in_specs=in_specs,
        out_specs=out_specs,
        out_shape=out_shape,
        input_output_aliases=aliases,
        scratch_shapes=[pltpu.VMEM((M_GROUP, 2, M_DH, M_AUG), F32),
                        pltpu.VMEM((M_GROUP, 2, 1, LANES), F32),
                        pltpu.VMEM((t, gw), F32),
                        pltpu.VMEM((t, gw), F32),
                        pltpu.VMEM((M_GROUP, 2, t // CHUNK, CHUNK, LANES), F32),
                        pltpu.VMEM((M_GROUP, 2, t // CHUNK, CHUNK, LANES), F32),
                        pltpu.VMEM((M_GROUP, 2, t // CHUNK, SUBLANES, CHUNK), F32)],
        compiler_params=_cparams(("arbitrary", "arbitrary"), VMEM_LIMIT),
        name="mlstm",
    )(*args)


R_GROUP = 2

def _ret_kernel(*refs, n_chunks, rope, has_init, emit_state, n_prev):
    q_ref, kt_ref, v_ref, g_ref, dec_ref, ng_ref = refs[:6]
    pos = 6
    if rope:
        cos_ref, sin_ref, cos_t_ref, sin_t_ref = refs[pos:pos + 4]
        pos += 4
    if has_init:
        s0_ref = refs[pos]
        pos += 1
    pos += n_prev
    y_ref = refs[pos]
    pos += 1
    if emit_state:
        so_ref = refs[pos]
        pos += 1
    s_sc, of_sc, ob_sc, ktr_sc = refs[pos:pos + 4]
    qr_sc = refs[pos + 4] if rope else None

    zero_state = (not has_init) and n_chunks == 1
    if has_init:
        for g in range(R_GROUP):
            s_sc[g] = s0_ref[0, :, g]
    elif not zero_state:
        s_sc[...] = jnp.zeros_like(s_sc)

    ri = lax.broadcasted_iota(jnp.int32, (CHUNK, CHUNK), 0)
    ci = lax.broadcasted_iota(jnp.int32, (CHUNK, CHUNK), 1)
    quarter = R_DH // 4
    if rope:
        fr = lax.broadcasted_iota(jnp.int32, (R_DH, R_DH), 0)
        fc = lax.broadcasted_iota(jnp.int32, (R_DH, R_DH), 1)
        partner = fc + jnp.where((fc % (2 * quarter)) < quarter, quarter, -quarter)
        perm_b = jnp.where(fr == partner, 1.0, 0.0).astype(BF16)

    def prep_body(c, carry):
        rows = _chunk(c)
        for g in range(R_GROUP):
            hcols = slice(g * R_DH, (g + 1) * R_DH)
            ktf = kt_ref[hcols, rows].astype(F32) * (R_DH ** -0.5)
            if rope:
                swapped = jnp.concatenate([ktf[quarter:2 * quarter], ktf[0:quarter],
                                           ktf[3 * quarter:], ktf[2 * quarter:3 * quarter]], axis=0)
                ktf = ktf * cos_t_ref[:, rows] + swapped * sin_t_ref[:, rows]
                q = q_ref[rows, hcols]
                qr = q.astype(F32) * cos_ref[rows, :] + _dot(q, perm_b) * sin_ref[rows, :]
                qr_sc[rows, hcols] = qr.astype(BF16)
            ktr_sc[hcols, rows] = ktf.astype(BF16)
        return carry

    lax.fori_loop(0, n_chunks, prep_body, 0, unroll=min(n_chunks, 2))
    qsrc = qr_sc if rope else q_ref

    lane = lax.broadcasted_iota(jnp.int32, (1, CHUNK), 1).astype(F32)
    row_f = lax.broadcasted_iota(jnp.int32, (CHUNK, R_DH), 0).astype(F32)
    consts = []
    for gd in range(2 * R_GROUP):
        g, d = divmod(gd, 2)
        lg = -jnp.exp(dec_ref[g, d:d + 1, :])
        lg11 = lg[:, 0:1]
        diff = (ri - ci) if d == 0 else (ci - ri)
        causal = diff >= 0
        dmat = jnp.where(causal, jnp.exp(lg11 * jnp.where(causal, diff, 0).astype(F32)), 0.0)
        if d == 0:
            q_dec = jnp.exp(lg * (row_f + 1.0))
            k_dec = jnp.exp(lg11 * (CHUNK - 1.0 - lane))
        else:
            q_dec = jnp.exp(lg * (CHUNK - row_f))
            k_dec = jnp.exp(lg11 * lane)
        g_chunk = jnp.exp(lg * float(CHUNK))
        consts.append((dmat, q_dec, k_dec, g_chunk))

    def step(g, d, c):
        dmat, q_dec, k_dec, g_chunk = consts[2 * g + d]
        rows = _chunk(c)
        hcols = slice(g * R_DH, (g + 1) * R_DH)
        q = qsrc[rows, hcols]
        kt = ktr_sc[hcols, rows]
        v = v_ref[rows, hcols]
        s = _dot(q, kt) * dmat
        o = _dot(s.astype(BF16), v)
        if not zero_state:
            o = o + q_dec * _dot(q, s_sc[g, d].astype(BF16))
        if d == 0:
            of_sc[rows, hcols] = o
        else:
            ob_sc[rows, hcols] = o
        kd = (kt.astype(F32) * k_dec).astype(BF16)
        if zero_state:
            s_sc[g, d] = _dot(kd, v)
        else:
            s_sc[g, d] = g_chunk * s_sc[g, d] + _dot(kd, v)

    def scan_body(c, carry):
        for g in range(R_GROUP):
            step(g, 0, c)
            step(g, 1, n_chunks - 1 - c)
        return carry

    lax.fori_loop(0, n_chunks, scan_body, 0, unroll=min(n_chunks, 2))

    def out_body(c, carry):
        rows = _chunk(c)
        for g in range(R_GROUP):
            hcols = slice(g * R_DH, (g + 1) * R_DH)
            y = _rms(of_sc[rows, hcols] + ob_sc[rows, hcols]) * ng_ref[:, hcols]
            gate = g_ref[rows, hcols].astype(F32)
            y_ref[rows, hcols] = (y * (gate * _sigmoid(gate))).astype(y_ref.dtype)
        return carry

    lax.fori_loop(0, n_chunks, out_body, 0, unroll=min(n_chunks, 2))

    if emit_state:
        for g in range(R_GROUP):
            so_ref[0, 0, :, g] = s_sc[g]


def _ret_call(proj, proj_t, decay, norm_g, rope_tabs, state, prev, *, t, n_seq, row_block0, layer, emit_state):
    has_init = state is not None
    rope = rope_tabs is not None

    gw = R_GROUP * R_DH

    def col(base):
        return lambda s, h: (row_block0 + s, base // gw + h)

    st_s = pl.BlockSpec((1, 2, R_GROUP, R_DH, R_DH), lambda s, h: (s, 0, h, 0, 0))
    in_specs = [pl.BlockSpec((t, gw), col(COL_RQ)),
                pl.BlockSpec((gw, t), lambda s, h: (ROW_RK // gw + h, row_block0 + s)),
                pl.BlockSpec((t, gw), col(COL_RV)),
                pl.BlockSpec((t, gw), col(COL_RG)),
                pl.BlockSpec((R_GROUP, 2, LANES), lambda s, h: (h, 0, 0)),
                pl.BlockSpec((1, gw), lambda s, h: (0, h))]
    args = [proj, proj_t, proj, proj, decay, norm_g.reshape(1, D)]
    if rope:
        in_specs += [pl.BlockSpec((t, R_DH), lambda s, h: (0, 0))] * 2
        in_specs += [pl.BlockSpec((R_DH, t), lambda s, h: (0, 0))] * 2
        args += list(rope_tabs)
    if has_init:
        in_specs.append(st_s)
        args.append(state)
    out_specs = [pl.BlockSpec((t, gw), lambda s, h: (row_block0 + s, h))]
    out_shape = [jax.ShapeDtypeStruct((ROWS, D), BF16)]
    if emit_state:
        out_specs.append(pl.BlockSpec((1, 1, 2, R_GROUP, R_DH, R_DH), lambda s, h: (s, layer, 0, h, 0, 0)))
        out_shape.append(jax.ShapeDtypeStruct((n_seq, DEPTH, 2, R_HEADS, R_DH, R_DH), F32))
    aliases, n_prev = _alias_prev(args, in_specs, prev)
    scratch = [pltpu.VMEM((R_GROUP, 2, R_DH, R_DH), F32),
               pltpu.VMEM((t, gw), F32),
               pltpu.VMEM((t, gw), F32),
               pltpu.VMEM((gw, t), BF16)]
    if rope:
        scratch.append(pltpu.VMEM((t, gw), BF16))
    return pl.pallas_call(
        functools.partial(_ret_kernel, n_chunks=t // CHUNK, rope=rope, has_init=has_init,
                          emit_state=emit_state, n_prev=n_prev),
        grid=(n_seq, R_HEADS // R_GROUP),
        in_specs=in_specs,
        out_specs=out_specs,
        out_shape=out_shape,
        input_output_aliases=aliases,
        scratch_shapes=scratch,
        compiler_params=_cparams(("arbitrary", "arbitrary"), VMEM_LIMIT),
        name="retention",
    )(*args)


LRU_SLAB = 64


def _tile_scan(a, b, reverse):
    row = lax.broadcasted_iota(jnp.int32, (SUBLANES, L_BW), 0)
    for k in (1, 2, 4):
        if reverse:
            keep = row < SUBLANES - k
            shift = SUBLANES - k
        else:
            keep = row >= k
            shift = k
        a_sh = jnp.where(keep, pltpu.roll(a, shift, 0), 1.0)
        b_sh = jnp.where(keep, pltpu.roll(b, shift, 0), 0.0)
        b = a * b_sh + b
        a = a * a_sh
    return a, b


def _lru_kernel(*refs, t, has_init, emit_state, n_prev):
    lx_ref, lz_ref, cw_ref, cb_ref, wr_ref, wi_ref, br_ref, bi_ref, lam_ref = refs[:9]
    pos = 9
    if has_init:
        h0_ref = refs[pos]
        pos += 1
    pos += n_prev
    y_ref = refs[pos]
    pos += 1
    if emit_state:
        hfin_ref = refs[pos]
        pos += 1
    xpad_sc, a_sc, b_sc = refs[pos:]
    rc = min(t, 256)

    xpad_sc[0:8, :] = jnp.zeros((8, L_BW), F32)
    xpad_sc[t + 8:t + 16, :] = jnp.zeros((8, L_BW), F32)

    def pad_body(c, carry):
        r0 = pl.multiple_of(c * rc, rc)
        xpad_sc[pl.ds(r0 + 8, rc), :] = lx_ref[pl.ds(r0, rc), :].astype(F32)
        return carry

    lax.fori_loop(0, t // rc, pad_body, 0)

    wr = [(0.5 * wr_ref[d, 0]).astype(BF16) for d in range(2)]
    wi = [(0.5 * wi_ref[d, 0]).astype(BF16) for d in range(2)]
    half_br = [0.5 * br_ref[d:d + 1, :] for d in range(2)]
    half_bi = [0.5 * bi_ref[d:d + 1, :] for d in range(2)]
    half_c = []
    for d in range(2):
        lam = lam_ref[d:d + 1, :]
        softplus_neg = jnp.maximum(-lam, 0.0) + jnp.log1p(jnp.exp(-jnp.abs(lam)))
        half_c.append(-0.5 * L_C * softplus_neg)

    def gate_body(c, carry):
        r0 = pl.multiple_of(c * rc, rc)
        xe = xpad_sc[pl.ds(r0, rc + 16), :]
        n = rc + 16
        u = (cb_ref[...] + cw_ref[0:1, :] * pltpu.roll(xe, 2, 0)[8:8 + rc]
             + cw_ref[1:2, :] * pltpu.roll(xe, 1, 0)[8:8 + rc]
             + cw_ref[2:3, :] * xe[8:8 + rc]
             + cw_ref[3:4, :] * pltpu.roll(xe, n - 1, 0)[8:8 + rc])
        ub = u.astype(BF16)
        half_u = 0.5 * u
        for d in range(2):
            tr = jnp.tanh(_dot(ub, wr[d]) + half_br[d])
            ti = jnp.tanh(_dot(ub, wi[d]) + half_bi[d])
            log_a = half_c[d] * tr + half_c[d]
            a = jnp.exp(log_a)
            z = jnp.tanh(log_a) * (-1.0 - a * a)
            mult = z * lax.rsqrt(jnp.maximum(z, F32_TINY))
            a_sc[d, pl.ds(r0, rc), :] = a
            b_sc[d, pl.ds(r0, rc), :] = mult * (ti * half_u + half_u)
        return carry

    lax.fori_loop(0, t // rc, gate_body, 0)

    n_slabs = t // LRU_SLAB
    tiles = LRU_SLAB // SUBLANES

    row8 = lax.broadcasted_iota(jnp.int32, (SUBLANES, L_BW), 0)

    def bcast_row(x, r):
        return jnp.broadcast_to(x[r:r + 1, :], (SUBLANES, L_BW))

    def slab_scan(d, r0, c_in):
        reverse = d == 1
        edge = 0 if reverse else SUBLANES - 1
        local = []
        spa = jnp.ones((SUBLANES, L_BW), F32)
        shl = jnp.zeros((SUBLANES, L_BW), F32)
        for k in range(tiles):
            rows = pl.ds(r0 + k * SUBLANES, SUBLANES)
            pa, hl = _tile_scan(a_sc[d, rows, :], b_sc[d, rows, :], reverse)
            local.append((rows, pa, hl))
            spa = jnp.where(row8 == k, bcast_row(pa, edge), spa)
            shl = jnp.where(row8 == k, bcast_row(hl, edge), shl)
        cpa, chl = _tile_scan(spa, shl, reverse)
        after = chl + cpa * c_in
        for k, (rows, pa, hl) in enumerate(local):
            prev = k + 1 if reverse else k - 1
            cin = c_in if (prev < 0 or prev >= tiles) else bcast_row(after, prev)
            b_sc[d, rows, :] = hl + pa * cin
        return bcast_row(after, 0 if reverse else tiles - 1)

    def scan_body(c, carry):
        cf, cbk = carry
        cf = slab_scan(0, pl.multiple_of(c * LRU_SLAB, LRU_SLAB), cf)
        cbk = slab_scan(1, pl.multiple_of((n_slabs - 1 - c) * LRU_SLAB, LRU_SLAB), cbk)
        return cf, cbk

    if has_init:
        cf0 = jnp.broadcast_to(h0_ref[0, 0:1, :], (SUBLANES, L_BW))
        cb0 = jnp.broadcast_to(h0_ref[0, 1:2, :], (SUBLANES, L_BW))
    else:
        cf0 = jnp.zeros((SUBLANES, L_BW), F32)
        cb0 = jnp.zeros((SUBLANES, L_BW), F32)
    cf, cbk = lax.fori_loop(0, n_slabs, scan_body, (cf0, cb0))
    if emit_state:
        hfin_ref[0, 0, 0:1, :] = cf[0:1, :]
        hfin_ref[0, 0, 1:2, :] = cbk[0:1, :]

    def out_body(c, carry):
        rows = pl.ds(pl.multiple_of(c * rc, rc), rc)
        z = lz_ref[rows, :].astype(F32)
        gelu = 0.5 * z * (1.0 + jnp.tanh(0.7978845608028654 * (z + 0.044715 * (z * z * z))))
        y_ref[rows, :] = ((b_sc[0, rows, :] + b_sc[1, rows, :]) * gelu).astype(y_ref.dtype)
        return carry

    lax.fori_loop(0, t // rc, out_body, 0)


def _lru_call(proj, p, state, prev, *, t, n_seq, row_block0, layer, emit_state):
    has_init = state is not None

    def col(base):
        return lambda s, b: (row_block0 + s, base // L_BW + b)

    vec2 = pl.BlockSpec((2, L_BW), lambda s, b: (0, b))
    st = pl.BlockSpec((1, 2, L_BW), lambda s, b: (s, 0, b))
    in_specs = [pl.BlockSpec((t, L_BW), col(COL_LX)),
                pl.BlockSpec((t, L_BW), col(COL_LZ)),
                pl.BlockSpec((4, L_BW), lambda s, b: (0, b)),
                pl.BlockSpec((1, L_BW), lambda s, b: (0, b)),
                pl.BlockSpec((2, 1, L_BW, L_BW), lambda s, b: (0, b, 0, 0)),
                pl.BlockSpec((2, 1, L_BW, L_BW), lambda s, b: (0, b, 0, 0)),
                vec2, vec2, vec2]
    args = [proj, proj, p['conv_w'], p['conv_b'].reshape(1, D), p['wr'], p['wi'],
            p['br'], p['bi'], p['lam']]
    if has_init:
        in_specs.append(st)
        args.append(state)
    out_specs = [pl.BlockSpec((t, L_BW), lambda s, b: (row_block0 + s, b))]
    out_shape = [jax.ShapeDtypeStruct((ROWS, D), BF16)]
    if emit_state:
        out_specs.append(pl.BlockSpec((1, 1, 2, L_BW), lambda s, b: (s, layer, 0, b)))
        out_shape.append(jax.ShapeDtypeStruct((n_seq, DEPTH, 2, D), F32))
    aliases, n_prev = _alias_prev(args, in_specs, prev)
    return pl.pallas_call(
        functools.partial(_lru_kernel, t=t, has_init=has_init, emit_state=emit_state, n_prev=n_prev),
        grid=(n_seq, L_BLOCKS),
        in_specs=in_specs,
        out_specs=out_specs,
        out_shape=out_shape,
        input_output_aliases=aliases,
        scratch_shapes=[pltpu.VMEM((t + 16, L_BW), F32),
                        pltpu.VMEM((2, t, L_BW), F32),
                        pltpu.VMEM((2, t, L_BW), F32)],
        compiler_params=_cparams(("arbitrary", "arbitrary"), VMEM_LIMIT),
        name="rglru",
    )(*args)


def _merge_kernel(h_ref, ym_ref, yl_ref, yr_ref, x_ref, mod_ref, wm_ref, bm_ref, wb_ref, wo_ref,
                  n2_ref, rwh_ref, rwl_ref, xo_ref, h2_ref, lg_ref):
    h = h_ref[...]
    merged = None
    for k, y_ref in enumerate((ym_ref, yl_ref, yr_ref)):
        gate = jax.nn.sigmoid(_dot(h, wm_ref[:, k * D:(k + 1) * D]) + bm_ref[:, k * D:(k + 1) * D])
        term = gate * _dot(y_ref[...], wb_ref[k])
        merged = term if merged is None else merged + term
    mix = _dot(merged.astype(BF16), wo_ref[...])
    xn = x_ref[...] + mod_ref[0, 2:3, :] * mix
    xo_ref[...] = xn
    hn = _rms(xn) * n2_ref[...] * (1.0 + mod_ref[0, 4:5, :]) + mod_ref[0, 3:4, :]
    hi = hn.astype(BF16)
    h2_ref[...] = hi
    lo = (hn - hi.astype(F32)).astype(BF16)
    lg_ref[...] = _dot(hi, rwh_ref[...]) + _dot(lo, rwh_ref[...]) + _dot(hi, rwl_ref[...])


def _merge_call(h, ym, yl, yr, x, mod, wm, bm, wb, wo, n2g, rwh, rwl):
    tm = 256
    row = lambda i: (i, 0)
    const2 = lambda i: (0, 0)
    return pl.pallas_call(
        _merge_kernel,
        grid=(ROWS // tm,),
        in_specs=[pl.BlockSpec((tm, D), row),
                  pl.BlockSpec((tm, D), row),
                  pl.BlockSpec((tm, D), row),
                  pl.BlockSpec((tm, D), row),
                  pl.BlockSpec((tm, D), row),
                  pl.BlockSpec((1, 6, D), lambda i: (i * tm // GROUP_ROWS, 0, 0)),
                  pl.BlockSpec((D, 3 * D), const2),
                  pl.BlockSpec((1, 3 * D), const2),
                  pl.BlockSpec((3, D, D), lambda i: (0, 0, 0)),
                  pl.BlockSpec((D, D), const2),
                  pl.BlockSpec((1, D), const2),
                  pl.BlockSpec((D, LANES), const2),
                  pl.BlockSpec((D, LANES), const2)],
        out_specs=[pl.BlockSpec((tm, D), row),
                   pl.BlockSpec((tm, D), row),
                   pl.BlockSpec((tm, LANES), row)],
        out_shape=[jax.ShapeDtypeStruct((ROWS, D), F32),
                   jax.ShapeDtypeStruct((ROWS, D), BF16),
                   jax.ShapeDtypeStruct((ROWS, LANES), F32)],
        compiler_params=_cparams(("arbitrary",), VMEM_LIMIT),
        name="merge",
    )(h, ym, yl, yr, x, mod, wm, bm.reshape(1, 3 * D), wb, wo, n2g.reshape(1, D), rwh, rwl)


EXPERT_TM = 512
assert N_PROMPT_SEQ * CAP_PROMPT == EXPERT_TM and CAP_SAMPLE == EXPERT_TM


def _expert_kernel(xp_ref, xs_ref, g_ref, mod_ref, w1_ref, w3_ref, w2_ref, yp_ref, ys_ref,
                   w1_sc, w3_sc, w2_sc):
    m = pl.program_id(1)

    @pl.when(m == 0)
    def _():
        w1_sc[...] = w1_ref[0, 0].astype(BF16)
        w3_sc[...] = w3_ref[0, 0].astype(BF16)
        w2_sc[...] = w2_ref[0, 0].astype(BF16)

    xs = jnp.where(m == 0, xp_ref[0], xs_ref[0])
    a = _dot(xs, w1_sc[...])
    b = _dot(xs, w3_sc[...])
    mid = (a * jax.nn.sigmoid(a) * b).astype(BF16)

    def down(y_ref):
        y_ref[0] = (_dot(mid, w2_sc[...]) * g_ref[0]) * mod_ref[0, 5:6, :]

    pl.when(m == 0)(functools.partial(down, yp_ref))
    pl.when(m > 0)(functools.partial(down, ys_ref))


def _expert_call(xs_p, xs_s, gv, mod, w1, w3, w2, layer):
    tm = EXPERT_TM
    wspec = pl.BlockSpec((1, 1, D, D), lambda e, m: (layer, e, 0, 0))
    p_spec = pl.BlockSpec((1, tm, D), lambda e, m: (e, 0, 0))
    s_spec = pl.BlockSpec((1, tm, D), lambda e, m: (e, jnp.maximum(m - 1, 0), 0))
    return pl.pallas_call(
        _expert_kernel,
        grid=(N_EXPERTS, ROWS_PER_EXPERT // tm),
        in_specs=[p_spec, s_spec,
                  pl.BlockSpec((1, tm, 1), lambda e, m: (e, m, 0)),
                  pl.BlockSpec((1, 6, D), lambda e, m: (m, 0, 0)),
                  wspec, wspec, wspec],
        out_specs=[p_spec, s_spec],
        out_shape=[jax.ShapeDtypeStruct((N_EXPERTS, tm, D), F32),
                   jax.ShapeDtypeStruct((N_EXPERTS, N_SAMPLE_SEQ * tm, D), F32)],
        scratch_shapes=[pltpu.VMEM((D, D), BF16)] * 3,
        compiler_params=_cparams(("arbitrary", "arbitrary"), VMEM_LIMIT),
        name="experts",
    )(xs_p, xs_s, gv, mod, w1, w3, w2)


PROMPT_SLOTS = N_EXPERTS * CAP_PROMPT


def _slot_onehot(rank_rows):
    slot = lax.broadcasted_iota(jnp.int32, (CAP_PROMPT, T_PROMPT), 0).astype(F32)
    return jnp.concatenate(
        [jnp.where(slot == rank_rows[e:e + 1, :], 1.0, 0.0) for e in range(N_EXPERTS)], axis=0)


def _prompt_route_kernel(lg_ref, h_ref, xs_ref, gv_ref, rank_ref, rank_sc):
    lane = lax.broadcasted_iota(jnp.int32, (T_PROMPT, LANES), 1)
    x = jnp.where(lane < N_EXPERTS, lg_ref[...], -jnp.inf)
    e = jnp.exp(x - jnp.max(x, axis=-1, keepdims=True))
    aff = e / jnp.sum(e, axis=-1, keepdims=True)
    aff_t = aff.T
    ti = lax.broadcasted_iota(jnp.int32, (T_PROMPT, T_PROMPT), 0)
    tj = lax.broadcasted_iota(jnp.int32, (T_PROMPT, T_PROMPT), 1)
    earlier = jnp.where(ti < tj, 1.0, 0.0)
    for ex in range(N_EXPERTS):
        a_row = aff_t[ex:ex + 1, :]
        a_col = aff[:, ex:ex + 1]
        ahead = jnp.where(a_col > a_row, 1.0, jnp.where(a_col == a_row, earlier, 0.0))
        rank_sc[ex:ex + 1, :] = jnp.sum(ahead, axis=0, keepdims=True)
    rank = rank_sc[...]
    onehot = _slot_onehot(rank)
    xs_ref[...] = _dot(onehot.astype(BF16), h_ref[...]).astype(xs_ref.dtype).reshape(
        N_EXPERTS, CAP_PROMPT, D)
    for ex in range(N_EXPERTS):
        blk = onehot[ex * CAP_PROMPT:(ex + 1) * CAP_PROMPT, :]
        gv_ref[ex] = jnp.sum(blk * aff_t[ex:ex + 1, :], axis=-1, keepdims=True)
    rank_ref[0] = rank


def _prompt_route_call(logits, h2):
    return pl.pallas_call(
        _prompt_route_kernel,
        grid=(N_PROMPT_SEQ,),
        in_specs=[pl.BlockSpec((T_PROMPT, LANES), lambda s: (s, 0)),
                  pl.BlockSpec((T_PROMPT, D), lambda s: (s, 0))],
        out_specs=[pl.BlockSpec((N_EXPERTS, CAP_PROMPT, D), lambda s: (0, s, 0)),
                   pl.BlockSpec((N_EXPERTS, CAP_PROMPT, 1), lambda s: (0, s, 0)),
                   pl.BlockSpec((1, N_EXPERTS, T_PROMPT), lambda s: (s, 0, 0))],
        out_shape=[jax.ShapeDtypeStruct((N_EXPERTS, N_PROMPT_SEQ * CAP_PROMPT, D), BF16),
                   jax.ShapeDtypeStruct((N_EXPERTS, N_PROMPT_SEQ * CAP_PROMPT, 1), F32),
                   jax.ShapeDtypeStruct((N_PROMPT_SEQ, N_EXPERTS, T_PROMPT), F32)],
        scratch_shapes=[pltpu.VMEM((N_EXPERTS, T_PROMPT), F32)],
        compiler_params=_cparams(("arbitrary",), VMEM_LIMIT),
        name="prompt_route",
    )(logits, h2)


def _prompt_combine_kernel(x_ref, y_ref, rank_ref, o_ref):
    onehot = _slot_onehot(rank_ref[0]).astype(BF16)
    y = y_ref[...].reshape(PROMPT_SLOTS, D)
    hi = y.astype(BF16)
    lo = (y - hi.astype(F32)).astype(BF16)
    tn = (((0,), (0,)), ((), ()))
    o_ref[...] = x_ref[...] + (lax.dot_general(onehot, hi, tn, preferred_element_type=F32)
                               + lax.dot_general(onehot, lo, tn, preferred_element_type=F32))


def _prompt_combine_call(x, y_p, rank):
    return pl.pallas_call(
        _prompt_combine_kernel,
        grid=(N_PROMPT_SEQ,),
        in_specs=[pl.BlockSpec((T_PROMPT, D), lambda s: (s, 0)),
                  pl.BlockSpec((N_EXPERTS, CAP_PROMPT, D), lambda s: (0, s, 0)),
                  pl.BlockSpec((1, N_EXPERTS, T_PROMPT), lambda s: (s, 0, 0))],
        out_specs=pl.BlockSpec((T_PROMPT, D), lambda s: (s, 0)),
        out_shape=jax.ShapeDtypeStruct((ROWS, D), F32),
        input_output_aliases={0: 0},
        compiler_params=_cparams(("arbitrary",), VMEM_LIMIT),
        name="prompt_combine",
    )(x, y_p, rank)


def _final_norm_kernel(x_ref, g_ref, o_ref):
    o_ref[...] = _rms(x_ref[...]) * g_ref[...]


def _final_norm_call(x, g, row0, n_rows):
    tm = 512
    return pl.pallas_call(
        _final_norm_kernel,
        grid=(n_rows // tm,),
        in_specs=[pl.BlockSpec((tm, D), lambda i: (row0 // tm + i, 0)),
                  pl.BlockSpec((1, D), lambda i: (0, 0))],
        out_specs=pl.BlockSpec((tm, D), lambda i: (i, 0)),
        out_shape=jax.ShapeDtypeStruct((n_rows, D), F32),
        compiler_params=_cparams(("arbitrary",), VMEM_LIMIT),
        name="final_norm",
    )(x, g.reshape(1, D))


def _rope_tables():
    tpos = np.arange(T_SAMPLE)
    lane = np.arange(R_DH)
    pos = np.where(lane[None, :] < R_DH // 2, (tpos // GRID_W)[:, None], (tpos % GRID_W)[:, None])
    n_freq = R_DH // 4
    freqs = np.power(np.float32(ROPE_BASE), -np.arange(n_freq, dtype=np.float32) / np.float32(n_freq))
    ang = pos.astype(np.float32) * freqs[lane % n_freq][None, :]
    first = ((lane % (R_DH // 2)) < n_freq)[None, :]
    cos, sin = np.cos(ang), np.sin(ang)
    sin = np.where(first, -sin, sin)
    return tuple(jnp.asarray(a, F32) for a in (cos, sin, np.ascontiguousarray(cos.T), np.ascontiguousarray(sin.T)))


def _route_sample(logits):
    aff = jax.nn.softmax(logits[ROWS_PROMPT:, :N_EXPERTS], axis=-1)
    gs, is_ = lax.top_k(aff.reshape(N_SAMPLE_SEQ, T_SAMPLE, N_EXPERTS).swapaxes(1, 2), CAP_SAMPLE)
    is_ = is_ + ROWS_PROMPT + (jnp.arange(N_SAMPLE_SEQ) * T_SAMPLE)[:, None, None]
    rows_s = is_.swapaxes(0, 1).reshape(N_EXPERTS, -1)
    return rows_s, gs.swapaxes(0, 1).reshape(N_EXPERTS, -1)


def _layer(x, mod, p, states, rope_tabs, stacked, layer, prev_states):
    mq, mk, mv, mo, mg, lx, lz, rq, rk, rv, rg = jnp.split(
        p['w_in'].T, [1024, 2048, 3072, 4096, 4112, 5136, 6160, 7184, 8208, 9232], axis=0)
    w_main_t = jnp.concatenate([mq, mv, mo, lx, lz, rq, rv, rg], axis=0).astype(BF16)
    w_kt = jnp.concatenate([mk.astype(BF16) * (M_DH ** -0.5), rk.astype(BF16)], axis=0)
    w_gate = jnp.pad(mg.T, ((0, 0), (0, LANES - N_GATE_COLS))).astype(BF16)

    h, gates = _norm1_call(x, p['norm1_g'], mod, w_gate)
    proj = _inproj_call(h, w_main_t)
    proj_t = _inproj_t_call(h, w_kt)

    gb = gates[:, :N_GATE_COLS] + p['mlstm_gate_bias'].reshape(1, N_GATE_COLS)
    grow = gb.reshape(ROWS, 4, M_HEADS).transpose(2, 1, 0)
    grow = jnp.pad(grow, ((0, 0), (0, SUBLANES - 4), (0, 0)))

    sm_c, sm_n, sm_m, s_lh, s_rs = states
    prompt_kw = dict(t=T_PROMPT, n_seq=N_PROMPT_SEQ, row_block0=0, layer=layer, emit_state=True)
    sample_kw = dict(t=T_SAMPLE, n_seq=N_SAMPLE_SEQ, row_block0=ROWS_PROMPT // T_SAMPLE, layer=layer,
                     emit_state=False)
    pc, pn, pm, plh, prs = prev_states

    caug0 = jnp.concatenate(
        [sm_c, jnp.broadcast_to(sm_n[..., None], sm_n.shape + (LANES,))], axis=-1)
    m0 = jnp.broadcast_to(sm_m[..., None, None], (N_SAMPLE_SEQ, 2, M_HEADS, 1, LANES))
    (ym,) = _mlstm_call(proj, proj_t, grow, p['mlstm_norm_g'], (caug0, m0), [None], **sample_kw)
    ym, new_c, new_n, new_m = _mlstm_call(proj, proj_t, grow, p['mlstm_norm_g'], None,
                                          [ym, pc, pn, pm], **prompt_kw)

    lru_p = dict(conv_w=p['lru_conv_w'], conv_b=p['lru_conv_b'], wr=p['lru_wr'], wi=p['lru_wi'],
                 br=p['lru_br'], bi=p['lru_bi'], lam=p['lru_lambda'])
    (yl,) = _lru_call(proj, lru_p, s_lh, [None], **sample_kw)
    yl, new_lh = _lru_call(proj, lru_p, None, [yl, plh], **prompt_kw)

    decay = jnp.broadcast_to(p['ret_decay'].T[:, :, None], (R_HEADS, 2, LANES))
    (yr,) = _ret_call(proj, proj_t, decay, p['ret_norm_g'], rope_tabs, s_rs, [None], **sample_kw)
    yr, new_rs = _ret_call(proj, proj_t, decay, p['ret_norm_g'], None, None, [yr, prs], **prompt_kw)

    rw = jnp.pad(p['router_w'], ((0, 0), (0, LANES - N_EXPERTS)))
    rwh = rw.astype(BF16)
    rwl = (rw - rwh.astype(F32)).astype(BF16)
    xn, h2, logits = _merge_call(h, ym, yl, yr, x, mod, p['w_merge'].astype(BF16), p['b_merge'],
                                 p['w_branch'].astype(BF16), p['w_out'].astype(BF16), p['norm2_g'],
                                 rwh, rwl)

    xs_p, gv_p, rank_p = _prompt_route_call(logits, h2)
    rows_s, gv_s = _route_sample(logits)
    xs_s = h2.at[rows_s].get(mode='promise_in_bounds')
    gv = jnp.concatenate([gv_p, gv_s[..., None]], axis=1)
    y_p, y_s = _expert_call(xs_p, xs_s, gv, mod, stacked['exp_w1'], stacked['exp_w3'],
                            stacked['exp_w2'], layer)
    x_out = xn.at[rows_s.reshape(-1)].add(y_s.reshape(-1, D), mode='promise_in_bounds')
    x_out = _prompt_combine_call(x_out, y_p, rank_p)
    return x_out, (new_c, new_n, new_m, new_lh, new_rs)


def kernel(x_prompt, x_sample, c, state_mlstm_C, state_mlstm_n, state_mlstm_m, state_lru_h, state_ret_S, c_ctx, w_ada, b_ada, norm1_g, norm2_g, w_in, mlstm_gate_bias, mlstm_norm_g, lru_conv_w, lru_conv_b, lru_wr, lru_br, lru_wi, lru_bi, lru_lambda, ret_decay, ret_norm_g, w_branch, w_merge, b_merge, w_out, router_w, exp_w1, exp_w3, exp_w2, final_g):
    x = jnp.concatenate([x_prompt.reshape(ROWS_PROMPT, D), x_sample.reshape(-1, D)], axis=0)
    cond8 = jnp.concatenate([c_ctx[None, :], c, jnp.zeros((8 - 1 - N_SAMPLE_SEQ, D), F32)], axis=0)
    mod_all = _ada_call(cond8, w_ada, b_ada).reshape(DEPTH, 8, 6, D)
    rope_tabs = _rope_tables()
    stacked = dict(exp_w1=exp_w1, exp_w3=exp_w3, exp_w2=exp_w2)

    new_states = (None,) * 5
    for l in range(DEPTH):
        p = dict(norm1_g=norm1_g[l], norm2_g=norm2_g[l], w_in=w_in[l],
                 mlstm_gate_bias=mlstm_gate_bias[l], mlstm_norm_g=mlstm_norm_g[l],
                 lru_conv_w=lru_conv_w[l], lru_conv_b=lru_conv_b[l], lru_wr=lru_wr[l],
                 lru_br=lru_br[l], lru_wi=lru_wi[l], lru_bi=lru_bi[l], lru_lambda=lru_lambda[l],
                 ret_decay=ret_decay[l], ret_norm_g=ret_norm_g[l], w_branch=w_branch[l],
                 w_merge=w_merge[l], b_merge=b_merge[l], w_out=w_out[l], router_w=router_w[l])
        states = (state_mlstm_C[:, l], state_mlstm_n[:, l], state_mlstm_m[:, l],
                  state_lru_h[:, l], state_ret_S[:, l])
        x, new_states = _layer(x, mod_all[l], p, states, rope_tabs, stacked, l, new_states)

    y_prompt = _final_norm_call(x, final_g, 0, ROWS_PROMPT).reshape(N_PROMPT_SEQ, T_PROMPT, D)
    y_sample = _final_norm_call(x, final_g, ROWS_PROMPT, ROWS - ROWS_PROMPT).reshape(
        N_SAMPLE_SEQ, T_SAMPLE, D)
    new_c, new_n, new_m, new_lh, new_rs = new_states
    return (y_prompt, y_sample, new_c, new_n[:, :, :, :, 0, :], new_m[:, :, :, :, 0, 0], new_lh, new_rs)
```

```python
import functools

import jax
import jax.numpy as jnp
import numpy as np
from jax import lax
from jax.experimental import pallas as pl
from jax.experimental.pallas import tpu as pltpu

F32 = jnp.float32
BF16 = jnp.bfloat16

D = 1024
DEPTH = 2
N_PROMPT_SEQ = 16
T_PROMPT = 256
N_SAMPLE_SEQ = 2
T_SAMPLE = 4096
ROWS_PROMPT = N_PROMPT_SEQ * T_PROMPT
ROWS = ROWS_PROMPT + N_SAMPLE_SEQ * T_SAMPLE
GROUP_ROWS = 4096
GRID_W = 64
CHUNK = 256
EPS = 1e-6
F32_TINY = 1.1754944e-38
M_HEADS = 4
M_DH = 256
R_HEADS = 8
R_DH = 128
L_BLOCKS = 8
L_BW = 128
L_C = 8.0
ROPE_BASE = 10000.0
N_EXPERTS = 16
CAP_PROMPT = 2 * T_PROMPT // N_EXPERTS
CAP_SAMPLE = 2 * T_SAMPLE // N_EXPERTS
ROWS_PER_EXPERT = N_PROMPT_SEQ * CAP_PROMPT + N_SAMPLE_SEQ * CAP_SAMPLE
N_GATE_COLS = 16
LANES = 128
SUBLANES = 8
VMEM_LIMIT = 56 * 2 ** 20

COL_MQ, COL_MV, COL_MO, COL_LX, COL_LZ, COL_RQ, COL_RV, COL_RG = (i * D for i in range(8))
D_MAIN = 8 * D
ROW_MK, ROW_RK = 0, D
D_KT = 2 * D


def _cparams(sem, vmem=None):
    return pltpu.CompilerParams(dimension_semantics=sem, vmem_limit_bytes=vmem)


def _dot(a, b):
    return jnp.dot(a, b, preferred_element_type=F32)


def _dot_nt(a, b):
    return lax.dot_general(a, b, (((1,), (1,)), ((), ())), preferred_element_type=F32)


def _split3(x):
    a = x.astype(BF16)
    r = x - a.astype(F32)
    b = r.astype(BF16)
    c = (r - b.astype(F32)).astype(BF16)
    return a, b, c


def _log_sigmoid(x):
    return jnp.minimum(x, 0.0) - jnp.log1p(jnp.exp(-jnp.abs(x)))


def _sigmoid(x):
    return 0.5 * jnp.tanh(0.5 * x) + 0.5


def _rms(x):
    return x * lax.rsqrt(jnp.mean(x * x, axis=-1, keepdims=True) + EPS)


def _chunk(c):
    return pl.ds(pl.multiple_of(c * CHUNK, CHUNK), CHUNK)


def _alias_prev(args, in_specs, prev):
    aliases = {}
    for out_idx, arr in enumerate(prev):
        if arr is not None:
            aliases[len(args)] = out_idx
            args.append(arr)
            in_specs.append(pl.BlockSpec(memory_space=pl.ANY))
    return aliases, len(aliases)


def _ada_kernel(c_ref, w_ref, b_ref, o_ref):
    c = c_ref[...]
    s = (c * jax.nn.sigmoid(c)).astype(BF16)
    o_ref[0] = _dot(s, w_ref[0].astype(BF16)) + b_ref[0]


def _ada_call(cond8, w_ada, b_ada):
    tn = 1536
    return pl.pallas_call(
        _ada_kernel,
        grid=(DEPTH, 6 * D // tn),
        in_specs=[pl.BlockSpec((8, D), lambda l, j: (0, 0)),
                  pl.BlockSpec((1, D, tn), lambda l, j: (l, 0, j)),
                  pl.BlockSpec((1, 1, tn), lambda l, j: (l, 0, j))],
        out_specs=pl.BlockSpec((1, 8, tn), lambda l, j: (l, 0, j)),
        out_shape=jax.ShapeDtypeStruct((DEPTH, 8, 6 * D), F32),
        compiler_params=_cparams(("arbitrary", "arbitrary"), VMEM_LIMIT),
        name="ada",
    )(cond8, w_ada, b_ada.reshape(DEPTH, 1, 6 * D))


def _norm1_kernel(x_ref, g_ref, mod_ref, wg_ref, h_ref, gate_ref):
    y = _rms(x_ref[...]) * g_ref[...]
    h = (y * (1.0 + mod_ref[0, 1:2, :]) + mod_ref[0, 0:1, :]).astype(BF16)
    h_ref[...] = h
    gate_ref[...] = _dot(h, wg_ref[...])


def _norm1_call(x, g, mod, w_gate):
    tm = 512
    return pl.pallas_call(
        _norm1_kernel,
        grid=(ROWS // tm,),
        in_specs=[pl.BlockSpec((tm, D), lambda i: (i, 0)),
                  pl.BlockSpec((1, D), lambda i: (0, 0)),
                  pl.BlockSpec((1, 6, D), lambda i: (i * tm // GROUP_ROWS, 0, 0)),
                  pl.BlockSpec((D, LANES), lambda i: (0, 0))],
        out_specs=[pl.BlockSpec((tm, D), lambda i: (i, 0)),
                   pl.BlockSpec((tm, LANES), lambda i: (i, 0))],
        out_shape=[jax.ShapeDtypeStruct((ROWS, D), BF16),
                   jax.ShapeDtypeStruct((ROWS, LANES), F32)],
        compiler_params=_cparams(("arbitrary",), VMEM_LIMIT),
        name="norm1",
    )(x, g.reshape(1, D), mod, w_gate)


def _mm_kernel(x_ref, wt_ref, o_ref):
    o_ref[...] = _dot_nt(x_ref[...], wt_ref[...]).astype(o_ref.dtype)


def _inproj_call(h, w_main_t):
    tm, tn = 1024, 1024
    return pl.pallas_call(
        _mm_kernel,
        grid=(D_MAIN // tn, ROWS // tm),
        in_specs=[pl.BlockSpec((tm, D), lambda j, i: (i, 0)),
                  pl.BlockSpec((tn, D), lambda j, i: (j, 0))],
        out_specs=pl.BlockSpec((tm, tn), lambda j, i: (i, j)),
        out_shape=jax.ShapeDtypeStruct((ROWS, D_MAIN), BF16),
        compiler_params=_cparams(("arbitrary", "arbitrary"), VMEM_LIMIT),
        name="inproj",
    )(h, w_main_t)


def _mm_t_kernel(wt_ref, x_ref, o_ref):
    o_ref[...] = _dot_nt(wt_ref[...], x_ref[...]).astype(o_ref.dtype)


def _inproj_t_call(h, w_kt):
    tm, tn = 1024, 1024
    return pl.pallas_call(
        _mm_t_kernel,
        grid=(D_KT // tn, ROWS // tm),
        in_specs=[pl.BlockSpec((tn, D), lambda j, i: (j, 0)),
                  pl.BlockSpec((tm, D), lambda j, i: (i, 0))],
        out_specs=pl.BlockSpec((tn, tm), lambda j, i: (j, i)),
        out_shape=jax.ShapeDtypeStruct((D_KT, ROWS), BF16),
        compiler_params=_cparams(("arbitrary", "arbitrary"), VMEM_LIMIT),
        name="inproj_t",
    )(w_kt, h)


M_AUG = M_DH + LANES
M_GROUP_SHORT = 2


def _mlstm_kernel(*refs, n_chunks, has_init, emit_state, n_prev):
    q_ref, kt_ref, v_ref, o_ref, g_ref, ng_ref = refs[:6]
    pos = 6
    if has_init:
        c0_ref, m0_ref = refs[pos:pos + 2]
        pos += 2
    pos += n_prev
    y_ref = refs[pos]
    pos += 1
    if emit_state:
        co_ref, no_ref, mo_ref = refs[pos:pos + 3]
        pos += 3
    caug_sc, m_sc, hf_sc, hb_sc, gp_mp, gp_bc, gp_row = refs[pos:]
    n_group = caug_sc.shape[0]

    zero_state = (not has_init) and n_chunks == 1
    if has_init:
        for g in range(n_group):
            caug_sc[g] = c0_ref[0, :, g]
            m_sc[g] = m0_ref[0, :, g]
    else:
        if not zero_state:
            caug_sc[...] = jnp.zeros_like(caug_sc)
        m_sc[...] = jnp.zeros_like(m_sc)

    ri = lax.broadcasted_iota(jnp.int32, (CHUNK, CHUNK), 0)
    ci = lax.broadcasted_iota(jnp.int32, (CHUNK, CHUNK), 1)
    r8 = lax.broadcasted_iota(jnp.int32, (SUBLANES, CHUNK), 0)
    ones_b = jnp.ones((CHUNK, LANES), BF16)
    allowed = [ri >= ci, ci >= ri]
    tri = [jnp.where(ri <= ci, 1.0, 0.0).astype(BF16), jnp.where(ri >= ci, 1.0, 0.0).astype(BF16)]

    def gate_pass(c, carry):
        rows = _chunk(c)
        for g in range(n_group):
            g8 = g_ref[g, :, rows]
            for d in range(2):
                i_idx, f_idx = (0, 1) if d == 0 else (2, 3)
                ls = jnp.where(r8 == f_idx, _log_sigmoid(g8), g8)
                cum = sum(_dot(p, tri[d]) for p in _split3(ls))
                b_row = cum[f_idx:f_idx + 1]
                g_row = ls[i_idx:i_idx + 1] - b_row
                gmat = jnp.where(allowed[d], jnp.broadcast_to(g_row, (CHUNK, CHUNK)), -jnp.inf)
                mp_col = jnp.max(gmat, axis=-1, keepdims=True)
                gp_bc[g, d, c] = jnp.broadcast_to(b_row, (LANES, CHUNK)).T
                gp_mp[g, d, c] = jnp.broadcast_to(mp_col, (CHUNK, LANES))
                b_last_11 = b_row[:, CHUNK - 1:CHUNK] if d == 0 else b_row[:, 0:1]
                log_k = b_last_11 + g_row
                a_11 = jnp.max(log_k, axis=-1, keepdims=True)
                gp_row[g, d, c] = jnp.where(
                    r8 == 0, g_row, jnp.where(
                        r8 == 1, jnp.exp(log_k - a_11), jnp.where(r8 == 2, b_last_11, a_11)))
        return carry

    lax.fori_loop(0, n_chunks, gate_pass, 0, unroll=min(n_chunks, 8))

    def step(g, d, c, m_t):
        rows = _chunk(c)
        hcols = slice(g * M_DH, (g + 1) * M_DH)
        mp = gp_mp[g, d, c]
        bc = gp_bc[g, d, c]
        row_t = gp_row[g, d, c]
        gmat = jnp.where(allowed[d], jnp.broadcast_to(row_t[0:1], (CHUNK, CHUNK)), -jnp.inf)
        e1 = jnp.exp(gmat - jnp.concatenate([mp] * (CHUNK // LANES), axis=1))

        q = q_ref[rows, hcols]
        kt = kt_ref[hcols, rows]
        vaug = jnp.concatenate([v_ref[rows, hcols], ones_b], axis=1)
        s1 = _dot(q, kt) * e1
        sv1 = _dot(s1.astype(BF16), vaug)
        kw1 = (kt.astype(F32) * row_t[1:2]).astype(BF16)
        u1 = _dot(kw1, vaug)
        b_last = jnp.broadcast_to(row_t[2:3, 0:LANES], (SUBLANES, LANES))
        a_t = jnp.broadcast_to(row_t[3:4, 0:LANES], (SUBLANES, LANES))

        m_b = jnp.broadcast_to(m_t[0:1, :], (CHUNK, LANES))
        mi = jnp.maximum(m_b, mp)
        r = jnp.exp(mp - mi)
        w_prev = jnp.exp(m_b - mi)
        floor = jnp.exp(-(bc + mi))
        if zero_state:
            qc = None
            den = r * sv1[:, M_DH:]
        else:
            qc = _dot(q, caug_sc[g, d].astype(BF16))
            den = r * sv1[:, M_DH:] + w_prev * qc[:, M_DH:]
        inv = 1.0 / jnp.maximum(jnp.abs(den), floor)
        h_sc = hf_sc if d == 0 else hb_sc
        for blk in range(M_DH // LANES):
            cols = slice(blk * LANES, (blk + 1) * LANES)
            num = r * sv1[:, cols]
            if not zero_state:
                num = num + w_prev * qc[:, cols]
            h_sc[rows, slice(g * M_DH + blk * LANES, g * M_DH + (blk + 1) * LANES)] = num * inv

        m_new = jnp.maximum(b_last + m_t, a_t)
        w_c = jnp.exp(b_last + m_t - m_new)[0:1, :]
        w_u = jnp.exp(a_t - m_new)[0:1, :]
        for blk in range(M_AUG // LANES):
            cols = slice(blk * LANES, (blk + 1) * LANES)
            if zero_state:
                caug_sc[g, d, :, cols] = w_u * u1[:, cols]
            else:
                caug_sc[g, d, :, cols] = w_c * caug_sc[g, d, :, cols] + w_u * u1[:, cols]
        return m_new

    def scan_body(c, carry):
        new = []
        for g in range(n_group):
            new.append(step(g, 0, c, carry[2 * g]))
            new.append(step(g, 1, n_chunks - 1 - c, carry[2 * g + 1]))
        return tuple(new)

    m_init = tuple(jnp.broadcast_to(m_sc[g, d], (SUBLANES, LANES))
                   for g in range(n_group) for d in range(2))
    m_fin = lax.fori_loop(0, n_chunks, scan_body, m_init)
    for g in range(n_group):
        for d in range(2):
            m_sc[g, d] = m_fin[2 * g + d][0:1, :]

    def out_body(c, carry):
        rows = _chunk(c)
        for g in range(n_group):
            hcols = slice(g * M_DH, (g + 1) * M_DH)
            y = _rms(hf_sc[rows, hcols] + hb_sc[rows, hcols]) * ng_ref[:, hcols]
            y_ref[rows, hcols] = (y * _sigmoid(o_ref[rows, hcols].astype(F32))).astype(y_ref.dtype)
        return carry

    lax.fori_loop(0, n_chunks, out_body, 0)

    if emit_state:
        for g in range(n_group):
            co_ref[0, 0, :, g] = caug_sc[g, :, :, 0:M_DH]
            for d in range(2):
                no_ref[0, 0, d, g] = caug_sc[g, d, :, M_DH:].T[0:1, :]
            mo_ref[0, 0, :, g] = m_sc[g]


def _mlstm_call(proj, proj_t, grow, norm_g, state, prev, *, t, n_seq, row_block0, layer, emit_state):
    has_init = state is not None
    group = M_GROUP_SHORT if t <= CHUNK else 1
    gw = group * M_DH
    big = dict(pipeline_mode=pl.Buffered(1)) if t * gw * 2 > 2 ** 21 else {}

    def col(base):
        return lambda s, h: (row_block0 + s, base // gw + h)

    st_caug = pl.BlockSpec((1, 2, group, M_DH, M_AUG), lambda s, h: (s, 0, h, 0, 0))
    st_m = pl.BlockSpec((1, 2, group, 1, LANES), lambda s, h: (s, 0, h, 0, 0))
    so_c = pl.BlockSpec((1, 1, 2, group, M_DH, M_DH), lambda s, h: (s, layer, 0, h, 0, 0))
    so_n = pl.BlockSpec((1, 1, 2, group, 1, M_DH), lambda s, h: (s, layer, 0, h, 0, 0))
    so_m = pl.BlockSpec((1, 1, 2, group, 1, LANES), lambda s, h: (s, layer, 0, h, 0, 0))
    in_specs = [pl.BlockSpec((t, gw), col(COL_MQ), **big),
                pl.BlockSpec((gw, t), lambda s, h: (ROW_MK // gw + h, row_block0 + s), **big),
                pl.BlockSpec((t, gw), col(COL_MV), **big),
                pl.BlockSpec((t, gw), col(COL_MO), **big),
                pl.BlockSpec((group, SUBLANES, t), lambda s, h: (h, 0, row_block0 + s)),
                pl.BlockSpec((1, gw), lambda s, h: (0, h))]
    args = [proj, proj_t, proj, proj, grow, norm_g.reshape(1, D)]
    if has_init:
        in_specs += [st_caug, st_m]
        args += list(state)
    out_specs = [pl.BlockSpec((t, gw), lambda s, h: (row_block0 + s, h))]
    out_shape = [jax.ShapeDtypeStruct((ROWS, D), BF16)]
    if emit_state:
        out_specs += [so_c, so_n, so_m]
        out_shape += [jax.ShapeDtypeStruct((n_seq, DEPTH, 2, M_HEADS, M_DH, M_DH), F32),
                      jax.ShapeDtypeStruct((n_seq, DEPTH, 2, M_HEADS, 1, M_DH), F32),
                      jax.ShapeDtypeStruct((n_seq, DEPTH, 2, M_HEADS, 1, LANES), F32)]
    aliases, n_prev = _alias_prev(args, in_specs, prev)
    return pl.pallas_call(
        functools.partial(_mlstm_kernel, n_chunks=t // CHUNK, has_init=has_init,
                          emit_state=emit_state, n_prev=n_prev),
        grid=(n_seq, M_HEADS // group),
        in_specs=in_specs,
        out_specs=out_specs,
        out_shape=out_shape,
        input_output_aliases=aliases,
        scratch_shapes=[pltpu.VMEM((group, 2, M_DH, M_AUG), F32),
                        pltpu.VMEM((group, 2, 1, LANES), F32),
                        pltpu.VMEM((t, gw), F32),
                        pltpu.VMEM((t, gw), F32),
                        pltpu.VMEM((group, 2, t // CHUNK, CHUNK, LANES), F32),
                        pltpu.VMEM((group, 2, t // CHUNK, CHUNK, LANES), F32),
                        pltpu.VMEM((group, 2, t // CHUNK, SUBLANES, CHUNK), F32)],
        compiler_params=_cparams(("arbitrary", "arbitrary"), VMEM_LIMIT),
        name="mlstm",
    )(*args)


R_GROUP_SHORT, R_GROUP_LONG = 4, 2

def _ret_kernel(*refs, n_chunks, rope, has_init, emit_state, n_prev):
    q_ref, kt_ref, v_ref, g_ref, dec_ref, ng_ref = refs[:6]
    pos = 6
    if rope:
        cos_ref, sin_ref, cos_t_ref, sin_t_ref = refs[pos:pos + 4]
        pos += 4
    if has_init:
        s0_ref = refs[pos]
        pos += 1
    pos += n_prev
    y_ref = refs[pos]
    pos += 1
    if emit_state:
        so_ref = refs[pos]
        pos += 1
    s_sc, of_sc, ob_sc, ktr_sc = refs[pos:pos + 4]
    qr_sc = refs[pos + 4] if rope else None
    n_group = s_sc.shape[0]

    zero_state = (not has_init) and n_chunks == 1
    if has_init:
        for g in range(n_group):
            s_sc[g] = s0_ref[0, :, g]
    elif not zero_state:
        s_sc[...] = jnp.zeros_like(s_sc)

    ri = lax.broadcasted_iota(jnp.int32, (CHUNK, CHUNK), 0)
    ci = lax.broadcasted_iota(jnp.int32, (CHUNK, CHUNK), 1)
    quarter = R_DH // 4
    if rope:
        fr = lax.broadcasted_iota(jnp.int32, (R_DH, R_DH), 0)
        fc = lax.broadcasted_iota(jnp.int32, (R_DH, R_DH), 1)
        partner = fc + jnp.where((fc % (2 * quarter)) < quarter, quarter, -quarter)
        perm_b = jnp.where(fr == partner, 1.0, 0.0).astype(BF16)

    def prep_body(c, carry):
        rows = _chunk(c)
        for g in range(n_group):
            hcols = slice(g * R_DH, (g + 1) * R_DH)
            ktf = kt_ref[hcols, rows].astype(F32) * (R_DH ** -0.5)
            if rope:
                swapped = jnp.concatenate([ktf[quarter:2 * quarter], ktf[0:quarter],
                                           ktf[3 * quarter:], ktf[2 * quarter:3 * quarter]], axis=0)
                ktf = ktf * cos_t_ref[:, rows] + swapped * sin_t_ref[:, rows]
                q = q_ref[rows, hcols]
                qr = q.astype(F32) * cos_ref[rows, :] + _dot(q, perm_b) * sin_ref[rows, :]
                qr_sc[rows, hcols] = qr.astype(BF16)
            ktr_sc[hcols, rows] = ktf.astype(BF16)
        return carry

    lax.fori_loop(0, n_chunks, prep_body, 0, unroll=min(n_chunks, 2))
    qsrc = qr_sc if rope else q_ref

    lane = lax.broadcasted_iota(jnp.int32, (1, CHUNK), 1).astype(F32)
    row_f = lax.broadcasted_iota(jnp.int32, (CHUNK, R_DH), 0).astype(F32)
    consts = []
    for gd in range(2 * n_group):
        g, d = divmod(gd, 2)
        lg = -jnp.exp(dec_ref[g, d:d + 1, :])
        lg11 = lg[:, 0:1]
        diff = (ri - ci) if d == 0 else (ci - ri)
        causal = diff >= 0
        dmat = jnp.where(causal, jnp.exp(lg11 * jnp.where(causal, diff, 0).astype(F32)), 0.0)
        if d == 0:
            q_dec = jnp.exp(lg * (row_f + 1.0))
            k_dec = jnp.exp(lg11 * (CHUNK - 1.0 - lane))
        else:
            q_dec = jnp.exp(lg * (CHUNK - row_f))
            k_dec = jnp.exp(lg11 * lane)
        g_chunk = jnp.exp(lg * float(CHUNK))
        consts.append((dmat, q_dec, k_dec, g_chunk))

    def step(g, d, c):
        dmat, q_dec, k_dec, g_chunk = consts[2 * g + d]
        rows = _chunk(c)
        hcols = slice(g * R_DH, (g + 1) * R_DH)
        q = qsrc[rows, hcols]
        kt = ktr_sc[hcols, rows]
        v = v_ref[rows, hcols]
        s = _dot(q, kt) * dmat
        o = _dot(s.astype(BF16), v)
        if not zero_state:
            o = o + q_dec * _dot(q, s_sc[g, d].astype(BF16))
        if d == 0:
            of_sc[rows, hcols] = o
        else:
            ob_sc[rows, hcols] = o
        kd = (kt.astype(F32) * k_dec).astype(BF16)
        if zero_state:
            s_sc[g, d] = _dot(kd, v)
        else:
            s_sc[g, d] = g_chunk * s_sc[g, d] + _dot(kd, v)

    def scan_body(c, carry):
        for g in range(n_group):
            step(g, 0, c)
            step(g, 1, n_chunks - 1 - c)
        return carry

    lax.fori_loop(0, n_chunks, scan_body, 0, unroll=min(n_chunks, 2))

    def out_body(c, carry):
        rows = _chunk(c)
        for g in range(n_group):
            hcols = slice(g * R_DH, (g + 1) * R_DH)
            y = _rms(of_sc[rows, hcols] + ob_sc[rows, hcols]) * ng_ref[:, hcols]
            gate = g_ref[rows, hcols].astype(F32)
            y_ref[rows, hcols] = (y * (gate * _sigmoid(gate))).astype(y_ref.dtype)
        return carry

    lax.fori_loop(0, n_chunks, out_body, 0, unroll=min(n_chunks, 2))

    if emit_state:
        for g in range(n_group):
            so_ref[0, 0, :, g] = s_sc[g]


def _ret_call(proj, proj_t, decay, norm_g, rope_tabs, state, prev, *, t, n_seq, row_block0, layer, emit_state):
    has_init = state is not None
    rope = rope_tabs is not None

    group = R_GROUP_SHORT if t <= CHUNK else R_GROUP_LONG
    gw = group * R_DH

    def col(base):
        return lambda s, h: (row_block0 + s, base // gw + h)

    st_s = pl.BlockSpec((1, 2, group, R_DH, R_DH), lambda s, h: (s, 0, h, 0, 0))
    in_specs = [pl.BlockSpec((t, gw), col(COL_RQ)),
                pl.BlockSpec((gw, t), lambda s, h: (ROW_RK // gw + h, row_block0 + s)),
                pl.BlockSpec((t, gw), col(COL_RV)),
                pl.BlockSpec((t, gw), col(COL_RG)),
                pl.BlockSpec((group, 2, LANES), lambda s, h: (h, 0, 0)),
                pl.BlockSpec((1, gw), lambda s, h: (0, h))]
    args = [proj, proj_t, proj, proj, decay, norm_g.reshape(1, D)]
    if rope:
        in_specs += [pl.BlockSpec((t, R_DH), lambda s, h: (0, 0))] * 2
        in_specs += [pl.BlockSpec((R_DH, t), lambda s, h: (0, 0))] * 2
        args += list(rope_tabs)
    if has_init:
        in_specs.append(st_s)
        args.append(state)
    out_specs = [pl.BlockSpec((t, gw), lambda s, h: (row_block0 + s, h))]
    out_shape = [jax.ShapeDtypeStruct((ROWS, D), BF16)]
    if emit_state:
        out_specs.append(pl.BlockSpec((1, 1, 2, group, R_DH, R_DH), lambda s, h: (s, layer, 0, h, 0, 0)))
        out_shape.append(jax.ShapeDtypeStruct((n_seq, DEPTH, 2, R_HEADS, R_DH, R_DH), F32))
    aliases, n_prev = _alias_prev(args, in_specs, prev)
    scratch = [pltpu.VMEM((group, 2, R_DH, R_DH), F32),
               pltpu.VMEM((t, gw), F32),
               pltpu.VMEM((t, gw), F32),
               pltpu.VMEM((gw, t), BF16)]
    if rope:
        scratch.append(pltpu.VMEM((t, gw), BF16))
    return pl.pallas_call(
        functools.partial(_ret_kernel, n_chunks=t // CHUNK, rope=rope, has_init=has_init,
                          emit_state=emit_state, n_prev=n_prev),
        grid=(n_seq, R_HEADS // group),
        in_specs=in_specs,
        out_specs=out_specs,
        out_shape=out_shape,
        input_output_aliases=aliases,
        scratch_shapes=scratch,
        compiler_params=_cparams(("arbitrary", "arbitrary"), VMEM_LIMIT),
        name="retention",
    )(*args)


LRU_SLAB = 64


def _tile_scan(a, b, reverse):
    row = lax.broadcasted_iota(jnp.int32, (SUBLANES, L_BW), 0)
    for k in (1, 2, 4):
        if reverse:
            keep = row < SUBLANES - k
            shift = SUBLANES - k
        else:
            keep = row >= k
            shift = k
        a_sh = jnp.where(keep, pltpu.roll(a, shift, 0), 1.0)
        b_sh = jnp.where(keep, pltpu.roll(b, shift, 0), 0.0)
        b = a * b_sh + b
        a = a * a_sh
    return a, b


def _lru_kernel(*refs, t, has_init, emit_state, n_prev):
    lx_ref, lz_ref, cw_ref, cb_ref, wr_ref, wi_ref, br_ref, bi_ref, lam_ref = refs[:9]
    pos = 9
    if has_init:
        h0_ref = refs[pos]
        pos += 1
    pos += n_prev
    y_ref = refs[pos]
    pos += 1
    if emit_state:
        hfin_ref = refs[pos]
        pos += 1
    xpad_sc, a_sc, b_sc = refs[pos:]
    rc = min(t, 256)

    xpad_sc[0:8, :] = jnp.zeros((8, L_BW), F32)
    xpad_sc[t + 8:t + 16, :] = jnp.zeros((8, L_BW), F32)

    def pad_body(c, carry):
        r0 = pl.multiple_of(c * rc, rc)
        xpad_sc[pl.ds(r0 + 8, rc), :] = lx_ref[pl.ds(r0, rc), :].astype(F32)
        return carry

    lax.fori_loop(0, t // rc, pad_body, 0)

    wr = [(0.5 * wr_ref[d, 0]).astype(BF16) for d in range(2)]
    wi = [(0.5 * wi_ref[d, 0]).astype(BF16) for d in range(2)]
    half_br = [0.5 * br_ref[d:d + 1, :] for d in range(2)]
    half_bi = [0.5 * bi_ref[d:d + 1, :] for d in range(2)]
    half_c = []
    for d in range(2):
        lam = lam_ref[d:d + 1, :]
        softplus_neg = jnp.maximum(-lam, 0.0) + jnp.log1p(jnp.exp(-jnp.abs(lam)))
        half_c.append(-0.5 * L_C * softplus_neg)

    def gate_body(c, carry):
        r0 = pl.multiple_of(c * rc, rc)
        xe = xpad_sc[pl.ds(r0, rc + 16), :]
        n = rc + 16
        u = (cb_ref[...] + cw_ref[0:1, :] * pltpu.roll(xe, 2, 0)[8:8 + rc]
             + cw_ref[1:2, :] * pltpu.roll(xe, 1, 0)[8:8 + rc]
             + cw_ref[2:3, :] * xe[8:8 + rc]
             + cw_ref[3:4, :] * pltpu.roll(xe, n - 1, 0)[8:8 + rc])
        ub = u.astype(BF16)
        half_u = 0.5 * u
        for d in range(2):
            tr = jnp.tanh(_dot(ub, wr[d]) + half_br[d])
            ti = jnp.tanh(_dot(ub, wi[d]) + half_bi[d])
            log_a = half_c[d] * tr + half_c[d]
            a = jnp.exp(log_a)
            z = jnp.tanh(log_a) * (-1.0 - a * a)
            mult = z * lax.rsqrt(jnp.maximum(z, F32_TINY))
            a_sc[d, pl.ds(r0, rc), :] = a
            b_sc[d, pl.ds(r0, rc), :] = mult * (ti * half_u + half_u)
        return carry

    lax.fori_loop(0, t // rc, gate_body, 0, unroll=min(t // rc, 2))

    n_slabs = t // LRU_SLAB
    tiles = LRU_SLAB // SUBLANES

    row8 = lax.broadcasted_iota(jnp.int32, (SUBLANES, L_BW), 0)

    def bcast_row(x, r):
        return jnp.broadcast_to(x[r:r + 1, :], (SUBLANES, L_BW))

    def slab_scan(d, r0, c_in):
        reverse = d == 1
        edge = 0 if reverse else SUBLANES - 1
        local = []
        spa = jnp.ones((SUBLANES, L_BW), F32)
        shl = jnp.zeros((SUBLANES, L_BW), F32)
        for k in range(tiles):
            rows = pl.ds(r0 + k * SUBLANES, SUBLANES)
            pa, hl = _tile_scan(a_sc[d, rows, :], b_sc[d, rows, :], reverse)
            local.append((rows, pa, hl))
            spa = jnp.where(row8 == k, bcast_row(pa, edge), spa)
            shl = jnp.where(row8 == k, bcast_row(hl, edge), shl)
        cpa, chl = _tile_scan(spa, shl, reverse)
        after = chl + cpa * c_in
        for k, (rows, pa, hl) in enumerate(local):
            prev = k + 1 if reverse else k - 1
            cin = c_in if (prev < 0 or prev >= tiles) else bcast_row(after, prev)
            b_sc[d, rows, :] = hl + pa * cin
        return bcast_row(after, 0 if reverse else tiles - 1)

    def scan_body(c, carry):
        cf, cbk = carry
        cf = slab_scan(0, pl.multiple_of(c * LRU_SLAB, LRU_SLAB), cf)
        cbk = slab_scan(1, pl.multiple_of((n_slabs - 1 - c) * LRU_SLAB, LRU_SLAB), cbk)
        return cf, cbk

    if has_init:
        cf0 = jnp.broadcast_to(h0_ref[0, 0:1, :], (SUBLANES, L_BW))
        cb0 = jnp.broadcast_to(h0_ref[0, 1:2, :], (SUBLANES, L_BW))
    else:
        cf0 = jnp.zeros((SUBLANES, L_BW), F32)
        cb0 = jnp.zeros((SUBLANES, L_BW), F32)
    cf, cbk = lax.fori_loop(0, n_slabs, scan_body, (cf0, cb0), unroll=2)
    if emit_state:
        hfin_ref[0, 0, 0:1, :] = cf[0:1, :]
        hfin_ref[0, 0, 1:2, :] = cbk[0:1, :]

    def out_body(c, carry):
        rows = pl.ds(pl.multiple_of(c * rc, rc), rc)
        z = lz_ref[rows, :].astype(F32)
        gelu = 0.5 * z * (1.0 + jnp.tanh(0.7978845608028654 * (z + 0.044715 * (z * z * z))))
        y_ref[rows, :] = ((b_sc[0, rows, :] + b_sc[1, rows, :]) * gelu).astype(y_ref.dtype)
        return carry

    lax.fori_loop(0, t // rc, out_body, 0)


def _lru_call(proj, p, state, prev, *, t, n_seq, row_block0, layer, emit_state):
    has_init = state is not None

    def col(base):
        return lambda s, b: (row_block0 + s, base // L_BW + b)

    vec2 = pl.BlockSpec((2, L_BW), lambda s, b: (0, b))
    st = pl.BlockSpec((1, 2, L_BW), lambda s, b: (s, 0, b))
    in_specs = [pl.BlockSpec((t, L_BW), col(COL_LX)),
                pl.BlockSpec((t, L_BW), col(COL_LZ)),
                pl.BlockSpec((4, L_BW), lambda s, b: (0, b)),
                pl.BlockSpec((1, L_BW), lambda s, b: (0, b)),
                pl.BlockSpec((2, 1, L_BW, L_BW), lambda s, b: (0, b, 0, 0)),
                pl.BlockSpec((2, 1, L_BW, L_BW), lambda s, b: (0, b, 0, 0)),
                vec2, vec2, vec2]
    args = [proj, proj, p['conv_w'], p['conv_b'].reshape(1, D), p['wr'], p['wi'],
            p['br'], p['bi'], p['lam']]
    if has_init:
        in_specs.append(st)
        args.append(state)
    out_specs = [pl.BlockSpec((t, L_BW), lambda s, b: (row_block0 + s, b))]
    out_shape = [jax.ShapeDtypeStruct((ROWS, D), BF16)]
    if emit_state:
        out_specs.append(pl.BlockSpec((1, 1, 2, L_BW), lambda s, b: (s, layer, 0, b)))
        out_shape.append(jax.ShapeDtypeStruct((n_seq, DEPTH, 2, D), F32))
    aliases, n_prev = _alias_prev(args, in_specs, prev)
    return pl.pallas_call(
        functools.partial(_lru_kernel, t=t, has_init=has_init, emit_state=emit_state, n_prev=n_prev),
        grid=(n_seq, L_BLOCKS),
        in_specs=in_specs,
        out_specs=out_specs,
        out_shape=out_shape,
        input_output_aliases=aliases,
        scratch_shapes=[pltpu.VMEM((t + 16, L_BW), F32),
                        pltpu.VMEM((2, t, L_BW), F32),
                        pltpu.VMEM((2, t, L_BW), F32)],
        compiler_params=_cparams(("arbitrary", "arbitrary"), VMEM_LIMIT),
        name="rglru",
    )(*args)


def _merge_kernel(h_ref, ym_ref, yl_ref, yr_ref, x_ref, mod_ref, wm_ref, bm_ref, wb_ref, wo_ref,
                  n2_ref, rwh_ref, rwl_ref, xo_ref, h2_ref, lg_ref):
    h = h_ref[...]
    merged = None
    for k, y_ref in enumerate((ym_ref, yl_ref, yr_ref)):
        gate = jax.nn.sigmoid(_dot(h, wm_ref[:, k * D:(k + 1) * D]) + bm_ref[:, k * D:(k + 1) * D])
        term = gate * _dot(y_ref[...], wb_ref[k])
        merged = term if merged is None else merged + term
    mix = _dot(merged.astype(BF16), wo_ref[...])
    xn = x_ref[...] + mod_ref[0, 2:3, :] * mix
    xo_ref[...] = xn
    hn = _rms(xn) * n2_ref[...] * (1.0 + mod_ref[0, 4:5, :]) + mod_ref[0, 3:4, :]
    hi = hn.astype(BF16)
    h2_ref[...] = hi
    lo = (hn - hi.astype(F32)).astype(BF16)
    lg_ref[...] = _dot(hi, rwh_ref[...]) + _dot(lo, rwh_ref[...]) + _dot(hi, rwl_ref[...])


def _merge_call(h, ym, yl, yr, x, mod, wm, bm, wb, wo, n2g, rwh, rwl):
    tm = 256
    row = lambda i: (i, 0)
    const2 = lambda i: (0, 0)
    return pl.pallas_call(
        _merge_kernel,
        grid=(ROWS // tm,),
        in_specs=[pl.BlockSpec((tm, D), row),
                  pl.BlockSpec((tm, D), row),
                  pl.BlockSpec((tm, D), row),
                  pl.BlockSpec((tm, D), row),
                  pl.BlockSpec((tm, D), row),
                  pl.BlockSpec((1, 6, D), lambda i: (i * tm // GROUP_ROWS, 0, 0)),
                  pl.BlockSpec((D, 3 * D), const2),
                  pl.BlockSpec((1, 3 * D), const2),
                  pl.BlockSpec((3, D, D), lambda i: (0, 0, 0)),
                  pl.BlockSpec((D, D), const2),
                  pl.BlockSpec((1, D), const2),
                  pl.BlockSpec((D, LANES), const2),
                  pl.BlockSpec((D, LANES), const2)],
        out_specs=[pl.BlockSpec((tm, D), row),
                   pl.BlockSpec((tm, D), row),
                   pl.BlockSpec((tm, LANES), row)],
        out_shape=[jax.ShapeDtypeStruct((ROWS, D), F32),
                   jax.ShapeDtypeStruct((ROWS, D), BF16),
                   jax.ShapeDtypeStruct((ROWS, LANES), F32)],
        compiler_params=_cparams(("arbitrary",), VMEM_LIMIT),
        name="merge",
    )(h, ym, yl, yr, x, mod, wm, bm.reshape(1, 3 * D), wb, wo, n2g.reshape(1, D), rwh, rwl)


EXPERT_TM = 512
assert N_PROMPT_SEQ * CAP_PROMPT == EXPERT_TM and CAP_SAMPLE == EXPERT_TM


def _expert_kernel(xp_ref, xs_ref, g_ref, mod_ref, w1_ref, w3_ref, w2_ref, yp_ref, ys_ref,
                   w1_sc, w3_sc, w2_sc):
    m = pl.program_id(1)

    @pl.when(m == 0)
    def _():
        w1_sc[...] = w1_ref[0, 0].astype(BF16)
        w3_sc[...] = w3_ref[0, 0].astype(BF16)
        w2_sc[...] = w2_ref[0, 0].astype(BF16)

    xs = jnp.where(m == 0, xp_ref[0], xs_ref[0])
    a = _dot(xs, w1_sc[...])
    b = _dot(xs, w3_sc[...])
    mid = (a * jax.nn.sigmoid(a) * b).astype(BF16)

    def down(y_ref):
        y_ref[0] = (_dot(mid, w2_sc[...]) * g_ref[0]) * mod_ref[0, 5:6, :]

    pl.when(m == 0)(functools.partial(down, yp_ref))
    pl.when(m > 0)(functools.partial(down, ys_ref))


def _expert_call(xs_p, xs_s, gv, mod, w1, w3, w2, layer):
    tm = EXPERT_TM
    wspec = pl.BlockSpec((1, 1, D, D), lambda e, m: (layer, e, 0, 0))
    p_spec = pl.BlockSpec((1, tm, D), lambda e, m: (e, 0, 0))
    s_spec = pl.BlockSpec((1, tm, D), lambda e, m: (e, jnp.maximum(m - 1, 0), 0))
    return pl.pallas_call(
        _expert_kernel,
        grid=(N_EXPERTS, ROWS_PER_EXPERT // tm),
        in_specs=[p_spec, s_spec,
                  pl.BlockSpec((1, tm, 1), lambda e, m: (e, m, 0)),
                  pl.BlockSpec((1, 6, D), lambda e, m: (m, 0, 0)),
                  wspec, wspec, wspec],
        out_specs=[p_spec, s_spec],
        out_shape=[jax.ShapeDtypeStruct((N_EXPERTS, tm, D), F32),
                   jax.ShapeDtypeStruct((N_EXPERTS, N_SAMPLE_SEQ * tm, D), F32)],
        scratch_shapes=[pltpu.VMEM((D, D), BF16)] * 3,
        compiler_params=_cparams(("arbitrary", "arbitrary"), VMEM_LIMIT),
        name="experts",
    )(xs_p, xs_s, gv, mod, w1, w3, w2)


PROMPT_SLOTS = N_EXPERTS * CAP_PROMPT


def _slot_onehot(rank_rows):
    slot = lax.broadcasted_iota(jnp.int32, (CAP_PROMPT, T_PROMPT), 0).astype(F32)
    return jnp.concatenate(
        [jnp.where(slot == rank_rows[e:e + 1, :], 1.0, 0.0) for e in range(N_EXPERTS)], axis=0)


def _prompt_route_kernel(lg_ref, h_ref, xs_ref, gv_ref, rank_ref, rank_sc):
    lane = lax.broadcasted_iota(jnp.int32, (T_PROMPT, LANES), 1)
    x = jnp.where(lane < N_EXPERTS, lg_ref[...], -jnp.inf)
    e = jnp.exp(x - jnp.max(x, axis=-1, keepdims=True))
    aff = e / jnp.sum(e, axis=-1, keepdims=True)
    aff_t = aff.T
    ti = lax.broadcasted_iota(jnp.int32, (T_PROMPT, T_PROMPT), 0)
    tj = lax.broadcasted_iota(jnp.int32, (T_PROMPT, T_PROMPT), 1)
    earlier = jnp.where(ti < tj, 1.0, 0.0)
    for ex in range(N_EXPERTS):
        a_row = aff_t[ex:ex + 1, :]
        a_col = aff[:, ex:ex + 1]
        ahead = jnp.where(a_col > a_row, 1.0, jnp.where(a_col == a_row, earlier, 0.0))
        rank_sc[ex:ex + 1, :] = jnp.sum(ahead, axis=0, keepdims=True)
    rank = rank_sc[...]
    onehot = _slot_onehot(rank)
    xs_ref[...] = _dot(onehot.astype(BF16), h_ref[...]).astype(xs_ref.dtype).reshape(
        N_EXPERTS, CAP_PROMPT, D)
    for ex in range(N_EXPERTS):
        blk = onehot[ex * CAP_PROMPT:(ex + 1) * CAP_PROMPT, :]
        gv_ref[ex] = jnp.sum(blk * aff_t[ex:ex + 1, :], axis=-1, keepdims=True)
    rank_ref[0] = rank


def _prompt_route_call(logits, h2):
    return pl.pallas_call(
        _prompt_route_kernel,
        grid=(N_PROMPT_SEQ,),
        in_specs=[pl.BlockSpec((T_PROMPT, LANES), lambda s: (s, 0)),
                  pl.BlockSpec((T_PROMPT, D), lambda s: (s, 0))],
        out_specs=[pl.BlockSpec((N_EXPERTS, CAP_PROMPT, D), lambda s: (0, s, 0)),
                   pl.BlockSpec((N_EXPERTS, CAP_PROMPT, 1), lambda s: (0, s, 0)),
                   pl.BlockSpec((1, N_EXPERTS, T_PROMPT), lambda s: (s, 0, 0))],
        out_shape=[jax.ShapeDtypeStruct((N_EXPERTS, N_PROMPT_SEQ * CAP_PROMPT, D), BF16),
                   jax.ShapeDtypeStruct((N_EXPERTS, N_PROMPT_SEQ * CAP_PROMPT, 1), F32),
                   jax.ShapeDtypeStruct((N_PROMPT_SEQ, N_EXPERTS, T_PROMPT), F32)],
        scratch_shapes=[pltpu.VMEM((N_EXPERTS, T_PROMPT), F32)],
        compiler_params=_cparams(("arbitrary",), VMEM_LIMIT),
        name="prompt_route",
    )(logits, h2)


def _prompt_combine_kernel(x_ref, y_ref, rank_ref, o_ref):
    onehot = _slot_onehot(rank_ref[0]).astype(BF16)
    y = y_ref[...].reshape(PROMPT_SLOTS, D)
    hi = y.astype(BF16)
    lo = (y - hi.astype(F32)).astype(BF16)
    tn = (((0,), (0,)), ((), ()))
    o_ref[...] = x_ref[...] + (lax.dot_general(onehot, hi, tn, preferred_element_type=F32)
                               + lax.dot_general(onehot, lo, tn, preferred_element_type=F32))


def _prompt_combine_call(x, y_p, rank):
    return pl.pallas_call(
        _prompt_combine_kernel,
        grid=(N_PROMPT_SEQ,),
        in_specs=[pl.BlockSpec((T_PROMPT, D), lambda s: (s, 0)),
                  pl.BlockSpec((N_EXPERTS, CAP_PROMPT, D), lambda s: (0, s, 0)),
                  pl.BlockSpec((1, N_EXPERTS, T_PROMPT), lambda s: (s, 0, 0))],
        out_specs=pl.BlockSpec((T_PROMPT, D), lambda s: (s, 0)),
        out_shape=jax.ShapeDtypeStruct((ROWS, D), F32),
        input_output_aliases={0: 0},
        compiler_params=_cparams(("arbitrary",), VMEM_LIMIT),
        name="prompt_combine",
    )(x, y_p, rank)


def _final_norm_kernel(x_ref, g_ref, o_ref):
    o_ref[...] = _rms(x_ref[...]) * g_ref[...]


def _final_norm_call(x, g, row0, n_rows):
    tm = 512
    return pl.pallas_call(
        _final_norm_kernel,
        grid=(n_rows // tm,),
        in_specs=[pl.BlockSpec((tm, D), lambda i: (row0 // tm + i, 0)),
                  pl.BlockSpec((1, D), lambda i: (0, 0))],
        out_specs=pl.BlockSpec((tm, D), lambda i: (i, 0)),
        out_shape=jax.ShapeDtypeStruct((n_rows, D), F32),
        compiler_params=_cparams(("arbitrary",), VMEM_LIMIT),
        name="final_norm",
    )(x, g.reshape(1, D))


def _rope_tables():
    tpos = np.arange(T_SAMPLE)
    lane = np.arange(R_DH)
    pos = np.where(lane[None, :] < R_DH // 2, (tpos // GRID_W)[:, None], (tpos % GRID_W)[:, None])
    n_freq = R_DH // 4
    freqs = np.power(np.float32(ROPE_BASE), -np.arange(n_freq, dtype=np.float32) / np.float32(n_freq))
    ang = pos.astype(np.float32) * freqs[lane % n_freq][None, :]
    first = ((lane % (R_DH // 2)) < n_freq)[None, :]
    cos, sin = np.cos(ang), np.sin(ang)
    sin = np.where(first, -sin, sin)
    return tuple(jnp.asarray(a, F32) for a in (cos, sin, np.ascontiguousarray(cos.T), np.ascontiguousarray(sin.T)))


def _route_sample(logits):
    aff = jax.nn.softmax(logits[ROWS_PROMPT:, :N_EXPERTS], axis=-1)
    gs, is_ = lax.top_k(aff.reshape(N_SAMPLE_SEQ, T_SAMPLE, N_EXPERTS).swapaxes(1, 2), CAP_SAMPLE)
    is_ = is_ + ROWS_PROMPT + (jnp.arange(N_SAMPLE_SEQ) * T_SAMPLE)[:, None, None]
    rows_s = is_.swapaxes(0, 1).reshape(N_EXPERTS, -1)
    return rows_s, gs.swapaxes(0, 1).reshape(N_EXPERTS, -1)


def _layer(x, mod, p, states, rope_tabs, stacked, layer, prev_states):
    mq, mk, mv, mo, mg, lx, lz, rq, rk, rv, rg = jnp.split(
        p['w_in'].T, [1024, 2048, 3072, 4096, 4112, 5136, 6160, 7184, 8208, 9232], axis=0)
    w_main_t = jnp.concatenate([mq, mv, mo, lx, lz, rq, rv, rg], axis=0).astype(BF16)
    w_kt = jnp.concatenate([mk.astype(BF16) * (M_DH ** -0.5), rk.astype(BF16)], axis=0)
    w_gate = jnp.pad(mg.T, ((0, 0), (0, LANES - N_GATE_COLS))).astype(BF16)

    h, gates = _norm1_call(x, p['norm1_g'], mod, w_gate)
    proj = _inproj_call(h, w_main_t)
    proj_t = _inproj_t_call(h, w_kt)

    gb = gates[:, :N_GATE_COLS] + p['mlstm_gate_bias'].reshape(1, N_GATE_COLS)
    grow = gb.reshape(ROWS, 4, M_HEADS).transpose(2, 1, 0)
    grow = jnp.pad(grow, ((0, 0), (0, SUBLANES - 4), (0, 0)))

    sm_c, sm_n, sm_m, s_lh, s_rs = states
    prompt_kw = dict(t=T_PROMPT, n_seq=N_PROMPT_SEQ, row_block0=0, layer=layer, emit_state=True)
    sample_kw = dict(t=T_SAMPLE, n_seq=N_SAMPLE_SEQ, row_block0=ROWS_PROMPT // T_SAMPLE, layer=layer,
                     emit_state=False)
    pc, pn, pm, plh, prs = prev_states

    caug0 = jnp.concatenate(
        [sm_c, jnp.broadcast_to(sm_n[..., None], sm_n.shape + (LANES,))], axis=-1)
    m0 = jnp.broadcast_to(sm_m[..., None, None], (N_SAMPLE_SEQ, 2, M_HEADS, 1, LANES))
    (ym,) = _mlstm_call(proj, proj_t, grow, p['mlstm_norm_g'], (caug0, m0), [None], **sample_kw)
    ym, new_c, new_n, new_m = _mlstm_call(proj, proj_t, grow, p['mlstm_norm_g'], None,
                                          [ym, pc, pn, pm], **prompt_kw)

    lru_p = dict(conv_w=p['lru_conv_w'], conv_b=p['lru_conv_b'], wr=p['lru_wr'], wi=p['lru_wi'],
                 br=p['lru_br'], bi=p['lru_bi'], lam=p['lru_lambda'])
    (yl,) = _lru_call(proj, lru_p, s_lh, [None], **sample_kw)
    yl, new_lh = _lru_call(proj, lru_p, None, [yl, plh], **prompt_kw)

    decay = jnp.broadcast_to(p['ret_decay'].T[:, :, None], (R_HEADS, 2, LANES))
    (yr,) = _ret_call(proj, proj_t, decay, p['ret_norm_g'], rope_tabs, s_rs, [None], **sample_kw)
    yr, new_rs = _ret_call(proj, proj_t, decay, p['ret_norm_g'], None, None, [yr, prs], **prompt_kw)

    rw = jnp.pad(p['router_w'], ((0, 0), (0, LANES - N_EXPERTS)))
    rwh = rw.astype(BF16)
    rwl = (rw - rwh.astype(F32)).astype(BF16)
    xn, h2, logits = _merge_call(h, ym, yl, yr, x, mod, p['w_merge'].astype(BF16), p['b_merge'],
                                 p['w_branch'].astype(BF16), p['w_out'].astype(BF16), p['norm2_g'],
                                 rwh, rwl)

    xs_p, gv_p, rank_p = _prompt_route_call(logits, h2)
    rows_s, gv_s = _route_sample(logits)
    xs_s = h2.at[rows_s].get(mode='promise_in_bounds')
    gv = jnp.concatenate([gv_p, gv_s[..., None]], axis=1)
    y_p, y_s = _expert_call(xs_p, xs_s, gv, mod, stacked['exp_w1'], stacked['exp_w3'],
                            stacked['exp_w2'], layer)
    x_out = xn.at[rows_s.reshape(-1)].add(y_s.reshape(-1, D), mode='promise_in_bounds')
    x_out = _prompt_combine_call(x_out, y_p, rank_p)
    return x_out, (new_c, new_n, new_m, new_lh, new_rs)


def kernel(x_prompt, x_sample, c, state_mlstm_C, state_mlstm_n, state_mlstm_m, state_lru_h, state_ret_S, c_ctx, w_ada, b_ada, norm1_g, norm2_g, w_in, mlstm_gate_bias, mlstm_norm_g, lru_conv_w, lru_conv_b, lru_wr, lru_br, lru_wi, lru_bi, lru_lambda, ret_decay, ret_norm_g, w_branch, w_merge, b_merge, w_out, router_w, exp_w1, exp_w3, exp_w2, final_g):
    x = jnp.concatenate([x_prompt.reshape(ROWS_PROMPT, D), x_sample.reshape(-1, D)], axis=0)
    cond8 = jnp.concatenate([c_ctx[None, :], c, jnp.zeros((8 - 1 - N_SAMPLE_SEQ, D), F32)], axis=0)
    mod_all = _ada_call(cond8, w_ada, b_ada).reshape(DEPTH, 8, 6, D)
    rope_tabs = _rope_tables()
    stacked = dict(exp_w1=exp_w1, exp_w3=exp_w3, exp_w2=exp_w2)

    new_states = (None,) * 5
    for l in range(DEPTH):
        p = dict(norm1_g=norm1_g[l], norm2_g=norm2_g[l], w_in=w_in[l],
                 mlstm_gate_bias=mlstm_gate_bias[l], mlstm_norm_g=mlstm_norm_g[l],
                 lru_conv_w=lru_conv_w[l], lru_conv_b=lru_conv_b[l], lru_wr=lru_wr[l],
                 lru_br=lru_br[l], lru_wi=lru_wi[l], lru_bi=lru_bi[l], lru_lambda=lru_lambda[l],
                 ret_decay=ret_decay[l], ret_norm_g=ret_norm_g[l], w_branch=w_branch[l],
                 w_merge=w_merge[l], b_merge=b_merge[l], w_out=w_out[l], router_w=router_w[l])
        states = (state_mlstm_C[:, l], state_mlstm_n[:, l], state_mlstm_m[:, l],
                  state_lru_h[:, l], state_ret_S[:, l])
        x, new_states = _layer(x, mod_all[l], p, states, rope_tabs, stacked, l, new_states)

    y_prompt = _final_norm_call(x, final_g, 0, ROWS_PROMPT).reshape(N_PROMPT_SEQ, T_PROMPT, D)
    y_sample = _final_norm_call(x, final_g, ROWS_PROMPT, ROWS - ROWS_PROMPT).reshape(
        N_SAMPLE_SEQ, T_SAMPLE, D)
    new_c, new_n, new_m, new_lh, new_rs = new_states
    return (y_prompt, y_sample, new_c, new_n[:, :, :, :, 0, :], new_m[:, :, :, :, 0, 0], new_lh, new_rs)
```

```python
import functools

import jax
import jax.numpy as jnp
import numpy as np
from jax import lax
from jax.experimental import pallas as pl
from jax.experimental.pallas import tpu as pltpu

F32 = jnp.float32
BF16 = jnp.bfloat16

D = 1024
DEPTH = 2
N_PROMPT_SEQ = 16
T_PROMPT = 256
N_SAMPLE_SEQ = 2
T_SAMPLE = 4096
ROWS_PROMPT = N_PROMPT_SEQ * T_PROMPT
ROWS = ROWS_PROMPT + N_SAMPLE_SEQ * T_SAMPLE
GROUP_ROWS = 4096
GRID_W = 64
CHUNK = 256
EPS = 1e-6
F32_TINY = 1.1754944e-38
M_HEADS = 4
M_DH = 256
R_HEADS = 8
R_DH = 128
L_BLOCKS = 8
L_BW = 128
L_C = 8.0
ROPE_BASE = 10000.0
N_EXPERTS = 16
CAP_PROMPT = 2 * T_PROMPT // N_EXPERTS
CAP_SAMPLE = 2 * T_SAMPLE // N_EXPERTS
ROWS_PER_EXPERT = N_PROMPT_SEQ * CAP_PROMPT + N_SAMPLE_SEQ * CAP_SAMPLE
N_GATE_COLS = 16
LANES = 128
SUBLANES = 8
VMEM_LIMIT = 56 * 2 ** 20

COL_MQ, COL_MV, COL_MO, COL_LX, COL_LZ, COL_RQ, COL_RV, COL_RG = (i * D for i in range(8))
D_MAIN = 8 * D
ROW_MK, ROW_RK = 0, D
D_KT = 2 * D


def _cparams(sem, vmem=None):
    return pltpu.CompilerParams(dimension_semantics=sem, vmem_limit_bytes=vmem)


def _dot(a, b):
    return jnp.dot(a, b, preferred_element_type=F32)


def _dot_nt(a, b):
    return lax.dot_general(a, b, (((1,), (1,)), ((), ())), preferred_element_type=F32)


def _split3(x):
    a = x.astype(BF16)
    r = x - a.astype(F32)
    b = r.astype(BF16)
    c = (r - b.astype(F32)).astype(BF16)
    return a, b, c


def _log_sigmoid(x):
    return jnp.minimum(x, 0.0) - jnp.log1p(jnp.exp(-jnp.abs(x)))


def _sigmoid(x):
    return 0.5 * jnp.tanh(0.5 * x) + 0.5


def _rms(x):
    return x * lax.rsqrt(jnp.mean(x * x, axis=-1, keepdims=True) + EPS)


def _chunk(c):
    return pl.ds(pl.multiple_of(c * CHUNK, CHUNK), CHUNK)


def _alias_prev(args, in_specs, prev):
    aliases = {}
    for out_idx, arr in enumerate(prev):
        if arr is not None:
            aliases[len(args)] = out_idx
            args.append(arr)
            in_specs.append(pl.BlockSpec(memory_space=pl.ANY))
    return aliases, len(aliases)


def _ada_kernel(c_ref, w_ref, b_ref, o_ref):
    c = c_ref[...]
    s = (c * jax.nn.sigmoid(c)).astype(BF16)
    o_ref[0] = _dot(s, w_ref[0].astype(BF16)) + b_ref[0]


def _ada_call(cond8, w_ada, b_ada):
    tn = 1536
    return pl.pallas_call(
        _ada_kernel,
        grid=(DEPTH, 6 * D // tn),
        in_specs=[pl.BlockSpec((8, D), lambda l, j: (0, 0)),
                  pl.BlockSpec((1, D, tn), lambda l, j: (l, 0, j)),
                  pl.BlockSpec((1, 1, tn), lambda l, j: (l, 0, j))],
        out_specs=pl.BlockSpec((1, 8, tn), lambda l, j: (l, 0, j)),
        out_shape=jax.ShapeDtypeStruct((DEPTH, 8, 6 * D), F32),
        compiler_params=_cparams(("arbitrary", "arbitrary"), VMEM_LIMIT),
        name="ada",
    )(cond8, w_ada, b_ada.reshape(DEPTH, 1, 6 * D))


def _norm1_kernel(x_ref, g_ref, mod_ref, wg_ref, h_ref, gate_ref):
    y = _rms(x_ref[...]) * g_ref[...]
    h = (y * (1.0 + mod_ref[0, 1:2, :]) + mod_ref[0, 0:1, :]).astype(BF16)
    h_ref[...] = h
    gate_ref[...] = _dot(h, wg_ref[...])


def _norm1_call(x, g, mod, w_gate):
    tm = 512
    return pl.pallas_call(
        _norm1_kernel,
        grid=(ROWS // tm,),
        in_specs=[pl.BlockSpec((tm, D), lambda i: (i, 0)),
                  pl.BlockSpec((1, D), lambda i: (0, 0)),
                  pl.BlockSpec((1, 6, D), lambda i: (i * tm // GROUP_ROWS, 0, 0)),
                  pl.BlockSpec((D, LANES), lambda i: (0, 0))],
        out_specs=[pl.BlockSpec((tm, D), lambda i: (i, 0)),
                   pl.BlockSpec((tm, LANES), lambda i: (i, 0))],
        out_shape=[jax.ShapeDtypeStruct((ROWS, D), BF16),
                   jax.ShapeDtypeStruct((ROWS, LANES), F32)],
        compiler_params=_cparams(("arbitrary",), VMEM_LIMIT),
        name="norm1",
    )(x, g.reshape(1, D), mod, w_gate)


def _mm_kernel(x_ref, wt_ref, o_ref):
    o_ref[...] = _dot_nt(x_ref[...], wt_ref[...]).astype(o_ref.dtype)


def _inproj_call(h, w_main_t):
    tm, tn = 1024, 1024
    return pl.pallas_call(
        _mm_kernel,
        grid=(D_MAIN // tn, ROWS // tm),
        in_specs=[pl.BlockSpec((tm, D), lambda j, i: (i, 0)),
                  pl.BlockSpec((tn, D), lambda j, i: (j, 0))],
        out_specs=pl.BlockSpec((tm, tn), lambda j, i: (i, j)),
        out_shape=jax.ShapeDtypeStruct((ROWS, D_MAIN), BF16),
        compiler_params=_cparams(("arbitrary", "arbitrary"), VMEM_LIMIT),
        name="inproj",
    )(h, w_main_t)


def _mm_t_kernel(wt_ref, x_ref, o_ref):
    o_ref[...] = _dot_nt(wt_ref[...], x_ref[...]).astype(o_ref.dtype)


def _inproj_t_call(h, w_kt):
    tm, tn = 1024, 1024
    return pl.pallas_call(
        _mm_t_kernel,
        grid=(D_KT // tn, ROWS // tm),
        in_specs=[pl.BlockSpec((tn, D), lambda j, i: (j, 0)),
                  pl.BlockSpec((tm, D), lambda j, i: (i, 0))],
        out_specs=pl.BlockSpec((tn, tm), lambda j, i: (j, i)),
        out_shape=jax.ShapeDtypeStruct((D_KT, ROWS), BF16),
        compiler_params=_cparams(("arbitrary", "arbitrary"), VMEM_LIMIT),
        name="inproj_t",
    )(w_kt, h)


M_AUG = M_DH + LANES
M_GROUP_SHORT = 2


def _mlstm_kernel(*refs, n_chunks, has_init, emit_state, n_prev):
    q_ref, kt_ref, v_ref, o_ref, g_ref, ng_ref = refs[:6]
    pos = 6
    if has_init:
        c0_ref, m0_ref = refs[pos:pos + 2]
        pos += 2
    pos += n_prev
    y_ref = refs[pos]
    pos += 1
    if emit_state:
        co_ref, no_ref, mo_ref = refs[pos:pos + 3]
        pos += 3
    caug_sc, m_sc, hf_sc, hb_sc, gp_mp, gp_bc, gp_row = refs[pos:]
    n_group = caug_sc.shape[0]

    zero_state = (not has_init) and n_chunks == 1
    if has_init:
        for g in range(n_group):
            caug_sc[g] = c0_ref[0, :, g]
            m_sc[g] = m0_ref[0, :, g]
    else:
        if not zero_state:
            caug_sc[...] = jnp.zeros_like(caug_sc)
        m_sc[...] = jnp.zeros_like(m_sc)

    ri = lax.broadcasted_iota(jnp.int32, (CHUNK, CHUNK), 0)
    ci = lax.broadcasted_iota(jnp.int32, (CHUNK, CHUNK), 1)
    r8 = lax.broadcasted_iota(jnp.int32, (SUBLANES, CHUNK), 0)
    ones_b = jnp.ones((CHUNK, LANES), BF16)
    allowed = [ri >= ci, ci >= ri]
    tri = [jnp.where(ri <= ci, 1.0, 0.0).astype(BF16), jnp.where(ri >= ci, 1.0, 0.0).astype(BF16)]

    def gate_pass(c, carry):
        rows = _chunk(c)
        for g in range(n_group):
            g8 = g_ref[g, :, rows]
            for d in range(2):
                i_idx, f_idx = (0, 1) if d == 0 else (2, 3)
                ls = jnp.where(r8 == f_idx, _log_sigmoid(g8), g8)
                cum = sum(_dot(p, tri[d]) for p in _split3(ls))
                b_row = cum[f_idx:f_idx + 1]
                g_row = ls[i_idx:i_idx + 1] - b_row
                gmat = jnp.where(allowed[d], jnp.broadcast_to(g_row, (CHUNK, CHUNK)), -jnp.inf)
                mp_col = jnp.max(gmat, axis=-1, keepdims=True)
                gp_bc[g, d, c] = jnp.broadcast_to(b_row, (LANES, CHUNK)).T
                gp_mp[g, d, c] = jnp.broadcast_to(mp_col, (CHUNK, LANES))
                b_last_11 = b_row[:, CHUNK - 1:CHUNK] if d == 0 else b_row[:, 0:1]
                log_k = b_last_11 + g_row
                a_11 = jnp.max(log_k, axis=-1, keepdims=True)
                gp_row[g, d, c] = jnp.where(
                    r8 == 0, g_row, jnp.where(
                        r8 == 1, jnp.exp(log_k - a_11), jnp.where(r8 == 2, b_last_11, a_11)))
        return carry

    lax.fori_loop(0, n_chunks, gate_pass, 0, unroll=min(n_chunks, 8))

    def step(g, d, c, m_t):
        rows = _chunk(c)
        hcols = slice(g * M_DH, (g + 1) * M_DH)
        mp = gp_mp[g, d, c]
        bc = gp_bc[g, d, c]
        row_t = gp_row[g, d, c]
        gmat = jnp.where(allowed[d], jnp.broadcast_to(row_t[0:1], (CHUNK, CHUNK)), -jnp.inf)
        e1 = jnp.exp(gmat - jnp.concatenate([mp] * (CHUNK // LANES), axis=1))

        q = q_ref[rows, hcols]
        kt = kt_ref[hcols, rows]
        vaug = jnp.concatenate([v_ref[rows, hcols], ones_b], axis=1)
        s1 = _dot(q, kt) * e1
        sv1 = _dot(s1.astype(BF16), vaug)
        kw1 = (kt.astype(F32) * row_t[1:2]).astype(BF16)
        u1 = _dot(kw1, vaug)
        b_last = jnp.broadcast_to(row_t[2:3, 0:LANES], (SUBLANES, LANES))
        a_t = jnp.broadcast_to(row_t[3:4, 0:LANES], (SUBLANES, LANES))

        m_b = jnp.broadcast_to(m_t[0:1, :], (CHUNK, LANES))
        mi = jnp.maximum(m_b, mp)
        r = jnp.exp(mp - mi)
        w_prev = jnp.exp(m_b - mi)
        floor = jnp.exp(-(bc + mi))
        if zero_state:
            qc = None
            den = r * sv1[:, M_DH:]
        else:
            qc = _dot(q, caug_sc[g, d].astype(BF16))
            den = r * sv1[:, M_DH:] + w_prev * qc[:, M_DH:]
        inv = 1.0 / jnp.maximum(jnp.abs(den), floor)
        h_sc = hf_sc if d == 0 else hb_sc
        for blk in range(M_DH // LANES):
            cols = slice(blk * LANES, (blk + 1) * LANES)
            num = r * sv1[:, cols]
            if not zero_state:
                num = num + w_prev * qc[:, cols]
            h_sc[rows, slice(g * M_DH + blk * LANES, g * M_DH + (blk + 1) * LANES)] = num * inv

        m_new = jnp.maximum(b_last + m_t, a_t)
        w_c = jnp.exp(b_last + m_t - m_new)[0:1, :]
        w_u = jnp.exp(a_t - m_new)[0:1, :]
        for blk in range(M_AUG // LANES):
            cols = slice(blk * LANES, (blk + 1) * LANES)
            if zero_state:
                caug_sc[g, d, :, cols] = w_u * u1[:, cols]
            else:
                caug_sc[g, d, :, cols] = w_c * caug_sc[g, d, :, cols] + w_u * u1[:, cols]
        return m_new

    def scan_body(c, carry):
        new = []
        for g in range(n_group):
            new.append(step(g, 0, c, carry[2 * g]))
            new.append(step(g, 1, n_chunks - 1 - c, carry[2 * g + 1]))
        return tuple(new)

    m_init = tuple(jnp.broadcast_to(m_sc[g, d], (SUBLANES, LANES))
                   for g in range(n_group) for d in range(2))
    m_fin = lax.fori_loop(0, n_chunks, scan_body, m_init)
    for g in range(n_group):
        for d in range(2):
            m_sc[g, d] = m_fin[2 * g + d][0:1, :]

    def out_body(c, carry):
        rows = _chunk(c)
        for g in range(n_group):
            hcols = slice(g * M_DH, (g + 1) * M_DH)
            y = _rms(hf_sc[rows, hcols] + hb_sc[rows, hcols]) * ng_ref[:, hcols]
            y_ref[rows, hcols] = (y * _sigmoid(o_ref[rows, hcols].astype(F32))).astype(y_ref.dtype)
        return carry

    lax.fori_loop(0, n_chunks, out_body, 0)

    if emit_state:
        for g in range(n_group):
            co_ref[0, 0, :, g] = caug_sc[g, :, :, 0:M_DH]
            for d in range(2):
                no_ref[0, 0, d, g] = caug_sc[g, d, :, M_DH:].T[0:1, :]
            mo_ref[0, 0, :, g] = m_sc[g]


def _mlstm_call(proj, proj_t, grow, norm_g, state, prev, *, t, n_seq, row_block0, layer, emit_state):
    has_init = state is not None
    group = M_GROUP_SHORT if t <= CHUNK else 1
    gw = group * M_DH
    big = dict(pipeline_mode=pl.Buffered(1)) if t * gw * 2 > 2 ** 21 else {}

    def col(base):
        return lambda s, h: (row_block0 + s, base // gw + h)

    st_caug = pl.BlockSpec((1, 2, group, M_DH, M_AUG), lambda s, h: (s, 0, h, 0, 0))
    st_m = pl.BlockSpec((1, 2, group, 1, LANES), lambda s, h: (s, 0, h, 0, 0))
    so_c = pl.BlockSpec((1, 1, 2, group, M_DH, M_DH), lambda s, h: (s, layer, 0, h, 0, 0))
    so_n = pl.BlockSpec((1, 1, 2, group, 1, M_DH), lambda s, h: (s, layer, 0, h, 0, 0))
    so_m = pl.BlockSpec((1, 1, 2, group, 1, LANES), lambda s, h: (s, layer, 0, h, 0, 0))
    in_specs = [pl.BlockSpec((t, gw), col(COL_MQ), **big),
                pl.BlockSpec((gw, t), lambda s, h: (ROW_MK // gw + h, row_block0 + s), **big),
                pl.BlockSpec((t, gw), col(COL_MV), **big),
                pl.BlockSpec((t, gw), col(COL_MO), **big),
                pl.BlockSpec((group, SUBLANES, t), lambda s, h: (h, 0, row_block0 + s)),
                pl.BlockSpec((1, gw), lambda s, h: (0, h))]
    args = [proj, proj_t, proj, proj, grow, norm_g.reshape(1, D)]
    if has_init:
        in_specs += [st_caug, st_m]
        args += list(state)
    out_specs = [pl.BlockSpec((t, gw), lambda s, h: (row_block0 + s, h))]
    out_shape = [jax.ShapeDtypeStruct((ROWS, D), BF16)]
    if emit_state:
        out_specs += [so_c, so_n, so_m]
        out_shape += [jax.ShapeDtypeStruct((n_seq, DEPTH, 2, M_HEADS, M_DH, M_DH), F32),
                      jax.ShapeDtypeStruct((n_seq, DEPTH, 2, M_HEADS, 1, M_DH), F32),
                      jax.ShapeDtypeStruct((n_seq, DEPTH, 2, M_HEADS, 1, LANES), F32)]
    aliases, n_prev = _alias_prev(args, in_specs, prev)
    return pl.pallas_call(
        functools.partial(_mlstm_kernel, n_chunks=t // CHUNK, has_init=has_init,
                          emit_state=emit_state, n_prev=n_prev),
        grid=(n_seq, M_HEADS // group),
        in_specs=in_specs,
        out_specs=out_specs,
        out_shape=out_shape,
        input_output_aliases=aliases,
        scratch_shapes=[pltpu.VMEM((group, 2, M_DH, M_AUG), F32),
                        pltpu.VMEM((group, 2, 1, LANES), F32),
                        pltpu.VMEM((t, gw), F32),
                        pltpu.VMEM((t, gw), F32),
                        pltpu.VMEM((group, 2, t // CHUNK, CHUNK, LANES), F32),
                        pltpu.VMEM((group, 2, t // CHUNK, CHUNK, LANES), F32),
                        pltpu.VMEM((group, 2, t // CHUNK, SUBLANES, CHUNK), F32)],
        compiler_params=_cparams(("arbitrary", "arbitrary"), VMEM_LIMIT),
        name="mlstm",
    )(*args)


R_GROUP_SHORT, R_GROUP_LONG = 4, 2

def _ret_kernel(*refs, n_chunks, rope, has_init, emit_state, n_prev):
    q_ref, kt_ref, v_ref, g_ref, dec_ref, ng_ref = refs[:6]
    pos = 6
    if rope:
        cos_ref, sin_ref, cos_t_ref, sin_t_ref = refs[pos:pos + 4]
        pos += 4
    if has_init:
        s0_ref = refs[pos]
        pos += 1
    pos += n_prev
    y_ref = refs[pos]
    pos += 1
    if emit_state:
        so_ref = refs[pos]
        pos += 1
    s_sc, of_sc, ob_sc, ktr_sc = refs[pos:pos + 4]
    qr_sc = refs[pos + 4] if rope else None
    n_group = s_sc.shape[0]

    zero_state = (not has_init) and n_chunks == 1
    if has_init:
        for g in range(n_group):
            s_sc[g] = s0_ref[0, :, g]
    elif not zero_state:
        s_sc[...] = jnp.zeros_like(s_sc)

    ri = lax.broadcasted_iota(jnp.int32, (CHUNK, CHUNK), 0)
    ci = lax.broadcasted_iota(jnp.int32, (CHUNK, CHUNK), 1)
    quarter = R_DH // 4
    if rope:
        fr = lax.broadcasted_iota(jnp.int32, (R_DH, R_DH), 0)
        fc = lax.broadcasted_iota(jnp.int32, (R_DH, R_DH), 1)
        partner = fc + jnp.where((fc % (2 * quarter)) < quarter, quarter, -quarter)
        perm_b = jnp.where(fr == partner, 1.0, 0.0).astype(BF16)

    def prep_body(c, carry):
        rows = _chunk(c)
        for g in range(n_group):
            hcols = slice(g * R_DH, (g + 1) * R_DH)
            ktf = kt_ref[hcols, rows].astype(F32) * (R_DH ** -0.5)
            if rope:
                swapped = jnp.concatenate([ktf[quarter:2 * quarter], ktf[0:quarter],
                                           ktf[3 * quarter:], ktf[2 * quarter:3 * quarter]], axis=0)
                ktf = ktf * cos_t_ref[:, rows] + swapped * sin_t_ref[:, rows]
                q = q_ref[rows, hcols]
                qr = q.astype(F32) * cos_ref[rows, :] + _dot(q, perm_b) * sin_ref[rows, :]
                qr_sc[rows, hcols] = qr.astype(BF16)
            ktr_sc[hcols, rows] = ktf.astype(BF16)
        return carry

    lax.fori_loop(0, n_chunks, prep_body, 0, unroll=min(n_chunks, 2))
    qsrc = qr_sc if rope else q_ref

    lane = lax.broadcasted_iota(jnp.int32, (1, CHUNK), 1).astype(F32)
    row_f = lax.broadcasted_iota(jnp.int32, (CHUNK, R_DH), 0).astype(F32)
    consts = []
    for gd in range(2 * n_group):
        g, d = divmod(gd, 2)
        lg = -jnp.exp(dec_ref[g, d:d + 1, :])
        lg11 = lg[:, 0:1]
        diff = (ri - ci) if d == 0 else (ci - ri)
        causal = diff >= 0
        dmat = jnp.where(causal, jnp.exp(lg11 * jnp.where(causal, diff, 0).astype(F32)), 0.0)
        if d == 0:
            q_dec = jnp.exp(lg * (row_f + 1.0))
            k_dec = jnp.exp(lg11 * (CHUNK - 1.0 - lane))
        else:
            q_dec = jnp.exp(lg * (CHUNK - row_f))
            k_dec = jnp.exp(lg11 * lane)
        g_chunk = jnp.exp(lg * float(CHUNK))
        consts.append((dmat, q_dec, k_dec, g_chunk))

    def step(g, d, c):
        dmat, q_dec, k_dec, g_chunk = consts[2 * g + d]
        rows = _chunk(c)
        hcols = slice(g * R_DH, (g + 1) * R_DH)
        q = qsrc[rows, hcols]
        kt = ktr_sc[hcols, rows]
        v = v_ref[rows, hcols]
        s = _dot(q, kt) * dmat
        o = _dot(s.astype(BF16), v)
        if not zero_state:
            o = o + q_dec * _dot(q, s_sc[g, d].astype(BF16))
        if d == 0:
            of_sc[rows, hcols] = o
        else:
            ob_sc[rows, hcols] = o
        kd = (kt.astype(F32) * k_dec).astype(BF16)
        if zero_state:
            s_sc[g, d] = _dot(kd, v)
        else:
            s_sc[g, d] = g_chunk * s_sc[g, d] + _dot(kd, v)

    def scan_body(c, carry):
        for g in range(n_group):
            step(g, 0, c)
            step(g, 1, n_chunks - 1 - c)
        return carry

    lax.fori_loop(0, n_chunks, scan_body, 0, unroll=min(n_chunks, 2))

    def out_body(c, carry):
        rows = _chunk(c)
        for g in range(n_group):
            hcols = slice(g * R_DH, (g + 1) * R_DH)
            y = _rms(of_sc[rows, hcols] + ob_sc[rows, hcols]) * ng_ref[:, hcols]
            gate = g_ref[rows, hcols].astype(F32)
            y_ref[rows, hcols] = (y * (gate * _sigmoid(gate))).astype(y_ref.dtype)
        return carry

    lax.fori_loop(0, n_chunks, out_body, 0, unroll=min(n_chunks, 2))

    if emit_state:
        for g in range(n_group):
            so_ref[0, 0, :, g] = s_sc[g]


def _ret_call(proj, proj_t, decay, norm_g, rope_tabs, state, prev, *, t, n_seq, row_block0, layer, emit_state):
    has_init = state is not None
    rope = rope_tabs is not None

    group = R_GROUP_SHORT if t <= CHUNK else R_GROUP_LONG
    gw = group * R_DH

    def col(base):
        return lambda s, h: (row_block0 + s, base // gw + h)

    st_s = pl.BlockSpec((1, 2, group, R_DH, R_DH), lambda s, h: (s, 0, h, 0, 0))
    in_specs = [pl.BlockSpec((t, gw), col(COL_RQ)),
                pl.BlockSpec((gw, t), lambda s, h: (ROW_RK // gw + h, row_block0 + s)),
                pl.BlockSpec((t, gw), col(COL_RV)),
                pl.BlockSpec((t, gw), col(COL_RG)),
                pl.BlockSpec((group, 2, LANES), lambda s, h: (h, 0, 0)),
                pl.BlockSpec((1, gw), lambda s, h: (0, h))]
    args = [proj, proj_t, proj, proj, decay, norm_g.reshape(1, D)]
    if rope:
        in_specs += [pl.BlockSpec((t, R_DH), lambda s, h: (0, 0))] * 2
        in_specs += [pl.BlockSpec((R_DH, t), lambda s, h: (0, 0))] * 2
        args += list(rope_tabs)
    if has_init:
        in_specs.append(st_s)
        args.append(state)
    out_specs = [pl.BlockSpec((t, gw), lambda s, h: (row_block0 + s, h))]
    out_shape = [jax.ShapeDtypeStruct((ROWS, D), BF16)]
    if emit_state:
        out_specs.append(pl.BlockSpec((1, 1, 2, group, R_DH, R_DH), lambda s, h: (s, layer, 0, h, 0, 0)))
        out_shape.append(jax.ShapeDtypeStruct((n_seq, DEPTH, 2, R_HEADS, R_DH, R_DH), F32))
    aliases, n_prev = _alias_prev(args, in_specs, prev)
    scratch = [pltpu.VMEM((group, 2, R_DH, R_DH), F32),
               pltpu.VMEM((t, gw), F32),
               pltpu.VMEM((t, gw), F32),
               pltpu.VMEM((gw, t), BF16)]
    if rope:
        scratch.append(pltpu.VMEM((t, gw), BF16))
    return pl.pallas_call(
        functools.partial(_ret_kernel, n_chunks=t // CHUNK, rope=rope, has_init=has_init,
                          emit_state=emit_state, n_prev=n_prev),
        grid=(n_seq, R_HEADS // group),
        in_specs=in_specs,
        out_specs=out_specs,
        out_shape=out_shape,
        input_output_aliases=aliases,
        scratch_shapes=scratch,
        compiler_params=_cparams(("arbitrary", "arbitrary"), VMEM_LIMIT),
        name="retention",
    )(*args)


LRU_SLAB = 64
L_GROUP_SHORT = 2


def _tile_scan(a, b, reverse):
    row = lax.broadcasted_iota(jnp.int32, (SUBLANES, L_BW), 0)
    for k in (1, 2, 4):
        if reverse:
            keep = row < SUBLANES - k
            shift = SUBLANES - k
        else:
            keep = row >= k
            shift = k
        a_sh = jnp.where(keep, pltpu.roll(a, shift, 0), 1.0)
        b_sh = jnp.where(keep, pltpu.roll(b, shift, 0), 0.0)
        b = a * b_sh + b
        a = a * a_sh
    return a, b


def _lru_kernel(*refs, t, has_init, emit_state, n_prev):
    lx_ref, lz_ref, cw_ref, cb_ref, wr_ref, wi_ref, br_ref, bi_ref, lam_ref = refs[:9]
    pos = 9
    if has_init:
        h0_ref = refs[pos]
        pos += 1
    pos += n_prev
    y_ref = refs[pos]
    pos += 1
    if emit_state:
        hfin_ref = refs[pos]
        pos += 1
    xpad_sc, a_sc, b_sc = refs[pos:]
    rc = min(t, 256)
    gw = xpad_sc.shape[1]
    n_group = gw // L_BW

    def cs(g):
        return slice(g * L_BW, (g + 1) * L_BW)

    xpad_sc[0:8, :] = jnp.zeros((8, gw), F32)
    xpad_sc[t + 8:t + 16, :] = jnp.zeros((8, gw), F32)

    def pad_body(c, carry):
        r0 = pl.multiple_of(c * rc, rc)
        xpad_sc[pl.ds(r0 + 8, rc), :] = lx_ref[pl.ds(r0, rc), :].astype(F32)
        return carry

    lax.fori_loop(0, t // rc, pad_body, 0)

    wr = [[(0.5 * wr_ref[d, g]).astype(BF16) for d in range(2)] for g in range(n_group)]
    wi = [[(0.5 * wi_ref[d, g]).astype(BF16) for d in range(2)] for g in range(n_group)]
    half_br = 0.5 * br_ref[...]
    half_bi = 0.5 * bi_ref[...]
    lam = lam_ref[...]
    half_c = -0.5 * L_C * (jnp.maximum(-lam, 0.0) + jnp.log1p(jnp.exp(-jnp.abs(lam))))

    def gate_body(c, carry):
        r0 = pl.multiple_of(c * rc, rc)
        n = rc + 16
        for g in range(n_group):
            xe = xpad_sc[pl.ds(r0, n), cs(g)]
            u = (cb_ref[:, cs(g)] + cw_ref[0:1, cs(g)] * pltpu.roll(xe, 2, 0)[8:8 + rc]
                 + cw_ref[1:2, cs(g)] * pltpu.roll(xe, 1, 0)[8:8 + rc]
                 + cw_ref[2:3, cs(g)] * xe[8:8 + rc]
                 + cw_ref[3:4, cs(g)] * pltpu.roll(xe, n - 1, 0)[8:8 + rc])
            ub = u.astype(BF16)
            half_u = 0.5 * u
            for d in range(2):
                hc = half_c[d:d + 1, cs(g)]
                tr = jnp.tanh(_dot(ub, wr[g][d]) + half_br[d:d + 1, cs(g)])
                ti = jnp.tanh(_dot(ub, wi[g][d]) + half_bi[d:d + 1, cs(g)])
                log_a = hc * tr + hc
                a = jnp.exp(log_a)
                z = jnp.tanh(log_a) * (-1.0 - a * a)
                mult = z * lax.rsqrt(jnp.maximum(z, F32_TINY))
                a_sc[d, pl.ds(r0, rc), cs(g)] = a
                b_sc[d, pl.ds(r0, rc), cs(g)] = mult * (ti * half_u + half_u)
        return carry

    lax.fori_loop(0, t // rc, gate_body, 0, unroll=min(t // rc, 2))

    n_slabs = t // LRU_SLAB
    tiles = LRU_SLAB // SUBLANES

    row8 = lax.broadcasted_iota(jnp.int32, (SUBLANES, L_BW), 0)

    def bcast_row(x, r):
        return jnp.broadcast_to(x[r:r + 1, :], (SUBLANES, L_BW))

    def slab_scan(g, d, r0, c_in):
        reverse = d == 1
        edge = 0 if reverse else SUBLANES - 1
        local = []
        spa = jnp.ones((SUBLANES, L_BW), F32)
        shl = jnp.zeros((SUBLANES, L_BW), F32)
        for k in range(tiles):
            rows = pl.ds(r0 + k * SUBLANES, SUBLANES)
            pa, hl = _tile_scan(a_sc[d, rows, cs(g)], b_sc[d, rows, cs(g)], reverse)
            local.append((rows, pa, hl))
            spa = jnp.where(row8 == k, bcast_row(pa, edge), spa)
            shl = jnp.where(row8 == k, bcast_row(hl, edge), shl)
        cpa, chl = _tile_scan(spa, shl, reverse)
        after = chl + cpa * c_in
        for k, (rows, pa, hl) in enumerate(local):
            prev = k + 1 if reverse else k - 1
            cin = c_in if (prev < 0 or prev >= tiles) else bcast_row(after, prev)
            b_sc[d, rows, cs(g)] = hl + pa * cin
        return bcast_row(after, 0 if reverse else tiles - 1)

    def scan_body(c, carry):
        new = []
        for g in range(n_group):
            new.append(slab_scan(g, 0, pl.multiple_of(c * LRU_SLAB, LRU_SLAB), carry[2 * g]))
            new.append(slab_scan(g, 1, pl.multiple_of((n_slabs - 1 - c) * LRU_SLAB, LRU_SLAB),
                                 carry[2 * g + 1]))
        return tuple(new)

    if has_init:
        c0 = tuple(jnp.broadcast_to(h0_ref[0, d:d + 1, cs(g)], (SUBLANES, L_BW))
                   for g in range(n_group) for d in range(2))
    else:
        c0 = tuple(jnp.zeros((SUBLANES, L_BW), F32) for _ in range(2 * n_group))
    c_fin = lax.fori_loop(0, n_slabs, scan_body, c0, unroll=2)
    if emit_state:
        for g in range(n_group):
            for d in range(2):
                hfin_ref[0, 0, d:d + 1, cs(g)] = c_fin[2 * g + d][0:1, :]

    def out_body(c, carry):
        rows = pl.ds(pl.multiple_of(c * rc, rc), rc)
        z = lz_ref[rows, :].astype(F32)
        gelu = 0.5 * z * (1.0 + jnp.tanh(0.7978845608028654 * (z + 0.044715 * (z * z * z))))
        y_ref[rows, :] = ((b_sc[0, rows, :] + b_sc[1, rows, :]) * gelu).astype(y_ref.dtype)
        return carry

    lax.fori_loop(0, t // rc, out_body, 0)


def _lru_call(proj, p, state, prev, *, t, n_seq, row_block0, layer, emit_state):
    has_init = state is not None

    group = L_GROUP_SHORT if t <= 256 else 1
    gw = group * L_BW

    def col(base):
        return lambda s, b: (row_block0 + s, base // gw + b)

    vec2 = pl.BlockSpec((2, gw), lambda s, b: (0, b))
    st = pl.BlockSpec((1, 2, gw), lambda s, b: (s, 0, b))
    in_specs = [pl.BlockSpec((t, gw), col(COL_LX)),
                pl.BlockSpec((t, gw), col(COL_LZ)),
                pl.BlockSpec((4, gw), lambda s, b: (0, b)),
                pl.BlockSpec((1, gw), lambda s, b: (0, b)),
                pl.BlockSpec((2, group, L_BW, L_BW), lambda s, b: (0, b, 0, 0)),
                pl.BlockSpec((2, group, L_BW, L_BW), lambda s, b: (0, b, 0, 0)),
                vec2, vec2, vec2]
    args = [proj, proj, p['conv_w'], p['conv_b'].reshape(1, D), p['wr'], p['wi'],
            p['br'], p['bi'], p['lam']]
    if has_init:
        in_specs.append(st)
        args.append(state)
    out_specs = [pl.BlockSpec((t, gw), lambda s, b: (row_block0 + s, b))]
    out_shape = [jax.ShapeDtypeStruct((ROWS, D), BF16)]
    if emit_state:
        out_specs.append(pl.BlockSpec((1, 1, 2, gw), lambda s, b: (s, layer, 0, b)))
        out_shape.append(jax.ShapeDtypeStruct((n_seq, DEPTH, 2, D), F32))
    aliases, n_prev = _alias_prev(args, in_specs, prev)
    return pl.pallas_call(
        functools.partial(_lru_kernel, t=t, has_init=has_init, emit_state=emit_state, n_prev=n_prev),
        grid=(n_seq, L_BLOCKS // group),
        in_specs=in_specs,
        out_specs=out_specs,
        out_shape=out_shape,
        input_output_aliases=aliases,
        scratch_shapes=[pltpu.VMEM((t + 16, gw), F32),
                        pltpu.VMEM((2, t, gw), F32),
                        pltpu.VMEM((2, t, gw), F32)],
        compiler_params=_cparams(("arbitrary", "arbitrary"), VMEM_LIMIT),
        name="rglru",
    )(*args)


def _merge_kernel(h_ref, ym_ref, yl_ref, yr_ref, x_ref, mod_ref, wm_ref, bm_ref, wb_ref, wo_ref,
                  n2_ref, rwh_ref, rwl_ref, xo_ref, h2_ref, lg_ref):
    h = h_ref[...]
    merged = None
    for k, y_ref in enumerate((ym_ref, yl_ref, yr_ref)):
        gate = jax.nn.sigmoid(_dot(h, wm_ref[:, k * D:(k + 1) * D]) + bm_ref[:, k * D:(k + 1) * D])
        term = gate * _dot(y_ref[...], wb_ref[k])
        merged = term if merged is None else merged + term
    mix = _dot(merged.astype(BF16), wo_ref[...])
    xn = x_ref[...] + mod_ref[0, 2:3, :] * mix
    xo_ref[...] = xn
    hn = _rms(xn) * n2_ref[...] * (1.0 + mod_ref[0, 4:5, :]) + mod_ref[0, 3:4, :]
    hi = hn.astype(BF16)
    h2_ref[...] = hi
    lo = (hn - hi.astype(F32)).astype(BF16)
    lg_ref[...] = _dot(hi, rwh_ref[...]) + _dot(lo, rwh_ref[...]) + _dot(hi, rwl_ref[...])


def _merge_call(h, ym, yl, yr, x, mod, wm, bm, wb, wo, n2g, rwh, rwl):
    tm = 256
    row = lambda i: (i, 0)
    const2 = lambda i: (0, 0)
    return pl.pallas_call(
        _merge_kernel,
        grid=(ROWS // tm,),
        in_specs=[pl.BlockSpec((tm, D), row),
                  pl.BlockSpec((tm, D), row),
                  pl.BlockSpec((tm, D), row),
                  pl.BlockSpec((tm, D), row),
                  pl.BlockSpec((tm, D), row),
                  pl.BlockSpec((1, 6, D), lambda i: (i * tm // GROUP_ROWS, 0, 0)),
                  pl.BlockSpec((D, 3 * D), const2),
                  pl.BlockSpec((1, 3 * D), const2),
                  pl.BlockSpec((3, D, D), lambda i: (0, 0, 0)),
                  pl.BlockSpec((D, D), const2),
                  pl.BlockSpec((1, D), const2),
                  pl.BlockSpec((D, LANES), const2),
                  pl.BlockSpec((D, LANES), const2)],
        out_specs=[pl.BlockSpec((tm, D), row),
                   pl.BlockSpec((tm, D), row),
                   pl.BlockSpec((tm, LANES), row)],
        out_shape=[jax.ShapeDtypeStruct((ROWS, D), F32),
                   jax.ShapeDtypeStruct((ROWS, D), BF16),
                   jax.ShapeDtypeStruct((ROWS, LANES), F32)],
        compiler_params=_cparams(("arbitrary",), VMEM_LIMIT),
        name="merge",
    )(h, ym, yl, yr, x, mod, wm, bm.reshape(1, 3 * D), wb, wo, n2g.reshape(1, D), rwh, rwl)


EXPERT_TM = 512
assert N_PROMPT_SEQ * CAP_PROMPT == EXPERT_TM and CAP_SAMPLE == EXPERT_TM


def _expert_kernel(xp_ref, xs_ref, g_ref, mod_ref, w1_ref, w3_ref, w2_ref, yp_ref, ys_ref,
                   w1_sc, w3_sc, w2_sc):
    m = pl.program_id(1)

    @pl.when(m == 0)
    def _():
        w1_sc[...] = w1_ref[0, 0].astype(BF16)
        w3_sc[...] = w3_ref[0, 0].astype(BF16)
        w2_sc[...] = w2_ref[0, 0].astype(BF16)

    xs = jnp.where(m == 0, xp_ref[0], xs_ref[0])
    a = _dot(xs, w1_sc[...])
    b = _dot(xs, w3_sc[...])
    mid = (a * jax.nn.sigmoid(a) * b).astype(BF16)

    def down(y_ref):
        y_ref[0] = (_dot(mid, w2_sc[...]) * g_ref[0]) * mod_ref[0, 5:6, :]

    pl.when(m == 0)(functools.partial(down, yp_ref))
    pl.when(m > 0)(functools.partial(down, ys_ref))


def _expert_call(xs_p, xs_s, gv, mod, w1, w3, w2, layer):
    tm = EXPERT_TM
    wspec = pl.BlockSpec((1, 1, D, D), lambda e, m: (layer, e, 0, 0))
    p_spec = pl.BlockSpec((1, tm, D), lambda e, m: (e, 0, 0))
    s_spec = pl.BlockSpec((1, tm, D), lambda e, m: (e, jnp.maximum(m - 1, 0), 0))
    return pl.pallas_call(
        _expert_kernel,
        grid=(N_EXPERTS, ROWS_PER_EXPERT // tm),
        in_specs=[p_spec, s_spec,
                  pl.BlockSpec((1, tm, 1), lambda e, m: (e, m, 0)),
                  pl.BlockSpec((1, 6, D), lambda e, m: (m, 0, 0)),
                  wspec, wspec, wspec],
        out_specs=[p_spec, s_spec],
        out_shape=[jax.ShapeDtypeStruct((N_EXPERTS, tm, D), F32),
                   jax.ShapeDtypeStruct((N_EXPERTS, N_SAMPLE_SEQ * tm, D), F32)],
        scratch_shapes=[pltpu.VMEM((D, D), BF16)] * 3,
        compiler_params=_cparams(("arbitrary", "arbitrary"), VMEM_LIMIT),
        name="experts",
    )(xs_p, xs_s, gv, mod, w1, w3, w2)


PROMPT_SLOTS = N_EXPERTS * CAP_PROMPT


def _slot_onehot(rank_rows):
    slot = lax.broadcasted_iota(jnp.int32, (CAP_PROMPT, T_PROMPT), 0).astype(F32)
    return jnp.concatenate(
        [jnp.where(slot == rank_rows[e:e + 1, :], 1.0, 0.0) for e in range(N_EXPERTS)], axis=0)


def _prompt_route_kernel(lg_ref, h_ref, xs_ref, gv_ref, rank_ref, rank_sc):
    lane = lax.broadcasted_iota(jnp.int32, (T_PROMPT, LANES), 1)
    x = jnp.where(lane < N_EXPERTS, lg_ref[...], -jnp.inf)
    e = jnp.exp(x - jnp.max(x, axis=-1, keepdims=True))
    aff = e / jnp.sum(e, axis=-1, keepdims=True)
    aff_t = aff.T
    ti = lax.broadcasted_iota(jnp.int32, (T_PROMPT, T_PROMPT), 0)
    tj = lax.broadcasted_iota(jnp.int32, (T_PROMPT, T_PROMPT), 1)
    earlier = jnp.where(ti < tj, 1.0, 0.0)
    for ex in range(N_EXPERTS):
        a_row = aff_t[ex:ex + 1, :]
        a_col = aff[:, ex:ex + 1]
        ahead = jnp.where(a_col > a_row, 1.0, jnp.where(a_col == a_row, earlier, 0.0))
        rank_sc[ex:ex + 1, :] = jnp.sum(ahead, axis=0, keepdims=True)
    rank = rank_sc[...]
    onehot = _slot_onehot(rank)
    xs_ref[...] = _dot(onehot.astype(BF16), h_ref[...]).astype(xs_ref.dtype).reshape(
        N_EXPERTS, CAP_PROMPT, D)
    for ex in range(N_EXPERTS):
        blk = onehot[ex * CAP_PROMPT:(ex + 1) * CAP_PROMPT, :]
        gv_ref[ex] = jnp.sum(blk * aff_t[ex:ex + 1, :], axis=-1, keepdims=True)
    rank_ref[0] = rank


def _prompt_route_call(logits, h2):
    return pl.pallas_call(
        _prompt_route_kernel,
        grid=(N_PROMPT_SEQ,),
        in_specs=[pl.BlockSpec((T_PROMPT, LANES), lambda s: (s, 0)),
                  pl.BlockSpec((T_PROMPT, D), lambda s: (s, 0))],
        out_specs=[pl.BlockSpec((N_EXPERTS, CAP_PROMPT, D), lambda s: (0, s, 0)),
                   pl.BlockSpec((N_EXPERTS, CAP_PROMPT, 1), lambda s: (0, s, 0)),
                   pl.BlockSpec((1, N_EXPERTS, T_PROMPT), lambda s: (s, 0, 0))],
        out_shape=[jax.ShapeDtypeStruct((N_EXPERTS, N_PROMPT_SEQ * CAP_PROMPT, D), BF16),
                   jax.ShapeDtypeStruct((N_EXPERTS, N_PROMPT_SEQ * CAP_PROMPT, 1), F32),
                   jax.ShapeDtypeStruct((N_PROMPT_SEQ, N_EXPERTS, T_PROMPT), F32)],
        scratch_shapes=[pltpu.VMEM((N_EXPERTS, T_PROMPT), F32)],
        compiler_params=_cparams(("arbitrary",), VMEM_LIMIT),
        name="prompt_route",
    )(logits, h2)


def _prompt_combine_kernel(x_ref, y_ref, rank_ref, o_ref):
    onehot = _slot_onehot(rank_ref[0]).astype(BF16)
    y = y_ref[...].reshape(PROMPT_SLOTS, D)
    hi = y.astype(BF16)
    lo = (y - hi.astype(F32)).astype(BF16)
    tn = (((0,), (0,)), ((), ()))
    o_ref[...] = x_ref[...] + (lax.dot_general(onehot, hi, tn, preferred_element_type=F32)
                               + lax.dot_general(onehot, lo, tn, preferred_element_type=F32))


def _prompt_combine_call(x, y_p, rank):
    return pl.pallas_call(
        _prompt_combine_kernel,
        grid=(N_PROMPT_SEQ,),
        in_specs=[pl.BlockSpec((T_PROMPT, D), lambda s: (s, 0)),
                  pl.BlockSpec((N_EXPERTS, CAP_PROMPT, D), lambda s: (0, s, 0)),
                  pl.BlockSpec((1, N_EXPERTS, T_PROMPT), lambda s: (s, 0, 0))],
        out_specs=pl.BlockSpec((T_PROMPT, D), lambda s: (s, 0)),
        out_shape=jax.ShapeDtypeStruct((ROWS, D), F32),
        input_output_aliases={0: 0},
        compiler_params=_cparams(("arbitrary",), VMEM_LIMIT),
        name="prompt_combine",
    )(x, y_p, rank)


def _final_norm_kernel(x_ref, g_ref, o_ref):
    o_ref[...] = _rms(x_ref[...]) * g_ref[...]


def _final_norm_call(x, g, row0, n_rows):
    tm = 512
    return pl.pallas_call(
        _final_norm_kernel,
        grid=(n_rows // tm,),
        in_specs=[pl.BlockSpec((tm, D), lambda i: (row0 // tm + i, 0)),
                  pl.BlockSpec((1, D), lambda i: (0, 0))],
        out_specs=pl.BlockSpec((tm, D), lambda i: (i, 0)),
        out_shape=jax.ShapeDtypeStruct((n_rows, D), F32),
        compiler_params=_cparams(("arbitrary",), VMEM_LIMIT),
        name="final_norm",
    )(x, g.reshape(1, D))


def _rope_tables():
    tpos = np.arange(T_SAMPLE)
    lane = np.arange(R_DH)
    pos = np.where(lane[None, :] < R_DH // 2, (tpos // GRID_W)[:, None], (tpos % GRID_W)[:, None])
    n_freq = R_DH // 4
    freqs = np.power(np.float32(ROPE_BASE), -np.arange(n_freq, dtype=np.float32) / np.float32(n_freq))
    ang = pos.astype(np.float32) * freqs[lane % n_freq][None, :]
    first = ((lane % (R_DH // 2)) < n_freq)[None, :]
    cos, sin = np.cos(ang), np.sin(ang)
    sin = np.where(first, -sin, sin)
    return tuple(jnp.asarray(a, F32) for a in (cos, sin, np.ascontiguousarray(cos.T), np.ascontiguousarray(sin.T)))


def _route_sample(logits):
    aff = jax.nn.softmax(logits[ROWS_PROMPT:, :N_EXPERTS], axis=-1)
    gs, is_ = lax.top_k(aff.reshape(N_SAMPLE_SEQ, T_SAMPLE, N_EXPERTS).swapaxes(1, 2), CAP_SAMPLE)
    is_ = is_ + ROWS_PROMPT + (jnp.arange(N_SAMPLE_SEQ) * T_SAMPLE)[:, None, None]
    rows_s = is_.swapaxes(0, 1).reshape(N_EXPERTS, -1)
    return rows_s, gs.swapaxes(0, 1).reshape(N_EXPERTS, -1)


def _layer(x, mod, p, states, rope_tabs, stacked, layer, prev_states):
    mq, mk, mv, mo, mg, lx, lz, rq, rk, rv, rg = jnp.split(
        p['w_in'].T, [1024, 2048, 3072, 4096, 4112, 5136, 6160, 7184, 8208, 9232], axis=0)
    w_main_t = jnp.concatenate([mq, mv, mo, lx, lz, rq, rv, rg], axis=0).astype(BF16)
    w_kt = jnp.concatenate([mk.astype(BF16) * (M_DH ** -0.5), rk.astype(BF16)], axis=0)
    w_gate = jnp.pad(mg.T, ((0, 0), (0, LANES - N_GATE_COLS))).astype(BF16)

    h, gates = _norm1_call(x, p['norm1_g'], mod, w_gate)
    proj = _inproj_call(h, w_main_t)
    proj_t = _inproj_t_call(h, w_kt)

    gb = gates[:, :N_GATE_COLS] + p['mlstm_gate_bias'].reshape(1, N_GATE_COLS)
    grow = gb.reshape(ROWS, 4, M_HEADS).transpose(2, 1, 0)
    grow = jnp.pad(grow, ((0, 0), (0, SUBLANES - 4), (0, 0)))

    sm_c, sm_n, sm_m, s_lh, s_rs = states
    prompt_kw = dict(t=T_PROMPT, n_seq=N_PROMPT_SEQ, row_block0=0, layer=layer, emit_state=True)
    sample_kw = dict(t=T_SAMPLE, n_seq=N_SAMPLE_SEQ, row_block0=ROWS_PROMPT // T_SAMPLE, layer=layer,
                     emit_state=False)
    pc, pn, pm, plh, prs = prev_states

    caug0 = jnp.concatenate(
        [sm_c, jnp.broadcast_to(sm_n[..., None], sm_n.shape + (LANES,))], axis=-1)
    m0 = jnp.broadcast_to(sm_m[..., None, None], (N_SAMPLE_SEQ, 2, M_HEADS, 1, LANES))
    (ym,) = _mlstm_call(proj, proj_t, grow, p['mlstm_norm_g'], (caug0, m0), [None], **sample_kw)
    ym, new_c, new_n, new_m = _mlstm_call(proj, proj_t, grow, p['mlstm_norm_g'], None,
                                          [ym, pc, pn, pm], **prompt_kw)

    lru_p = dict(conv_w=p['lru_conv_w'], conv_b=p['lru_conv_b'], wr=p['lru_wr'], wi=p['lru_wi'],
                 br=p['lru_br'], bi=p['lru_bi'], lam=p['lru_lambda'])
    (yl,) = _lru_call(proj, lru_p, s_lh, [None], **sample_kw)
    yl, new_lh = _lru_call(proj, lru_p, None, [yl, plh], **prompt_kw)

    decay = jnp.broadcast_to(p['ret_decay'].T[:, :, None], (R_HEADS, 2, LANES))
    (yr,) = _ret_call(proj, proj_t, decay, p['ret_norm_g'], rope_tabs, s_rs, [None], **sample_kw)
    yr, new_rs = _ret_call(proj, proj_t, decay, p['ret_norm_g'], None, None, [yr, prs], **prompt_kw)

    rw = jnp.pad(p['router_w'], ((0, 0), (0, LANES - N_EXPERTS)))
    rwh = rw.astype(BF16)
    rwl = (rw - rwh.astype(F32)).astype(BF16)
    xn, h2, logits = _merge_call(h, ym, yl, yr, x, mod, p['w_merge'].astype(BF16), p['b_merge'],
                                 p['w_branch'].astype(BF16), p['w_out'].astype(BF16), p['norm2_g'],
                                 rwh, rwl)

    xs_p, gv_p, rank_p = _prompt_route_call(logits, h2)
    rows_s, gv_s = _route_sample(logits)
    xs_s = h2.at[rows_s].get(mode='promise_in_bounds')
    gv = jnp.concatenate([gv_p, gv_s[..., None]], axis=1)
    y_p, y_s = _expert_call(xs_p, xs_s, gv, mod, stacked['exp_w1'], stacked['exp_w3'],
                            stacked['exp_w2'], layer)
    x_out = xn.at[rows_s.reshape(-1)].add(y_s.reshape(-1, D), mode='promise_in_bounds')
    x_out = _prompt_combine_call(x_out, y_p, rank_p)
    return x_out, (new_c, new_n, new_m, new_lh, new_rs)


def kernel(x_prompt, x_sample, c, state_mlstm_C, state_mlstm_n, state_mlstm_m, state_lru_h, state_ret_S, c_ctx, w_ada, b_ada, norm1_g, norm2_g, w_in, mlstm_gate_bias, mlstm_norm_g, lru_conv_w, lru_conv_b, lru_wr, lru_br, lru_wi, lru_bi, lru_lambda, ret_decay, ret_norm_g, w_branch, w_merge, b_merge, w_out, router_w, exp_w1, exp_w3, exp_w2, final_g):
    x = jnp.concatenate([x_prompt.reshape(ROWS_PROMPT, D), x_sample.reshape(-1, D)], axis=0)
    cond8 = jnp.concatenate([c_ctx[None, :], c, jnp.zeros((8 - 1 - N_SAMPLE_SEQ, D), F32)], axis=0)
    mod_all = _ada_call(cond8, w_ada, b_ada).reshape(DEPTH, 8, 6, D)
    rope_tabs = _rope_tables()
    stacked = dict(exp_w1=exp_w1, exp_w3=exp_w3, exp_w2=exp_w2)

    new_states = (None,) * 5
    for l in range(DEPTH):
        p = dict(norm1_g=norm1_g[l], norm2_g=norm2_g[l], w_in=w_in[l],
                 mlstm_gate_bias=mlstm_gate_bias[l], mlstm_norm_g=mlstm_norm_g[l],
                 lru_conv_w=lru_conv_w[l], lru_conv_b=lru_conv_b[l], lru_wr=lru_wr[l],
                 lru_br=lru_br[l], lru_wi=lru_wi[l], lru_bi=lru_bi[l], lru_lambda=lru_lambda[l],
                 ret_decay=ret_decay[l], ret_norm_g=ret_norm_g[l], w_branch=w_branch[l],
                 w_merge=w_merge[l], b_merge=b_merge[l], w_out=w_out[l], router_w=router_w[l])
        states = (state_mlstm_C[:, l], state_mlstm_n[:, l], state_mlstm_m[:, l],
                  state_lru_h[:, l], state_ret_S[:, l])
        x, new_states = _layer(x, mod_all[l], p, states, rope_tabs, stacked, l, new_states)

    y_prompt = _final_norm_call(x, final_g, 0, ROWS_PROMPT).reshape(N_PROMPT_SEQ, T_PROMPT, D)
    y_sample = _final_norm_call(x, final_g, ROWS_PROMPT, ROWS - ROWS_PROMPT).reshape(
        N_SAMPLE_SEQ, T_SAMPLE, D)
    new_c, new_n, new_m, new_lh, new_rs = new_states
    return (y_prompt, y_sample, new_c, new_n[:, :, :, :, 0, :], new_m[:, :, :, :, 0, 0], new_lh, new_rs)
```

```python
import functools

import jax
import jax.numpy as jnp
import numpy as np
from jax import lax
from jax.experimental import pallas as pl
from jax.experimental.pallas import tpu as pltpu

F32 = jnp.float32
BF16 = jnp.bfloat16

D = 1024
DEPTH = 2
N_PROMPT_SEQ = 16
T_PROMPT = 256
N_SAMPLE_SEQ = 2
T_SAMPLE = 4096
ROWS_PROMPT = N_PROMPT_SEQ * T_PROMPT
ROWS = ROWS_PROMPT + N_SAMPLE_SEQ * T_SAMPLE
GROUP_ROWS = 4096
GRID_W = 64
CHUNK = 256
EPS = 1e-6
F32_TINY = 1.1754944e-38
M_HEADS = 4
M_DH = 256
R_HEADS = 8
R_DH = 128
L_BLOCKS = 8
L_BW = 128
L_C = 8.0
ROPE_BASE = 10000.0
N_EXPERTS = 16
CAP_PROMPT = 2 * T_PROMPT // N_EXPERTS
CAP_SAMPLE = 2 * T_SAMPLE // N_EXPERTS
ROWS_PER_EXPERT = N_PROMPT_SEQ * CAP_PROMPT + N_SAMPLE_SEQ * CAP_SAMPLE
N_GATE_COLS = 16
LANES = 128
SUBLANES = 8
VMEM_LIMIT = 56 * 2 ** 20

COL_MQ, COL_MV, COL_MO, COL_LX, COL_LZ, COL_RQ, COL_RV, COL_RG = (i * D for i in range(8))
D_MAIN = 8 * D
ROW_MK, ROW_RK = 0, D
D_KT = 2 * D


def _cparams(sem, vmem=None):
    return pltpu.CompilerParams(dimension_semantics=sem, vmem_limit_bytes=vmem)


def _dot(a, b):
    return jnp.dot(a, b, preferred_element_type=F32)


def _dot_nt(a, b):
    return lax.dot_general(a, b, (((1,), (1,)), ((), ())), preferred_element_type=F32)


def _split3(x):
    a = x.astype(BF16)
    r = x - a.astype(F32)
    b = r.astype(BF16)
    c = (r - b.astype(F32)).astype(BF16)
    return a, b, c


def _log_sigmoid(x):
    return jnp.minimum(x, 0.0) - jnp.log1p(jnp.exp(-jnp.abs(x)))


def _sigmoid(x):
    return 0.5 * jnp.tanh(0.5 * x) + 0.5


def _rms(x):
    return x * lax.rsqrt(jnp.mean(x * x, axis=-1, keepdims=True) + EPS)


def _chunk(c):
    return pl.ds(pl.multiple_of(c * CHUNK, CHUNK), CHUNK)


def _alias_prev(args, in_specs, prev):
    aliases = {}
    for out_idx, arr in enumerate(prev):
        if arr is not None:
            aliases[len(args)] = out_idx
            args.append(arr)
            in_specs.append(pl.BlockSpec(memory_space=pl.ANY))
    return aliases, len(aliases)


def _ada_kernel(c_ref, w_ref, b_ref, o_ref):
    c = c_ref[...]
    s = (c * jax.nn.sigmoid(c)).astype(BF16)
    o_ref[0] = _dot(s, w_ref[0].astype(BF16)) + b_ref[0]


def _ada_call(cond8, w_ada, b_ada):
    tn = 1536
    return pl.pallas_call(
        _ada_kernel,
        grid=(DEPTH, 6 * D // tn),
        in_specs=[pl.BlockSpec((8, D), lambda l, j: (0, 0)),
                  pl.BlockSpec((1, D, tn), lambda l, j: (l, 0, j)),
                  pl.BlockSpec((1, 1, tn), lambda l, j: (l, 0, j))],
        out_specs=pl.BlockSpec((1, 8, tn), lambda l, j: (l, 0, j)),
        out_shape=jax.ShapeDtypeStruct((DEPTH, 8, 6 * D), F32),
        compiler_params=_cparams(("arbitrary", "arbitrary"), VMEM_LIMIT),
        name="ada",
    )(cond8, w_ada, b_ada.reshape(DEPTH, 1, 6 * D))


def _norm1_kernel(x_ref, g_ref, mod_ref, wg_ref, h_ref, gate_ref):
    y = _rms(x_ref[...]) * g_ref[...]
    h = (y * (1.0 + mod_ref[0, 1:2, :]) + mod_ref[0, 0:1, :]).astype(BF16)
    h_ref[...] = h
    gate_ref[...] = _dot(h, wg_ref[...])


def _norm1_call(x, g, mod, w_gate):
    tm = 512
    return pl.pallas_call(
        _norm1_kernel,
        grid=(ROWS // tm,),
        in_specs=[pl.BlockSpec((tm, D), lambda i: (i, 0)),
                  pl.BlockSpec((1, D), lambda i: (0, 0)),
                  pl.BlockSpec((1, 6, D), lambda i: (i * tm // GROUP_ROWS, 0, 0)),
                  pl.BlockSpec((D, LANES), lambda i: (0, 0))],
        out_specs=[pl.BlockSpec((tm, D), lambda i: (i, 0)),
                   pl.BlockSpec((tm, LANES), lambda i: (i, 0))],
        out_shape=[jax.ShapeDtypeStruct((ROWS, D), BF16),
                   jax.ShapeDtypeStruct((ROWS, LANES), F32)],
        compiler_params=_cparams(("arbitrary",), VMEM_LIMIT),
        name="norm1",
    )(x, g.reshape(1, D), mod, w_gate)


def _mm_kernel(x_ref, wt_ref, o_ref):
    o_ref[...] = _dot_nt(x_ref[...], wt_ref[...]).astype(o_ref.dtype)


def _inproj_call(h, w_main_t):
    tm, tn = 1024, 1024
    return pl.pallas_call(
        _mm_kernel,
        grid=(D_MAIN // tn, ROWS // tm),
        in_specs=[pl.BlockSpec((tm, D), lambda j, i: (i, 0)),
                  pl.BlockSpec((tn, D), lambda j, i: (j, 0))],
        out_specs=pl.BlockSpec((tm, tn), lambda j, i: (i, j)),
        out_shape=jax.ShapeDtypeStruct((ROWS, D_MAIN), BF16),
        compiler_params=_cparams(("arbitrary", "arbitrary"), VMEM_LIMIT),
        name="inproj",
    )(h, w_main_t)


def _mm_t_kernel(wt_ref, x_ref, o_ref):
    o_ref[...] = _dot_nt(wt_ref[...], x_ref[...]).astype(o_ref.dtype)


def _inproj_t_call(h, w_kt):
    tm, tn = 1024, 1024
    return pl.pallas_call(
        _mm_t_kernel,
        grid=(D_KT // tn, ROWS // tm),
        in_specs=[pl.BlockSpec((tn, D), lambda j, i: (j, 0)),
                  pl.BlockSpec((tm, D), lambda j, i: (i, 0))],
        out_specs=pl.BlockSpec((tn, tm), lambda j, i: (j, i)),
        out_shape=jax.ShapeDtypeStruct((D_KT, ROWS), BF16),
        compiler_params=_cparams(("arbitrary", "arbitrary"), VMEM_LIMIT),
        name="inproj_t",
    )(w_kt, h)


M_AUG = M_DH + LANES
M_GROUP_SHORT = 2


def _mlstm_kernel(*refs, n_chunks, has_init, emit_state, n_prev):
    q_ref, kt_ref, v_ref, o_ref, g_ref, ng_ref = refs[:6]
    pos = 6
    if has_init:
        c0_ref, m0_ref = refs[pos:pos + 2]
        pos += 2
    pos += n_prev
    y_ref = refs[pos]
    pos += 1
    if emit_state:
        co_ref, no_ref, mo_ref = refs[pos:pos + 3]
        pos += 3
    caug_sc, m_sc, hf_sc, hb_sc, gp_mp, gp_bc, gp_row = refs[pos:]
    n_group = caug_sc.shape[0]

    zero_state = (not has_init) and n_chunks == 1
    if has_init:
        for g in range(n_group):
            caug_sc[g] = c0_ref[0, :, g]
            m_sc[g] = m0_ref[0, :, g]
    else:
        if not zero_state:
            caug_sc[...] = jnp.zeros_like(caug_sc)
        m_sc[...] = jnp.zeros_like(m_sc)

    ri = lax.broadcasted_iota(jnp.int32, (CHUNK, CHUNK), 0)
    ci = lax.broadcasted_iota(jnp.int32, (CHUNK, CHUNK), 1)
    r8 = lax.broadcasted_iota(jnp.int32, (SUBLANES, CHUNK), 0)
    ones_b = jnp.ones((CHUNK, LANES), BF16)
    allowed = [ri >= ci, ci >= ri]
    tri = [jnp.where(ri <= ci, 1.0, 0.0).astype(BF16), jnp.where(ri >= ci, 1.0, 0.0).astype(BF16)]

    def gate_pass(c, carry):
        rows = _chunk(c)
        for g in range(n_group):
            g8 = g_ref[g, :, rows]
            for d in range(2):
                i_idx, f_idx = (0, 1) if d == 0 else (2, 3)
                ls = jnp.where(r8 == f_idx, _log_sigmoid(g8), g8)
                cum = sum(_dot(p, tri[d]) for p in _split3(ls))
                b_row = cum[f_idx:f_idx + 1]
                g_row = ls[i_idx:i_idx + 1] - b_row
                gmat = jnp.where(allowed[d], jnp.broadcast_to(g_row, (CHUNK, CHUNK)), -jnp.inf)
                mp_col = jnp.max(gmat, axis=-1, keepdims=True)
                gp_bc[g, d, c] = jnp.broadcast_to(b_row, (LANES, CHUNK)).T
                gp_mp[g, d, c] = jnp.broadcast_to(mp_col, (CHUNK, LANES))
                b_last_11 = b_row[:, CHUNK - 1:CHUNK] if d == 0 else b_row[:, 0:1]
                log_k = b_last_11 + g_row
                a_11 = jnp.max(log_k, axis=-1, keepdims=True)
                gp_row[g, d, c] = jnp.where(
                    r8 == 0, g_row, jnp.where(
                        r8 == 1, jnp.exp(log_k - a_11), jnp.where(r8 == 2, b_last_11, a_11)))
        return carry

    lax.fori_loop(0, n_chunks, gate_pass, 0, unroll=min(n_chunks, 8))

    def step(g, d, c, m_t):
        rows = _chunk(c)
        hcols = slice(g * M_DH, (g + 1) * M_DH)
        mp = gp_mp[g, d, c]
        bc = gp_bc[g, d, c]
        row_t = gp_row[g, d, c]
        gmat = jnp.where(allowed[d], jnp.broadcast_to(row_t[0:1], (CHUNK, CHUNK)), -jnp.inf)
        e1 = jnp.exp(gmat - jnp.concatenate([mp] * (CHUNK // LANES), axis=1))

        q = q_ref[rows, hcols]
        kt = kt_ref[hcols, rows]
        vaug = jnp.concatenate([v_ref[rows, hcols], ones_b], axis=1)
        s1 = _dot(q, kt) * e1
        sv1 = _dot(s1.astype(BF16), vaug)
        kw1 = (kt.astype(F32) * row_t[1:2]).astype(BF16)
        u1 = _dot(kw1, vaug)
        b_last = jnp.broadcast_to(row_t[2:3, 0:LANES], (SUBLANES, LANES))
        a_t = jnp.broadcast_to(row_t[3:4, 0:LANES], (SUBLANES, LANES))

        m_b = jnp.broadcast_to(m_t[0:1, :], (CHUNK, LANES))
        mi = jnp.maximum(m_b, mp)
        r = jnp.exp(mp - mi)
        w_prev = jnp.exp(m_b - mi)
        floor = jnp.exp(-(bc + mi))
        if zero_state:
            qc = None
            den = r * sv1[:, M_DH:]
        else:
            qc = _dot(q, caug_sc[g, d].astype(BF16))
            den = r * sv1[:, M_DH:] + w_prev * qc[:, M_DH:]
        inv = 1.0 / jnp.maximum(jnp.abs(den), floor)
        h_sc = hf_sc if d == 0 else hb_sc
        for blk in range(M_DH // LANES):
            cols = slice(blk * LANES, (blk + 1) * LANES)
            num = r * sv1[:, cols]
            if not zero_state:
                num = num + w_prev * qc[:, cols]
            h_sc[rows, slice(g * M_DH + blk * LANES, g * M_DH + (blk + 1) * LANES)] = num * inv

        m_new = jnp.maximum(b_last + m_t, a_t)
        w_c = jnp.exp(b_last + m_t - m_new)[0:1, :]
        w_u = jnp.exp(a_t - m_new)[0:1, :]
        for blk in range(M_AUG // LANES):
            cols = slice(blk * LANES, (blk + 1) * LANES)
            if zero_state:
                caug_sc[g, d, :, cols] = w_u * u1[:, cols]
            else:
                caug_sc[g, d, :, cols] = w_c * caug_sc[g, d, :, cols] + w_u * u1[:, cols]
        return m_new

    def scan_body(c, carry):
        new = []
        for g in range(n_group):
            new.append(step(g, 0, c, carry[2 * g]))
            new.append(step(g, 1, n_chunks - 1 - c, carry[2 * g + 1]))
        return tuple(new)

    m_init = tuple(jnp.broadcast_to(m_sc[g, d], (SUBLANES, LANES))
                   for g in range(n_group) for d in range(2))
    m_fin = lax.fori_loop(0, n_chunks, scan_body, m_init, unroll=min(n_chunks, 2))
    for g in range(n_group):
        for d in range(2):
            m_sc[g, d] = m_fin[2 * g + d][0:1, :]

    def out_body(c, carry):
        rows = _chunk(c)
        for g in range(n_group):
            hcols = slice(g * M_DH, (g + 1) * M_DH)
            y = _rms(hf_sc[rows, hcols] + hb_sc[rows, hcols]) * ng_ref[:, hcols]
            y_ref[rows, hcols] = (y * _sigmoid(o_ref[rows, hcols].astype(F32))).astype(y_ref.dtype)
        return carry

    lax.fori_loop(0, n_chunks, out_body, 0, unroll=min(n_chunks, 2))

    if emit_state:
        for g in range(n_group):
            co_ref[0, 0, :, g] = caug_sc[g, :, :, 0:M_DH]
            for d in range(2):
                no_ref[0, 0, d, g] = caug_sc[g, d, :, M_DH:].T[0:1, :]
            mo_ref[0, 0, :, g] = m_sc[g]


def _mlstm_call(proj, proj_t, grow, norm_g, state, prev, *, t, n_seq, row_block0, layer, emit_state):
    has_init = state is not None
    group = M_GROUP_SHORT if t <= CHUNK else 1
    gw = group * M_DH
    big = dict(pipeline_mode=pl.Buffered(1)) if t * gw * 2 > 2 ** 21 else {}

    def col(base):
        return lambda s, h: (row_block0 + s, base // gw + h)

    st_caug = pl.BlockSpec((1, 2, group, M_DH, M_AUG), lambda s, h: (s, 0, h, 0, 0))
    st_m = pl.BlockSpec((1, 2, group, 1, LANES), lambda s, h: (s, 0, h, 0, 0))
    so_c = pl.BlockSpec((1, 1, 2, group, M_DH, M_DH), lambda s, h: (s, layer, 0, h, 0, 0))
    so_n = pl.BlockSpec((1, 1, 2, group, 1, M_DH), lambda s, h: (s, layer, 0, h, 0, 0))
    so_m = pl.BlockSpec((1, 1, 2, group, 1, LANES), lambda s, h: (s, layer, 0, h, 0, 0))
    in_specs = [pl.BlockSpec((t, gw), col(COL_MQ), **big),
                pl.BlockSpec((gw, t), lambda s, h: (ROW_MK // gw + h, row_block0 + s), **big),
                pl.BlockSpec((t, gw), col(COL_MV), **big),
                pl.BlockSpec((t, gw), col(COL_MO), **big),
                pl.BlockSpec((group, SUBLANES, t), lambda s, h: (h, 0, row_block0 + s)),
                pl.BlockSpec((1, gw), lambda s, h: (0, h))]
    args = [proj, proj_t, proj, proj, grow, norm_g.reshape(1, D)]
    if has_init:
        in_specs += [st_caug, st_m]
        args += list(state)
    out_specs = [pl.BlockSpec((t, gw), lambda s, h: (row_block0 + s, h))]
    out_shape = [jax.ShapeDtypeStruct((ROWS, D), BF16)]
    if emit_state:
        out_specs += [so_c, so_n, so_m]
        out_shape += [jax.ShapeDtypeStruct((n_seq, DEPTH, 2, M_HEADS, M_DH, M_DH), F32),
                      jax.ShapeDtypeStruct((n_seq, DEPTH, 2, M_HEADS, 1, M_DH), F32),
                      jax.ShapeDtypeStruct((n_seq, DEPTH, 2, M_HEADS, 1, LANES), F32)]
    aliases, n_prev = _alias_prev(args, in_specs, prev)
    return pl.pallas_call(
        functools.partial(_mlstm_kernel, n_chunks=t // CHUNK, has_init=has_init,
                          emit_state=emit_state, n_prev=n_prev),
        grid=(n_seq, M_HEADS // group),
        in_specs=in_specs,
        out_specs=out_specs,
        out_shape=out_shape,
        input_output_aliases=aliases,
        scratch_shapes=[pltpu.VMEM((group, 2, M_DH, M_AUG), F32),
                        pltpu.VMEM((group, 2, 1, LANES), F32),
                        pltpu.VMEM((t, gw), F32),
                        pltpu.VMEM((t, gw), F32),
                        pltpu.VMEM((group, 2, t // CHUNK, CHUNK, LANES), F32),
                        pltpu.VMEM((group, 2, t // CHUNK, CHUNK, LANES), F32),
                        pltpu.VMEM((group, 2, t // CHUNK, SUBLANES, CHUNK), F32)],
        compiler_params=_cparams(("arbitrary", "arbitrary"), VMEM_LIMIT),
        name="mlstm",
    )(*args)


R_GROUP_SHORT, R_GROUP_LONG = 4, 2

def _ret_kernel(*refs, n_chunks, rope, has_init, emit_state, n_prev):
    q_ref, kt_ref, v_ref, g_ref, dec_ref, ng_ref = refs[:6]
    pos = 6
    if rope:
        cos_ref, sin_ref, cos_t_ref, sin_t_ref = refs[pos:pos + 4]
        pos += 4
    if has_init:
        s0_ref = refs[pos]
        pos += 1
    pos += n_prev
    y_ref = refs[pos]
    pos += 1
    if emit_state:
        so_ref = refs[pos]
        pos += 1
    s_sc, of_sc, ob_sc, ktr_sc = refs[pos:pos + 4]
    qr_sc = refs[pos + 4] if rope else None
    n_group = s_sc.shape[0]

    zero_state = (not has_init) and n_chunks == 1
    if has_init:
        for g in range(n_group):
            s_sc[g] = s0_ref[0, :, g]
    elif not zero_state:
        s_sc[...] = jnp.zeros_like(s_sc)

    ri = lax.broadcasted_iota(jnp.int32, (CHUNK, CHUNK), 0)
    ci = lax.broadcasted_iota(jnp.int32, (CHUNK, CHUNK), 1)
    quarter = R_DH // 4
    if rope:
        fr = lax.broadcasted_iota(jnp.int32, (R_DH, R_DH), 0)
        fc = lax.broadcasted_iota(jnp.int32, (R_DH, R_DH), 1)
        partner = fc + jnp.where((fc % (2 * quarter)) < quarter, quarter, -quarter)
        perm_b = jnp.where(fr == partner, 1.0, 0.0).astype(BF16)

    def prep_body(c, carry):
        rows = _chunk(c)
        for g in range(n_group):
            hcols = slice(g * R_DH, (g + 1) * R_DH)
            ktf = kt_ref[hcols, rows].astype(F32) * (R_DH ** -0.5)
            if rope:
                swapped = jnp.concatenate([ktf[quarter:2 * quarter], ktf[0:quarter],
                                           ktf[3 * quarter:], ktf[2 * quarter:3 * quarter]], axis=0)
                ktf = ktf * cos_t_ref[:, rows] + swapped * sin_t_ref[:, rows]
                q = q_ref[rows, hcols]
                qr = q.astype(F32) * cos_ref[rows, :] + _dot(q, perm_b) * sin_ref[rows, :]
                qr_sc[rows, hcols] = qr.astype(BF16)
            ktr_sc[hcols, rows] = ktf.astype(BF16)
        return carry

    lax.fori_loop(0, n_chunks, prep_body, 0, unroll=min(n_chunks, 4))
    qsrc = qr_sc if rope else q_ref

    lane = lax.broadcasted_iota(jnp.int32, (1, CHUNK), 1).astype(F32)
    row_f = lax.broadcasted_iota(jnp.int32, (CHUNK, R_DH), 0).astype(F32)
    consts = []
    for gd in range(2 * n_group):
        g, d = divmod(gd, 2)
        lg = -jnp.exp(dec_ref[g, d:d + 1, :])
        lg11 = lg[:, 0:1]
        diff = (ri - ci) if d == 0 else (ci - ri)
        causal = diff >= 0
        dmat = jnp.where(causal, jnp.exp(lg11 * jnp.where(causal, diff, 0).astype(F32)), 0.0)
        if d == 0:
            q_dec = jnp.exp(lg * (row_f + 1.0))
            k_dec = jnp.exp(lg11 * (CHUNK - 1.0 - lane))
        else:
            q_dec = jnp.exp(lg * (CHUNK - row_f))
            k_dec = jnp.exp(lg11 * lane)
        g_chunk = jnp.exp(lg * float(CHUNK))
        consts.append((dmat, q_dec, k_dec, g_chunk))

    def step(g, d, c):
        dmat, q_dec, k_dec, g_chunk = consts[2 * g + d]
        rows = _chunk(c)
        hcols = slice(g * R_DH, (g + 1) * R_DH)
        q = qsrc[rows, hcols]
        kt = ktr_sc[hcols, rows]
        v = v_ref[rows, hcols]
        s = _dot(q, kt) * dmat
        o = _dot(s.astype(BF16), v)
        if not zero_state:
            o = o + q_dec * _dot(q, s_sc[g, d].astype(BF16))
        if d == 0:
            of_sc[rows, hcols] = o
        else:
            ob_sc[rows, hcols] = o
        kd = (kt.astype(F32) * k_dec).astype(BF16)
        if zero_state:
            s_sc[g, d] = _dot(kd, v)
        else:
            s_sc[g, d] = g_chunk * s_sc[g, d] + _dot(kd, v)

    def scan_body(c, carry):
        for g in range(n_group):
            step(g, 0, c)
            step(g, 1, n_chunks - 1 - c)
        return carry

    lax.fori_loop(0, n_chunks, scan_body, 0, unroll=min(n_chunks, 2))

    def out_body(c, carry):
        rows = _chunk(c)
        for g in range(n_group):
            hcols = slice(g * R_DH, (g + 1) * R_DH)
            y = _rms(of_sc[rows, hcols] + ob_sc[rows, hcols]) * ng_ref[:, hcols]
            gate = g_ref[rows, hcols].astype(F32)
            y_ref[rows, hcols] = (y * (gate * _sigmoid(gate))).astype(y_ref.dtype)
        return carry

    lax.fori_loop(0, n_chunks, out_body, 0, unroll=min(n_chunks, 4))

    if emit_state:
        for g in range(n_group):
            so_ref[0, 0, :, g] = s_sc[g]


def _ret_call(proj, proj_t, decay, norm_g, rope_tabs, state, prev, *, t, n_seq, row_block0, layer, emit_state):
    has_init = state is not None
    rope = rope_tabs is not None

    group = R_GROUP_SHORT if t <= CHUNK else R_GROUP_LONG
    gw = group * R_DH

    def col(base):
        return lambda s, h: (row_block0 + s, base // gw + h)

    st_s = pl.BlockSpec((1, 2, group, R_DH, R_DH), lambda s, h: (s, 0, h, 0, 0))
    in_specs = [pl.BlockSpec((t, gw), col(COL_RQ)),
                pl.BlockSpec((gw, t), lambda s, h: (ROW_RK // gw + h, row_block0 + s)),
                pl.BlockSpec((t, gw), col(COL_RV)),
                pl.BlockSpec((t, gw), col(COL_RG)),
                pl.BlockSpec((group, 2, LANES), lambda s, h: (h, 0, 0)),
                pl.BlockSpec((1, gw), lambda s, h: (0, h))]
    args = [proj, proj_t, proj, proj, decay, norm_g.reshape(1, D)]
    if rope:
        in_specs += [pl.BlockSpec((t, R_DH), lambda s, h: (0, 0))] * 2
        in_specs += [pl.BlockSpec((R_DH, t), lambda s, h: (0, 0))] * 2
        args += list(rope_tabs)
    if has_init:
        in_specs.append(st_s)
        args.append(state)
    out_specs = [pl.BlockSpec((t, gw), lambda s, h: (row_block0 + s, h))]
    out_shape = [jax.ShapeDtypeStruct((ROWS, D), BF16)]
    if emit_state:
        out_specs.append(pl.BlockSpec((1, 1, 2, group, R_DH, R_DH), lambda s, h: (s, layer, 0, h, 0, 0)))
        out_shape.append(jax.ShapeDtypeStruct((n_seq, DEPTH, 2, R_HEADS, R_DH, R_DH), F32))
    aliases, n_prev = _alias_prev(args, in_specs, prev)
    scratch = [pltpu.VMEM((group, 2, R_DH, R_DH), F32),
               pltpu.VMEM((t, gw), F32),
               pltpu.VMEM((t, gw), F32),
               pltpu.VMEM((gw, t), BF16)]
    if rope:
        scratch.append(pltpu.VMEM((t, gw), BF16))
    return pl.pallas_call(
        functools.partial(_ret_kernel, n_chunks=t // CHUNK, rope=rope, has_init=has_init,
                          emit_state=emit_state, n_prev=n_prev),
        grid=(n_seq, R_HEADS // group),
        in_specs=in_specs,
        out_specs=out_specs,
        out_shape=out_shape,
        input_output_aliases=aliases,
        scratch_shapes=scratch,
        compiler_params=_cparams(("arbitrary", "arbitrary"), VMEM_LIMIT),
        name="retention",
    )(*args)


LRU_SLAB = 64
L_GROUP_SHORT = 2


def _tile_scan(a, b, reverse):
    row = lax.broadcasted_iota(jnp.int32, (SUBLANES, L_BW), 0)
    for k in (1, 2, 4):
        if reverse:
            keep = row < SUBLANES - k
            shift = SUBLANES - k
        else:
            keep = row >= k
            shift = k
        a_sh = jnp.where(keep, pltpu.roll(a, shift, 0), 1.0)
        b_sh = jnp.where(keep, pltpu.roll(b, shift, 0), 0.0)
        b = a * b_sh + b
        a = a * a_sh
    return a, b


def _lru_kernel(*refs, t, has_init, emit_state, n_prev):
    lx_ref, lz_ref, cw_ref, cb_ref, wr_ref, wi_ref, br_ref, bi_ref, lam_ref = refs[:9]
    pos = 9
    if has_init:
        h0_ref = refs[pos]
        pos += 1
    pos += n_prev
    y_ref = refs[pos]
    pos += 1
    if emit_state:
        hfin_ref = refs[pos]
        pos += 1
    xpad_sc, a_sc, b_sc = refs[pos:]
    rc = min(t, 256)
    gw = xpad_sc.shape[1]
    n_group = gw // L_BW

    def cs(g):
        return slice(g * L_BW, (g + 1) * L_BW)

    xpad_sc[0:8, :] = jnp.zeros((8, gw), F32)
    xpad_sc[t + 8:t + 16, :] = jnp.zeros((8, gw), F32)

    def pad_body(c, carry):
        r0 = pl.multiple_of(c * rc, rc)
        xpad_sc[pl.ds(r0 + 8, rc), :] = lx_ref[pl.ds(r0, rc), :].astype(F32)
        return carry

    lax.fori_loop(0, t // rc, pad_body, 0)

    wr = [[(0.5 * wr_ref[d, g]).astype(BF16) for d in range(2)] for g in range(n_group)]
    wi = [[(0.5 * wi_ref[d, g]).astype(BF16) for d in range(2)] for g in range(n_group)]
    half_br = 0.5 * br_ref[...]
    half_bi = 0.5 * bi_ref[...]
    lam = lam_ref[...]
    half_c = -0.5 * L_C * (jnp.maximum(-lam, 0.0) + jnp.log1p(jnp.exp(-jnp.abs(lam))))

    def gate_body(c, carry):
        r0 = pl.multiple_of(c * rc, rc)
        n = rc + 16
        for g in range(n_group):
            xe = xpad_sc[pl.ds(r0, n), cs(g)]
            u = (cb_ref[:, cs(g)] + cw_ref[0:1, cs(g)] * pltpu.roll(xe, 2, 0)[8:8 + rc]
                 + cw_ref[1:2, cs(g)] * pltpu.roll(xe, 1, 0)[8:8 + rc]
                 + cw_ref[2:3, cs(g)] * xe[8:8 + rc]
                 + cw_ref[3:4, cs(g)] * pltpu.roll(xe, n - 1, 0)[8:8 + rc])
            ub = u.astype(BF16)
            half_u = 0.5 * u
            for d in range(2):
                hc = half_c[d:d + 1, cs(g)]
                tr = jnp.tanh(_dot(ub, wr[g][d]) + half_br[d:d + 1, cs(g)])
                ti = jnp.tanh(_dot(ub, wi[g][d]) + half_bi[d:d + 1, cs(g)])
                log_a = hc * tr + hc
                a = jnp.exp(log_a)
                z = jnp.tanh(log_a) * (-1.0 - a * a)
                mult = z * lax.rsqrt(jnp.maximum(z, F32_TINY))
                a_sc[d, pl.ds(r0, rc), cs(g)] = a
                b_sc[d, pl.ds(r0, rc), cs(g)] = mult * (ti * half_u + half_u)
        return carry

    lax.fori_loop(0, t // rc, gate_body, 0, unroll=min(t // rc, 2))

    n_slabs = t // LRU_SLAB
    tiles = LRU_SLAB // SUBLANES

    row8 = lax.broadcasted_iota(jnp.int32, (SUBLANES, L_BW), 0)

    def bcast_row(x, r):
        return jnp.broadcast_to(x[r:r + 1, :], (SUBLANES, L_BW))

    def slab_scan(g, d, r0, c_in):
        reverse = d == 1
        edge = 0 if reverse else SUBLANES - 1
        local = []
        spa = jnp.ones((SUBLANES, L_BW), F32)
        shl = jnp.zeros((SUBLANES, L_BW), F32)
        for k in range(tiles):
            rows = pl.ds(r0 + k * SUBLANES, SUBLANES)
            pa, hl = _tile_scan(a_sc[d, rows, cs(g)], b_sc[d, rows, cs(g)], reverse)
            local.append((rows, pa, hl))
            spa = jnp.where(row8 == k, bcast_row(pa, edge), spa)
            shl = jnp.where(row8 == k, bcast_row(hl, edge), shl)
        cpa, chl = _tile_scan(spa, shl, reverse)
        after = chl + cpa * c_in
        for k, (rows, pa, hl) in enumerate(local):
            prev = k + 1 if reverse else k - 1
            cin = c_in if (prev < 0 or prev >= tiles) else bcast_row(after, prev)
            b_sc[d, rows, cs(g)] = hl + pa * cin
        return bcast_row(after, 0 if reverse else tiles - 1)

    def scan_body(c, carry):
        new = []
        for g in range(n_group):
            new.append(slab_scan(g, 0, pl.multiple_of(c * LRU_SLAB, LRU_SLAB), carry[2 * g]))
            new.append(slab_scan(g, 1, pl.multiple_of((n_slabs - 1 - c) * LRU_SLAB, LRU_SLAB),
                                 carry[2 * g + 1]))
        return tuple(new)

    if has_init:
        c0 = tuple(jnp.broadcast_to(h0_ref[0, d:d + 1, cs(g)], (SUBLANES, L_BW))
                   for g in range(n_group) for d in range(2))
    else:
        c0 = tuple(jnp.zeros((SUBLANES, L_BW), F32) for _ in range(2 * n_group))
    c_fin = lax.fori_loop(0, n_slabs, scan_body, c0, unroll=2)
    if emit_state:
        for g in range(n_group):
            for d in range(2):
                hfin_ref[0, 0, d:d + 1, cs(g)] = c_fin[2 * g + d][0:1, :]

    def out_body(c, carry):
        rows = pl.ds(pl.multiple_of(c * rc, rc), rc)
        z = lz_ref[rows, :].astype(F32)
        gelu = 0.5 * z * (1.0 + jnp.tanh(0.7978845608028654 * (z + 0.044715 * (z * z * z))))
        y_ref[rows, :] = ((b_sc[0, rows, :] + b_sc[1, rows, :]) * gelu).astype(y_ref.dtype)
        return carry

    lax.fori_loop(0, t // rc, out_body, 0)


def _lru_call(proj, p, state, prev, *, t, n_seq, row_block0, layer, emit_state):
    has_init = state is not None

    group = L_GROUP_SHORT if t <= 256 else 1
    gw = group * L_BW

    def col(base):
        return lambda s, b: (row_block0 + s, base // gw + b)

    vec2 = pl.BlockSpec((2, gw), lambda s, b: (0, b))
    st = pl.BlockSpec((1, 2, gw), lambda s, b: (s, 0, b))
    in_specs = [pl.BlockSpec((t, gw), col(COL_LX)),
                pl.BlockSpec((t, gw), col(COL_LZ)),
                pl.BlockSpec((4, gw), lambda s, b: (0, b)),
                pl.BlockSpec((1, gw), lambda s, b: (0, b)),
                pl.BlockSpec((2, group, L_BW, L_BW), lambda s, b: (0, b, 0, 0)),
                pl.BlockSpec((2, group, L_BW, L_BW), lambda s, b: (0, b, 0, 0)),
                vec2, vec2, vec2]
    args = [proj, proj, p['conv_w'], p['conv_b'].reshape(1, D), p['wr'], p['wi'],
            p['br'], p['bi'], p['lam']]
    if has_init:
        in_specs.append(st)
        args.append(state)
    out_specs = [pl.BlockSpec((t, gw), lambda s, b: (row_block0 + s, b))]
    out_shape = [jax.ShapeDtypeStruct((ROWS, D), BF16)]
    if emit_state:
        out_specs.append(pl.BlockSpec((1, 1, 2, gw), lambda s, b: (s, layer, 0, b)))
        out_shape.append(jax.ShapeDtypeStruct((n_seq, DEPTH, 2, D), F32))
    aliases, n_prev = _alias_prev(args, in_specs, prev)
    return pl.pallas_call(
        functools.partial(_lru_kernel, t=t, has_init=has_init, emit_state=emit_state, n_prev=n_prev),
        grid=(n_seq, L_BLOCKS // group),
        in_specs=in_specs,
        out_specs=out_specs,
        out_shape=out_shape,
        input_output_aliases=aliases,
        scratch_shapes=[pltpu.VMEM((t + 16, gw), F32),
                        pltpu.VMEM((2, t, gw), F32),
                        pltpu.VMEM((2, t, gw), F32)],
        compiler_params=_cparams(("arbitrary", "arbitrary"), VMEM_LIMIT),
        name="rglru",
    )(*args)


def _merge_kernel(h_ref, ym_ref, yl_ref, yr_ref, x_ref, mod_ref, wm_ref, bm_ref, wb_ref, wo_ref,
                  n2_ref, rwh_ref, rwl_ref, xo_ref, h2_ref, lg_ref):
    h = h_ref[...]
    merged = None
    for k, y_ref in enumerate((ym_ref, yl_ref, yr_ref)):
        gate = jax.nn.sigmoid(_dot(h, wm_ref[:, k * D:(k + 1) * D]) + bm_ref[:, k * D:(k + 1) * D])
        term = gate * _dot(y_ref[...], wb_ref[k])
        merged = term if merged is None else merged + term
    mix = _dot(merged.astype(BF16), wo_ref[...])
    xn = x_ref[...] + mod_ref[0, 2:3, :] * mix
    xo_ref[...] = xn
    hn = _rms(xn) * n2_ref[...] * (1.0 + mod_ref[0, 4:5, :]) + mod_ref[0, 3:4, :]
    hi = hn.astype(BF16)
    h2_ref[...] = hi
    lo = (hn - hi.astype(F32)).astype(BF16)
    lg_ref[...] = _dot(hi, rwh_ref[...]) + _dot(lo, rwh_ref[...]) + _dot(hi, rwl_ref[...])


def _merge_call(h, ym, yl, yr, x, mod, wm, bm, wb, wo, n2g, rwh, rwl):
    tm = 256
    row = lambda i: (i, 0)
    const2 = lambda i: (0, 0)
    return pl.pallas_call(
        _merge_kernel,
        grid=(ROWS // tm,),
        in_specs=[pl.BlockSpec((tm, D), row),
                  pl.BlockSpec((tm, D), row),
                  pl.BlockSpec((tm, D), row),
                  pl.BlockSpec((tm, D), row),
                  pl.BlockSpec((tm, D), row),
                  pl.BlockSpec((1, 6, D), lambda i: (i * tm // GROUP_ROWS, 0, 0)),
                  pl.BlockSpec((D, 3 * D), const2),
                  pl.BlockSpec((1, 3 * D), const2),
                  pl.BlockSpec((3, D, D), lambda i: (0, 0, 0)),
                  pl.BlockSpec((D, D), const2),
                  pl.BlockSpec((1, D), const2),
                  pl.BlockSpec((D, LANES), const2),
                  pl.BlockSpec((D, LANES), const2)],
        out_specs=[pl.BlockSpec((tm, D), row),
                   pl.BlockSpec((tm, D), row),
                   pl.BlockSpec((tm, LANES), row)],
        out_shape=[jax.ShapeDtypeStruct((ROWS, D), F32),
                   jax.ShapeDtypeStruct((ROWS, D), BF16),
                   jax.ShapeDtypeStruct((ROWS, LANES), F32)],
        compiler_params=_cparams(("arbitrary",), VMEM_LIMIT),
        name="merge",
    )(h, ym, yl, yr, x, mod, wm, bm.reshape(1, 3 * D), wb, wo, n2g.reshape(1, D), rwh, rwl)


EXPERT_TM = 512
assert N_PROMPT_SEQ * CAP_PROMPT == EXPERT_TM and CAP_SAMPLE == EXPERT_TM


def _expert_kernel(xp_ref, xs_ref, g_ref, mod_ref, w1_ref, w3_ref, w2_ref, yp_ref, ys_ref,
                   w1_sc, w3_sc, w2_sc):
    m = pl.program_id(1)

    @pl.when(m == 0)
    def _():
        w1_sc[...] = w1_ref[0, 0].astype(BF16)
        w3_sc[...] = w3_ref[0, 0].astype(BF16)
        w2_sc[...] = w2_ref[0, 0].astype(BF16)

    xs = jnp.where(m == 0, xp_ref[0], xs_ref[0])
    a = _dot(xs, w1_sc[...])
    b = _dot(xs, w3_sc[...])
    mid = (a * jax.nn.sigmoid(a) * b).astype(BF16)

    def down(y_ref):
        y_ref[0] = (_dot(mid, w2_sc[...]) * g_ref[0]) * mod_ref[0, 5:6, :]

    pl.when(m == 0)(functools.partial(down, yp_ref))
    pl.when(m > 0)(functools.partial(down, ys_ref))


def _expert_call(xs_p, xs_s, gv, mod, w1, w3, w2, layer):
    tm = EXPERT_TM
    wspec = pl.BlockSpec((1, 1, D, D), lambda e, m: (layer, e, 0, 0))
    p_spec = pl.BlockSpec((1, tm, D), lambda e, m: (e, 0, 0))
    s_spec = pl.BlockSpec((1, tm, D), lambda e, m: (e, jnp.maximum(m - 1, 0), 0))
    return pl.pallas_call(
        _expert_kernel,
        grid=(N_EXPERTS, ROWS_PER_EXPERT // tm),
        in_specs=[p_spec, s_spec,
                  pl.BlockSpec((1, tm, 1), lambda e, m: (e, m, 0)),
                  pl.BlockSpec((1, 6, D), lambda e, m: (m, 0, 0)),
                  wspec, wspec, wspec],
        out_specs=[p_spec, s_spec],
        out_shape=[jax.ShapeDtypeStruct((N_EXPERTS, tm, D), F32),
                   jax.ShapeDtypeStruct((N_EXPERTS, N_SAMPLE_SEQ * tm, D), F32)],
        scratch_shapes=[pltpu.VMEM((D, D), BF16)] * 3,
        compiler_params=_cparams(("arbitrary", "arbitrary"), VMEM_LIMIT),
        name="experts",
    )(xs_p, xs_s, gv, mod, w1, w3, w2)


PROMPT_SLOTS = N_EXPERTS * CAP_PROMPT


def _slot_onehot(rank_rows):
    slot = lax.broadcasted_iota(jnp.int32, (CAP_PROMPT, T_PROMPT), 0).astype(F32)
    return jnp.concatenate(
        [jnp.where(slot == rank_rows[e:e + 1, :], 1.0, 0.0) for e in range(N_EXPERTS)], axis=0)


def _prompt_route_kernel(lg_ref, h_ref, xs_ref, gv_ref, rank_ref, rank_sc):
    lane = lax.broadcasted_iota(jnp.int32, (T_PROMPT, LANES), 1)
    x = jnp.where(lane < N_EXPERTS, lg_ref[...], -jnp.inf)
    e = jnp.exp(x - jnp.max(x, axis=-1, keepdims=True))
    aff = e / jnp.sum(e, axis=-1, keepdims=True)
    aff_t = aff.T
    ti = lax.broadcasted_iota(jnp.int32, (T_PROMPT, T_PROMPT), 0)
    tj = lax.broadcasted_iota(jnp.int32, (T_PROMPT, T_PROMPT), 1)
    earlier = jnp.where(ti < tj, 1.0, 0.0)
    for ex in range(N_EXPERTS):
        a_row = aff_t[ex:ex + 1, :]
        a_col = aff[:, ex:ex + 1]
        ahead = jnp.where(a_col > a_row, 1.0, jnp.where(a_col == a_row, earlier, 0.0))
        rank_sc[ex:ex + 1, :] = jnp.sum(ahead, axis=0, keepdims=True)
    rank = rank_sc[...]
    onehot = _slot_onehot(rank)
    xs_ref[...] = _dot(onehot.astype(BF16), h_ref[...]).astype(xs_ref.dtype).reshape(
        N_EXPERTS, CAP_PROMPT, D)
    for ex in range(N_EXPERTS):
        blk = onehot[ex * CAP_PROMPT:(ex + 1) * CAP_PROMPT, :]
        gv_ref[ex] = jnp.sum(blk * aff_t[ex:ex + 1, :], axis=-1, keepdims=True)
    rank_ref[0] = rank


def _prompt_route_call(logits, h2):
    return pl.pallas_call(
        _prompt_route_kernel,
        grid=(N_PROMPT_SEQ,),
        in_specs=[pl.BlockSpec((T_PROMPT, LANES), lambda s: (s, 0)),
                  pl.BlockSpec((T_PROMPT, D), lambda s: (s, 0))],
        out_specs=[pl.BlockSpec((N_EXPERTS, CAP_PROMPT, D), lambda s: (0, s, 0)),
                   pl.BlockSpec((N_EXPERTS, CAP_PROMPT, 1), lambda s: (0, s, 0)),
                   pl.BlockSpec((1, N_EXPERTS, T_PROMPT), lambda s: (s, 0, 0))],
        out_shape=[jax.ShapeDtypeStruct((N_EXPERTS, N_PROMPT_SEQ * CAP_PROMPT, D), BF16),
                   jax.ShapeDtypeStruct((N_EXPERTS, N_PROMPT_SEQ * CAP_PROMPT, 1), F32),
                   jax.ShapeDtypeStruct((N_PROMPT_SEQ, N_EXPERTS, T_PROMPT), F32)],
        scratch_shapes=[pltpu.VMEM((N_EXPERTS, T_PROMPT), F32)],
        compiler_params=_cparams(("arbitrary",), VMEM_LIMIT),
        name="prompt_route",
    )(logits, h2)


def _prompt_combine_kernel(x_ref, y_ref, rank_ref, o_ref):
    onehot = _slot_onehot(rank_ref[0]).astype(BF16)
    y = y_ref[...].reshape(PROMPT_SLOTS, D)
    hi = y.astype(BF16)
    lo = (y - hi.astype(F32)).astype(BF16)
    tn = (((0,), (0,)), ((), ()))
    o_ref[...] = x_ref[...] + (lax.dot_general(onehot, hi, tn, preferred_element_type=F32)
                               + lax.dot_general(onehot, lo, tn, preferred_element_type=F32))


def _prompt_combine_call(x, y_p, rank):
    return pl.pallas_call(
        _prompt_combine_kernel,
        grid=(N_PROMPT_SEQ,),
        in_specs=[pl.BlockSpec((T_PROMPT, D), lambda s: (s, 0)),
                  pl.BlockSpec((N_EXPERTS, CAP_PROMPT, D), lambda s: (0, s, 0)),
                  pl.BlockSpec((1, N_EXPERTS, T_PROMPT), lambda s: (s, 0, 0))],
        out_specs=pl.BlockSpec((T_PROMPT, D), lambda s: (s, 0)),
        out_shape=jax.ShapeDtypeStruct((ROWS, D), F32),
        input_output_aliases={0: 0},
        compiler_params=_cparams(("arbitrary",), VMEM_LIMIT),
        name="prompt_combine",
    )(x, y_p, rank)


def _final_norm_kernel(x_ref, g_ref, o_ref):
    o_ref[...] = _rms(x_ref[...]) * g_ref[...]


def _final_norm_call(x, g, row0, n_rows):
    tm = 512
    return pl.pallas_call(
        _final_norm_kernel,
        grid=(n_rows // tm,),
        in_specs=[pl.BlockSpec((tm, D), lambda i: (row0 // tm + i, 0)),
                  pl.BlockSpec((1, D), lambda i: (0, 0))],
        out_specs=pl.BlockSpec((tm, D), lambda i: (i, 0)),
        out_shape=jax.ShapeDtypeStruct((n_rows, D), F32),
        compiler_params=_cparams(("arbitrary",), VMEM_LIMIT),
        name="final_norm",
    )(x, g.reshape(1, D))


def _rope_tables():
    tpos = np.arange(T_SAMPLE)
    lane = np.arange(R_DH)
    pos = np.where(lane[None, :] < R_DH // 2, (tpos // GRID_W)[:, None], (tpos % GRID_W)[:, None])
    n_freq = R_DH // 4
    freqs = np.power(np.float32(ROPE_BASE), -np.arange(n_freq, dtype=np.float32) / np.float32(n_freq))
    ang = pos.astype(np.float32) * freqs[lane % n_freq][None, :]
    first = ((lane % (R_DH // 2)) < n_freq)[None, :]
    cos, sin = np.cos(ang), np.sin(ang)
    sin = np.where(first, -sin, sin)
    return tuple(jnp.asarray(a, F32) for a in (cos, sin, np.ascontiguousarray(cos.T), np.ascontiguousarray(sin.T)))


def _route_sample(logits):
    aff = jax.nn.softmax(logits[ROWS_PROMPT:, :N_EXPERTS], axis=-1)
    gs, is_ = lax.top_k(aff.reshape(N_SAMPLE_SEQ, T_SAMPLE, N_EXPERTS).swapaxes(1, 2), CAP_SAMPLE)
    is_ = is_ + ROWS_PROMPT + (jnp.arange(N_SAMPLE_SEQ) * T_SAMPLE)[:, None, None]
    rows_s = is_.swapaxes(0, 1).reshape(N_EXPERTS, -1)
    return rows_s, gs.swapaxes(0, 1).reshape(N_EXPERTS, -1)


def _layer(x, mod, p, states, rope_tabs, stacked, layer, prev_states):
    mq, mk, mv, mo, mg, lx, lz, rq, rk, rv, rg = jnp.split(
        p['w_in'].T, [1024, 2048, 3072, 4096, 4112, 5136, 6160, 7184, 8208, 9232], axis=0)
    w_main_t = jnp.concatenate([mq, mv, mo, lx, lz, rq, rv, rg], axis=0).astype(BF16)
    w_kt = jnp.concatenate([mk.astype(BF16) * (M_DH ** -0.5), rk.astype(BF16)], axis=0)
    w_gate = jnp.pad(mg.T, ((0, 0), (0, LANES - N_GATE_COLS))).astype(BF16)

    h, gates = _norm1_call(x, p['norm1_g'], mod, w_gate)
    proj = _inproj_call(h, w_main_t)
    proj_t = _inproj_t_call(h, w_kt)

    gb = gates[:, :N_GATE_COLS] + p['mlstm_gate_bias'].reshape(1, N_GATE_COLS)
    grow = gb.reshape(ROWS, 4, M_HEADS).transpose(2, 1, 0)
    grow = jnp.pad(grow, ((0, 0), (0, SUBLANES - 4), (0, 0)))

    sm_c, sm_n, sm_m, s_lh, s_rs = states
    prompt_kw = dict(t=T_PROMPT, n_seq=N_PROMPT_SEQ, row_block0=0, layer=layer, emit_state=True)
    sample_kw = dict(t=T_SAMPLE, n_seq=N_SAMPLE_SEQ, row_block0=ROWS_PROMPT // T_SAMPLE, layer=layer,
                     emit_state=False)
    pc, pn, pm, plh, prs = prev_states

    caug0 = jnp.concatenate(
        [sm_c, jnp.broadcast_to(sm_n[..., None], sm_n.shape + (LANES,))], axis=-1)
    m0 = jnp.broadcast_to(sm_m[..., None, None], (N_SAMPLE_SEQ, 2, M_HEADS, 1, LANES))
    (ym,) = _mlstm_call(proj, proj_t, grow, p['mlstm_norm_g'], (caug0, m0), [None], **sample_kw)
    ym, new_c, new_n, new_m = _mlstm_call(proj, proj_t, grow, p['mlstm_norm_g'], None,
                                          [ym, pc, pn, pm], **prompt_kw)

    lru_p = dict(conv_w=p['lru_conv_w'], conv_b=p['lru_conv_b'], wr=p['lru_wr'], wi=p['lru_wi'],
                 br=p['lru_br'], bi=p['lru_bi'], lam=p['lru_lambda'])
    (yl,) = _lru_call(proj, lru_p, s_lh, [None], **sample_kw)
    yl, new_lh = _lru_call(proj, lru_p, None, [yl, plh], **prompt_kw)

    decay = jnp.broadcast_to(p['ret_decay'].T[:, :, None], (R_HEADS, 2, LANES))
    (yr,) = _ret_call(proj, proj_t, decay, p['ret_norm_g'], rope_tabs, s_rs, [None], **sample_kw)
    yr, new_rs = _ret_call(proj, proj_t, decay, p['ret_norm_g'], None, None, [yr, prs], **prompt_kw)

    rw = jnp.pad(p['router_w'], ((0, 0), (0, LANES - N_EXPERTS)))
    rwh = rw.astype(BF16)
    rwl = (rw - rwh.astype(F32)).astype(BF16)
    xn, h2, logits = _merge_call(h, ym, yl, yr, x, mod, p['w_merge'].astype(BF16), p['b_merge'],
                                 p['w_branch'].astype(BF16), p['w_out'].astype(BF16), p['norm2_g'],
                                 rwh, rwl)

    xs_p, gv_p, rank_p = _prompt_route_call(logits, h2)
    rows_s, gv_s = _route_sample(logits)
    xs_s = h2.at[rows_s].get(mode='promise_in_bounds')
    gv = jnp.concatenate([gv_p, gv_s[..., None]], axis=1)
    y_p, y_s = _expert_call(xs_p, xs_s, gv, mod, stacked['exp_w1'], stacked['exp_w3'],
                            stacked['exp_w2'], layer)
    x_out = xn.at[rows_s.reshape(-1)].add(y_s.reshape(-1, D), mode='promise_in_bounds')
    x_out = _prompt_combine_call(x_out, y_p, rank_p)
    return x_out, (new_c, new_n, new_m, new_lh, new_rs)


def kernel(x_prompt, x_sample, c, state_mlstm_C, state_mlstm_n, state_mlstm_m, state_lru_h, state_ret_S, c_ctx, w_ada, b_ada, norm1_g, norm2_g, w_in, mlstm_gate_bias, mlstm_norm_g, lru_conv_w, lru_conv_b, lru_wr, lru_br, lru_wi, lru_bi, lru_lambda, ret_decay, ret_norm_g, w_branch, w_merge, b_merge, w_out, router_w, exp_w1, exp_w3, exp_w2, final_g):
    x = jnp.concatenate([x_prompt.reshape(ROWS_PROMPT, D), x_sample.reshape(-1, D)], axis=0)
    cond8 = jnp.concatenate([c_ctx[None, :], c, jnp.zeros((8 - 1 - N_SAMPLE_SEQ, D), F32)], axis=0)
    mod_all = _ada_call(cond8, w_ada, b_ada).reshape(DEPTH, 8, 6, D)
    rope_tabs = _rope_tables()
    stacked = dict(exp_w1=exp_w1, exp_w3=exp_w3, exp_w2=exp_w2)

    new_states = (None,) * 5
    for l in range(DEPTH):
        p = dict(norm1_g=norm1_g[l], norm2_g=norm2_g[l], w_in=w_in[l],
                 mlstm_gate_bias=mlstm_gate_bias[l], mlstm_norm_g=mlstm_norm_g[l],
                 lru_conv_w=lru_conv_w[l], lru_conv_b=lru_conv_b[l], lru_wr=lru_wr[l],
                 lru_br=lru_br[l], lru_wi=lru_wi[l], lru_bi=lru_bi[l], lru_lambda=lru_lambda[l],
                 ret_decay=ret_decay[l], ret_norm_g=ret_norm_g[l], w_branch=w_branch[l],
                 w_merge=w_merge[l], b_merge=b_merge[l], w_out=w_out[l], router_w=router_w[l])
        states = (state_mlstm_C[:, l], state_mlstm_n[:, l], state_mlstm_m[:, l],
                  state_lru_h[:, l], state_ret_S[:, l])
        x, new_states = _layer(x, mod_all[l], p, states, rope_tabs, stacked, l, new_states)

    y_prompt = _final_norm_call(x, final_g, 0, ROWS_PROMPT).reshape(N_PROMPT_SEQ, T_PROMPT, D)
    y_sample = _final_norm_call(x, final_g, ROWS_PROMPT, ROWS - ROWS_PROMPT).reshape(
        N_SAMPLE_SEQ, T_SAMPLE, D)
    new_c, new_n, new_m, new_lh, new_rs = new_states
    return (y_prompt, y_sample, new_c, new_n[:, :, :, :, 0, :], new_m[:, :, :, :, 0, 0], new_lh, new_rs)
```

```python
import functools

import jax
import jax.numpy as jnp
import numpy as np
from jax import lax
from jax.experimental import pallas as pl
from jax.experimental.pallas import tpu as pltpu

F32 = jnp.float32
BF16 = jnp.bfloat16

D = 1024
DEPTH = 2
N_PROMPT_SEQ = 16
T_PROMPT = 256
N_SAMPLE_SEQ = 2
T_SAMPLE = 4096
ROWS_PROMPT = N_PROMPT_SEQ * T_PROMPT
ROWS = ROWS_PROMPT + N_SAMPLE_SEQ * T_SAMPLE
GROUP_ROWS = 4096
GRID_W = 64
CHUNK = 256
EPS = 1e-6
F32_TINY = 1.1754944e-38
M_HEADS = 4
M_DH = 256
R_HEADS = 8
R_DH = 128
L_BLOCKS = 8
L_BW = 128
L_C = 8.0
ROPE_BASE = 10000.0
N_EXPERTS = 16
CAP_PROMPT = 2 * T_PROMPT // N_EXPERTS
CAP_SAMPLE = 2 * T_SAMPLE // N_EXPERTS
ROWS_PER_EXPERT = N_PROMPT_SEQ * CAP_PROMPT + N_SAMPLE_SEQ * CAP_SAMPLE
N_GATE_COLS = 16
LANES = 128
SUBLANES = 8
VMEM_LIMIT = 56 * 2 ** 20

COL_MQ, COL_MV, COL_MO, COL_LX, COL_LZ, COL_RQ, COL_RV, COL_RG = (i * D for i in range(8))
D_MAIN = 8 * D
ROW_MK, ROW_RK = 0, D
D_KT = 2 * D


def _cparams(sem, vmem=None):
    return pltpu.CompilerParams(dimension_semantics=sem, vmem_limit_bytes=vmem)


def _dot(a, b):
    return jnp.dot(a, b, preferred_element_type=F32)


def _dot_nt(a, b):
    return lax.dot_general(a, b, (((1,), (1,)), ((), ())), preferred_element_type=F32)


def _split3(x):
    a = x.astype(BF16)
    r = x - a.astype(F32)
    b = r.astype(BF16)
    c = (r - b.astype(F32)).astype(BF16)
    return a, b, c


def _log_sigmoid(x):
    return jnp.minimum(x, 0.0) - jnp.log1p(jnp.exp(-jnp.abs(x)))


def _sigmoid(x):
    return 0.5 * jnp.tanh(0.5 * x) + 0.5


def _rms(x):
    return x * lax.rsqrt(jnp.mean(x * x, axis=-1, keepdims=True) + EPS)


def _chunk(c):
    return pl.ds(pl.multiple_of(c * CHUNK, CHUNK), CHUNK)


def _alias_prev(args, in_specs, prev):
    aliases = {}
    for out_idx, arr in enumerate(prev):
        if arr is not None:
            aliases[len(args)] = out_idx
            args.append(arr)
            in_specs.append(pl.BlockSpec(memory_space=pl.ANY))
    return aliases, len(aliases)


def _ada_kernel(c_ref, w_ref, b_ref, o_ref):
    c = c_ref[...]
    s = (c * jax.nn.sigmoid(c)).astype(BF16)
    o_ref[0] = _dot(s, w_ref[0].astype(BF16)) + b_ref[0]


def _ada_call(cond8, w_ada, b_ada):
    tn = 1536
    return pl.pallas_call(
        _ada_kernel,
        grid=(DEPTH, 6 * D // tn),
        in_specs=[pl.BlockSpec((8, D), lambda l, j: (0, 0)),
                  pl.BlockSpec((1, D, tn), lambda l, j: (l, 0, j)),
                  pl.BlockSpec((1, 1, tn), lambda l, j: (l, 0, j))],
        out_specs=pl.BlockSpec((1, 8, tn), lambda l, j: (l, 0, j)),
        out_shape=jax.ShapeDtypeStruct((DEPTH, 8, 6 * D), F32),
        compiler_params=_cparams(("arbitrary", "arbitrary"), VMEM_LIMIT),
        name="ada",
    )(cond8, w_ada, b_ada.reshape(DEPTH, 1, 6 * D))


def _norm1_kernel(x_ref, g_ref, mod_ref, wg_ref, h_ref, gate_ref):
    y = _rms(x_ref[...]) * g_ref[...]
    h = (y * (1.0 + mod_ref[0, 1:2, :]) + mod_ref[0, 0:1, :]).astype(BF16)
    h_ref[...] = h
    gate_ref[...] = _dot(h, wg_ref[...])


def _norm1_call(x, g, mod, w_gate):
    tm = 512
    return pl.pallas_call(
        _norm1_kernel,
        grid=(ROWS // tm,),
        in_specs=[pl.BlockSpec((tm, D), lambda i: (i, 0)),
                  pl.BlockSpec((1, D), lambda i: (0, 0)),
                  pl.BlockSpec((1, 6, D), lambda i: (i * tm // GROUP_ROWS, 0, 0)),
                  pl.BlockSpec((D, LANES), lambda i: (0, 0))],
        out_specs=[pl.BlockSpec((tm, D), lambda i: (i, 0)),
                   pl.BlockSpec((tm, LANES), lambda i: (i, 0))],
        out_shape=[jax.ShapeDtypeStruct((ROWS, D), BF16),
                   jax.ShapeDtypeStruct((ROWS, LANES), F32)],
        compiler_params=_cparams(("arbitrary",), VMEM_LIMIT),
        name="norm1",
    )(x, g.reshape(1, D), mod, w_gate)


def _mm_kernel(x_ref, wt_ref, o_ref):
    o_ref[...] = _dot_nt(x_ref[...], wt_ref[...]).astype(o_ref.dtype)


def _inproj_call(h, w_main_t):
    tm, tn = 1024, 1024
    return pl.pallas_call(
        _mm_kernel,
        grid=(D_MAIN // tn, ROWS // tm),
        in_specs=[pl.BlockSpec((tm, D), lambda j, i: (i, 0)),
                  pl.BlockSpec((tn, D), lambda j, i: (j, 0))],
        out_specs=pl.BlockSpec((tm, tn), lambda j, i: (i, j)),
        out_shape=jax.ShapeDtypeStruct((ROWS, D_MAIN), BF16),
        compiler_params=_cparams(("arbitrary", "arbitrary"), VMEM_LIMIT),
        name="inproj",
    )(h, w_main_t)


def _mm_t_kernel(wt_ref, x_ref, o_ref):
    o_ref[...] = _dot_nt(wt_ref[...], x_ref[...]).astype(o_ref.dtype)


def _inproj_t_call(h, w_kt):
    tm, tn = 1024, 1024
    return pl.pallas_call(
        _mm_t_kernel,
        grid=(D_KT // tn, ROWS // tm),
        in_specs=[pl.BlockSpec((tn, D), lambda j, i: (j, 0)),
                  pl.BlockSpec((tm, D), lambda j, i: (i, 0))],
        out_specs=pl.BlockSpec((tn, tm), lambda j, i: (j, i)),
        out_shape=jax.ShapeDtypeStruct((D_KT, ROWS), BF16),
        compiler_params=_cparams(("arbitrary", "arbitrary"), VMEM_LIMIT),
        name="inproj_t",
    )(w_kt, h)


M_AUG = M_DH + LANES
M_GROUP_SHORT = 2


def _mlstm_kernel(*refs, n_chunks, has_init, emit_state, n_prev):
    q_ref, kt_ref, v_ref, o_ref, g_ref, ng_ref = refs[:6]
    pos = 6
    if has_init:
        c0_ref, m0_ref = refs[pos:pos + 2]
        pos += 2
    pos += n_prev
    y_ref = refs[pos]
    pos += 1
    if emit_state:
        co_ref, no_ref, mo_ref = refs[pos:pos + 3]
        pos += 3
    caug_sc, m_sc, hf_sc, hb_sc, gp_mp, gp_bc, gp_row = refs[pos:]
    n_group = caug_sc.shape[0]

    zero_state = (not has_init) and n_chunks == 1
    if has_init:
        for g in range(n_group):
            caug_sc[g] = c0_ref[0, :, g]
            m_sc[g] = m0_ref[0, :, g]
    else:
        if not zero_state:
            caug_sc[...] = jnp.zeros_like(caug_sc)
        m_sc[...] = jnp.zeros_like(m_sc)

    ri = lax.broadcasted_iota(jnp.int32, (CHUNK, CHUNK), 0)
    ci = lax.broadcasted_iota(jnp.int32, (CHUNK, CHUNK), 1)
    r8 = lax.broadcasted_iota(jnp.int32, (SUBLANES, CHUNK), 0)
    ones_b = jnp.ones((CHUNK, LANES), BF16)
    allowed = [ri >= ci, ci >= ri]
    tri = [jnp.where(ri <= ci, 1.0, 0.0).astype(BF16), jnp.where(ri >= ci, 1.0, 0.0).astype(BF16)]

    def gate_pass(c, carry):
        rows = _chunk(c)
        for g in range(n_group):
            g8 = g_ref[g, :, rows]
            for d in range(2):
                i_idx, f_idx = (0, 1) if d == 0 else (2, 3)
                ls = jnp.where(r8 == f_idx, _log_sigmoid(g8), g8)
                cum = sum(_dot(p, tri[d]) for p in _split3(ls))
                b_row = cum[f_idx:f_idx + 1]
                g_row = ls[i_idx:i_idx + 1] - b_row
                gmat = jnp.where(allowed[d], jnp.broadcast_to(g_row, (CHUNK, CHUNK)), -jnp.inf)
                mp_col = jnp.max(gmat, axis=-1, keepdims=True)
                gp_bc[g, d, c] = jnp.broadcast_to(b_row, (LANES, CHUNK)).T
                gp_mp[g, d, c] = jnp.broadcast_to(mp_col, (CHUNK, LANES))
                b_last_11 = b_row[:, CHUNK - 1:CHUNK] if d == 0 else b_row[:, 0:1]
                log_k = b_last_11 + g_row
                a_11 = jnp.max(log_k, axis=-1, keepdims=True)
                gp_row[g, d, c] = jnp.where(
                    r8 == 0, g_row, jnp.where(
                        r8 == 1, jnp.exp(log_k - a_11), jnp.where(r8 == 2, b_last_11, a_11)))
        return carry

    lax.fori_loop(0, n_chunks, gate_pass, 0, unroll=min(n_chunks, 8))

    def step(g, d, c, m_t):
        rows = _chunk(c)
        hcols = slice(g * M_DH, (g + 1) * M_DH)
        mp = gp_mp[g, d, c]
        bc = gp_bc[g, d, c]
        row_t = gp_row[g, d, c]
        gmat = jnp.where(allowed[d], jnp.broadcast_to(row_t[0:1], (CHUNK, CHUNK)), -jnp.inf)
        e1 = jnp.exp(gmat - jnp.concatenate([mp] * (CHUNK // LANES), axis=1))

        q = q_ref[rows, hcols]
        kt = kt_ref[hcols, rows]
        vaug = jnp.concatenate([v_ref[rows, hcols], ones_b], axis=1)
        s1 = _dot(q, kt) * e1
        sv1 = _dot(s1.astype(BF16), vaug)
        kw1 = (kt.astype(F32) * row_t[1:2]).astype(BF16)
        u1 = _dot(kw1, vaug)
        b_last = jnp.broadcast_to(row_t[2:3, 0:LANES], (SUBLANES, LANES))
        a_t = jnp.broadcast_to(row_t[3:4, 0:LANES], (SUBLANES, LANES))

        m_b = jnp.broadcast_to(m_t[0:1, :], (CHUNK, LANES))
        mi = jnp.maximum(m_b, mp)
        r = jnp.exp(mp - mi)
        w_prev = jnp.exp(m_b - mi)
        floor = jnp.exp(-(bc + mi))
        if zero_state:
            qc = None
            den = r * sv1[:, M_DH:]
        else:
            qc = _dot(q, caug_sc[g, d].astype(BF16))
            den = r * sv1[:, M_DH:] + w_prev * qc[:, M_DH:]
        inv = 1.0 / jnp.maximum(jnp.abs(den), floor)
        h_sc = hf_sc if d == 0 else hb_sc
        for blk in range(M_DH // LANES):
            cols = slice(blk * LANES, (blk + 1) * LANES)
            num = r * sv1[:, cols]
            if not zero_state:
                num = num + w_prev * qc[:, cols]
            h_sc[rows, slice(g * M_DH + blk * LANES, g * M_DH + (blk + 1) * LANES)] = num * inv

        m_new = jnp.maximum(b_last + m_t, a_t)
        w_c = jnp.exp(b_last + m_t - m_new)[0:1, :]
        w_u = jnp.exp(a_t - m_new)[0:1, :]
        for blk in range(M_AUG // LANES):
            cols = slice(blk * LANES, (blk + 1) * LANES)
            if zero_state:
                caug_sc[g, d, :, cols] = w_u * u1[:, cols]
            else:
                caug_sc[g, d, :, cols] = w_c * caug_sc[g, d, :, cols] + w_u * u1[:, cols]
        return m_new

    def scan_body(c, carry):
        new = []
        for g in range(n_group):
            new.append(step(g, 0, c, carry[2 * g]))
            new.append(step(g, 1, n_chunks - 1 - c, carry[2 * g + 1]))
        return tuple(new)

    m_init = tuple(jnp.broadcast_to(m_sc[g, d], (SUBLANES, LANES))
                   for g in range(n_group) for d in range(2))
    m_fin = lax.fori_loop(0, n_chunks, scan_body, m_init, unroll=min(n_chunks, 2))
    for g in range(n_group):
        for d in range(2):
            m_sc[g, d] = m_fin[2 * g + d][0:1, :]

    def out_body(c, carry):
        rows = _chunk(c)
        for g in range(n_group):
            hcols = slice(g * M_DH, (g + 1) * M_DH)
            y = _rms(hf_sc[rows, hcols] + hb_sc[rows, hcols]) * ng_ref[:, hcols]
            y_ref[rows, hcols] = (y * _sigmoid(o_ref[rows, hcols].astype(F32))).astype(y_ref.dtype)
        return carry

    lax.fori_loop(0, n_chunks, out_body, 0, unroll=min(n_chunks, 2))

    if emit_state:
        for g in range(n_group):
            co_ref[0, 0, :, g] = caug_sc[g, :, :, 0:M_DH]
            for d in range(2):
                no_ref[0, 0, d, g] = caug_sc[g, d, :, M_DH:].T[0:1, :]
            mo_ref[0, 0, :, g] = m_sc[g]


def _mlstm_call(proj, proj_t, grow, norm_g, state, prev, *, t, n_seq, row_block0, layer, emit_state):
    has_init = state is not None
    group = M_GROUP_SHORT if t <= CHUNK else 1
    gw = group * M_DH
    big = dict(pipeline_mode=pl.Buffered(1)) if t * gw * 2 > 2 ** 21 else {}

    def col(base):
        return lambda s, h: (row_block0 + s, base // gw + h)

    st_caug = pl.BlockSpec((1, 2, group, M_DH, M_AUG), lambda s, h: (s, 0, h, 0, 0))
    st_m = pl.BlockSpec((1, 2, group, 1, LANES), lambda s, h: (s, 0, h, 0, 0))
    so_c = pl.BlockSpec((1, 1, 2, group, M_DH, M_DH), lambda s, h: (s, layer, 0, h, 0, 0))
    so_n = pl.BlockSpec((1, 1, 2, group, 1, M_DH), lambda s, h: (s, layer, 0, h, 0, 0))
    so_m = pl.BlockSpec((1, 1, 2, group, 1, LANES), lambda s, h: (s, layer, 0, h, 0, 0))
    in_specs = [pl.BlockSpec((t, gw), col(COL_MQ), **big),
                pl.BlockSpec((gw, t), lambda s, h: (ROW_MK // gw + h, row_block0 + s), **big),
                pl.BlockSpec((t, gw), col(COL_MV), **big),
                pl.BlockSpec((t, gw), col(COL_MO), **big),
                pl.BlockSpec((group, SUBLANES, t), lambda s, h: (h, 0, row_block0 + s)),
                pl.BlockSpec((1, gw), lambda s, h: (0, h))]
    args = [proj, proj_t, proj, proj, grow, norm_g.reshape(1, D)]
    if has_init:
        in_specs += [st_caug, st_m]
        args += list(state)
    out_specs = [pl.BlockSpec((t, gw), lambda s, h: (row_block0 + s, h))]
    out_shape = [jax.ShapeDtypeStruct((ROWS, D), BF16)]
    if emit_state:
        out_specs += [so_c, so_n, so_m]
        out_shape += [jax.ShapeDtypeStruct((n_seq, DEPTH, 2, M_HEADS, M_DH, M_DH), F32),
                      jax.ShapeDtypeStruct((n_seq, DEPTH, 2, M_HEADS, 1, M_DH), F32),
                      jax.ShapeDtypeStruct((n_seq, DEPTH, 2, M_HEADS, 1, LANES), F32)]
    aliases, n_prev = _alias_prev(args, in_specs, prev)
    return pl.pallas_call(
        functools.partial(_mlstm_kernel, n_chunks=t // CHUNK, has_init=has_init,
                          emit_state=emit_state, n_prev=n_prev),
        grid=(n_seq, M_HEADS // group),
        in_specs=in_specs,
        out_specs=out_specs,
        out_shape=out_shape,
        input_output_aliases=aliases,
        scratch_shapes=[pltpu.VMEM((group, 2, M_DH, M_AUG), F32),
                        pltpu.VMEM((group, 2, 1, LANES), F32),
                        pltpu.VMEM((t, gw), F32),
                        pltpu.VMEM((t, gw), F32),
                        pltpu.VMEM((group, 2, t // CHUNK, CHUNK, LANES), F32),
                        pltpu.VMEM((group, 2, t // CHUNK, CHUNK, LANES), F32),
                        pltpu.VMEM((group, 2, t // CHUNK, SUBLANES, CHUNK), F32)],
        compiler_params=_cparams(("arbitrary", "arbitrary"), VMEM_LIMIT),
        name="mlstm",
    )(*args)


R_GROUP_SHORT, R_GROUP_LONG = 4, 2

def _ret_kernel(*refs, n_chunks, rope, has_init, emit_state, n_prev):
    q_ref, kt_ref, v_ref, g_ref, dec_ref, ng_ref = refs[:6]
    pos = 6
    if rope:
        cos_ref, sin_ref, cos_t_ref, sin_t_ref = refs[pos:pos + 4]
        pos += 4
    if has_init:
        s0_ref = refs[pos]
        pos += 1
    pos += n_prev
    y_ref = refs[pos]
    pos += 1
    if emit_state:
        so_ref = refs[pos]
        pos += 1
    s_sc, of_sc, ob_sc, ktr_sc = refs[pos:pos + 4]
    qr_sc = refs[pos + 4] if rope else None
    n_group = s_sc.shape[0]

    zero_state = (not has_init) and n_chunks == 1
    if has_init:
        for g in range(n_group):
            s_sc[g] = s0_ref[0, :, g]
    elif not zero_state:
        s_sc[...] = jnp.zeros_like(s_sc)

    ri = lax.broadcasted_iota(jnp.int32, (CHUNK, CHUNK), 0)
    ci = lax.broadcasted_iota(jnp.int32, (CHUNK, CHUNK), 1)
    quarter = R_DH // 4
    if rope:
        fr = lax.broadcasted_iota(jnp.int32, (R_DH, R_DH), 0)
        fc = lax.broadcasted_iota(jnp.int32, (R_DH, R_DH), 1)
        partner = fc + jnp.where((fc % (2 * quarter)) < quarter, quarter, -quarter)
        perm_b = jnp.where(fr == partner, 1.0, 0.0).astype(BF16)

    def prep_body(c, carry):
        rows = _chunk(c)
        for g in range(n_group):
            hcols = slice(g * R_DH, (g + 1) * R_DH)
            ktf = kt_ref[hcols, rows].astype(F32) * (R_DH ** -0.5)
            if rope:
                swapped = jnp.concatenate([ktf[quarter:2 * quarter], ktf[0:quarter],
                                           ktf[3 * quarter:], ktf[2 * quarter:3 * quarter]], axis=0)
                ktf = ktf * cos_t_ref[:, rows] + swapped * sin_t_ref[:, rows]
                q = q_ref[rows, hcols]
                qr = q.astype(F32) * cos_ref[rows, :] + _dot(q, perm_b) * sin_ref[rows, :]
                qr_sc[rows, hcols] = qr.astype(BF16)
            ktr_sc[hcols, rows] = ktf.astype(BF16)
        return carry

    lax.fori_loop(0, n_chunks, prep_body, 0, unroll=min(n_chunks, 4))
    qsrc = qr_sc if rope else q_ref

    lane = lax.broadcasted_iota(jnp.int32, (1, CHUNK), 1).astype(F32)
    row_f = lax.broadcasted_iota(jnp.int32, (CHUNK, R_DH), 0).astype(F32)
    consts = []
    for gd in range(2 * n_group):
        g, d = divmod(gd, 2)
        lg = -jnp.exp(dec_ref[g, d:d + 1, :])
        lg11 = lg[:, 0:1]
        diff = (ri - ci) if d == 0 else (ci - ri)
        causal = diff >= 0
        dmat = jnp.where(causal, jnp.exp(lg11 * jnp.where(causal, diff, 0).astype(F32)), 0.0)
        if d == 0:
            q_dec = jnp.exp(lg * (row_f + 1.0))
            k_dec = jnp.exp(lg11 * (CHUNK - 1.0 - lane))
        else:
            q_dec = jnp.exp(lg * (CHUNK - row_f))
            k_dec = jnp.exp(lg11 * lane)
        g_chunk = jnp.exp(lg * float(CHUNK))
        consts.append((dmat, q_dec, k_dec, g_chunk))

    def step(g, d, c):
        dmat, q_dec, k_dec, g_chunk = consts[2 * g + d]
        rows = _chunk(c)
        hcols = slice(g * R_DH, (g + 1) * R_DH)
        q = qsrc[rows, hcols]
        kt = ktr_sc[hcols, rows]
        v = v_ref[rows, hcols]
        s = _dot(q, kt) * dmat
        o = _dot(s.astype(BF16), v)
        if not zero_state:
            o = o + q_dec * _dot(q, s_sc[g, d].astype(BF16))
        if d == 0:
            of_sc[rows, hcols] = o
        else:
            ob_sc[rows, hcols] = o
        kd = (kt.astype(F32) * k_dec).astype(BF16)
        if zero_state:
            s_sc[g, d] = _dot(kd, v)
        else:
            s_sc[g, d] = g_chunk * s_sc[g, d] + _dot(kd, v)

    def scan_body(c, carry):
        for g in range(n_group):
            step(g, 0, c)
            step(g, 1, n_chunks - 1 - c)
        return carry

    lax.fori_loop(0, n_chunks, scan_body, 0, unroll=min(n_chunks, 2))

    def out_body(c, carry):
        rows = _chunk(c)
        for g in range(n_group):
            hcols = slice(g * R_DH, (g + 1) * R_DH)
            y = _rms(of_sc[rows, hcols] + ob_sc[rows, hcols]) * ng_ref[:, hcols]
            gate = g_ref[rows, hcols].astype(F32)
            y_ref[rows, hcols] = (y * (gate * _sigmoid(gate))).astype(y_ref.dtype)
        return carry

    lax.fori_loop(0, n_chunks, out_body, 0, unroll=min(n_chunks, 4))

    if emit_state:
        for g in range(n_group):
            so_ref[0, 0, :, g] = s_sc[g]


def _ret_call(proj, proj_t, decay, norm_g, rope_tabs, state, prev, *, t, n_seq, row_block0, layer, emit_state):
    has_init = state is not None
    rope = rope_tabs is not None

    group = R_GROUP_SHORT if t <= CHUNK else R_GROUP_LONG
    gw = group * R_DH

    def col(base):
        return lambda s, h: (row_block0 + s, base // gw + h)

    st_s = pl.BlockSpec((1, 2, group, R_DH, R_DH), lambda s, h: (s, 0, h, 0, 0))
    in_specs = [pl.BlockSpec((t, gw), col(COL_RQ)),
                pl.BlockSpec((gw, t), lambda s, h: (ROW_RK // gw + h, row_block0 + s)),
                pl.BlockSpec((t, gw), col(COL_RV)),
                pl.BlockSpec((t, gw), col(COL_RG)),
                pl.BlockSpec((group, 2, LANES), lambda s, h: (h, 0, 0)),
                pl.BlockSpec((1, gw), lambda s, h: (0, h))]
    args = [proj, proj_t, proj, proj, decay, norm_g.reshape(1, D)]
    if rope:
        in_specs += [pl.BlockSpec((t, R_DH), lambda s, h: (0, 0))] * 2
        in_specs += [pl.BlockSpec((R_DH, t), lambda s, h: (0, 0))] * 2
        args += list(rope_tabs)
    if has_init:
        in_specs.append(st_s)
        args.append(state)
    out_specs = [pl.BlockSpec((t, gw), lambda s, h: (row_block0 + s, h))]
    out_shape = [jax.ShapeDtypeStruct((ROWS, D), BF16)]
    if emit_state:
        out_specs.append(pl.BlockSpec((1, 1, 2, group, R_DH, R_DH), lambda s, h: (s, layer, 0, h, 0, 0)))
        out_shape.append(jax.ShapeDtypeStruct((n_seq, DEPTH, 2, R_HEADS, R_DH, R_DH), F32))
    aliases, n_prev = _alias_prev(args, in_specs, prev)
    scratch = [pltpu.VMEM((group, 2, R_DH, R_DH), F32),
               pltpu.VMEM((t, gw), F32),
               pltpu.VMEM((t, gw), F32),
               pltpu.VMEM((gw, t), BF16)]
    if rope:
        scratch.append(pltpu.VMEM((t, gw), BF16))
    return pl.pallas_call(
        functools.partial(_ret_kernel, n_chunks=t // CHUNK, rope=rope, has_init=has_init,
                          emit_state=emit_state, n_prev=n_prev),
        grid=(n_seq, R_HEADS // group),
        in_specs=in_specs,
        out_specs=out_specs,
        out_shape=out_shape,
        input_output_aliases=aliases,
        scratch_shapes=scratch,
        compiler_params=_cparams(("arbitrary", "arbitrary"), VMEM_LIMIT),
        name="retention",
    )(*args)


LRU_SLAB = 64
L_GROUP_SHORT = 2


def _tile_scan(a, b, reverse):
    row = lax.broadcasted_iota(jnp.int32, (SUBLANES, L_BW), 0)
    for k in (1, 2, 4):
        if reverse:
            keep = row < SUBLANES - k
            shift = SUBLANES - k
        else:
            keep = row >= k
            shift = k
        a_sh = jnp.where(keep, pltpu.roll(a, shift, 0), 1.0)
        b_sh = jnp.where(keep, pltpu.roll(b, shift, 0), 0.0)
        b = a * b_sh + b
        a = a * a_sh
    return a, b


def _lru_kernel(*refs, t, has_init, emit_state, n_prev):
    lx_ref, lz_ref, cw_ref, cb_ref, wr_ref, wi_ref, br_ref, bi_ref, lam_ref = refs[:9]
    pos = 9
    if has_init:
        h0_ref = refs[pos]
        pos += 1
    pos += n_prev
    y_ref = refs[pos]
    pos += 1
    if emit_state:
        hfin_ref = refs[pos]
        pos += 1
    xpad_sc, a_sc, b_sc = refs[pos:]
    rc = min(t, 256)
    gw = xpad_sc.shape[1]
    n_group = gw // L_BW

    def cs(g):
        return slice(g * L_BW, (g + 1) * L_BW)

    xpad_sc[0:8, :] = jnp.zeros((8, gw), F32)
    xpad_sc[t + 8:t + 16, :] = jnp.zeros((8, gw), F32)

    def pad_body(c, carry):
        r0 = pl.multiple_of(c * rc, rc)
        xpad_sc[pl.ds(r0 + 8, rc), :] = lx_ref[pl.ds(r0, rc), :].astype(F32)
        return carry

    lax.fori_loop(0, t // rc, pad_body, 0)

    wr = [[(0.5 * wr_ref[d, g]).astype(BF16) for d in range(2)] for g in range(n_group)]
    wi = [[(0.5 * wi_ref[d, g]).astype(BF16) for d in range(2)] for g in range(n_group)]
    half_br = 0.5 * br_ref[...]
    half_bi = 0.5 * bi_ref[...]
    lam = lam_ref[...]
    half_c = -0.5 * L_C * (jnp.maximum(-lam, 0.0) + jnp.log1p(jnp.exp(-jnp.abs(lam))))

    def gate_body(c, carry):
        r0 = pl.multiple_of(c * rc, rc)
        n = rc + 16
        for g in range(n_group):
            xe = xpad_sc[pl.ds(r0, n), cs(g)]
            u = (cb_ref[:, cs(g)] + cw_ref[0:1, cs(g)] * pltpu.roll(xe, 2, 0)[8:8 + rc]
                 + cw_ref[1:2, cs(g)] * pltpu.roll(xe, 1, 0)[8:8 + rc]
                 + cw_ref[2:3, cs(g)] * xe[8:8 + rc]
                 + cw_ref[3:4, cs(g)] * pltpu.roll(xe, n - 1, 0)[8:8 + rc])
            ub = u.astype(BF16)
            half_u = 0.5 * u
            for d in range(2):
                hc = half_c[d:d + 1, cs(g)]
                tr = jnp.tanh(_dot(ub, wr[g][d]) + half_br[d:d + 1, cs(g)])
                ti = jnp.tanh(_dot(ub, wi[g][d]) + half_bi[d:d + 1, cs(g)])
                log_a = hc * tr + hc
                a = jnp.exp(log_a)
                z = jnp.tanh(log_a) * (-1.0 - a * a)
                mult = z * lax.rsqrt(jnp.maximum(z, F32_TINY))
                a_sc[d, pl.ds(r0, rc), cs(g)] = a
                b_sc[d, pl.ds(r0, rc), cs(g)] = mult * (ti * half_u + half_u)
        return carry

    lax.fori_loop(0, t // rc, gate_body, 0, unroll=min(t // rc, 2))

    n_slabs = t // LRU_SLAB
    tiles = LRU_SLAB // SUBLANES

    row8 = lax.broadcasted_iota(jnp.int32, (SUBLANES, L_BW), 0)

    def bcast_row(x, r):
        return jnp.broadcast_to(x[r:r + 1, :], (SUBLANES, L_BW))

    def slab_scan(g, d, r0, c_in):
        reverse = d == 1
        edge = 0 if reverse else SUBLANES - 1
        local = []
        spa = jnp.ones((SUBLANES, L_BW), F32)
        shl = jnp.zeros((SUBLANES, L_BW), F32)
        for k in range(tiles):
            rows = pl.ds(r0 + k * SUBLANES, SUBLANES)
            pa, hl = _tile_scan(a_sc[d, rows, cs(g)], b_sc[d, rows, cs(g)], reverse)
            local.append((rows, pa, hl))
            spa = jnp.where(row8 == k, bcast_row(pa, edge), spa)
            shl = jnp.where(row8 == k, bcast_row(hl, edge), shl)
        cpa, chl = _tile_scan(spa, shl, reverse)
        after = chl + cpa * c_in
        for k, (rows, pa, hl) in enumerate(local):
            prev = k + 1 if reverse else k - 1
            cin = c_in if (prev < 0 or prev >= tiles) else bcast_row(after, prev)
            b_sc[d, rows, cs(g)] = hl + pa * cin
        return bcast_row(after, 0 if reverse else tiles - 1)

    def scan_body(c, carry):
        new = []
        for g in range(n_group):
            new.append(slab_scan(g, 0, pl.multiple_of(c * LRU_SLAB, LRU_SLAB), carry[2 * g]))
            new.append(slab_scan(g, 1, pl.multiple_of((n_slabs - 1 - c) * LRU_SLAB, LRU_SLAB),
                                 carry[2 * g + 1]))
        return tuple(new)

    if has_init:
        c0 = tuple(jnp.broadcast_to(h0_ref[0, d:d + 1, cs(g)], (SUBLANES, L_BW))
                   for g in range(n_group) for d in range(2))
    else:
        c0 = tuple(jnp.zeros((SUBLANES, L_BW), F32) for _ in range(2 * n_group))
    c_fin = lax.fori_loop(0, n_slabs, scan_body, c0, unroll=4)
    if emit_state:
        for g in range(n_group):
            for d in range(2):
                hfin_ref[0, 0, d:d + 1, cs(g)] = c_fin[2 * g + d][0:1, :]

    def out_body(c, carry):
        rows = pl.ds(pl.multiple_of(c * rc, rc), rc)
        z = lz_ref[rows, :].astype(F32)
        gelu = 0.5 * z * (1.0 + jnp.tanh(0.7978845608028654 * (z + 0.044715 * (z * z * z))))
        y_ref[rows, :] = ((b_sc[0, rows, :] + b_sc[1, rows, :]) * gelu).astype(y_ref.dtype)
        return carry

    lax.fori_loop(0, t // rc, out_body, 0)


def _lru_call(proj, p, state, prev, *, t, n_seq, row_block0, layer, emit_state):
    has_init = state is not None

    group = L_GROUP_SHORT if t <= 256 else 1
    gw = group * L_BW

    def col(base):
        return lambda s, b: (row_block0 + s, base // gw + b)

    vec2 = pl.BlockSpec((2, gw), lambda s, b: (0, b))
    st = pl.BlockSpec((1, 2, gw), lambda s, b: (s, 0, b))
    in_specs = [pl.BlockSpec((t, gw), col(COL_LX)),
                pl.BlockSpec((t, gw), col(COL_LZ)),
                pl.BlockSpec((4, gw), lambda s, b: (0, b)),
                pl.BlockSpec((1, gw), lambda s, b: (0, b)),
                pl.BlockSpec((2, group, L_BW, L_BW), lambda s, b: (0, b, 0, 0)),
                pl.BlockSpec((2, group, L_BW, L_BW), lambda s, b: (0, b, 0, 0)),
                vec2, vec2, vec2]
    args = [proj, proj, p['conv_w'], p['conv_b'].reshape(1, D), p['wr'], p['wi'],
            p['br'], p['bi'], p['lam']]
    if has_init:
        in_specs.append(st)
        args.append(state)
    out_specs = [pl.BlockSpec((t, gw), lambda s, b: (row_block0 + s, b))]
    out_shape = [jax.ShapeDtypeStruct((ROWS, D), BF16)]
    if emit_state:
        out_specs.append(pl.BlockSpec((1, 1, 2, gw), lambda s, b: (s, layer, 0, b)))
        out_shape.append(jax.ShapeDtypeStruct((n_seq, DEPTH, 2, D), F32))
    aliases, n_prev = _alias_prev(args, in_specs, prev)
    return pl.pallas_call(
        functools.partial(_lru_kernel, t=t, has_init=has_init, emit_state=emit_state, n_prev=n_prev),
        grid=(n_seq, L_BLOCKS // group),
        in_specs=in_specs,
        out_specs=out_specs,
        out_shape=out_shape,
        input_output_aliases=aliases,
        scratch_shapes=[pltpu.VMEM((t + 16, gw), F32),
                        pltpu.VMEM((2, t, gw), F32),
                        pltpu.VMEM((2, t, gw), F32)],
        compiler_params=_cparams(("arbitrary", "arbitrary"), VMEM_LIMIT),
        name="rglru",
    )(*args)


def _merge_kernel(h_ref, ym_ref, yl_ref, yr_ref, x_ref, mod_ref, wm_ref, bm_ref, wb_ref, wo_ref,
                  n2_ref, rwh_ref, rwl_ref, xo_ref, h2_ref, lg_ref):
    h = h_ref[...]
    merged = None
    for k, y_ref in enumerate((ym_ref, yl_ref, yr_ref)):
        gate = jax.nn.sigmoid(_dot(h, wm_ref[:, k * D:(k + 1) * D]) + bm_ref[:, k * D:(k + 1) * D])
        term = gate * _dot(y_ref[...], wb_ref[k])
        merged = term if merged is None else merged + term
    mix = _dot(merged.astype(BF16), wo_ref[...])
    xn = x_ref[...] + mod_ref[0, 2:3, :] * mix
    xo_ref[...] = xn
    hn = _rms(xn) * n2_ref[...] * (1.0 + mod_ref[0, 4:5, :]) + mod_ref[0, 3:4, :]
    hi = hn.astype(BF16)
    h2_ref[...] = hi
    lo = (hn - hi.astype(F32)).astype(BF16)
    lg_ref[...] = _dot(hi, rwh_ref[...]) + _dot(lo, rwh_ref[...]) + _dot(hi, rwl_ref[...])


def _merge_call(h, ym, yl, yr, x, mod, wm, bm, wb, wo, n2g, rwh, rwl):
    tm = 256
    row = lambda i: (i, 0)
    const2 = lambda i: (0, 0)
    return pl.pallas_call(
        _merge_kernel,
        grid=(ROWS // tm,),
        in_specs=[pl.BlockSpec((tm, D), row),
                  pl.BlockSpec((tm, D), row),
                  pl.BlockSpec((tm, D), row),
                  pl.BlockSpec((tm, D), row),
                  pl.BlockSpec((tm, D), row),
                  pl.BlockSpec((1, 6, D), lambda i: (i * tm // GROUP_ROWS, 0, 0)),
                  pl.BlockSpec((D, 3 * D), const2),
                  pl.BlockSpec((1, 3 * D), const2),
                  pl.BlockSpec((3, D, D), lambda i: (0, 0, 0)),
                  pl.BlockSpec((D, D), const2),
                  pl.BlockSpec((1, D), const2),
                  pl.BlockSpec((D, LANES), const2),
                  pl.BlockSpec((D, LANES), const2)],
        out_specs=[pl.BlockSpec((tm, D), row),
                   pl.BlockSpec((tm, D), row),
                   pl.BlockSpec((tm, LANES), row)],
        out_shape=[jax.ShapeDtypeStruct((ROWS, D), F32),
                   jax.ShapeDtypeStruct((ROWS, D), BF16),
                   jax.ShapeDtypeStruct((ROWS, LANES), F32)],
        compiler_params=_cparams(("arbitrary",), VMEM_LIMIT),
        name="merge",
    )(h, ym, yl, yr, x, mod, wm, bm.reshape(1, 3 * D), wb, wo, n2g.reshape(1, D), rwh, rwl)


EXPERT_TM = 512
assert N_PROMPT_SEQ * CAP_PROMPT == EXPERT_TM and CAP_SAMPLE == EXPERT_TM


def _expert_kernel(xp_ref, xs_ref, g_ref, mod_ref, w1_ref, w3_ref, w2_ref, yp_ref, ys_ref,
                   w1_sc, w3_sc, w2_sc):
    m = pl.program_id(1)

    @pl.when(m == 0)
    def _():
        w1_sc[...] = w1_ref[0, 0].astype(BF16)
        w3_sc[...] = w3_ref[0, 0].astype(BF16)
        w2_sc[...] = w2_ref[0, 0].astype(BF16)

    xs = jnp.where(m == 0, xp_ref[0], xs_ref[0])
    a = _dot(xs, w1_sc[...])
    b = _dot(xs, w3_sc[...])
    mid = (a * jax.nn.sigmoid(a) * b).astype(BF16)

    def down(y_ref):
        y_ref[0] = (_dot(mid, w2_sc[...]) * g_ref[0]) * mod_ref[0, 5:6, :]

    pl.when(m == 0)(functools.partial(down, yp_ref))
    pl.when(m > 0)(functools.partial(down, ys_ref))


def _expert_call(xs_p, xs_s, gv, mod, w1, w3, w2, layer):
    tm = EXPERT_TM
    wspec = pl.BlockSpec((1, 1, D, D), lambda e, m: (layer, e, 0, 0))
    p_spec = pl.BlockSpec((1, tm, D), lambda e, m: (e, 0, 0))
    s_spec = pl.BlockSpec((1, tm, D), lambda e, m: (e, jnp.maximum(m - 1, 0), 0))
    return pl.pallas_call(
        _expert_kernel,
        grid=(N_EXPERTS, ROWS_PER_EXPERT // tm),
        in_specs=[p_spec, s_spec,
                  pl.BlockSpec((1, tm, 1), lambda e, m: (e, m, 0)),
                  pl.BlockSpec((1, 6, D), lambda e, m: (m, 0, 0)),
                  wspec, wspec, wspec],
        out_specs=[p_spec, s_spec],
        out_shape=[jax.ShapeDtypeStruct((N_EXPERTS, tm, D), F32),
                   jax.ShapeDtypeStruct((N_EXPERTS, N_SAMPLE_SEQ * tm, D), F32)],
        scratch_shapes=[pltpu.VMEM((D, D), BF16)] * 3,
        compiler_params=_cparams(("arbitrary", "arbitrary"), VMEM_LIMIT),
        name="experts",
    )(xs_p, xs_s, gv, mod, w1, w3, w2)


PROMPT_SLOTS = N_EXPERTS * CAP_PROMPT


def _slot_onehot(rank_rows):
    slot = lax.broadcasted_iota(jnp.int32, (CAP_PROMPT, T_PROMPT), 0).astype(F32)
    return jnp.concatenate(
        [jnp.where(slot == rank_rows[e:e + 1, :], 1.0, 0.0) for e in range(N_EXPERTS)], axis=0)


def _prompt_route_kernel(lg_ref, h_ref, xs_ref, gv_ref, rank_ref, rank_sc):
    lane = lax.broadcasted_iota(jnp.int32, (T_PROMPT, LANES), 1)
    x = jnp.where(lane < N_EXPERTS, lg_ref[...], -jnp.inf)
    e = jnp.exp(x - jnp.max(x, axis=-1, keepdims=True))
    aff = e / jnp.sum(e, axis=-1, keepdims=True)
    aff_t = aff.T
    ti = lax.broadcasted_iota(jnp.int32, (T_PROMPT, T_PROMPT), 0)
    tj = lax.broadcasted_iota(jnp.int32, (T_PROMPT, T_PROMPT), 1)
    earlier = jnp.where(ti < tj, 1.0, 0.0)
    for ex in range(N_EXPERTS):
        a_row = aff_t[ex:ex + 1, :]
        a_col = aff[:, ex:ex + 1]
        ahead = jnp.where(a_col > a_row, 1.0, jnp.where(a_col == a_row, earlier, 0.0))
        rank_sc[ex:ex + 1, :] = jnp.sum(ahead, axis=0, keepdims=True)
    rank = rank_sc[...]
    onehot = _slot_onehot(rank)
    xs_ref[...] = _dot(onehot.astype(BF16), h_ref[...]).astype(xs_ref.dtype).reshape(
        N_EXPERTS, CAP_PROMPT, D)
    for ex in range(N_EXPERTS):
        blk = onehot[ex * CAP_PROMPT:(ex + 1) * CAP_PROMPT, :]
        gv_ref[ex] = jnp.sum(blk * aff_t[ex:ex + 1, :], axis=-1, keepdims=True)
    rank_ref[0] = rank


def _prompt_route_call(logits, h2):
    return pl.pallas_call(
        _prompt_route_kernel,
        grid=(N_PROMPT_SEQ,),
        in_specs=[pl.BlockSpec((T_PROMPT, LANES), lambda s: (s, 0)),
                  pl.BlockSpec((T_PROMPT, D), lambda s: (s, 0))],
        out_specs=[pl.BlockSpec((N_EXPERTS, CAP_PROMPT, D), lambda s: (0, s, 0)),
                   pl.BlockSpec((N_EXPERTS, CAP_PROMPT, 1), lambda s: (0, s, 0)),
                   pl.BlockSpec((1, N_EXPERTS, T_PROMPT), lambda s: (s, 0, 0))],
        out_shape=[jax.ShapeDtypeStruct((N_EXPERTS, N_PROMPT_SEQ * CAP_PROMPT, D), BF16),
                   jax.ShapeDtypeStruct((N_EXPERTS, N_PROMPT_SEQ * CAP_PROMPT, 1), F32),
                   jax.ShapeDtypeStruct((N_PROMPT_SEQ, N_EXPERTS, T_PROMPT), F32)],
        scratch_shapes=[pltpu.VMEM((N_EXPERTS, T_PROMPT), F32)],
        compiler_params=_cparams(("arbitrary",), VMEM_LIMIT),
        name="prompt_route",
    )(logits, h2)


def _prompt_combine_kernel(x_ref, y_ref, rank_ref, o_ref):
    onehot = _slot_onehot(rank_ref[0]).astype(BF16)
    y = y_ref[...].reshape(PROMPT_SLOTS, D)
    hi = y.astype(BF16)
    lo = (y - hi.astype(F32)).astype(BF16)
    tn = (((0,), (0,)), ((), ()))
    o_ref[...] = x_ref[...] + (lax.dot_general(onehot, hi, tn, preferred_element_type=F32)
                               + lax.dot_general(onehot, lo, tn, preferred_element_type=F32))


def _prompt_combine_call(x, y_p, rank):
    return pl.pallas_call(
        _prompt_combine_kernel,
        grid=(N_PROMPT_SEQ,),
        in_specs=[pl.BlockSpec((T_PROMPT, D), lambda s: (s, 0)),
                  pl.BlockSpec((N_EXPERTS, CAP_PROMPT, D), lambda s: (0, s, 0)),
                  pl.BlockSpec((1, N_EXPERTS, T_PROMPT), lambda s: (s, 0, 0))],
        out_specs=pl.BlockSpec((T_PROMPT, D), lambda s: (s, 0)),
        out_shape=jax.ShapeDtypeStruct((ROWS, D), F32),
        input_output_aliases={0: 0},
        compiler_params=_cparams(("arbitrary",), VMEM_LIMIT),
        name="prompt_combine",
    )(x, y_p, rank)


def _final_norm_kernel(x_ref, g_ref, o_ref):
    o_ref[...] = _rms(x_ref[...]) * g_ref[...]


def _final_norm_call(x, g, row0, n_rows):
    tm = 512
    return pl.pallas_call(
        _final_norm_kernel,
        grid=(n_rows // tm,),
        in_specs=[pl.BlockSpec((tm, D), lambda i: (row0 // tm + i, 0)),
                  pl.BlockSpec((1, D), lambda i: (0, 0))],
        out_specs=pl.BlockSpec((tm, D), lambda i: (i, 0)),
        out_shape=jax.ShapeDtypeStruct((n_rows, D), F32),
        compiler_params=_cparams(("arbitrary",), VMEM_LIMIT),
        name="final_norm",
    )(x, g.reshape(1, D))


def _rope_tables():
    tpos = np.arange(T_SAMPLE)
    lane = np.arange(R_DH)
    pos = np.where(lane[None, :] < R_DH // 2, (tpos // GRID_W)[:, None], (tpos % GRID_W)[:, None])
    n_freq = R_DH // 4
    freqs = np.power(np.float32(ROPE_BASE), -np.arange(n_freq, dtype=np.float32) / np.float32(n_freq))
    ang = pos.astype(np.float32) * freqs[lane % n_freq][None, :]
    first = ((lane % (R_DH // 2)) < n_freq)[None, :]
    cos, sin = np.cos(ang), np.sin(ang)
    sin = np.where(first, -sin, sin)
    return tuple(jnp.asarray(a, F32) for a in (cos, sin, np.ascontiguousarray(cos.T), np.ascontiguousarray(sin.T)))


def _route_sample(logits):
    aff = jax.nn.softmax(logits[ROWS_PROMPT:, :N_EXPERTS], axis=-1)
    gs, is_ = lax.top_k(aff.reshape(N_SAMPLE_SEQ, T_SAMPLE, N_EXPERTS).swapaxes(1, 2), CAP_SAMPLE)
    is_ = is_ + ROWS_PROMPT + (jnp.arange(N_SAMPLE_SEQ) * T_SAMPLE)[:, None, None]
    rows_s = is_.swapaxes(0, 1).reshape(N_EXPERTS, -1)
    return rows_s, gs.swapaxes(0, 1).reshape(N_EXPERTS, -1)


def _layer(x, mod, p, states, rope_tabs, stacked, layer, prev_states):
    mq, mk, mv, mo, mg, lx, lz, rq, rk, rv, rg = jnp.split(
        p['w_in'].T, [1024, 2048, 3072, 4096, 4112, 5136, 6160, 7184, 8208, 9232], axis=0)
    w_main_t = jnp.concatenate([mq, mv, mo, lx, lz, rq, rv, rg], axis=0).astype(BF16)
    w_kt = jnp.concatenate([mk.astype(BF16) * (M_DH ** -0.5), rk.astype(BF16)], axis=0)
    w_gate = jnp.pad(mg.T, ((0, 0), (0, LANES - N_GATE_COLS))).astype(BF16)

    h, gates = _norm1_call(x, p['norm1_g'], mod, w_gate)
    proj = _inproj_call(h, w_main_t)
    proj_t = _inproj_t_call(h, w_kt)

    gb = gates[:, :N_GATE_COLS] + p['mlstm_gate_bias'].reshape(1, N_GATE_COLS)
    grow = gb.reshape(ROWS, 4, M_HEADS).transpose(2, 1, 0)
    grow = jnp.pad(grow, ((0, 0), (0, SUBLANES - 4), (0, 0)))

    sm_c, sm_n, sm_m, s_lh, s_rs = states
    prompt_kw = dict(t=T_PROMPT, n_seq=N_PROMPT_SEQ, row_block0=0, layer=layer, emit_state=True)
    sample_kw = dict(t=T_SAMPLE, n_seq=N_SAMPLE_SEQ, row_block0=ROWS_PROMPT // T_SAMPLE, layer=layer,
                     emit_state=False)
    pc, pn, pm, plh, prs = prev_states

    caug0 = jnp.concatenate(
        [sm_c, jnp.broadcast_to(sm_n[..., None], sm_n.shape + (LANES,))], axis=-1)
    m0 = jnp.broadcast_to(sm_m[..., None, None], (N_SAMPLE_SEQ, 2, M_HEADS, 1, LANES))
    (ym,) = _mlstm_call(proj, proj_t, grow, p['mlstm_norm_g'], (caug0, m0), [None], **sample_kw)
    ym, new_c, new_n, new_m = _mlstm_call(proj, proj_t, grow, p['mlstm_norm_g'], None,
                                          [ym, pc, pn, pm], **prompt_kw)

    lru_p = dict(conv_w=p['lru_conv_w'], conv_b=p['lru_conv_b'], wr=p['lru_wr'], wi=p['lru_wi'],
                 br=p['lru_br'], bi=p['lru_bi'], lam=p['lru_lambda'])
    (yl,) = _lru_call(proj, lru_p, s_lh, [None], **sample_kw)
    yl, new_lh = _lru_call(proj, lru_p, None, [yl, plh], **prompt_kw)

    decay = jnp.broadcast_to(p['ret_decay'].T[:, :, None], (R_HEADS, 2, LANES))
    (yr,) = _ret_call(proj, proj_t, decay, p['ret_norm_g'], rope_tabs, s_rs, [None], **sample_kw)
    yr, new_rs = _ret_call(proj, proj_t, decay, p['ret_norm_g'], None, None, [yr, prs], **prompt_kw)

    rw = jnp.pad(p['router_w'], ((0, 0), (0, LANES - N_EXPERTS)))
    rwh = rw.astype(BF16)
    rwl = (rw - rwh.astype(F32)).astype(BF16)
    xn, h2, logits = _merge_call(h, ym, yl, yr, x, mod, p['w_merge'].astype(BF16), p['b_merge'],
                                 p['w_branch'].astype(BF16), p['w_out'].astype(BF16), p['norm2_g'],
                                 rwh, rwl)

    xs_p, gv_p, rank_p = _prompt_route_call(logits, h2)
    rows_s, gv_s = _route_sample(logits)
    xs_s = h2.at[rows_s].get(mode='promise_in_bounds')
    gv = jnp.concatenate([gv_p, gv_s[..., None]], axis=1)
    y_p, y_s = _expert_call(xs_p, xs_s, gv, mod, stacked['exp_w1'], stacked['exp_w3'],
                            stacked['exp_w2'], layer)
    x_out = xn.at[rows_s.reshape(-1)].add(y_s.reshape(-1, D), mode='promise_in_bounds')
    x_out = _prompt_combine_call(x_out, y_p, rank_p)
    return x_out, (new_c, new_n, new_m, new_lh, new_rs)


def kernel(x_prompt, x_sample, c, state_mlstm_C, state_mlstm_n, state_mlstm_m, state_lru_h, state_ret_S, c_ctx, w_ada, b_ada, norm1_g, norm2_g, w_in, mlstm_gate_bias, mlstm_norm_g, lru_conv_w, lru_conv_b, lru_wr, lru_br, lru_wi, lru_bi, lru_lambda, ret_decay, ret_norm_g, w_branch, w_merge, b_merge, w_out, router_w, exp_w1, exp_w3, exp_w2, final_g):
    x = jnp.concatenate([x_prompt.reshape(ROWS_PROMPT, D), x_sample.reshape(-1, D)], axis=0)
    cond8 = jnp.concatenate([c_ctx[None, :], c, jnp.zeros((8 - 1 - N_SAMPLE_SEQ, D), F32)], axis=0)
    mod_all = _ada_call(cond8, w_ada, b_ada).reshape(DEPTH, 8, 6, D)
    rope_tabs = _rope_tables()
    stacked = dict(exp_w1=exp_w1, exp_w3=exp_w3, exp_w2=exp_w2)

    new_states = (None,) * 5
    for l in range(DEPTH):
        p = dict(norm1_g=norm1_g[l], norm2_g=norm2_g[l], w_in=w_in[l],
                 mlstm_gate_bias=mlstm_gate_bias[l], mlstm_norm_g=mlstm_norm_g[l],
                 lru_conv_w=lru_conv_w[l], lru_conv_b=lru_conv_b[l], lru_wr=lru_wr[l],
                 lru_br=lru_br[l], lru_wi=lru_wi[l], lru_bi=lru_bi[l], lru_lambda=lru_lambda[l],
                 ret_decay=ret_decay[l], ret_norm_g=ret_norm_g[l], w_branch=w_branch[l],
                 w_merge=w_merge[l], b_merge=b_merge[l], w_out=w_out[l], router_w=router_w[l])
        states = (state_mlstm_C[:, l], state_mlstm_n[:, l], state_mlstm_m[:, l],
                  state_lru_h[:, l], state_ret_S[:, l])
        x, new_states = _layer(x, mod_all[l], p, states, rope_tabs, stacked, l, new_states)

    y_prompt = _final_norm_call(x, final_g, 0, ROWS_PROMPT).reshape(N_PROMPT_SEQ, T_PROMPT, D)
    y_sample = _final_norm_call(x, final_g, ROWS_PROMPT, ROWS - ROWS_PROMPT).reshape(
        N_SAMPLE_SEQ, T_SAMPLE, D)
    new_c, new_n, new_m, new_lh, new_rs = new_states
    return (y_prompt, y_sample, new_c, new_n[:, :, :, :, 0, :], new_m[:, :, :, :, 0, 0], new_lh, new_rs)
```

```python
import functools

import jax
import jax.numpy as jnp
import numpy as np
from jax import lax
from jax.experimental import pallas as pl
from jax.experimental.pallas import tpu as pltpu

F32 = jnp.float32
BF16 = jnp.bfloat16

D = 1024
DEPTH = 2
N_PROMPT_SEQ = 16
T_PROMPT = 256
N_SAMPLE_SEQ = 2
T_SAMPLE = 4096
ROWS_PROMPT = N_PROMPT_SEQ * T_PROMPT
ROWS = ROWS_PROMPT + N_SAMPLE_SEQ * T_SAMPLE
GROUP_ROWS = 4096
GRID_W = 64
CHUNK = 256
EPS = 1e-6
F32_TINY = 1.1754944e-38
M_HEADS = 4
M_DH = 256
R_HEADS = 8
R_DH = 128
L_BLOCKS = 8
L_BW = 128
L_C = 8.0
ROPE_BASE = 10000.0
N_EXPERTS = 16
CAP_PROMPT = 2 * T_PROMPT // N_EXPERTS
CAP_SAMPLE = 2 * T_SAMPLE // N_EXPERTS
ROWS_PER_EXPERT = N_PROMPT_SEQ * CAP_PROMPT + N_SAMPLE_SEQ * CAP_SAMPLE
N_GATE_COLS = 16
LANES = 128
SUBLANES = 8
VMEM_LIMIT = 56 * 2 ** 20

COL_MQ, COL_MV, COL_MO, COL_LX, COL_LZ, COL_RQ, COL_RV, COL_RG = (i * D for i in range(8))
D_MAIN = 8 * D
ROW_MK, ROW_RK = 0, D
D_KT = 2 * D


def _cparams(sem, vmem=None):
    return pltpu.CompilerParams(dimension_semantics=sem, vmem_limit_bytes=vmem)


def _dot(a, b):
    return jnp.dot(a, b, preferred_element_type=F32)


def _dot_nt(a, b):
    return lax.dot_general(a, b, (((1,), (1,)), ((), ())), preferred_element_type=F32)


def _split3(x):
    a = x.astype(BF16)
    r = x - a.astype(F32)
    b = r.astype(BF16)
    c = (r - b.astype(F32)).astype(BF16)
    return a, b, c


def _log_sigmoid(x):
    return jnp.minimum(x, 0.0) - jnp.log1p(jnp.exp(-jnp.abs(x)))


def _sigmoid(x):
    return 0.5 * jnp.tanh(0.5 * x) + 0.5


def _rms(x):
    return x * lax.rsqrt(jnp.mean(x * x, axis=-1, keepdims=True) + EPS)


def _chunk(c):
    return pl.ds(pl.multiple_of(c * CHUNK, CHUNK), CHUNK)


def _alias_prev(args, in_specs, prev):
    aliases = {}
    for out_idx, arr in enumerate(prev):
        if arr is not None:
            aliases[len(args)] = out_idx
            args.append(arr)
            in_specs.append(pl.BlockSpec(memory_space=pl.ANY))
    return aliases, len(aliases)


def _ada_kernel(c_ref, w_ref, b_ref, o_ref):
    c = c_ref[...]
    s = (c * jax.nn.sigmoid(c)).astype(BF16)
    o_ref[0] = _dot(s, w_ref[0].astype(BF16)) + b_ref[0]


def _ada_call(cond8, w_ada, b_ada):
    tn = 1536
    return pl.pallas_call(
        _ada_kernel,
        grid=(DEPTH, 6 * D // tn),
        in_specs=[pl.BlockSpec((8, D), lambda l, j: (0, 0)),
                  pl.BlockSpec((1, D, tn), lambda l, j: (l, 0, j)),
                  pl.BlockSpec((1, 1, tn), lambda l, j: (l, 0, j))],
        out_specs=pl.BlockSpec((1, 8, tn), lambda l, j: (l, 0, j)),
        out_shape=jax.ShapeDtypeStruct((DEPTH, 8, 6 * D), F32),
        compiler_params=_cparams(("arbitrary", "arbitrary"), VMEM_LIMIT),
        name="ada",
    )(cond8, w_ada, b_ada.reshape(DEPTH, 1, 6 * D))


def _norm1_kernel(x_ref, g_ref, mod_ref, wg_ref, h_ref, gate_ref):
    y = _rms(x_ref[...]) * g_ref[...]
    h = (y * (1.0 + mod_ref[0, 1:2, :]) + mod_ref[0, 0:1, :]).astype(BF16)
    h_ref[...] = h
    gate_ref[...] = _dot(h, wg_ref[...])


def _norm1_call(x, g, mod, w_gate):
    tm = 512
    return pl.pallas_call(
        _norm1_kernel,
        grid=(ROWS // tm,),
        in_specs=[pl.BlockSpec((tm, D), lambda i: (i, 0)),
                  pl.BlockSpec((1, D), lambda i: (0, 0)),
                  pl.BlockSpec((1, 6, D), lambda i: (i * tm // GROUP_ROWS, 0, 0)),
                  pl.BlockSpec((D, LANES), lambda i: (0, 0))],
        out_specs=[pl.BlockSpec((tm, D), lambda i: (i, 0)),
                   pl.BlockSpec((tm, LANES), lambda i: (i, 0))],
        out_shape=[jax.ShapeDtypeStruct((ROWS, D), BF16),
                   jax.ShapeDtypeStruct((ROWS, LANES), F32)],
        compiler_params=_cparams(("arbitrary",), VMEM_LIMIT),
        name="norm1",
    )(x, g.reshape(1, D), mod, w_gate)


def _mm_kernel(x_ref, wt_ref, o_ref):
    o_ref[...] = _dot_nt(x_ref[...], wt_ref[...]).astype(o_ref.dtype)


def _inproj_call(h, w_main_t):
    tm, tn = 1024, 1024
    return pl.pallas_call(
        _mm_kernel,
        grid=(D_MAIN // tn, ROWS // tm),
        in_specs=[pl.BlockSpec((tm, D), lambda j, i: (i, 0)),
                  pl.BlockSpec((tn, D), lambda j, i: (j, 0))],
        out_specs=pl.BlockSpec((tm, tn), lambda j, i: (i, j)),
        out_shape=jax.ShapeDtypeStruct((ROWS, D_MAIN), BF16),
        compiler_params=_cparams(("arbitrary", "arbitrary"), VMEM_LIMIT),
        name="inproj",
    )(h, w_main_t)


def _mm_t_kernel(wt_ref, x_ref, o_ref):
    o_ref[...] = _dot_nt(wt_ref[...], x_ref[...]).astype(o_ref.dtype)


def _inproj_t_call(h, w_kt):
    tm, tn = 1024, 1024
    return pl.pallas_call(
        _mm_t_kernel,
        grid=(D_KT // tn, ROWS // tm),
        in_specs=[pl.BlockSpec((tn, D), lambda j, i: (j, 0)),
                  pl.BlockSpec((tm, D), lambda j, i: (i, 0))],
        out_specs=pl.BlockSpec((tn, tm), lambda j, i: (j, i)),
        out_shape=jax.ShapeDtypeStruct((D_KT, ROWS), BF16),
        compiler_params=_cparams(("arbitrary", "arbitrary"), VMEM_LIMIT),
        name="inproj_t",
    )(w_kt, h)


M_AUG = M_DH + LANES
M_GROUP_SHORT = 2


def _mlstm_kernel(*refs, n_chunks, has_init, emit_state, n_prev):
    q_ref, kt_ref, v_ref, o_ref, g_ref, ng_ref = refs[:6]
    pos = 6
    if has_init:
        c0_ref, m0_ref = refs[pos:pos + 2]
        pos += 2
    pos += n_prev
    y_ref = refs[pos]
    pos += 1
    if emit_state:
        co_ref, no_ref, mo_ref = refs[pos:pos + 3]
        pos += 3
    caug_sc, m_sc, hf_sc, hb_sc, gp_mp, gp_bc, gp_row = refs[pos:]
    n_group = caug_sc.shape[0]

    zero_state = (not has_init) and n_chunks == 1
    if has_init:
        for g in range(n_group):
            caug_sc[g] = c0_ref[0, :, g]
            m_sc[g] = m0_ref[0, :, g]
    else:
        if not zero_state:
            caug_sc[...] = jnp.zeros_like(caug_sc)
        m_sc[...] = jnp.zeros_like(m_sc)

    ri = lax.broadcasted_iota(jnp.int32, (CHUNK, CHUNK), 0)
    ci = lax.broadcasted_iota(jnp.int32, (CHUNK, CHUNK), 1)
    r8 = lax.broadcasted_iota(jnp.int32, (SUBLANES, CHUNK), 0)
    ones_b = jnp.ones((CHUNK, LANES), BF16)
    allowed = [ri >= ci, ci >= ri]
    tri = [jnp.where(ri <= ci, 1.0, 0.0).astype(BF16), jnp.where(ri >= ci, 1.0, 0.0).astype(BF16)]

    def gate_pass(c, carry):
        rows = _chunk(c)
        for g in range(n_group):
            g8 = g_ref[g, :, rows]
            for d in range(2):
                i_idx, f_idx = (0, 1) if d == 0 else (2, 3)
                ls = jnp.where(r8 == f_idx, _log_sigmoid(g8), g8)
                cum = sum(_dot(p, tri[d]) for p in _split3(ls))
                b_row = cum[f_idx:f_idx + 1]
                g_row = ls[i_idx:i_idx + 1] - b_row
                gmat = jnp.where(allowed[d], jnp.broadcast_to(g_row, (CHUNK, CHUNK)), -jnp.inf)
                mp_col = jnp.max(gmat, axis=-1, keepdims=True)
                gp_bc[g, d, c] = jnp.broadcast_to(b_row, (LANES, CHUNK)).T
                gp_mp[g, d, c] = jnp.broadcast_to(mp_col, (CHUNK, LANES))
                b_last_11 = b_row[:, CHUNK - 1:CHUNK] if d == 0 else b_row[:, 0:1]
                log_k = b_last_11 + g_row
                a_11 = jnp.max(log_k, axis=-1, keepdims=True)
                gp_row[g, d, c] = jnp.where(
                    r8 == 0, g_row, jnp.where(
                        r8 == 1, jnp.exp(log_k - a_11), jnp.where(r8 == 2, b_last_11, a_11)))
        return carry

    lax.fori_loop(0, n_chunks, gate_pass, 0, unroll=min(n_chunks, 8))

    def step(g, d, c, m_t):
        rows = _chunk(c)
        hcols = slice(g * M_DH, (g + 1) * M_DH)
        mp = gp_mp[g, d, c]
        bc = gp_bc[g, d, c]
        row_t = gp_row[g, d, c]
        gmat = jnp.where(allowed[d], jnp.broadcast_to(row_t[0:1], (CHUNK, CHUNK)), -jnp.inf)
        e1 = jnp.exp(gmat - jnp.concatenate([mp] * (CHUNK // LANES), axis=1))

        q = q_ref[rows, hcols]
        kt = kt_ref[hcols, rows]
        vaug = jnp.concatenate([v_ref[rows, hcols], ones_b], axis=1)
        s1 = _dot(q, kt) * e1
        sv1 = _dot(s1.astype(BF16), vaug)
        kw1 = (kt.astype(F32) * row_t[1:2]).astype(BF16)
        u1 = _dot(kw1, vaug)
        b_last = jnp.broadcast_to(row_t[2:3, 0:LANES], (SUBLANES, LANES))
        a_t = jnp.broadcast_to(row_t[3:4, 0:LANES], (SUBLANES, LANES))

        m_b = jnp.broadcast_to(m_t[0:1, :], (CHUNK, LANES))
        mi = jnp.maximum(m_b, mp)
        r = jnp.exp(mp - mi)
        w_prev = jnp.exp(m_b - mi)
        floor = jnp.exp(-(bc + mi))
        if zero_state:
            qc = None
            den = r * sv1[:, M_DH:]
        else:
            qc = _dot(q, caug_sc[g, d].astype(BF16))
            den = r * sv1[:, M_DH:] + w_prev * qc[:, M_DH:]
        inv = 1.0 / jnp.maximum(jnp.abs(den), floor)
        h_sc = hf_sc if d == 0 else hb_sc
        for blk in range(M_DH // LANES):
            cols = slice(blk * LANES, (blk + 1) * LANES)
            num = r * sv1[:, cols]
            if not zero_state:
                num = num + w_prev * qc[:, cols]
            h_sc[rows, slice(g * M_DH + blk * LANES, g * M_DH + (blk + 1) * LANES)] = num * inv

        m_new = jnp.maximum(b_last + m_t, a_t)
        w_c = jnp.exp(b_last + m_t - m_new)[0:1, :]
        w_u = jnp.exp(a_t - m_new)[0:1, :]
        for blk in range(M_AUG // LANES):
            cols = slice(blk * LANES, (blk + 1) * LANES)
            if zero_state:
                caug_sc[g, d, :, cols] = w_u * u1[:, cols]
            else:
                caug_sc[g, d, :, cols] = w_c * caug_sc[g, d, :, cols] + w_u * u1[:, cols]
        return m_new

    def scan_body(c, carry):
        new = []
        for g in range(n_group):
            new.append(step(g, 0, c, carry[2 * g]))
            new.append(step(g, 1, n_chunks - 1 - c, carry[2 * g + 1]))
        return tuple(new)

    m_init = tuple(jnp.broadcast_to(m_sc[g, d], (SUBLANES, LANES))
                   for g in range(n_group) for d in range(2))
    m_fin = lax.fori_loop(0, n_chunks, scan_body, m_init, unroll=min(n_chunks, 2))
    for g in range(n_group):
        for d in range(2):
            m_sc[g, d] = m_fin[2 * g + d][0:1, :]

    def out_body(c, carry):
        rows = _chunk(c)
        for g in range(n_group):
            hcols = slice(g * M_DH, (g + 1) * M_DH)
            y = _rms(hf_sc[rows, hcols] + hb_sc[rows, hcols]) * ng_ref[:, hcols]
            y_ref[rows, hcols] = (y * _sigmoid(o_ref[rows, hcols].astype(F32))).astype(y_ref.dtype)
        return carry

    lax.fori_loop(0, n_chunks, out_body, 0, unroll=min(n_chunks, 2))

    if emit_state:
        for g in range(n_group):
            co_ref[0, 0, :, g] = caug_sc[g, :, :, 0:M_DH]
            for d in range(2):
                no_ref[0, 0, d, g] = caug_sc[g, d, :, M_DH:].T[0:1, :]
            mo_ref[0, 0, :, g] = m_sc[g]


def _mlstm_call(proj, proj_t, grow, norm_g, state, prev, *, t, n_seq, row_block0, layer, emit_state):
    has_init = state is not None
    group = M_GROUP_SHORT if t <= CHUNK else 1
    gw = group * M_DH
    big = dict(pipeline_mode=pl.Buffered(1)) if t * gw * 2 > 2 ** 21 else {}

    def col(base):
        return lambda s, h: (row_block0 + s, base // gw + h)

    st_caug = pl.BlockSpec((1, 2, group, M_DH, M_AUG), lambda s, h: (s, 0, h, 0, 0))
    st_m = pl.BlockSpec((1, 2, group, 1, LANES), lambda s, h: (s, 0, h, 0, 0))
    so_c = pl.BlockSpec((1, 1, 2, group, M_DH, M_DH), lambda s, h: (s, layer, 0, h, 0, 0))
    so_n = pl.BlockSpec((1, 1, 2, group, 1, M_DH), lambda s, h: (s, layer, 0, h, 0, 0))
    so_m = pl.BlockSpec((1, 1, 2, group, 1, LANES), lambda s, h: (s, layer, 0, h, 0, 0))
    in_specs = [pl.BlockSpec((t, gw), col(COL_MQ), **big),
                pl.BlockSpec((gw, t), lambda s, h: (ROW_MK // gw + h, row_block0 + s), **big),
                pl.BlockSpec((t, gw), col(COL_MV), **big),
                pl.BlockSpec((t, gw), col(COL_MO), **big),
                pl.BlockSpec((group, SUBLANES, t), lambda s, h: (h, 0, row_block0 + s)),
                pl.BlockSpec((1, gw), lambda s, h: (0, h))]
    args = [proj, proj_t, proj, proj, grow, norm_g.reshape(1, D)]
    if has_init:
        in_specs += [st_caug, st_m]
        args += list(state)
    out_specs = [pl.BlockSpec((t, gw), lambda s, h: (row_block0 + s, h))]
    out_shape = [jax.ShapeDtypeStruct((ROWS, D), BF16)]
    if emit_state:
        out_specs += [so_c, so_n, so_m]
        out_shape += [jax.ShapeDtypeStruct((n_seq, DEPTH, 2, M_HEADS, M_DH, M_DH), F32),
                      jax.ShapeDtypeStruct((n_seq, DEPTH, 2, M_HEADS, 1, M_DH), F32),
                      jax.ShapeDtypeStruct((n_seq, DEPTH, 2, M_HEADS, 1, LANES), F32)]
    aliases, n_prev = _alias_prev(args, in_specs, prev)
    return pl.pallas_call(
        functools.partial(_mlstm_kernel, n_chunks=t // CHUNK, has_init=has_init,
                          emit_state=emit_state, n_prev=n_prev),
        grid=(n_seq, M_HEADS // group),
        in_specs=in_specs,
        out_specs=out_specs,
        out_shape=out_shape,
        input_output_aliases=aliases,
        scratch_shapes=[pltpu.VMEM((group, 2, M_DH, M_AUG), F32),
                        pltpu.VMEM((group, 2, 1, LANES), F32),
                        pltpu.VMEM((t, gw), F32),
                        pltpu.VMEM((t, gw), F32),
                        pltpu.VMEM((group, 2, t // CHUNK, CHUNK, LANES), F32),
                        pltpu.VMEM((group, 2, t // CHUNK, CHUNK, LANES), F32),
                        pltpu.VMEM((group, 2, t // CHUNK, SUBLANES, CHUNK), F32)],
        compiler_params=_cparams(("arbitrary", "arbitrary"), VMEM_LIMIT),
        name="mlstm",
    )(*args)


R_GROUP_SHORT, R_GROUP_LONG = 4, 2

def _ret_kernel(*refs, n_chunks, rope, has_init, emit_state, n_prev):
    q_ref, kt_ref, v_ref, g_ref, dec_ref, ng_ref = refs[:6]
    pos = 6
    if rope:
        cos_ref, sin_ref, cos_t_ref, sin_t_ref = refs[pos:pos + 4]
        pos += 4
    if has_init:
        s0_ref = refs[pos]
        pos += 1
    pos += n_prev
    y_ref = refs[pos]
    pos += 1
    if emit_state:
        so_ref = refs[pos]
        pos += 1
    s_sc, of_sc, ob_sc, ktr_sc = refs[pos:pos + 4]
    qr_sc = refs[pos + 4] if rope else None
    n_group = s_sc.shape[0]

    zero_state = (not has_init) and n_chunks == 1
    if has_init:
        for g in range(n_group):
            s_sc[g] = s0_ref[0, :, g]
    elif not zero_state:
        s_sc[...] = jnp.zeros_like(s_sc)

    ri = lax.broadcasted_iota(jnp.int32, (CHUNK, CHUNK), 0)
    ci = lax.broadcasted_iota(jnp.int32, (CHUNK, CHUNK), 1)
    quarter = R_DH // 4
    if rope:
        fr = lax.broadcasted_iota(jnp.int32, (R_DH, R_DH), 0)
        fc = lax.broadcasted_iota(jnp.int32, (R_DH, R_DH), 1)
        partner = fc + jnp.where((fc % (2 * quarter)) < quarter, quarter, -quarter)
        perm_b = jnp.where(fr == partner, 1.0, 0.0).astype(BF16)

    def prep_body(c, carry):
        rows = _chunk(c)
        for g in range(n_group):
            hcols = slice(g * R_DH, (g + 1) * R_DH)
            ktf = kt_ref[hcols, rows].astype(F32) * (R_DH ** -0.5)
            if rope:
                swapped = jnp.concatenate([ktf[quarter:2 * quarter], ktf[0:quarter],
                                           ktf[3 * quarter:], ktf[2 * quarter:3 * quarter]], axis=0)
                ktf = ktf * cos_t_ref[:, rows] + swapped * sin_t_ref[:, rows]
                q = q_ref[rows, hcols]
                qr = q.astype(F32) * cos_ref[rows, :] + _dot(q, perm_b) * sin_ref[rows, :]
                qr_sc[rows, hcols] = qr.astype(BF16)
            ktr_sc[hcols, rows] = ktf.astype(BF16)
        return carry

    lax.fori_loop(0, n_chunks, prep_body, 0, unroll=min(n_chunks, 4))
    qsrc = qr_sc if rope else q_ref

    lane = lax.broadcasted_iota(jnp.int32, (1, CHUNK), 1).astype(F32)
    row_f = lax.broadcasted_iota(jnp.int32, (CHUNK, R_DH), 0).astype(F32)
    consts = []
    for gd in range(2 * n_group):
        g, d = divmod(gd, 2)
        lg = -jnp.exp(dec_ref[g, d:d + 1, :])
        lg11 = lg[:, 0:1]
        diff = (ri - ci) if d == 0 else (ci - ri)
        causal = diff >= 0
        dmat = jnp.where(causal, jnp.exp(lg11 * jnp.where(causal, diff, 0).astype(F32)), 0.0)
        if d == 0:
            q_dec = jnp.exp(lg * (row_f + 1.0))
            k_dec = jnp.exp(lg11 * (CHUNK - 1.0 - lane))
        else:
            q_dec = jnp.exp(lg * (CHUNK - row_f))
            k_dec = jnp.exp(lg11 * lane)
        g_chunk = jnp.exp(lg * float(CHUNK))
        consts.append((dmat, q_dec, k_dec, g_chunk))

    def step(g, d, c):
        dmat, q_dec, k_dec, g_chunk = consts[2 * g + d]
        rows = _chunk(c)
        hcols = slice(g * R_DH, (g + 1) * R_DH)
        q = qsrc[rows, hcols]
        kt = ktr_sc[hcols, rows]
        v = v_ref[rows, hcols]
        s = _dot(q, kt) * dmat
        o = _dot(s.astype(BF16), v)
        if not zero_state:
            o = o + q_dec * _dot(q, s_sc[g, d].astype(BF16))
        if d == 0:
            of_sc[rows, hcols] = o
        else:
            ob_sc[rows, hcols] = o
        kd = (kt.astype(F32) * k_dec).astype(BF16)
        if zero_state:
            s_sc[g, d] = _dot(kd, v)
        else:
            s_sc[g, d] = g_chunk * s_sc[g, d] + _dot(kd, v)

    def scan_body(c, carry):
        for g in range(n_group):
            step(g, 0, c)
            step(g, 1, n_chunks - 1 - c)
        return carry

    lax.fori_loop(0, n_chunks, scan_body, 0, unroll=min(n_chunks, 2))

    def out_body(c, carry):
        rows = _chunk(c)
        for g in range(n_group):
            hcols = slice(g * R_DH, (g + 1) * R_DH)
            y = _rms(of_sc[rows, hcols] + ob_sc[rows, hcols]) * ng_ref[:, hcols]
            gate = g_ref[rows, hcols].astype(F32)
            y_ref[rows, hcols] = (y * (gate * _sigmoid(gate))).astype(y_ref.dtype)
        return carry

    lax.fori_loop(0, n_chunks, out_body, 0, unroll=min(n_chunks, 4))

    if emit_state:
        for g in range(n_group):
            so_ref[0, 0, :, g] = s_sc[g]


def _ret_call(proj, proj_t, decay, norm_g, rope_tabs, state, prev, *, t, n_seq, row_block0, layer, emit_state):
    has_init = state is not None
    rope = rope_tabs is not None

    group = R_GROUP_SHORT if t <= CHUNK else R_GROUP_LONG
    gw = group * R_DH

    def col(base):
        return lambda s, h: (row_block0 + s, base // gw + h)

    st_s = pl.BlockSpec((1, 2, group, R_DH, R_DH), lambda s, h: (s, 0, h, 0, 0))
    in_specs = [pl.BlockSpec((t, gw), col(COL_RQ)),
                pl.BlockSpec((gw, t), lambda s, h: (ROW_RK // gw + h, row_block0 + s)),
                pl.BlockSpec((t, gw), col(COL_RV)),
                pl.BlockSpec((t, gw), col(COL_RG)),
                pl.BlockSpec((group, 2, LANES), lambda s, h: (h, 0, 0)),
                pl.BlockSpec((1, gw), lambda s, h: (0, h))]
    args = [proj, proj_t, proj, proj, decay, norm_g.reshape(1, D)]
    if rope:
        in_specs += [pl.BlockSpec((t, R_DH), lambda s, h: (0, 0))] * 2
        in_specs += [pl.BlockSpec((R_DH, t), lambda s, h: (0, 0))] * 2
        args += list(rope_tabs)
    if has_init:
        in_specs.append(st_s)
        args.append(state)
    out_specs = [pl.BlockSpec((t, gw), lambda s, h: (row_block0 + s, h))]
    out_shape = [jax.ShapeDtypeStruct((ROWS, D), BF16)]
    if emit_state:
        out_specs.append(pl.BlockSpec((1, 1, 2, group, R_DH, R_DH), lambda s, h: (s, layer, 0, h, 0, 0)))
        out_shape.append(jax.ShapeDtypeStruct((n_seq, DEPTH, 2, R_HEADS, R_DH, R_DH), F32))
    aliases, n_prev = _alias_prev(args, in_specs, prev)
    scratch = [pltpu.VMEM((group, 2, R_DH, R_DH), F32),
               pltpu.VMEM((t, gw), F32),
               pltpu.VMEM((t, gw), F32),
               pltpu.VMEM((gw, t), BF16)]
    if rope:
        scratch.append(pltpu.VMEM((t, gw), BF16))
    return pl.pallas_call(
        functools.partial(_ret_kernel, n_chunks=t // CHUNK, rope=rope, has_init=has_init,
                          emit_state=emit_state, n_prev=n_prev),
        grid=(n_seq, R_HEADS // group),
        in_specs=in_specs,
        out_specs=out_specs,
        out_shape=out_shape,
        input_output_aliases=aliases,
        scratch_shapes=scratch,
        compiler_params=_cparams(("arbitrary", "arbitrary"), VMEM_LIMIT),
        name="retention",
    )(*args)


LRU_SLAB = 64
L_GROUP_SHORT = 2


def _tile_scan(a, b, reverse):
    row = lax.broadcasted_iota(jnp.int32, (SUBLANES, L_BW), 0)
    for k in (1, 2, 4):
        if reverse:
            keep = row < SUBLANES - k
            shift = SUBLANES - k
        else:
            keep = row >= k
            shift = k
        a_sh = jnp.where(keep, pltpu.roll(a, shift, 0), 1.0)
        b_sh = jnp.where(keep, pltpu.roll(b, shift, 0), 0.0)
        b = a * b_sh + b
        a = a * a_sh
    return a, b


def _lru_kernel(*refs, t, has_init, emit_state, n_prev):
    lx_ref, lz_ref, cw_ref, cb_ref, wr_ref, wi_ref, br_ref, bi_ref, lam_ref = refs[:9]
    pos = 9
    if has_init:
        h0_ref = refs[pos]
        pos += 1
    pos += n_prev
    y_ref = refs[pos]
    pos += 1
    if emit_state:
        hfin_ref = refs[pos]
        pos += 1
    xpad_sc, a_sc, b_sc = refs[pos:]
    rc = min(t, 256)
    gw = xpad_sc.shape[1]
    n_group = gw // L_BW

    def cs(g):
        return slice(g * L_BW, (g + 1) * L_BW)

    xpad_sc[0:8, :] = jnp.zeros((8, gw), F32)
    xpad_sc[t + 8:t + 16, :] = jnp.zeros((8, gw), F32)

    def pad_body(c, carry):
        r0 = pl.multiple_of(c * rc, rc)
        xpad_sc[pl.ds(r0 + 8, rc), :] = lx_ref[pl.ds(r0, rc), :].astype(F32)
        return carry

    lax.fori_loop(0, t // rc, pad_body, 0)

    wr = [[(0.5 * wr_ref[d, g]).astype(BF16) for d in range(2)] for g in range(n_group)]
    wi = [[(0.5 * wi_ref[d, g]).astype(BF16) for d in range(2)] for g in range(n_group)]
    half_br = 0.5 * br_ref[...]
    half_bi = 0.5 * bi_ref[...]
    lam = lam_ref[...]
    half_c = -0.5 * L_C * (jnp.maximum(-lam, 0.0) + jnp.log1p(jnp.exp(-jnp.abs(lam))))

    def gate_body(c, carry):
        r0 = pl.multiple_of(c * rc, rc)
        n = rc + 16
        for g in range(n_group):
            xe = xpad_sc[pl.ds(r0, n), cs(g)]
            u = (cb_ref[:, cs(g)] + cw_ref[0:1, cs(g)] * pltpu.roll(xe, 2, 0)[8:8 + rc]
                 + cw_ref[1:2, cs(g)] * pltpu.roll(xe, 1, 0)[8:8 + rc]
                 + cw_ref[2:3, cs(g)] * xe[8:8 + rc]
                 + cw_ref[3:4, cs(g)] * pltpu.roll(xe, n - 1, 0)[8:8 + rc])
            ub = u.astype(BF16)
            half_u = 0.5 * u
            for d in range(2):
                hc = half_c[d:d + 1, cs(g)]
                tr = jnp.tanh(_dot(ub, wr[g][d]) + half_br[d:d + 1, cs(g)])
                ti = jnp.tanh(_dot(ub, wi[g][d]) + half_bi[d:d + 1, cs(g)])
                log_a = hc * tr + hc
                a = jnp.exp(log_a)
                z = jnp.tanh(log_a) * (-1.0 - a * a)
                mult = z * lax.rsqrt(jnp.maximum(z, F32_TINY))
                a_sc[d, pl.ds(r0, rc), cs(g)] = a
                b_sc[d, pl.ds(r0, rc), cs(g)] = mult * (ti * half_u + half_u)
        return carry

    lax.fori_loop(0, t // rc, gate_body, 0, unroll=min(t // rc, 2))

    n_slabs = t // LRU_SLAB
    tiles = LRU_SLAB // SUBLANES

    row8 = lax.broadcasted_iota(jnp.int32, (SUBLANES, L_BW), 0)

    def bcast_row(x, r):
        return jnp.broadcast_to(x[r:r + 1, :], (SUBLANES, L_BW))

    def slab_scan(g, d, r0, c_in):
        reverse = d == 1
        edge = 0 if reverse else SUBLANES - 1
        local = []
        spa = jnp.ones((SUBLANES, L_BW), F32)
        shl = jnp.zeros((SUBLANES, L_BW), F32)
        for k in range(tiles):
            rows = pl.ds(r0 + k * SUBLANES, SUBLANES)
            pa, hl = _tile_scan(a_sc[d, rows, cs(g)], b_sc[d, rows, cs(g)], reverse)
            local.append((rows, pa, hl))
            spa = jnp.where(row8 == k, bcast_row(pa, edge), spa)
            shl = jnp.where(row8 == k, bcast_row(hl, edge), shl)
        cpa, chl = _tile_scan(spa, shl, reverse)
        after = chl + cpa * c_in
        for k, (rows, pa, hl) in enumerate(local):
            prev = k + 1 if reverse else k - 1
            cin = c_in if (prev < 0 or prev >= tiles) else bcast_row(after, prev)
            b_sc[d, rows, cs(g)] = hl + pa * cin
        return bcast_row(after, 0 if reverse else tiles - 1)

    def scan_body(c, carry):
        new = []
        for g in range(n_group):
            new.append(slab_scan(g, 0, pl.multiple_of(c * LRU_SLAB, LRU_SLAB), carry[2 * g]))
            new.append(slab_scan(g, 1, pl.multiple_of((n_slabs - 1 - c) * LRU_SLAB, LRU_SLAB),
                                 carry[2 * g + 1]))
        return tuple(new)

    if has_init:
        c0 = tuple(jnp.broadcast_to(h0_ref[0, d:d + 1, cs(g)], (SUBLANES, L_BW))
                   for g in range(n_group) for d in range(2))
    else:
        c0 = tuple(jnp.zeros((SUBLANES, L_BW), F32) for _ in range(2 * n_group))
    c_fin = lax.fori_loop(0, n_slabs, scan_body, c0, unroll=4)
    if emit_state:
        for g in range(n_group):
            for d in range(2):
                hfin_ref[0, 0, d:d + 1, cs(g)] = c_fin[2 * g + d][0:1, :]

    def out_body(c, carry):
        rows = pl.ds(pl.multiple_of(c * rc, rc), rc)
        z = lz_ref[rows, :].astype(F32)
        gelu = 0.5 * z * (1.0 + jnp.tanh(0.7978845608028654 * (z + 0.044715 * (z * z * z))))
        y_ref[rows, :] = ((b_sc[0, rows, :] + b_sc[1, rows, :]) * gelu).astype(y_ref.dtype)
        return carry

    lax.fori_loop(0, t // rc, out_body, 0)


def _lru_call(proj, p, state, prev, *, t, n_seq, row_block0, layer, emit_state):
    has_init = state is not None

    group = L_GROUP_SHORT if t <= 256 else 1
    gw = group * L_BW

    def col(base):
        return lambda s, b: (row_block0 + s, base // gw + b)

    vec2 = pl.BlockSpec((2, gw), lambda s, b: (0, b))
    st = pl.BlockSpec((1, 2, gw), lambda s, b: (s, 0, b))
    in_specs = [pl.BlockSpec((t, gw), col(COL_LX)),
                pl.BlockSpec((t, gw), col(COL_LZ)),
                pl.BlockSpec((4, gw), lambda s, b: (0, b)),
                pl.BlockSpec((1, gw), lambda s, b: (0, b)),
                pl.BlockSpec((2, group, L_BW, L_BW), lambda s, b: (0, b, 0, 0)),
                pl.BlockSpec((2, group, L_BW, L_BW), lambda s, b: (0, b, 0, 0)),
                vec2, vec2, vec2]
    args = [proj, proj, p['conv_w'], p['conv_b'].reshape(1, D), p['wr'], p['wi'],
            p['br'], p['bi'], p['lam']]
    if has_init:
        in_specs.append(st)
        args.append(state)
    out_specs = [pl.BlockSpec((t, gw), lambda s, b: (row_block0 + s, b))]
    out_shape = [jax.ShapeDtypeStruct((ROWS, D), BF16)]
    if emit_state:
        out_specs.append(pl.BlockSpec((1, 1, 2, gw), lambda s, b: (s, layer, 0, b)))
        out_shape.append(jax.ShapeDtypeStruct((n_seq, DEPTH, 2, D), F32))
    aliases, n_prev = _alias_prev(args, in_specs, prev)
    return pl.pallas_call(
        functools.partial(_lru_kernel, t=t, has_init=has_init, emit_state=emit_state, n_prev=n_prev),
        grid=(n_seq, L_BLOCKS // group),
        in_specs=in_specs,
        out_specs=out_specs,
        out_shape=out_shape,
        input_output_aliases=aliases,
        scratch_shapes=[pltpu.VMEM((t + 16, gw), F32),
                        pltpu.VMEM((2, t, gw), F32),
                        pltpu.VMEM((2, t, gw), F32)],
        compiler_params=_cparams(("arbitrary", "arbitrary"), VMEM_LIMIT),
        name="rglru",
    )(*args)


def _merge_kernel(h_ref, ym_ref, yl_ref, yr_ref, x_ref, mod_ref, wm_ref, bm_ref, wb_ref, wo_ref,
                  n2_ref, rwh_ref, rwl_ref, xo_ref, h2_ref, lg_ref):
    h = h_ref[...]
    half = D // 2
    halves = []
    for c0 in (0, half):
        merged = None
        for k, y_ref in enumerate((ym_ref, yl_ref, yr_ref)):
            cols = slice(k * D + c0, k * D + c0 + half)
            gate = jax.nn.sigmoid(_dot(h, wm_ref[:, cols]) + bm_ref[:, cols])
            term = gate * _dot(y_ref[...], wb_ref[k, :, c0:c0 + half])
            merged = term if merged is None else merged + term
        halves.append(merged.astype(BF16))
    mix = _dot(halves[0], wo_ref[0:half, :]) + _dot(halves[1], wo_ref[half:D, :])
    xn = x_ref[...] + mod_ref[0, 2:3, :] * mix
    xo_ref[...] = xn
    hn = _rms(xn) * n2_ref[...] * (1.0 + mod_ref[0, 4:5, :]) + mod_ref[0, 3:4, :]
    hi = hn.astype(BF16)
    h2_ref[...] = hi
    lo = (hn - hi.astype(F32)).astype(BF16)
    lg_ref[...] = _dot(hi, rwh_ref[...]) + _dot(lo, rwh_ref[...]) + _dot(hi, rwl_ref[...])


def _merge_call(h, ym, yl, yr, x, mod, wm, bm, wb, wo, n2g, rwh, rwl):
    tm = 256
    row = lambda i: (i, 0)
    const2 = lambda i: (0, 0)
    return pl.pallas_call(
        _merge_kernel,
        grid=(ROWS // tm,),
        in_specs=[pl.BlockSpec((tm, D), row),
                  pl.BlockSpec((tm, D), row),
                  pl.BlockSpec((tm, D), row),
                  pl.BlockSpec((tm, D), row),
                  pl.BlockSpec((tm, D), row),
                  pl.BlockSpec((1, 6, D), lambda i: (i * tm // GROUP_ROWS, 0, 0)),
                  pl.BlockSpec((D, 3 * D), const2),
                  pl.BlockSpec((1, 3 * D), const2),
                  pl.BlockSpec((3, D, D), lambda i: (0, 0, 0)),
                  pl.BlockSpec((D, D), const2),
                  pl.BlockSpec((1, D), const2),
                  pl.BlockSpec((D, LANES), const2),
                  pl.BlockSpec((D, LANES), const2)],
        out_specs=[pl.BlockSpec((tm, D), row),
                   pl.BlockSpec((tm, D), row),
                   pl.BlockSpec((tm, LANES), row)],
        out_shape=[jax.ShapeDtypeStruct((ROWS, D), F32),
                   jax.ShapeDtypeStruct((ROWS, D), BF16),
                   jax.ShapeDtypeStruct((ROWS, LANES), F32)],
        compiler_params=_cparams(("arbitrary",), VMEM_LIMIT),
        name="merge",
    )(h, ym, yl, yr, x, mod, wm, bm.reshape(1, 3 * D), wb, wo, n2g.reshape(1, D), rwh, rwl)


EXPERT_TM = 512
assert N_PROMPT_SEQ * CAP_PROMPT == EXPERT_TM and CAP_SAMPLE == EXPERT_TM


def _expert_kernel(xp_ref, xs_ref, g_ref, mod_ref, w1_ref, w3_ref, w2_ref, yp_ref, ys_ref,
                   w1_sc, w3_sc, w2_sc):
    m = pl.program_id(1)

    @pl.when(m == 0)
    def _():
        w1_sc[...] = w1_ref[0, 0].astype(BF16)
        w3_sc[...] = w3_ref[0, 0].astype(BF16)
        w2_sc[...] = w2_ref[0, 0].astype(BF16)

    xs = jnp.where(m == 0, xp_ref[0], xs_ref[0])
    a = _dot(xs, w1_sc[...])
    b = _dot(xs, w3_sc[...])
    mid = (a * jax.nn.sigmoid(a) * b).astype(BF16)

    def down(y_ref):
        y_ref[0] = (_dot(mid, w2_sc[...]) * g_ref[0]) * mod_ref[0, 5:6, :]

    pl.when(m == 0)(functools.partial(down, yp_ref))
    pl.when(m > 0)(functools.partial(down, ys_ref))


def _expert_call(xs_p, xs_s, gv, mod, w1, w3, w2, layer):
    tm = EXPERT_TM
    wspec = pl.BlockSpec((1, 1, D, D), lambda e, m: (layer, e, 0, 0))
    p_spec = pl.BlockSpec((1, tm, D), lambda e, m: (e, 0, 0))
    s_spec = pl.BlockSpec((1, tm, D), lambda e, m: (e, jnp.maximum(m - 1, 0), 0))
    return pl.pallas_call(
        _expert_kernel,
        grid=(N_EXPERTS, ROWS_PER_EXPERT // tm),
        in_specs=[p_spec, s_spec,
                  pl.BlockSpec((1, tm, 1), lambda e, m: (e, m, 0)),
                  pl.BlockSpec((1, 6, D), lambda e, m: (m, 0, 0)),
                  wspec, wspec, wspec],
        out_specs=[p_spec, s_spec],
        out_shape=[jax.ShapeDtypeStruct((N_EXPERTS, tm, D), F32),
                   jax.ShapeDtypeStruct((N_EXPERTS, N_SAMPLE_SEQ * tm, D), F32)],
        scratch_shapes=[pltpu.VMEM((D, D), BF16)] * 3,
        compiler_params=_cparams(("arbitrary", "arbitrary"), VMEM_LIMIT),
        name="experts",
    )(xs_p, xs_s, gv, mod, w1, w3, w2)


PROMPT_SLOTS = N_EXPERTS * CAP_PROMPT


def _slot_onehot(rank_rows):
    slot = lax.broadcasted_iota(jnp.int32, (CAP_PROMPT, T_PROMPT), 0).astype(F32)
    return jnp.concatenate(
        [jnp.where(slot == rank_rows[e:e + 1, :], 1.0, 0.0) for e in range(N_EXPERTS)], axis=0)


def _prompt_route_kernel(lg_ref, h_ref, xs_ref, gv_ref, rank_ref, rank_sc):
    lane = lax.broadcasted_iota(jnp.int32, (T_PROMPT, LANES), 1)
    x = jnp.where(lane < N_EXPERTS, lg_ref[...], -jnp.inf)
    e = jnp.exp(x - jnp.max(x, axis=-1, keepdims=True))
    aff = e / jnp.sum(e, axis=-1, keepdims=True)
    aff_t = aff.T
    ti = lax.broadcasted_iota(jnp.int32, (T_PROMPT, T_PROMPT), 0)
    tj = lax.broadcasted_iota(jnp.int32, (T_PROMPT, T_PROMPT), 1)
    earlier = jnp.where(ti < tj, 1.0, 0.0)
    for ex in range(N_EXPERTS):
        a_row = aff_t[ex:ex + 1, :]
        a_col = aff[:, ex:ex + 1]
        ahead = jnp.where(a_col > a_row, 1.0, jnp.where(a_col == a_row, earlier, 0.0))
        rank_sc[ex:ex + 1, :] = jnp.sum(ahead, axis=0, keepdims=True)
    rank = rank_sc[...]
    onehot = _slot_onehot(rank)
    xs_ref[...] = _dot(onehot.astype(BF16), h_ref[...]).astype(xs_ref.dtype).reshape(
        N_EXPERTS, CAP_PROMPT, D)
    for ex in range(N_EXPERTS):
        blk = onehot[ex * CAP_PROMPT:(ex + 1) * CAP_PROMPT, :]
        gv_ref[ex] = jnp.sum(blk * aff_t[ex:ex + 1, :], axis=-1, keepdims=True)
    rank_ref[0] = rank


def _prompt_route_call(logits, h2):
    return pl.pallas_call(
        _prompt_route_kernel,
        grid=(N_PROMPT_SEQ,),
        in_specs=[pl.BlockSpec((T_PROMPT, LANES), lambda s: (s, 0)),
                  pl.BlockSpec((T_PROMPT, D), lambda s: (s, 0))],
        out_specs=[pl.BlockSpec((N_EXPERTS, CAP_PROMPT, D), lambda s: (0, s, 0)),
                   pl.BlockSpec((N_EXPERTS, CAP_PROMPT, 1), lambda s: (0, s, 0)),
                   pl.BlockSpec((1, N_EXPERTS, T_PROMPT), lambda s: (s, 0, 0))],
        out_shape=[jax.ShapeDtypeStruct((N_EXPERTS, N_PROMPT_SEQ * CAP_PROMPT, D), BF16),
                   jax.ShapeDtypeStruct((N_EXPERTS, N_PROMPT_SEQ * CAP_PROMPT, 1), F32),
                   jax.ShapeDtypeStruct((N_PROMPT_SEQ, N_EXPERTS, T_PROMPT), F32)],
        scratch_shapes=[pltpu.VMEM((N_EXPERTS, T_PROMPT), F32)],
        compiler_params=_cparams(("arbitrary",), VMEM_LIMIT),
        name="prompt_route",
    )(logits, h2)


def _prompt_combine_kernel(x_ref, y_ref, rank_ref, o_ref):
    onehot = _slot_onehot(rank_ref[0]).astype(BF16)
    y = y_ref[...].reshape(PROMPT_SLOTS, D)
    hi = y.astype(BF16)
    lo = (y - hi.astype(F32)).astype(BF16)
    tn = (((0,), (0,)), ((), ()))
    o_ref[...] = x_ref[...] + (lax.dot_general(onehot, hi, tn, preferred_element_type=F32)
                               + lax.dot_general(onehot, lo, tn, preferred_element_type=F32))


def _prompt_combine_call(x, y_p, rank):
    return pl.pallas_call(
        _prompt_combine_kernel,
        grid=(N_PROMPT_SEQ,),
        in_specs=[pl.BlockSpec((T_PROMPT, D), lambda s: (s, 0)),
                  pl.BlockSpec((N_EXPERTS, CAP_PROMPT, D), lambda s: (0, s, 0)),
                  pl.BlockSpec((1, N_EXPERTS, T_PROMPT), lambda s: (s, 0, 0))],
        out_specs=pl.BlockSpec((T_PROMPT, D), lambda s: (s, 0)),
        out_shape=jax.ShapeDtypeStruct((ROWS, D), F32),
        input_output_aliases={0: 0},
        compiler_params=_cparams(("arbitrary",), VMEM_LIMIT),
        name="prompt_combine",
    )(x, y_p, rank)


def _final_norm_kernel(x_ref, g_ref, o_ref):
    o_ref[...] = _rms(x_ref[...]) * g_ref[...]


def _final_norm_call(x, g, row0, n_rows):
    tm = 512
    return pl.pallas_call(
        _final_norm_kernel,
        grid=(n_rows // tm,),
        in_specs=[pl.BlockSpec((tm, D), lambda i: (row0 // tm + i, 0)),
                  pl.BlockSpec((1, D), lambda i: (0, 0))],
        out_specs=pl.BlockSpec((tm, D), lambda i: (i, 0)),
        out_shape=jax.ShapeDtypeStruct((n_rows, D), F32),
        compiler_params=_cparams(("arbitrary",), VMEM_LIMIT),
        name="final_norm",
    )(x, g.reshape(1, D))


def _rope_tables():
    tpos = np.arange(T_SAMPLE)
    lane = np.arange(R_DH)
    pos = np.where(lane[None, :] < R_DH // 2, (tpos // GRID_W)[:, None], (tpos % GRID_W)[:, None])
    n_freq = R_DH // 4
    freqs = np.power(np.float32(ROPE_BASE), -np.arange(n_freq, dtype=np.float32) / np.float32(n_freq))
    ang = pos.astype(np.float32) * freqs[lane % n_freq][None, :]
    first = ((lane % (R_DH // 2)) < n_freq)[None, :]
    cos, sin = np.cos(ang), np.sin(ang)
    sin = np.where(first, -sin, sin)
    return tuple(jnp.asarray(a, F32) for a in (cos, sin, np.ascontiguousarray(cos.T), np.ascontiguousarray(sin.T)))


def _route_sample(logits):
    aff = jax.nn.softmax(logits[ROWS_PROMPT:, :N_EXPERTS], axis=-1)
    gs, is_ = lax.top_k(aff.reshape(N_SAMPLE_SEQ, T_SAMPLE, N_EXPERTS).swapaxes(1, 2), CAP_SAMPLE)
    is_ = is_ + ROWS_PROMPT + (jnp.arange(N_SAMPLE_SEQ) * T_SAMPLE)[:, None, None]
    rows_s = is_.swapaxes(0, 1).reshape(N_EXPERTS, -1)
    return rows_s, gs.swapaxes(0, 1).reshape(N_EXPERTS, -1)


def _layer(x, mod, p, states, rope_tabs, stacked, layer, prev_states):
    mq, mk, mv, mo, mg, lx, lz, rq, rk, rv, rg = jnp.split(
        p['w_in'].T, [1024, 2048, 3072, 4096, 4112, 5136, 6160, 7184, 8208, 9232], axis=0)
    w_main_t = jnp.concatenate([mq, mv, mo, lx, lz, rq, rv, rg], axis=0).astype(BF16)
    w_kt = jnp.concatenate([mk.astype(BF16) * (M_DH ** -0.5), rk.astype(BF16)], axis=0)
    w_gate = jnp.pad(mg.T, ((0, 0), (0, LANES - N_GATE_COLS))).astype(BF16)

    h, gates = _norm1_call(x, p['norm1_g'], mod, w_gate)
    proj = _inproj_call(h, w_main_t)
    proj_t = _inproj_t_call(h, w_kt)

    gb = gates[:, :N_GATE_COLS] + p['mlstm_gate_bias'].reshape(1, N_GATE_COLS)
    grow = gb.reshape(ROWS, 4, M_HEADS).transpose(2, 1, 0)
    grow = jnp.pad(grow, ((0, 0), (0, SUBLANES - 4), (0, 0)))

    sm_c, sm_n, sm_m, s_lh, s_rs = states
    prompt_kw = dict(t=T_PROMPT, n_seq=N_PROMPT_SEQ, row_block0=0, layer=layer, emit_state=True)
    sample_kw = dict(t=T_SAMPLE, n_seq=N_SAMPLE_SEQ, row_block0=ROWS_PROMPT // T_SAMPLE, layer=layer,
                     emit_state=False)
    pc, pn, pm, plh, prs = prev_states

    caug0 = jnp.concatenate(
        [sm_c, jnp.broadcast_to(sm_n[..., None], sm_n.shape + (LANES,))], axis=-1)
    m0 = jnp.broadcast_to(sm_m[..., None, None], (N_SAMPLE_SEQ, 2, M_HEADS, 1, LANES))
    (ym,) = _mlstm_call(proj, proj_t, grow, p['mlstm_norm_g'], (caug0, m0), [None], **sample_kw)
    ym, new_c, new_n, new_m = _mlstm_call(proj, proj_t, grow, p['mlstm_norm_g'], None,
                                          [ym, pc, pn, pm], **prompt_kw)

    lru_p = dict(conv_w=p['lru_conv_w'], conv_b=p['lru_conv_b'], wr=p['lru_wr'], wi=p['lru_wi'],
                 br=p['lru_br'], bi=p['lru_bi'], lam=p['lru_lambda'])
    (yl,) = _lru_call(proj, lru_p, s_lh, [None], **sample_kw)
    yl, new_lh = _lru_call(proj, lru_p, None, [yl, plh], **prompt_kw)

    decay = jnp.broadcast_to(p['ret_decay'].T[:, :, None], (R_HEADS, 2, LANES))
    (yr,) = _ret_call(proj, proj_t, decay, p['ret_norm_g'], rope_tabs, s_rs, [None], **sample_kw)
    yr, new_rs = _ret_call(proj, proj_t, decay, p['ret_norm_g'], None, None, [yr, prs], **prompt_kw)

    rw = jnp.pad(p['router_w'], ((0, 0), (0, LANES - N_EXPERTS)))
    rwh = rw.astype(BF16)
    rwl = (rw - rwh.astype(F32)).astype(BF16)
    xn, h2, logits = _merge_call(h, ym, yl, yr, x, mod, p['w_merge'].astype(BF16), p['b_merge'],
                                 p['w_branch'].astype(BF16), p['w_out'].astype(BF16), p['norm2_g'],
                                 rwh, rwl)

    xs_p, gv_p, rank_p = _prompt_route_call(logits, h2)
    rows_s, gv_s = _route_sample(logits)
    xs_s = h2.at[rows_s].get(mode='promise_in_bounds')
    gv = jnp.concatenate([gv_p, gv_s[..., None]], axis=1)
    y_p, y_s = _expert_call(xs_p, xs_s, gv, mod, stacked['exp_w1'], stacked['exp_w3'],
                            stacked['exp_w2'], layer)
    x_out = xn.at[rows_s.reshape(-1)].add(y_s.reshape(-1, D), mode='promise_in_bounds')
    x_out = _prompt_combine_call(x_out, y_p, rank_p)
    return x_out, (new_c, new_n, new_m, new_lh, new_rs)


def kernel(x_prompt, x_sample, c, state_mlstm_C, state_mlstm_n, state_mlstm_m, state_lru_h, state_ret_S, c_ctx, w_ada, b_ada, norm1_g, norm2_g, w_in, mlstm_gate_bias, mlstm_norm_g, lru_conv_w, lru_conv_b, lru_wr, lru_br, lru_wi, lru_bi, lru_lambda, ret_decay, ret_norm_g, w_branch, w_merge, b_merge, w_out, router_w, exp_w1, exp_w3, exp_w2, final_g):
    x = jnp.concatenate([x_prompt.reshape(ROWS_PROMPT, D), x_sample.reshape(-1, D)], axis=0)
    cond8 = jnp.concatenate([c_ctx[None, :], c, jnp.zeros((8 - 1 - N_SAMPLE_SEQ, D), F32)], axis=0)
    mod_all = _ada_call(cond8, w_ada, b_ada).reshape(DEPTH, 8, 6, D)
    rope_tabs = _rope_tables()
    stacked = dict(exp_w1=exp_w1, exp_w3=exp_w3, exp_w2=exp_w2)

    new_states = (None,) * 5
    for l in range(DEPTH):
        p = dict(norm1_g=norm1_g[l], norm2_g=norm2_g[l], w_in=w_in[l],
                 mlstm_gate_bias=mlstm_gate_bias[l], mlstm_norm_g=mlstm_norm_g[l],
                 lru_conv_w=lru_conv_w[l], lru_conv_b=lru_conv_b[l], lru_wr=lru_wr[l],
                 lru_br=lru_br[l], lru_wi=lru_wi[l], lru_bi=lru_bi[l], lru_lambda=lru_lambda[l],
                 ret_decay=ret_decay[l], ret_norm_g=ret_norm_g[l], w_branch=w_branch[l],
                 w_merge=w_merge[l], b_merge=b_merge[l], w_out=w_out[l], router_w=router_w[l])
        states = (state_mlstm_C[:, l], state_mlstm_n[:, l], state_mlstm_m[:, l],
                  state_lru_h[:, l], state_ret_S[:, l])
        x, new_states = _layer(x, mod_all[l], p, states, rope_tabs, stacked, l, new_states)

    y_prompt = _final_norm_call(x, final_g, 0, ROWS_PROMPT).reshape(N_PROMPT_SEQ, T_PROMPT, D)
    y_sample = _final_norm_call(x, final_g, ROWS_PROMPT, ROWS - ROWS_PROMPT).reshape(
        N_SAMPLE_SEQ, T_SAMPLE, D)
    new_c, new_n, new_m, new_lh, new_rs = new_states
    return (y_prompt, y_sample, new_c, new_n[:, :, :, :, 0, :], new_m[:, :, :, :, 0, 0], new_lh, new_rs)
```
